```python
import math
import jax, jax.numpy as jnp
from jax import lax
import numpy as np

D_MODEL = 1024
BATCH = 4
SEQ = 4096
DEPTH = 2
DEC_BATCH = 128
DEC_SEQ = 8
PAST_LEN = 16384
PAGE_SIZE = 128

N_EVEN = (DEPTH + 1) // 2
N_ODD = DEPTH // 2
D_CONV = D_MODEL // 2
CONV_W = 3
HEAD_DIM = 64
N_Q = (D_MODEL // 2) // HEAD_DIM
N_KV = 2
GQA_G = N_Q // N_KV
WINDOW = 128
ROPE_THETA = 10000.0
D_IN_EVEN = 3 * D_CONV + (N_Q + 2 * N_KV) * HEAD_DIM
D_MIX_EVEN = D_CONV + N_Q * HEAD_DIM
C_HEADS = 8
C_DK = D_MODEL // C_HEADS
C_DV = D_MODEL // C_HEADS
C_HK = C_HEADS * C_DK
C_HV = C_HEADS * C_DV
D_IN_ODD = 2 * C_HK + 2 * C_HV
C_CHUNK = 64
N_GROUPS = 8
EXP_PER_GROUP = 8
N_EXPERTS = N_GROUPS * EXP_PER_GROUP
TOP_K = 2
D_EXPERT = 512
MOE_BLOCK = 128
EPS = 1e-6

kernel_name = 'hybrid_conv_swa_hgrn2_hmoe_step'

F32 = jnp.float32


def rmsnorm(x, g):
    xf = x.astype(F32)
    y = xf * lax.rsqrt(jnp.mean(xf * xf, axis=-1, keepdims=True) + EPS)
    return (y * g.astype(F32)).astype(x.dtype)


def rope(x, pos):
    half = HEAD_DIM // 2
    inv = ROPE_THETA ** (-jnp.arange(0, HEAD_DIM, 2, dtype=F32) / HEAD_DIM)
    ang = pos.astype(F32)[:, None] * inv[None, :]
    cos = jnp.cos(ang)[:, None, :]
    sin = jnp.sin(ang)[:, None, :]
    xf = x.astype(F32)
    x1, x2 = xf[..., :half], xf[..., half:]
    return jnp.concatenate([x1 * cos - x2 * sin, x2 * cos + x1 * sin], axis=-1).astype(x.dtype)


def sink_attention(q, k, v, mask, sinks):
    s = jnp.einsum('...qhgd,...khd->...hgqk', q, k, preferred_element_type=F32) * (HEAD_DIM ** -0.5)
    s = jnp.where(mask, s, -jnp.inf)
    sink = jnp.broadcast_to(sinks.astype(F32).reshape(N_KV, GQA_G, 1, 1), s.shape[:-1] + (1,))
    p = jax.nn.softmax(jnp.concatenate([s, sink], axis=-1), axis=-1)[..., :-1]
    return jnp.einsum('...hgqk,...khd->...qhgd', p.astype(v.dtype), v)


def even_mixer(h, pos, bufs, w_in, conv_w, q_g, k_g, sinks, w_out):
    B, L, _ = h.shape
    proj = h @ w_in
    cuts = [D_CONV, 2 * D_CONV, 3 * D_CONV, 3 * D_CONV + N_Q * HEAD_DIM,
            3 * D_CONV + (N_Q + N_KV) * HEAD_DIM]
    gb, gc, hx, q, k, v = jnp.split(proj, cuts, axis=-1)
    u = gc * hx
    if bufs is None:
        pad = jnp.zeros((B, CONV_W - 1, D_CONV), u.dtype)
    else:
        pad = bufs[0].astype(u.dtype)
    up = jnp.concatenate([pad, u], axis=1)
    conv = sum(conv_w[j] * up[:, j:j + L] for j in range(CONV_W))
    a_out = gb * conv
    new_conv = up[:, -(CONV_W - 1):]
    q = rope(rmsnorm(q.reshape(B, L, N_Q, HEAD_DIM), q_g), pos)
    k = rope(rmsnorm(k.reshape(B, L, N_KV, HEAD_DIM), k_g), pos)
    v = v.reshape(B, L, N_KV, HEAD_DIM)
    if bufs is None:
        nb = L // WINDOW
        qb = q.reshape(B, nb, WINDOW, N_KV, GQA_G, HEAD_DIM)
        kb = k.reshape(B, nb, WINDOW, N_KV, HEAD_DIM)
        vb = v.reshape(B, nb, WINDOW, N_KV, HEAD_DIM)
        kk = jnp.concatenate([jnp.concatenate([jnp.zeros_like(kb[:, :1]), kb[:, :-1]], axis=1), kb], axis=2)
        vv = jnp.concatenate([jnp.concatenate([jnp.zeros_like(vb[:, :1]), vb[:, :-1]], axis=1), vb], axis=2)
        i = jnp.arange(WINDOW)[:, None]
        j = jnp.arange(2 * WINDOW)[None, :]
        diff = i + WINDOW - j
        blk = jnp.arange(nb)[:, None, None]
        mask = (diff >= 0) & (diff <= WINDOW) & ((blk > 0) | (j >= WINDOW))
        attn = sink_attention(qb, kk, vv, mask[:, None, None], sinks)
        k_all, v_all = k, v
    else:
        k_all = jnp.concatenate([bufs[1].astype(k.dtype), k], axis=1)
        v_all = jnp.concatenate([bufs[2].astype(v.dtype), v], axis=1)
        i = jnp.arange(L)[:, None]
        j = jnp.arange(WINDOW + L)[None, :]
        diff = i + WINDOW - j
        mask = (diff >= 0) & (diff <= WINDOW)
        attn = sink_attention(q.reshape(B, L, N_KV, GQA_G, HEAD_DIM), k_all, v_all, mask, sinks)
    attn = attn.reshape(B, L, N_Q * HEAD_DIM)
    out = jnp.concatenate([a_out, attn], axis=-1) @ w_out
    return out, new_conv, k_all[:, -WINDOW:], v_all[:, -WINDOW:]


def gla_chunked(q, k, v, log_f, s0):
    B, L, H, DK = q.shape
    DV = v.shape[-1]
    C = math.gcd(L, C_CHUNK)
    N = L // C

    def chunks(a):
        return a.reshape(B, N, C, H, a.shape[-1]).transpose(1, 0, 3, 2, 4)

    causal = jnp.tril(jnp.ones((C, C), bool))

    def step(S, inp):
        qc, kc, vc, gc = inp
        b = jnp.cumsum(gc, axis=2)
        inter = jnp.einsum('bhtd,bhdv->bhtv', qc * jnp.exp(b), S)
        decay = jnp.exp(jnp.where(causal[:, :, None], b[:, :, :, None, :] - b[:, :, None, :, :], -jnp.inf))
        scores = jnp.einsum('bhtd,bhsd,bhtsd->bhts', qc, kc, decay)
        intra = jnp.einsum('bhts,bhsv->bhtv', scores, vc)
        b_last = b[:, :, -1:, :]
        S_new = jnp.exp(b_last[:, :, 0, :, None]) * S + jnp.einsum('bhsd,bhsv->bhdv', kc * jnp.exp(b_last - b), vc)
        return S_new, inter + intra

    S, o = lax.scan(step, s0, (chunks(q), chunks(k), chunks(v), chunks(log_f)))
    o = o.transpose(1, 0, 3, 2, 4).reshape(B, L, H, DV)
    return o, S


def odd_mixer(h, s0, w_in, lb, o_g, w_out):
    B, L, _ = h.shape
    proj = (h @ w_in).astype(F32)
    q, fz, i, g = jnp.split(proj, [C_HK, 2 * C_HK, 2 * C_HK + C_HV], axis=-1)
    shp_k = (B, L, C_HEADS, C_DK)
    shp_v = (B, L, C_HEADS, C_DV)
    q = jax.nn.silu(q).reshape(shp_k)
    lbh = lb.reshape(C_HEADS, C_DK)
    f = lbh + (1.0 - lbh) * jax.nn.sigmoid(fz.reshape(shp_k))
    if s0 is None:
        s0 = jnp.zeros((B, C_HEADS, C_DK, C_DV), F32)
    else:
        s0 = s0.astype(F32)
    o, S = gla_chunked(q, 1.0 - f, i.reshape(shp_v), jnp.log(f), s0)
    o = rmsnorm(o, o_g) * jax.nn.silu(g.reshape(shp_v))
    return o.reshape(B, L, C_HV).astype(h.dtype) @ w_out, S


def hier_moe(x2, w_grp, b_grp, w_exp, b_exp, w1, w3, w2):
    T, D = x2.shape
    xf = x2.astype(F32)
    lg = xf @ w_grp.astype(F32) + b_grp.astype(F32)
    g_idx = jnp.argmax(lg, axis=-1)
    g_p = jnp.take_along_axis(jax.nn.softmax(lg, axis=-1), g_idx[:, None], axis=1)[:, 0]
    le = (xf @ w_exp.astype(F32) + b_exp.astype(F32)).reshape(T, N_GROUPS, EXP_PER_GROUP)
    le = jnp.take_along_axis(le, g_idx[:, None, None], axis=1)[:, 0]
    top_v, top_i = lax.top_k(jax.nn.softmax(le, axis=-1), TOP_K)
    gate = top_v / jnp.sum(top_v, axis=-1, keepdims=True) * g_p[:, None]
    expert = (g_idx[:, None] * EXP_PER_GROUP + top_i).astype(jnp.int32)
    n_assign = T * TOP_K
    e_flat = expert.reshape(-1)
    tok_flat = jnp.repeat(jnp.arange(T, dtype=jnp.int32), TOP_K)
    w_flat = gate.reshape(-1)
    order = jnp.argsort(e_flat)
    e_s, tok_s, w_s = e_flat[order], tok_flat[order], w_flat[order]
    counts = jax.ops.segment_sum(jnp.ones_like(e_flat), e_flat, num_segments=N_EXPERTS)
    start = jnp.cumsum(counts) - counts
    padded = (counts + MOE_BLOCK - 1) // MOE_BLOCK * MOE_BLOCK
    pend = jnp.cumsum(padded)
    pstart = pend - padded
    slot = pstart[e_s] + jnp.arange(n_assign, dtype=jnp.int32) - start[e_s]
    n_blocks = -(-n_assign // MOE_BLOCK) + N_EXPERTS
    n_slots = n_blocks * MOE_BLOCK
    slot_tok = jnp.zeros((n_slots,), jnp.int32).at[slot].set(tok_s)
    slot_w = jnp.zeros((n_slots,), F32).at[slot].set(w_s)
    blk_e = jnp.minimum(jnp.searchsorted(pend, jnp.arange(n_blocks, dtype=jnp.int32) * MOE_BLOCK, side='right'),
                        N_EXPERTS - 1)
    xb = x2[slot_tok].reshape(n_blocks, MOE_BLOCK, D)

    def expert_block(args):
        xblk, e = args
        hid = jax.nn.silu(xblk @ w1[e]) * (xblk @ w3[e])
        return hid @ w2[e]

    yb = lax.map(expert_block, (xb, blk_e))
    contrib = (yb.reshape(n_slots, D).astype(F32) * slot_w[:, None]).astype(x2.dtype)
    return jnp.zeros_like(x2).at[slot_tok].add(contrib)


def trunk(x, pos, caches, norm_mix, norm_ffn, ev_w_in, ev_conv, ev_q_norm, ev_k_norm, ev_sinks, ev_w_out,
          od_w_in, od_lb, od_o_norm, od_w_out, moe_w_grp, moe_b_grp, moe_w_exp, moe_b_exp, moe_w1, moe_w3, moe_w2):
    B, L, D = x.shape
    lb_all = jnp.cumsum(jax.nn.softmax(od_lb.astype(F32), axis=0), axis=0)
    lb_all = lb_all - lb_all[0]
    new_conv, new_k, new_v, new_s = [], [], [], []
    for l in range(DEPTH):
        h = rmsnorm(x, norm_mix[l])
        if l % 2 == 0:
            e = l // 2
            bufs = None if caches is None else (caches[0][e], caches[1][e], caches[2][e])
            m, c_b, k_b, v_b = even_mixer(h, pos, bufs, ev_w_in[e], ev_conv[e], ev_q_norm[e], ev_k_norm[e],
                                          ev_sinks[e], ev_w_out[e])
            new_conv.append(c_b)
            new_k.append(k_b)
            new_v.append(v_b)
        else:
            o = l // 2
            s0 = None if caches is None else caches[3][o]
            m, s = odd_mixer(h, s0, od_w_in[o], lb_all[l], od_o_norm[o], od_w_out[o])
            new_s.append(s)
        x = x + m
        h = rmsnorm(x, norm_ffn[l])
        x = x + hier_moe(h.reshape(B * L, D), moe_w_grp[l], moe_b_grp[l], moe_w_exp[l], moe_b_exp[l],
                         moe_w1[l], moe_w3[l], moe_w2[l]).reshape(B, L, D)
    return x, jnp.stack(new_conv), jnp.stack(new_k), jnp.stack(new_v), jnp.stack(new_s)


def setup_inputs(seed: int = 0) -> dict:
    key = jax.random.key(seed)
    ks = jax.random.split(key, 26)
    nrm = jax.random.normal
    D = D_MODEL
    return {
        'x_prompt': nrm(ks[0], (BATCH, SEQ, D), F32),
        'x_sample': nrm(ks[1], (DEC_BATCH, DEC_SEQ, D), F32),
        'cache_conv': nrm(ks[2], (N_EVEN, DEC_BATCH, CONV_W - 1, D_CONV), F32),
        'cache_k': nrm(ks[3], (N_EVEN, DEC_BATCH, WINDOW, N_KV, HEAD_DIM), F32),
        'cache_v': nrm(ks[4], (N_EVEN, DEC_BATCH, WINDOW, N_KV, HEAD_DIM), F32),
        'state_hgrn': 0.5 * nrm(ks[5], (N_ODD, DEC_BATCH, C_HEADS, C_DK, C_DV), F32),
        'norm_mix': 1.0 + 0.02 * nrm(ks[6], (DEPTH, D), F32),
        'norm_ffn': 1.0 + 0.02 * nrm(ks[7], (DEPTH, D), F32),
        'ev_w_in': nrm(ks[8], (N_EVEN, D, D_IN_EVEN), F32) * D ** -0.5,
        'ev_conv': nrm(ks[9], (N_EVEN, CONV_W, D_CONV), F32) * CONV_W ** -0.5,
        'ev_q_norm': 1.0 + 0.02 * nrm(ks[10], (N_EVEN, HEAD_DIM), F32),
        'ev_k_norm': 1.0 + 0.02 * nrm(ks[11], (N_EVEN, HEAD_DIM), F32),
        'ev_sinks': nrm(ks[12], (N_EVEN, N_Q), F32),
        'ev_w_out': nrm(ks[13], (N_EVEN, D_MIX_EVEN, D), F32) * D_MIX_EVEN ** -0.5,
        'od_w_in': nrm(ks[14], (N_ODD, D, D_IN_ODD), F32) * D ** -0.5,
        'od_lb': nrm(ks[15], (DEPTH, C_HK), F32),
        'od_o_norm': 1.0 + 0.02 * nrm(ks[16], (N_ODD, C_DV), F32),
        'od_w_out': nrm(ks[17], (N_ODD, C_HV, D), F32) * C_HV ** -0.5,
        'moe_w_grp': nrm(ks[18], (DEPTH, D, N_GROUPS), F32) * D ** -0.5,
        'moe_b_grp': 0.01 * nrm(ks[19], (DEPTH, N_GROUPS), F32),
        'moe_w_exp': nrm(ks[20], (DEPTH, D, N_EXPERTS), F32) * D ** -0.5,
        'moe_b_exp': 0.01 * nrm(ks[21], (DEPTH, N_EXPERTS), F32),
        'moe_w1': nrm(ks[22], (DEPTH, N_EXPERTS, D, D_EXPERT), F32) * D ** -0.5,
        'moe_w3': nrm(ks[23], (DEPTH, N_EXPERTS, D, D_EXPERT), F32) * D ** -0.5,
        'moe_w2': nrm(ks[24], (DEPTH, N_EXPERTS, D_EXPERT, D), F32) * D_EXPERT ** -0.5,
    }


def reference(x_prompt, x_sample, cache_conv, cache_k, cache_v, state_hgrn, norm_mix, norm_ffn,
              ev_w_in, ev_conv, ev_q_norm, ev_k_norm, ev_sinks, ev_w_out,
              od_w_in, od_lb, od_o_norm, od_w_out,
              moe_w_grp, moe_b_grp, moe_w_exp, moe_b_exp, moe_w1, moe_w3, moe_w2):
    pos_p = jnp.arange(x_prompt.shape[1], dtype=jnp.int32)
    pos_s = PAST_LEN + jnp.arange(x_sample.shape[1], dtype=jnp.int32)
    y_prompt, conv_p, k_p, v_p, s_p = trunk(
        x_prompt, pos_p, None, norm_mix, norm_ffn, ev_w_in, ev_conv, ev_q_norm, ev_k_norm, ev_sinks, ev_w_out,
        od_w_in, od_lb, od_o_norm, od_w_out, moe_w_grp, moe_b_grp, moe_w_exp, moe_b_exp, moe_w1, moe_w3, moe_w2)
    y_sample, conv_s, k_s, v_s, s_s = trunk(
        x_sample, pos_s, (cache_conv, cache_k, cache_v, state_hgrn), norm_mix, norm_ffn,
        ev_w_in, ev_conv, ev_q_norm, ev_k_norm, ev_sinks, ev_w_out,
        od_w_in, od_lb, od_o_norm, od_w_out, moe_w_grp, moe_b_grp, moe_w_exp, moe_b_exp, moe_w1, moe_w3, moe_w2)
    return (y_prompt, y_sample, conv_p, k_p, v_p, s_p, conv_s, k_s, v_s, s_s)
```

```python
import functools

import numpy as np
import jax
import jax.numpy as jnp
from jax import lax
from jax.experimental import pallas as pl
from jax.experimental.pallas import tpu as pltpu

F32 = jnp.float32
BF16 = jnp.bfloat16

D_MODEL = 1024
PAST_LEN = 16384
D_CONV = 512
CONV_W = 3
HEAD_DIM = 64
N_Q = 8
N_KV = 2
GQA_G = N_Q // N_KV
WINDOW = 128
ROPE_THETA = 10000.0
D_IN_EVEN = 3 * D_CONV + (N_Q + 2 * N_KV) * HEAD_DIM
C_HEADS = 8
C_DK = 128
C_DV = 128
C_HK = C_HEADS * C_DK
N_GROUPS = 8
EXP_PER_GROUP = 8
N_EXPERTS = N_GROUPS * EXP_PER_GROUP
D_EXPERT = 512
MOE_BLOCK = 128
EPS = 1e-6

LANES = 128
SUBLANES = 8
VMEM_LIMIT = 56 * 1024 * 1024

GLA_CHUNK = 64
GLA_STEP = 256
SAMPLE_GB = 2
ROUTE_TM = 512
LOGIT_W = 128


def _cparams(sem):
    return pltpu.CompilerParams(dimension_semantics=sem, vmem_limit_bytes=VMEM_LIMIT)


def _sigmoid(x):
    return 1.0 / (1.0 + jnp.exp(-x))


def _norm_matmul_kernel(x_ref, g_ref, w_ref, o_ref):
    x = x_ref[...]
    ms = jnp.mean(x * x, axis=-1, keepdims=True)
    h = (x * lax.rsqrt(ms + EPS) * g_ref[...]).astype(BF16)
    o_ref[...] = jnp.dot(h, w_ref[...], preferred_element_type=F32)


def _norm_matmul(x, g, w, tm):
    T, D = x.shape
    N = w.shape[1]
    return pl.pallas_call(
        _norm_matmul_kernel,
        grid=(T // tm,),
        in_specs=[pl.BlockSpec((tm, D), lambda i: (i, 0)),
                  pl.BlockSpec((1, D), lambda i: (0, 0)),
                  pl.BlockSpec((D, N), lambda i: (0, 0))],
        out_specs=pl.BlockSpec((tm, N), lambda i: (i, 0)),
        out_shape=jax.ShapeDtypeStruct((T, N), F32),
        compiler_params=_cparams(("arbitrary",)),
        name="norm_matmul",
    )(x, g.reshape(1, D), w)


def _rope_tables(pos):
    inv = ROPE_THETA ** (-np.arange(0, HEAD_DIM, 2, dtype=np.float64) / HEAD_DIM)
    ang = np.asarray(pos, np.float64)[:, None] * inv[None, :]
    cos = np.cos(ang)
    sin = np.sin(ang)
    cos_t = np.concatenate([cos, cos, cos, cos], axis=1)
    sin_t = np.concatenate([-sin, sin, -sin, sin], axis=1)
    return jnp.asarray(cos_t, F32), jnp.asarray(sin_t, F32)


def _headnorm_rope(x, g2, cos, sin):
    lane = lax.broadcasted_iota(jnp.int32, x.shape, 1)
    lo = lane < HEAD_DIM
    x2 = x * x
    s_lo = jnp.sum(jnp.where(lo, x2, 0.0), axis=-1, keepdims=True)
    s_hi = jnp.sum(jnp.where(lo, 0.0, x2), axis=-1, keepdims=True)
    ms = jnp.where(lo, s_lo, s_hi) * (1.0 / HEAD_DIM)
    y = x * lax.rsqrt(ms + EPS) * g2
    first_half = (lane & (HEAD_DIM // 2)) == 0
    swapped = jnp.where(first_half, pltpu.roll(y, LANES - HEAD_DIM // 2, 1),
                        pltpu.roll(y, HEAD_DIM // 2, 1))
    return y * cos + swapped * sin


def _gated_conv(gb, u, c2, c1, cw):
    R = u.shape[0]
    row = lax.broadcasted_iota(jnp.int32, u.shape, 0)
    u1 = jnp.where(row == 0, c1, pltpu.roll(u, 1, 0))
    u2 = jnp.where(row == 0, c2, jnp.where(row == 1, c1, pltpu.roll(u, 2, 0)))
    del R
    return gb * (cw[0:1, :] * u2 + cw[1:2, :] * u1 + cw[2:3, :] * u)


def _band_attention(qs, kk, vv, valid, sinkv):
    s = lax.dot_general(qs.astype(BF16), kk.astype(BF16), (((1,), (1,)), ((), ())),
                        preferred_element_type=F32) * (HEAD_DIM ** -0.5)
    s = jnp.where(valid, s, -jnp.inf)
    m = jnp.maximum(jnp.max(s, axis=-1, keepdims=True), sinkv)
    e = jnp.exp(s - m)
    den = jnp.sum(e, axis=-1, keepdims=True) + jnp.exp(sinkv - m)
    p = e / den
    return jnp.dot(p.astype(BF16), vv.astype(BF16), preferred_element_type=F32)


def _sink_column(sinks_ref, hk, rows_per_head):
    R = GQA_G * rows_per_head
    row = lax.broadcasted_iota(jnp.int32, (R, 1), 0)
    col = jnp.full((R, 1), sinks_ref[hk * GQA_G + GQA_G - 1], F32)
    for j in range(GQA_G - 2, -1, -1):
        col = jnp.where(row < (j + 1) * rows_per_head, sinks_ref[hk * GQA_G + j], col)
    return col


def _even_prompt_kernel(sinks_ref, proj_ref, cos_ref, sin_ref, cw_ref, qg_ref, kg_ref,
                        mix_ref, kl_ref, vl_ref, cl_ref, kprev, vprev, ucar):
    blk = pl.program_id(1)
    W = WINDOW

    @pl.when(blk == 0)
    def _():
        kprev[...] = jnp.zeros_like(kprev)
        vprev[...] = jnp.zeros_like(vprev)
        ucar[...] = jnp.zeros_like(ucar)

    gb = proj_ref[:, 0:D_CONV]
    u = proj_ref[:, D_CONV:2 * D_CONV] * proj_ref[:, 2 * D_CONV:3 * D_CONV]
    car = ucar[...]
    a_out = _gated_conv(gb, u, car[0:1, :], car[1:2, :], cw_ref[...])
    ucar[0:2, :] = u[W - 2:W, :]
    mix_ref[:, 0:D_CONV] = a_out.astype(BF16)

    cos = cos_ref[...]
    sin = sin_ref[...]
    q0 = 3 * D_CONV
    k0 = q0 + N_Q * HEAD_DIM
    v0 = k0 + N_KV * HEAD_DIM
    k_r = _headnorm_rope(proj_ref[:, k0:k0 + LANES], kg_ref[...], cos, sin)
    v_r = proj_ref[:, v0:v0 + LANES]
    q_r = [_headnorm_rope(proj_ref[:, q0 + LANES * j:q0 + LANES * (j + 1)], qg_ref[...], cos, sin)
           for j in range(N_Q * HEAD_DIM // LANES)]
    k_p = kprev[...]
    v_p = vprev[...]

    R = GQA_G * W
    i = lax.broadcasted_iota(jnp.int32, (R, 2 * W), 0) & (W - 1)
    j = lax.broadcasted_iota(jnp.int32, (R, 2 * W), 1)
    diff = i + W - j
    valid = (diff >= 0) & (diff <= W) & ((blk > 0) | (j >= W))

    for hk in range(N_KV):
        ls = slice(hk * HEAD_DIM, (hk + 1) * HEAD_DIM)
        kk = jnp.concatenate([k_p[:, ls], k_r[:, ls]], axis=0)
        vv = jnp.concatenate([v_p[:, ls], v_r[:, ls]], axis=0)
        heads = []
        for g in range(GQA_G):
            h = hk * GQA_G + g
            tile = q_r[h // 2]
            heads.append(tile[:, (h % 2) * HEAD_DIM:(h % 2 + 1) * HEAD_DIM])
        qs = jnp.concatenate(heads, axis=0)
        o = _band_attention(qs, kk, vv, valid, _sink_column(sinks_ref, hk, W))
        for g in range(GQA_G):
            h = hk * GQA_G + g
            mix_ref[:, D_CONV + h * HEAD_DIM:D_CONV + (h + 1) * HEAD_DIM] = \
                o[g * W:(g + 1) * W, :].astype(BF16)

    kprev[...] = k_r
    vprev[...] = v_r

    @pl.when(blk == pl.num_programs(1) - 1)
    def _():
        kl_ref[0] = k_r
        vl_ref[0] = v_r
        cl_ref[0] = u[W - 2:W, :]


def _even_prompt(proj, B, L, T, cw, qg2, kg2, sinks):
    nb = L // WINDOW
    cos, sin = _rope_tables(np.arange(L))
    full = lambda shape: pl.BlockSpec(shape, lambda b, i, *_: tuple(0 for _ in shape))
    grid_spec = pltpu.PrefetchScalarGridSpec(
        num_scalar_prefetch=1,
        grid=(B, nb),
        in_specs=[pl.BlockSpec((WINDOW, D_IN_EVEN), lambda b, i, s: (b * nb + i, 0)),
                  pl.BlockSpec((WINDOW, LANES), lambda b, i, s: (i, 0)),
                  pl.BlockSpec((WINDOW, LANES), lambda b, i, s: (i, 0)),
                  full((CONV_W, D_CONV)), full((1, LANES)), full((1, LANES))],
        out_specs=[pl.BlockSpec((WINDOW, D_MODEL), lambda b, i, s: (b * nb + i, 0)),
                   pl.BlockSpec((1, WINDOW, LANES), lambda b, i, s: (b, 0, 0)),
                   pl.BlockSpec((1, WINDOW, LANES), lambda b, i, s: (b, 0, 0)),
                   pl.BlockSpec((1, CONV_W - 1, D_CONV), lambda b, i, s: (b, 0, 0))],
        scratch_shapes=[pltpu.VMEM((WINDOW, LANES), F32), pltpu.VMEM((WINDOW, LANES), F32),
                        pltpu.VMEM((SUBLANES, D_CONV), F32)],
    )
    return pl.pallas_call(
        _even_prompt_kernel,
        grid_spec=grid_spec,
        out_shape=[jax.ShapeDtypeStruct((B * L, D_MODEL), BF16),
                   jax.ShapeDtypeStruct((B, WINDOW, LANES), F32),
                   jax.ShapeDtypeStruct((B, WINDOW, LANES), F32),
                   jax.ShapeDtypeStruct((B, CONV_W - 1, D_CONV), F32)],
        compiler_params=_cparams(("arbitrary", "arbitrary")),
        name="even_prompt",
    )(sinks, proj, cos, sin, cw, qg2, kg2)


def _even_sample_kernel(sinks_ref, proj_ref, cos_ref, sin_ref, cw_ref, qg_ref, kg_ref,
                        cc_ref, ck_ref, cv_ref, mix_ref, ko_ref, vo_ref, co_ref):
    W = WINDOW
    Ld = cos_ref.shape[0]
    cos = cos_ref[...]
    sin = sin_ref[...]
    q0 = 3 * D_CONV
    k0 = q0 + N_Q * HEAD_DIM
    v0 = k0 + N_KV * HEAD_DIM
    R = GQA_G * Ld
    i = lax.broadcasted_iota(jnp.int32, (R, 2 * W), 0) % Ld
    j = lax.broadcasted_iota(jnp.int32, (R, 2 * W), 1)
    diff = i + W - j
    valid = (diff >= 0) & (diff <= W)
    zpad = jnp.zeros((W - Ld, HEAD_DIM), F32)

    for bb in range(SAMPLE_GB):
        rs = slice(bb * Ld, (bb + 1) * Ld)
        gb = proj_ref[rs, 0:D_CONV]
        u = proj_ref[rs, D_CONV:2 * D_CONV] * proj_ref[rs, 2 * D_CONV:3 * D_CONV]
        car = cc_ref[bb]
        a_out = _gated_conv(gb, u, car[0:1, :], car[1:2, :], cw_ref[...])
        co_ref[bb] = u[Ld - 2:Ld, :]
        mix_ref[rs, 0:D_CONV] = a_out.astype(BF16)

        k_r = _headnorm_rope(proj_ref[rs, k0:k0 + LANES], kg_ref[...], cos, sin)
        v_r = proj_ref[rs, v0:v0 + LANES]
        q_r = [_headnorm_rope(proj_ref[rs, q0 + LANES * t:q0 + LANES * (t + 1)], qg_ref[...], cos, sin)
               for t in range(N_Q * HEAD_DIM // LANES)]
        k_c = ck_ref[bb]
        v_c = cv_ref[bb]
        ko_ref[bb, 0:W - Ld, :] = k_c[Ld:W, :]
        ko_ref[bb, W - Ld:W, :] = k_r
        vo_ref[bb, 0:W - Ld, :] = v_c[Ld:W, :]
        vo_ref[bb, W - Ld:W, :] = v_r

        for hk in range(N_KV):
            ls = slice(hk * HEAD_DIM, (hk + 1) * HEAD_DIM)
            kk = jnp.concatenate([k_c[:, ls], k_r[:, ls], zpad], axis=0)
            vv = jnp.concatenate([v_c[:, ls], v_r[:, ls], zpad], axis=0)
            heads = []
            for g in range(GQA_G):
                h = hk * GQA_G + g
                tile = q_r[h // 2]
                heads.append(tile[:, (h % 2) * HEAD_DIM:(h % 2 + 1) * HEAD_DIM])
            qs = jnp.concatenate(heads, axis=0)
            o = _band_attention(qs, kk, vv, valid, _sink_column(sinks_ref, hk, Ld))
            for g in range(GQA_G):
                h = hk * GQA_G + g
                mix_ref[rs, D_CONV + h * HEAD_DIM:D_CONV + (h + 1) * HEAD_DIM] = \
                    o[g * Ld:(g + 1) * Ld, :].astype(BF16)


def _even_sample(proj, Bd, Ld, row0, cw, qg2, kg2, sinks, cache_conv, cache_k, cache_v):
    GB = SAMPLE_GB
    rows = GB * Ld
    rb0 = row0 // rows
    cos, sin = _rope_tables(PAST_LEN + np.arange(Ld))
    full = lambda shape: pl.BlockSpec(shape, lambda i, *_: tuple(0 for _ in shape))
    grid_spec = pltpu.PrefetchScalarGridSpec(
        num_scalar_prefetch=1,
        grid=(Bd // GB,),
        in_specs=[pl.BlockSpec((rows, D_IN_EVEN), lambda i, s: (rb0 + i, 0)),
                  full((Ld, LANES)), full((Ld, LANES)),
                  full((CONV_W, D_CONV)), full((1, LANES)), full((1, LANES)),
                  pl.BlockSpec((GB, CONV_W - 1, D_CONV), lambda i, s: (i, 0, 0)),
                  pl.BlockSpec((GB, WINDOW, LANES), lambda i, s: (i, 0, 0)),
                  pl.BlockSpec((GB, WINDOW, LANES), lambda i, s: (i, 0, 0))],
        out_specs=[pl.BlockSpec((rows, D_MODEL), lambda i, s: (i, 0)),
                   pl.BlockSpec((GB, WINDOW, LANES), lambda i, s: (i, 0, 0)),
                   pl.BlockSpec((GB, WINDOW, LANES), lambda i, s: (i, 0, 0)),
                   pl.BlockSpec((GB, CONV_W - 1, D_CONV), lambda i, s: (i, 0, 0))],
    )
    return pl.pallas_call(
        _even_sample_kernel,
        grid_spec=grid_spec,
        out_shape=[jax.ShapeDtypeStruct((Bd * Ld, D_MODEL), BF16),
                   jax.ShapeDtypeStruct((Bd, WINDOW, LANES), F32),
                   jax.ShapeDtypeStruct((Bd, WINDOW, LANES), F32),
                   jax.ShapeDtypeStruct((Bd, CONV_W - 1, D_CONV), F32)],
        compiler_params=_cparams(("arbitrary",)),
        name="even_sample",
    )(sinks, proj, cos, sin, cw, qg2, kg2, cache_conv, cache_k, cache_v)


def _cumsum_rows(x):
    C = x.shape[0]
    row = lax.broadcasted_iota(jnp.int32, x.shape, 0)
    s = 1
    while s < C:
        x = x + jnp.where(row >= s, pltpu.roll(x, s, 0), 0.0)
        s *= 2
    return x


def _gla_chunk(qz, fz, v, lb, S):
    C = qz.shape[0]
    q = qz * _sigmoid(qz)
    f = lb + (1.0 - lb) * _sigmoid(fz)
    k = 1.0 - f
    b = _cumsum_rows(jnp.log(f))
    vb = v.astype(BF16)

    inter = jnp.dot((q * jnp.exp(b)).astype(BF16), S.astype(BF16), preferred_element_type=F32)

    row = lax.broadcasted_iota(jnp.int32, (C, C_DK), 0)
    si = lax.broadcasted_iota(jnp.int32, (C, C), 0)
    ti = lax.broadcasted_iota(jnp.int32, (C, C), 1)
    st = jnp.zeros((C, C), F32)
    m = C // 2
    while m >= SUBLANES:
        n = C // (2 * m)
        b3 = b.reshape(n, 2 * m, C_DK)
        rho = jnp.broadcast_to(b3[:, m - 1:m, :], (n, 2 * m, C_DK)).reshape(C, C_DK)
        upper = (row & (2 * m - 1)) >= m
        qt = jnp.where(upper, q * jnp.exp(jnp.minimum(b - rho, 0.0)), 0.0)
        kt = jnp.where(upper, 0.0, k * jnp.exp(jnp.minimum(rho - b, 0.0)))
        part = lax.dot_general(kt.astype(BF16), qt.astype(BF16), (((1,), (1,)), ((), ())),
                               preferred_element_type=F32)
        sh = (2 * m).bit_length() - 1
        same = (si >> sh) == (ti >> sh)
        st = st + jnp.where(same, part, 0.0)
        m //= 2
    lane_t = lax.broadcasted_iota(jnp.int32, (SUBLANES, C), 1)
    srow = lax.broadcasted_iota(jnp.int32, (SUBLANES, C_DK), 0)
    blocks = []
    for i8 in range(C // SUBLANES):
        r0 = i8 * SUBLANES
        kb = k[r0:r0 + SUBLANES, :]
        bb = b[r0:r0 + SUBLANES, :]
        acc = jnp.zeros((SUBLANES, C), F32)
        for t in range(SUBLANES):
            dec = jnp.exp(jnp.minimum(b[r0 + t:r0 + t + 1, :] - bb, 0.0))
            p = jnp.where(srow <= t, q[r0 + t:r0 + t + 1, :] * kb * dec, 0.0)
            col = jnp.sum(p, axis=-1, keepdims=True)
            acc = jnp.where(lane_t == r0 + t, col, acc)
        blocks.append(acc)
    st = st + (jnp.concatenate(blocks, axis=0) if len(blocks) > 1 else blocks[0])

    intra = lax.dot_general(st.astype(BF16), vb, (((0,), (0,)), ((), ())),
                            preferred_element_type=F32)

    b_last = b[C - 1:C, :]
    eye = (lax.broadcasted_iota(jnp.int32, (C_DK, C_DK), 0)
           == lax.broadcasted_iota(jnp.int32, (C_DK, C_DK), 1))
    dcol = jnp.sum(jnp.where(eye, jnp.broadcast_to(jnp.exp(b_last), (C_DK, C_DK)), 0.0),
                   axis=-1, keepdims=True)
    kd = k * jnp.exp(b_last - b)
    S_new = dcol * S + lax.dot_general(kd.astype(BF16), vb, (((0,), (0,)), ((), ())),
                                       preferred_element_type=F32)
    return inter + intra, S_new


def _gated_out(o, og, gz):
    ms = jnp.mean(o * o, axis=-1, keepdims=True)
    return (o * lax.rsqrt(ms + EPS) * og) * (gz * _sigmoid(gz))


def _hgrn_prompt_kernel(q_ref, f_ref, i_ref, g_ref, lb_ref, og_ref, o_ref, s_ref, S):
    c = pl.program_id(2)

    @pl.when(c == 0)
    def _():
        S[...] = jnp.zeros_like(S)

    lb = lb_ref[...]
    og = og_ref[...]

    def body(n, carry):
        rs = pl.ds(pl.multiple_of(n * GLA_CHUNK, GLA_CHUNK), GLA_CHUNK)
        o, S_new = _gla_chunk(q_ref[rs, :], f_ref[rs, :], i_ref[rs, :], lb, S[...])
        S[...] = S_new
        o_ref[rs, :] = _gated_out(o, og, g_ref[rs, :]).astype(BF16)
        return carry

    lax.fori_loop(0, GLA_STEP // GLA_CHUNK, body, 0)

    @pl.when(c == pl.num_programs(2) - 1)
    def _():
        s_ref[0, 0] = S[...]


def _hgrn_prompt(proj, B, L, T, lb, og):
    ns = L // GLA_STEP
    H = C_HEADS
    col = lambda off: pl.BlockSpec((GLA_STEP, LANES), lambda b, h, c: (b * ns + c, off + h))
    return pl.pallas_call(
        _hgrn_prompt_kernel,
        grid=(B, H, ns),
        in_specs=[col(0), col(H), col(2 * H), col(3 * H),
                  pl.BlockSpec((1, LANES), lambda b, h, c: (0, h)),
                  pl.BlockSpec((1, LANES), lambda b, h, c: (0, 0))],
        out_specs=[pl.BlockSpec((GLA_STEP, LANES), lambda b, h, c: (b * ns + c, h)),
                   pl.BlockSpec((1, 1, C_DK, C_DV), lambda b, h, c: (b, h, 0, 0))],
        out_shape=[jax.ShapeDtypeStruct((B * L, D_MODEL), BF16),
                   jax.ShapeDtypeStruct((B, H, C_DK, C_DV), F32)],
        scratch_shapes=[pltpu.VMEM((C_DK, C_DV), F32)],
        compiler_params=_cparams(("arbitrary", "arbitrary", "arbitrary")),
        name="hgrn_prompt",
    )(proj, proj, proj, proj, lb, og)


def _hgrn_sample_kernel(p_ref, lb_ref, og_ref, s0_ref, o_ref, s_ref):
    Ld = p_ref.shape[0] // SAMPLE_GB
    og = og_ref[...]
    for bb in range(SAMPLE_GB):
        rs = slice(bb * Ld, (bb + 1) * Ld)
        for h in range(C_HEADS):
            cs = lambda part: slice((part * C_HEADS + h) * LANES, (part * C_HEADS + h + 1) * LANES)
            o, S_new = _gla_chunk(p_ref[rs, cs(0)], p_ref[rs, cs(1)], p_ref[rs, cs(2)],
                                  lb_ref[:, h * LANES:(h + 1) * LANES], s0_ref[bb, h])
            s_ref[bb, h] = S_new
            o_ref[rs, h * LANES:(h + 1) * LANES] = _gated_out(o, og, p_ref[rs, cs(3)]).astype(BF16)


def _hgrn_sample(proj, Bd, Ld, row0, lb, og, s0):
    GB = SAMPLE_GB
    rows = GB * Ld
    rb0 = row0 // rows
    H = C_HEADS
    return pl.pallas_call(
        _hgrn_sample_kernel,
        grid=(Bd // GB,),
        in_specs=[pl.BlockSpec((rows, 4 * C_HK), lambda i: (rb0 + i, 0)),
                  pl.BlockSpec((1, C_HK), lambda i: (0, 0)),
                  pl.BlockSpec((1, LANES), lambda i: (0, 0)),
                  pl.BlockSpec((GB, H, C_DK, C_DV), lambda i: (i, 0, 0, 0))],
        out_specs=[pl.BlockSpec((rows, D_MODEL), lambda i: (i, 0)),
                   pl.BlockSpec((GB, H, C_DK, C_DV), lambda i: (i, 0, 0, 0))],
        out_shape=[jax.ShapeDtypeStruct((Bd * Ld, D_MODEL), BF16),
                   jax.ShapeDtypeStruct((Bd, H, C_DK, C_DV), F32)],
        compiler_params=_cparams(("arbitrary",)),
        name="hgrn_sample",
    )(proj, lb, og, s0)


def _out_proj_kernel(n_prompt_tiles, mp_ref, ms_ref, w_ref, x_ref, g_ref, wr_ref, br_ref,
                     x1_ref, h_ref, lg_ref):
    m = jnp.where(pl.program_id(0) < n_prompt_tiles, mp_ref[...], ms_ref[...])
    x1 = x_ref[...] + jnp.dot(m, w_ref[...], preferred_element_type=F32)
    x1_ref[...] = x1
    ms = jnp.mean(x1 * x1, axis=-1, keepdims=True)
    h = x1 * lax.rsqrt(ms + EPS) * g_ref[...]
    h_ref[...] = h
    lg_ref[...] = jnp.dot(h, wr_ref[...], preferred_element_type=F32,
                          precision=lax.Precision.HIGHEST) + br_ref[...]


def _out_proj(mix_p, mix_s, w, x, g, wr, br, tm):
    T, D = x.shape
    K = mix_p.shape[1]
    npt = mix_p.shape[0] // tm
    row = lambda n: pl.BlockSpec((tm, n), lambda i: (i, 0))
    full = lambda a, b: pl.BlockSpec((a, b), lambda i: (0, 0))
    return pl.pallas_call(
        functools.partial(_out_proj_kernel, npt),
        grid=(T // tm,),
        in_specs=[pl.BlockSpec((tm, K), lambda i: (jnp.minimum(i, npt - 1), 0)),
                  pl.BlockSpec((tm, K), lambda i: (jnp.maximum(i - npt, 0), 0)),
                  full(K, D), row(D), full(1, D), full(D, LOGIT_W), full(1, LOGIT_W)],
        out_specs=[row(D), row(D), row(LOGIT_W)],
        out_shape=[jax.ShapeDtypeStruct((T, D), F32), jax.ShapeDtypeStruct((T, D), F32),
                   jax.ShapeDtypeStruct((T, LOGIT_W), F32)],
        compiler_params=_cparams(("arbitrary",)),
        name="out_proj",
    )(mix_p, mix_s, w, x, g.reshape(1, D), wr, br)


def _route_kernel(lg_ref, slot_ref, gate_ref, cnt_ref, carry, total):
    phase = pl.program_id(0)
    step = pl.program_id(1)
    tm = lg_ref.shape[0]
    lg = lg_ref[...]
    lane = lax.broadcasted_iota(jnp.int32, lg.shape, 1)
    neg = -jnp.inf

    is_g = lane < N_GROUPS
    glog = jnp.where(is_g, lg, neg)
    gmax = jnp.max(glog, axis=-1, keepdims=True)
    g_idx = jnp.min(jnp.where(glog == gmax, lane, LOGIT_W), axis=-1, keepdims=True)
    g_p = 1.0 / jnp.sum(jnp.exp(glog - gmax), axis=-1, keepdims=True)

    e0 = N_GROUPS + g_idx * EXP_PER_GROUP
    in_grp = (lane >= e0) & (lane < e0 + EXP_PER_GROUP)
    elog = jnp.where(in_grp, lg, neg)
    emax = jnp.max(elog, axis=-1, keepdims=True)
    ee = jnp.exp(elog - emax)
    prob = ee / jnp.sum(ee, axis=-1, keepdims=True)
    prob = jnp.where(in_grp, prob, -1.0)
    p1 = jnp.max(prob, axis=-1, keepdims=True)
    i1 = jnp.min(jnp.where(prob == p1, lane, LOGIT_W), axis=-1, keepdims=True)
    prob2 = jnp.where(lane == i1, -1.0, prob)
    p2 = jnp.max(prob2, axis=-1, keepdims=True)
    i2 = jnp.min(jnp.where(prob2 == p2, lane, LOGIT_W), axis=-1, keepdims=True)
    psum = p1 + p2
    w1 = p1 / psum * g_p
    w2 = p2 / psum * g_p

    oh1 = lane == (i1 - N_GROUPS)
    oh2 = lane == (i2 - N_GROUPS)
    both = jnp.where(oh1 | oh2, 1.0, 0.0)

    @pl.when((phase == 0) & (step == 0))
    def _():
        total[...] = jnp.zeros_like(total)

    @pl.when(phase == 0)
    def _():
        total[...] += jnp.sum(both, axis=0, keepdims=True)

    @pl.when((phase == 1) & (step == 0))
    def _():
        carry[...] = jnp.zeros_like(carry)

    @pl.when(phase == 1)
    def _():
        cnt = total[...]
        padded = jnp.floor((cnt + (MOE_BLOCK - 1)) * (1.0 / MOE_BLOCK)) * MOE_BLOCK
        ri = lax.broadcasted_iota(jnp.int32, (LOGIT_W, LOGIT_W), 0)
        ci = lax.broadcasted_iota(jnp.int32, (LOGIT_W, LOGIT_W), 1)
        pcol = jnp.sum(jnp.where(ri == ci, jnp.broadcast_to(padded, (LOGIT_W, LOGIT_W)), 0.0),
                       axis=1, keepdims=True)
        pstart = jnp.sum(jnp.where(ri < ci, pcol, 0.0), axis=0, keepdims=True)
        tr = lax.broadcasted_iota(jnp.int32, (tm, tm), 0)
        tc = lax.broadcasted_iota(jnp.int32, (tm, tm), 1)
        tri = jnp.where(tc < tr, 1.0, 0.0).astype(BF16)
        before = jnp.dot(tri, both.astype(BF16), preferred_element_type=F32) + carry[...] + pstart
        s1 = jnp.sum(jnp.where(oh1, before, 0.0), axis=-1, keepdims=True)
        s2 = jnp.sum(jnp.where(oh2, before, 0.0), axis=-1, keepdims=True)
        carry[...] += jnp.sum(both, axis=0, keepdims=True)
        slot_ref[...] = jnp.where(lane == 0, s1, jnp.where(lane == 1, s2, 0.0)).astype(jnp.int32)
        gate_ref[...] = jnp.where(lane == 0, w1, jnp.where(lane == 1, w2, 0.0))
        cnt_ref[...] = cnt


def _route(logits):
    T = logits.shape[0]
    tm = ROUTE_TM
    return pl.pallas_call(
        _route_kernel,
        grid=(2, T // tm),
        in_specs=[pl.BlockSpec((tm, LOGIT_W), lambda p, i: (i, 0))],
        out_specs=[pl.BlockSpec((tm, LOGIT_W), lambda p, i: (i * p, 0)),
                   pl.BlockSpec((tm, LOGIT_W), lambda p, i: (i * p, 0)),
                   pl.BlockSpec((1, LOGIT_W), lambda p, i: (0, 0))],
        out_shape=[jax.ShapeDtypeStruct((T, LOGIT_W), jnp.int32),
                   jax.ShapeDtypeStruct((T, LOGIT_W), F32),
                   jax.ShapeDtypeStruct((1, LOGIT_W), F32)],
        scratch_shapes=[pltpu.VMEM((1, LOGIT_W), F32), pltpu.VMEM((1, LOGIT_W), F32)],
        compiler_params=_cparams(("arbitrary", "arbitrary")),
        name="route",
    )(logits)


DISPATCH_TM = 512


def _dispatch_kernel(slot_ref, h_ref, xs_in_ref, xs_ref, sem):
    del xs_in_ref
    i = pl.program_id(0)
    T = h_ref.shape[0]
    base = i * DISPATCH_TM

    def copy(r, k):
        t = base + r
        return pltpu.make_async_copy(h_ref.at[pl.ds(t, 1)],
                                     xs_ref.at[pl.ds(slot_ref[k * T + t], 1)], sem)

    def start(r, c):
        copy(r, 0).start()
        copy(r, 1).start()
        return c

    def wait(r, c):
        copy(r, 0).wait()
        copy(r, 1).wait()
        return c

    lax.fori_loop(0, DISPATCH_TM, start, 0)
    lax.fori_loop(0, DISPATCH_TM, wait, 0)


def _dispatch(slots, h, n_slots):
    T, D = h.shape
    xs0 = jnp.zeros((n_slots, D), F32)
    grid_spec = pltpu.PrefetchScalarGridSpec(
        num_scalar_prefetch=1,
        grid=(T // DISPATCH_TM,),
        in_specs=[pl.BlockSpec(memory_space=pl.ANY), pl.BlockSpec(memory_space=pl.ANY)],
        out_specs=pl.BlockSpec(memory_space=pl.ANY),
        scratch_shapes=[pltpu.SemaphoreType.DMA(())],
    )
    return pl.pallas_call(
        _dispatch_kernel,
        grid_spec=grid_spec,
        out_shape=jax.ShapeDtypeStruct((n_slots, D), F32),
        input_output_aliases={2: 0},
        compiler_params=_cparams(("arbitrary",)),
        name="moe_dispatch",
    )(slots, h, xs0)


def _ffn_kernel(be_ref, x_ref, w1_ref, w3_ref, w2_ref, y_ref):
    del be_ref
    x = x_ref[...].astype(BF16)
    a = jnp.dot(x, w1_ref[0].astype(BF16), preferred_element_type=F32)
    c = jnp.dot(x, w3_ref[0].astype(BF16), preferred_element_type=F32)
    hid = (a * _sigmoid(a)) * c
    y_ref[...] = jnp.dot(hid.astype(BF16), w2_ref[0].astype(BF16), preferred_element_type=F32)


def _ffn(blk_e, xs, w1, w3, w2):
    n_slots, D = xs.shape
    nblk = n_slots // MOE_BLOCK
    grid_spec = pltpu.PrefetchScalarGridSpec(
        num_scalar_prefetch=1,
        grid=(nblk,),
        in_specs=[pl.BlockSpec((MOE_BLOCK, D), lambda b, e: (b, 0)),
                  pl.BlockSpec((1, D, D_EXPERT), lambda b, e: (e[b], 0, 0)),
                  pl.BlockSpec((1, D, D_EXPERT), lambda b, e: (e[b], 0, 0)),
                  pl.BlockSpec((1, D_EXPERT, D), lambda b, e: (e[b], 0, 0))],
        out_specs=pl.BlockSpec((MOE_BLOCK, D), lambda b, e: (b, 0)),
    )
    return pl.pallas_call(
        _ffn_kernel,
        grid_spec=grid_spec,
        out_shape=jax.ShapeDtypeStruct((n_slots, D), F32),
        compiler_params=_cparams(("arbitrary",)),
        name="moe_ffn",
    )(blk_e, xs, w1, w3, w2)


COMBINE_TM = 256


def _combine_kernel(T, slot_ref, x_ref, gate_ref, ys_ref, o_ref, buf, sem):
    i = pl.program_id(0)
    base = i * COMBINE_TM

    def copy(r, k):
        return pltpu.make_async_copy(ys_ref.at[pl.ds(slot_ref[k * T + base + r], 1)],
                                     buf.at[k, pl.ds(r, 1)], sem)

    def start(r, c):
        copy(r, 0).start()
        copy(r, 1).start()
        return c

    def wait(r, c):
        copy(r, 0).wait()
        copy(r, 1).wait()
        return c

    lax.fori_loop(0, COMBINE_TM, start, 0)
    lax.fori_loop(0, COMBINE_TM, wait, 0)
    g = gate_ref[...]
    o_ref[...] = x_ref[...] + (buf[0] * g[:, 0:1] + buf[1] * g[:, 1:2])


def _combine(slots, x1, gates, ys):
    T, D = x1.shape
    tm = COMBINE_TM
    grid_spec = pltpu.PrefetchScalarGridSpec(
        num_scalar_prefetch=1,
        grid=(T // tm,),
        in_specs=[pl.BlockSpec((tm, D), lambda i, s: (i, 0)),
                  pl.BlockSpec((tm, LOGIT_W), lambda i, s: (i, 0)),
                  pl.BlockSpec(memory_space=pl.ANY)],
        out_specs=pl.BlockSpec((tm, D), lambda i, s: (i, 0)),
        scratch_shapes=[pltpu.VMEM((2, tm, D), F32), pltpu.SemaphoreType.DMA(())],
    )
    return pl.pallas_call(
        functools.partial(_combine_kernel, T),
        grid_spec=grid_spec,
        out_shape=jax.ShapeDtypeStruct((T, D), F32),
        compiler_params=_cparams(("arbitrary",)),
        name="moe_combine",
    )(slots, x1, gates, ys)


def _moe(x1, h, logits, w1, w3, w2):
    T = x1.shape[0]
    slot_l, gate_l, cnt = _route(logits)
    slots = jnp.concatenate([slot_l[:, 0], slot_l[:, 1]])
    n_blocks = -(-(2 * T) // MOE_BLOCK) + N_EXPERTS
    counts = cnt[0, :N_EXPERTS].astype(jnp.int32)
    pend = jnp.cumsum((counts + MOE_BLOCK - 1) // MOE_BLOCK * MOE_BLOCK)
    blk_e = jnp.minimum(jnp.searchsorted(pend, jnp.arange(n_blocks, dtype=jnp.int32) * MOE_BLOCK,
                                         side='right'), N_EXPERTS - 1).astype(jnp.int32)
    xs = _dispatch(slots, h, n_blocks * MOE_BLOCK)
    ys = _ffn(blk_e, xs, w1, w3, w2)
    return _combine(slots, x1, gate_l, ys)


def _router_weights(w_grp, b_grp, w_exp, b_exp):
    D = w_grp.shape[0]
    pad = LOGIT_W - N_GROUPS - N_EXPERTS
    wr = jnp.concatenate([w_grp, w_exp, jnp.zeros((D, pad), F32)], axis=1)
    br = jnp.concatenate([b_grp, b_exp, jnp.zeros((pad,), F32)]).reshape(1, LOGIT_W)
    return wr, br


def kernel(x_prompt, x_sample, cache_conv, cache_k, cache_v, state_hgrn, norm_mix, norm_ffn,
           ev_w_in, ev_conv, ev_q_norm, ev_k_norm, ev_sinks, ev_w_out,
           od_w_in, od_lb, od_o_norm, od_w_out,
           moe_w_grp, moe_b_grp, moe_w_exp, moe_b_exp, moe_w1, moe_w3, moe_w2):
    B, L, D = x_prompt.shape
    Bd, Ld, _ = x_sample.shape
    Tp = B * L
    T = Tp + Bd * Ld
    x = jnp.concatenate([x_prompt.reshape(Tp, D), x_sample.reshape(Bd * Ld, D)], axis=0)

    proj = _norm_matmul(x, norm_mix[0], ev_w_in[0].astype(BF16), 512)
    qg2 = jnp.tile(ev_q_norm[0], 2).reshape(1, LANES)
    kg2 = jnp.tile(ev_k_norm[0], 2).reshape(1, LANES)
    mix_p, k_p, v_p, conv_p = _even_prompt(proj, B, L, T, ev_conv[0], qg2, kg2, ev_sinks[0])
    mix_s, k_s, v_s, conv_s = _even_sample(
        proj, Bd, Ld, Tp, ev_conv[0], qg2, kg2, ev_sinks[0], cache_conv[0],
        cache_k[0].reshape(Bd, WINDOW, LANES), cache_v[0].reshape(Bd, WINDOW, LANES))
    wr, br = _router_weights(moe_w_grp[0], moe_b_grp[0], moe_w_exp[0], moe_b_exp[0])
    x1, h, logits = _out_proj(mix_p, mix_s, ev_w_out[0].astype(BF16), x, norm_ffn[0], wr, br, 512)
    x = _moe(x1, h, logits, moe_w1[0], moe_w3[0], moe_w2[0])

    lb_all = jnp.cumsum(jax.nn.softmax(od_lb.astype(F32), axis=0), axis=0)
    lb = (lb_all - lb_all[0])[1].reshape(1, C_HK)
    og = od_o_norm[0].reshape(1, C_DV)
    proj = _norm_matmul(x, norm_mix[1], od_w_in[0].astype(BF16), 256)
    o_p, s_p = _hgrn_prompt(proj, B, L, T, lb, og)
    o_s, s_s = _hgrn_sample(proj, Bd, Ld, Tp, lb, og, state_hgrn[0])
    wr, br = _router_weights(moe_w_grp[1], moe_b_grp[1], moe_w_exp[1], moe_b_exp[1])
    x1, h, logits = _out_proj(o_p, o_s, od_w_out[0].astype(BF16), x, norm_ffn[1], wr, br, 512)
    x = _moe(x1, h, logits, moe_w1[1], moe_w3[1], moe_w2[1])

    y_prompt = x[:Tp].reshape(B, L, D)
    y_sample = x[Tp:].reshape(Bd, Ld, D)
    return (y_prompt, y_sample,
            conv_p[None], k_p.reshape(1, B, WINDOW, N_KV, HEAD_DIM), v_p.reshape(1, B, WINDOW, N_KV, HEAD_DIM),
            s_p[None],
            conv_s[None], k_s.reshape(1, Bd, WINDOW, N_KV, HEAD_DIM), v_s.reshape(1, Bd, WINDOW, N_KV, HEAD_DIM),
            s_s[None])
```

```python
import functools

import numpy as np
import jax
import jax.numpy as jnp
from jax import lax
from jax.experimental import pallas as pl
from jax.experimental.pallas import tpu as pltpu

F32 = jnp.float32
BF16 = jnp.bfloat16

D_MODEL = 1024
PAST_LEN = 16384
D_CONV = 512
CONV_W = 3
HEAD_DIM = 64
N_Q = 8
N_KV = 2
GQA_G = N_Q // N_KV
WINDOW = 128
ROPE_THETA = 10000.0
D_IN_EVEN = 3 * D_CONV + (N_Q + 2 * N_KV) * HEAD_DIM
C_HEADS = 8
C_DK = 128
C_DV = 128
C_HK = C_HEADS * C_DK
N_GROUPS = 8
EXP_PER_GROUP = 8
N_EXPERTS = N_GROUPS * EXP_PER_GROUP
D_EXPERT = 512
MOE_BLOCK = 128
EPS = 1e-6

LANES = 128
SUBLANES = 8
VMEM_LIMIT = 56 * 1024 * 1024

GLA_CHUNK = 64
GLA_STEP = 256
SAMPLE_GB = 2
ROUTE_TM = 512
LOGIT_W = 128


def _cparams(sem):
    return pltpu.CompilerParams(dimension_semantics=sem, vmem_limit_bytes=VMEM_LIMIT)


def _sigmoid(x):
    return 1.0 / (1.0 + jnp.exp(-x))


def _norm_matmul_kernel(x_ref, g_ref, w_ref, o_ref):
    x = x_ref[...]
    ms = jnp.mean(x * x, axis=-1, keepdims=True)
    h = (x * lax.rsqrt(ms + EPS) * g_ref[...]).astype(BF16)
    o_ref[...] = jnp.dot(h, w_ref[...], preferred_element_type=F32)


def _norm_matmul(x, g, w, tm):
    T, D = x.shape
    N = w.shape[1]
    return pl.pallas_call(
        _norm_matmul_kernel,
        grid=(T // tm,),
        in_specs=[pl.BlockSpec((tm, D), lambda i: (i, 0)),
                  pl.BlockSpec((1, D), lambda i: (0, 0)),
                  pl.BlockSpec((D, N), lambda i: (0, 0))],
        out_specs=pl.BlockSpec((tm, N), lambda i: (i, 0)),
        out_shape=jax.ShapeDtypeStruct((T, N), F32),
        compiler_params=_cparams(("arbitrary",)),
        name="norm_matmul",
    )(x, g.reshape(1, D), w)


def _rope_tables(pos):
    inv = ROPE_THETA ** (-np.arange(0, HEAD_DIM, 2, dtype=np.float64) / HEAD_DIM)
    ang = np.asarray(pos, np.float64)[:, None] * inv[None, :]
    cos = np.cos(ang)
    sin = np.sin(ang)
    cos_t = np.concatenate([cos, cos, cos, cos], axis=1)
    sin_t = np.concatenate([-sin, sin, -sin, sin], axis=1)
    return jnp.asarray(cos_t, F32), jnp.asarray(sin_t, F32)


def _headnorm_rope(x, g2, cos, sin):
    lane = lax.broadcasted_iota(jnp.int32, x.shape, 1)
    lo = lane < HEAD_DIM
    x2 = x * x
    s_lo = jnp.sum(jnp.where(lo, x2, 0.0), axis=-1, keepdims=True)
    s_hi = jnp.sum(jnp.where(lo, 0.0, x2), axis=-1, keepdims=True)
    ms = jnp.where(lo, s_lo, s_hi) * (1.0 / HEAD_DIM)
    y = x * lax.rsqrt(ms + EPS) * g2
    first_half = (lane & (HEAD_DIM // 2)) == 0
    swapped = jnp.where(first_half, pltpu.roll(y, LANES - HEAD_DIM // 2, 1),
                        pltpu.roll(y, HEAD_DIM // 2, 1))
    return y * cos + swapped * sin


def _gated_conv(gb, u, c2, c1, cw):
    R = u.shape[0]
    row = lax.broadcasted_iota(jnp.int32, u.shape, 0)
    u1 = jnp.where(row == 0, c1, pltpu.roll(u, 1, 0))
    u2 = jnp.where(row == 0, c2, jnp.where(row == 1, c1, pltpu.roll(u, 2, 0)))
    del R
    return gb * (cw[0:1, :] * u2 + cw[1:2, :] * u1 + cw[2:3, :] * u)


def _band_attention(qs, kk, vv, valid, sinkv):
    s = lax.dot_general(qs.astype(BF16), kk.astype(BF16), (((1,), (1,)), ((), ())),
                        preferred_element_type=F32) * (HEAD_DIM ** -0.5)
    s = jnp.where(valid, s, -jnp.inf)
    m = jnp.maximum(jnp.max(s, axis=-1, keepdims=True), sinkv)
    e = jnp.exp(s - m)
    den = jnp.sum(e, axis=-1, keepdims=True) + jnp.exp(sinkv - m)
    p = e / den
    return jnp.dot(p.astype(BF16), vv.astype(BF16), preferred_element_type=F32)


def _sink_column(sinks_ref, hk, rows_per_head):
    R = GQA_G * rows_per_head
    row = lax.broadcasted_iota(jnp.int32, (R, 1), 0)
    col = jnp.full((R, 1), sinks_ref[hk * GQA_G + GQA_G - 1], F32)
    for j in range(GQA_G - 2, -1, -1):
        col = jnp.where(row < (j + 1) * rows_per_head, sinks_ref[hk * GQA_G + j], col)
    return col


def _even_prompt_kernel(sinks_ref, proj_ref, cos_ref, sin_ref, cw_ref, qg_ref, kg_ref,
                        mix_ref, kl_ref, vl_ref, cl_ref, kprev, vprev, ucar):
    blk = pl.program_id(1)
    W = WINDOW

    @pl.when(blk == 0)
    def _():
        kprev[...] = jnp.zeros_like(kprev)
        vprev[...] = jnp.zeros_like(vprev)
        ucar[...] = jnp.zeros_like(ucar)

    gb = proj_ref[:, 0:D_CONV]
    u = proj_ref[:, D_CONV:2 * D_CONV] * proj_ref[:, 2 * D_CONV:3 * D_CONV]
    car = ucar[...]
    a_out = _gated_conv(gb, u, car[0:1, :], car[1:2, :], cw_ref[...])
    ucar[0:2, :] = u[W - 2:W, :]
    mix_ref[:, 0:D_CONV] = a_out.astype(BF16)

    cos = cos_ref[...]
    sin = sin_ref[...]
    q0 = 3 * D_CONV
    k0 = q0 + N_Q * HEAD_DIM
    v0 = k0 + N_KV * HEAD_DIM
    k_r = _headnorm_rope(proj_ref[:, k0:k0 + LANES], kg_ref[...], cos, sin)
    v_r = proj_ref[:, v0:v0 + LANES]
    q_r = [_headnorm_rope(proj_ref[:, q0 + LANES * j:q0 + LANES * (j + 1)], qg_ref[...], cos, sin)
           for j in range(N_Q * HEAD_DIM // LANES)]
    k_p = kprev[...]
    v_p = vprev[...]

    R = GQA_G * W
    i = lax.broadcasted_iota(jnp.int32, (R, 2 * W), 0) & (W - 1)
    j = lax.broadcasted_iota(jnp.int32, (R, 2 * W), 1)
    diff = i + W - j
    valid = (diff >= 0) & (diff <= W) & ((blk > 0) | (j >= W))

    for hk in range(N_KV):
        ls = slice(hk * HEAD_DIM, (hk + 1) * HEAD_DIM)
        kk = jnp.concatenate([k_p[:, ls], k_r[:, ls]], axis=0)
        vv = jnp.concatenate([v_p[:, ls], v_r[:, ls]], axis=0)
        heads = []
        for g in range(GQA_G):
            h = hk * GQA_G + g
            tile = q_r[h // 2]
            heads.append(tile[:, (h % 2) * HEAD_DIM:(h % 2 + 1) * HEAD_DIM])
        qs = jnp.concatenate(heads, axis=0)
        o = _band_attention(qs, kk, vv, valid, _sink_column(sinks_ref, hk, W))
        for g in range(GQA_G):
            h = hk * GQA_G + g
            mix_ref[:, D_CONV + h * HEAD_DIM:D_CONV + (h + 1) * HEAD_DIM] = \
                o[g * W:(g + 1) * W, :].astype(BF16)

    kprev[...] = k_r
    vprev[...] = v_r

    @pl.when(blk == pl.num_programs(1) - 1)
    def _():
        kl_ref[0] = k_r
        vl_ref[0] = v_r
        cl_ref[0] = u[W - 2:W, :]


def _even_prompt(proj, B, L, T, cw, qg2, kg2, sinks):
    nb = L // WINDOW
    cos, sin = _rope_tables(np.arange(L))
    full = lambda shape: pl.BlockSpec(shape, lambda b, i, *_: tuple(0 for _ in shape))
    grid_spec = pltpu.PrefetchScalarGridSpec(
        num_scalar_prefetch=1,
        grid=(B, nb),
        in_specs=[pl.BlockSpec((WINDOW, D_IN_EVEN), lambda b, i, s: (b * nb + i, 0)),
                  pl.BlockSpec((WINDOW, LANES), lambda b, i, s: (i, 0)),
                  pl.BlockSpec((WINDOW, LANES), lambda b, i, s: (i, 0)),
                  full((CONV_W, D_CONV)), full((1, LANES)), full((1, LANES))],
        out_specs=[pl.BlockSpec((WINDOW, D_MODEL), lambda b, i, s: (b * nb + i, 0)),
                   pl.BlockSpec((1, WINDOW, LANES), lambda b, i, s: (b, 0, 0)),
                   pl.BlockSpec((1, WINDOW, LANES), lambda b, i, s: (b, 0, 0)),
                   pl.BlockSpec((1, CONV_W - 1, D_CONV), lambda b, i, s: (b, 0, 0))],
        scratch_shapes=[pltpu.VMEM((WINDOW, LANES), F32), pltpu.VMEM((WINDOW, LANES), F32),
                        pltpu.VMEM((SUBLANES, D_CONV), F32)],
    )
    return pl.pallas_call(
        _even_prompt_kernel,
        grid_spec=grid_spec,
        out_shape=[jax.ShapeDtypeStruct((B * L, D_MODEL), BF16),
                   jax.ShapeDtypeStruct((B, WINDOW, LANES), F32),
                   jax.ShapeDtypeStruct((B, WINDOW, LANES), F32),
                   jax.ShapeDtypeStruct((B, CONV_W - 1, D_CONV), F32)],
        compiler_params=_cparams(("arbitrary", "arbitrary")),
        name="even_prompt",
    )(sinks, proj, cos, sin, cw, qg2, kg2)


def _even_sample_kernel(sinks_ref, proj_ref, cos_ref, sin_ref, cw_ref, qg_ref, kg_ref,
                        cc_ref, ck_ref, cv_ref, mix_ref, ko_ref, vo_ref, co_ref):
    W = WINDOW
    Ld = cos_ref.shape[0]
    cos = cos_ref[...]
    sin = sin_ref[...]
    q0 = 3 * D_CONV
    k0 = q0 + N_Q * HEAD_DIM
    v0 = k0 + N_KV * HEAD_DIM
    R = GQA_G * Ld
    i = lax.broadcasted_iota(jnp.int32, (R, 2 * W), 0) % Ld
    j = lax.broadcasted_iota(jnp.int32, (R, 2 * W), 1)
    diff = i + W - j
    valid = (diff >= 0) & (diff <= W)
    zpad = jnp.zeros((W - Ld, HEAD_DIM), F32)

    for bb in range(SAMPLE_GB):
        rs = slice(bb * Ld, (bb + 1) * Ld)
        gb = proj_ref[rs, 0:D_CONV]
        u = proj_ref[rs, D_CONV:2 * D_CONV] * proj_ref[rs, 2 * D_CONV:3 * D_CONV]
        car = cc_ref[bb]
        a_out = _gated_conv(gb, u, car[0:1, :], car[1:2, :], cw_ref[...])
        co_ref[bb] = u[Ld - 2:Ld, :]
        mix_ref[rs, 0:D_CONV] = a_out.astype(BF16)

        k_r = _headnorm_rope(proj_ref[rs, k0:k0 + LANES], kg_ref[...], cos, sin)
        v_r = proj_ref[rs, v0:v0 + LANES]
        q_r = [_headnorm_rope(proj_ref[rs, q0 + LANES * t:q0 + LANES * (t + 1)], qg_ref[...], cos, sin)
               for t in range(N_Q * HEAD_DIM // LANES)]
        k_c = ck_ref[bb]
        v_c = cv_ref[bb]
        ko_ref[bb, 0:W - Ld, :] = k_c[Ld:W, :]
        ko_ref[bb, W - Ld:W, :] = k_r
        vo_ref[bb, 0:W - Ld, :] = v_c[Ld:W, :]
        vo_ref[bb, W - Ld:W, :] = v_r

        for hk in range(N_KV):
            ls = slice(hk * HEAD_DIM, (hk + 1) * HEAD_DIM)
            kk = jnp.concatenate([k_c[:, ls], k_r[:, ls], zpad], axis=0)
            vv = jnp.concatenate([v_c[:, ls], v_r[:, ls], zpad], axis=0)
            heads = []
            for g in range(GQA_G):
                h = hk * GQA_G + g
                tile = q_r[h // 2]
                heads.append(tile[:, (h % 2) * HEAD_DIM:(h % 2 + 1) * HEAD_DIM])
            qs = jnp.concatenate(heads, axis=0)
            o = _band_attention(qs, kk, vv, valid, _sink_column(sinks_ref, hk, Ld))
            for g in range(GQA_G):
                h = hk * GQA_G + g
                mix_ref[rs, D_CONV + h * HEAD_DIM:D_CONV + (h + 1) * HEAD_DIM] = \
                    o[g * Ld:(g + 1) * Ld, :].astype(BF16)


def _even_sample(proj, Bd, Ld, row0, cw, qg2, kg2, sinks, cache_conv, cache_k, cache_v):
    GB = SAMPLE_GB
    rows = GB * Ld
    rb0 = row0 // rows
    cos, sin = _rope_tables(PAST_LEN + np.arange(Ld))
    full = lambda shape: pl.BlockSpec(shape, lambda i, *_: tuple(0 for _ in shape))
    grid_spec = pltpu.PrefetchScalarGridSpec(
        num_scalar_prefetch=1,
        grid=(Bd // GB,),
        in_specs=[pl.BlockSpec((rows, D_IN_EVEN), lambda i, s: (rb0 + i, 0)),
                  full((Ld, LANES)), full((Ld, LANES)),
                  full((CONV_W, D_CONV)), full((1, LANES)), full((1, LANES)),
                  pl.BlockSpec((GB, CONV_W - 1, D_CONV), lambda i, s: (i, 0, 0)),
                  pl.BlockSpec((GB, WINDOW, LANES), lambda i, s: (i, 0, 0)),
                  pl.BlockSpec((GB, WINDOW, LANES), lambda i, s: (i, 0, 0))],
        out_specs=[pl.BlockSpec((rows, D_MODEL), lambda i, s: (i, 0)),
                   pl.BlockSpec((GB, WINDOW, LANES), lambda i, s: (i, 0, 0)),
                   pl.BlockSpec((GB, WINDOW, LANES), lambda i, s: (i, 0, 0)),
                   pl.BlockSpec((GB, CONV_W - 1, D_CONV), lambda i, s: (i, 0, 0))],
    )
    return pl.pallas_call(
        _even_sample_kernel,
        grid_spec=grid_spec,
        out_shape=[jax.ShapeDtypeStruct((Bd * Ld, D_MODEL), BF16),
                   jax.ShapeDtypeStruct((Bd, WINDOW, LANES), F32),
                   jax.ShapeDtypeStruct((Bd, WINDOW, LANES), F32),
                   jax.ShapeDtypeStruct((Bd, CONV_W - 1, D_CONV), F32)],
        compiler_params=_cparams(("arbitrary",)),
        name="even_sample",
    )(sinks, proj, cos, sin, cw, qg2, kg2, cache_conv, cache_k, cache_v)


def _cumsum_rows(x):
    C = x.shape[0]
    row = lax.broadcasted_iota(jnp.int32, x.shape, 0)
    s = 1
    while s < C:
        x = x + jnp.where(row >= s, pltpu.roll(x, s, 0), 0.0)
        s *= 2
    return x


def _gla_chunk(qz, fz, v, lb, S):
    C = qz.shape[0]
    q = qz * _sigmoid(qz)
    f = lb + (1.0 - lb) * _sigmoid(fz)
    k = 1.0 - f
    b = _cumsum_rows(jnp.log(f))
    vb = v.astype(BF16)

    inter = jnp.dot((q * jnp.exp(b)).astype(BF16), S.astype(BF16), preferred_element_type=F32)

    row = lax.broadcasted_iota(jnp.int32, (C, C_DK), 0)
    si = lax.broadcasted_iota(jnp.int32, (C, C), 0)
    ti = lax.broadcasted_iota(jnp.int32, (C, C), 1)
    st = jnp.zeros((C, C), F32)
    m = C // 2
    while m >= SUBLANES:
        n = C // (2 * m)
        b3 = b.reshape(n, 2 * m, C_DK)
        rho = jnp.broadcast_to(b3[:, m - 1:m, :], (n, 2 * m, C_DK)).reshape(C, C_DK)
        upper = (row & (2 * m - 1)) >= m
        qt = jnp.where(upper, q * jnp.exp(jnp.minimum(b - rho, 0.0)), 0.0)
        kt = jnp.where(upper, 0.0, k * jnp.exp(jnp.minimum(rho - b, 0.0)))
        part = lax.dot_general(kt.astype(BF16), qt.astype(BF16), (((1,), (1,)), ((), ())),
                               preferred_element_type=F32)
        sh = (2 * m).bit_length() - 1
        same = (si >> sh) == (ti >> sh)
        st = st + jnp.where(same, part, 0.0)
        m //= 2
    lane_t = lax.broadcasted_iota(jnp.int32, (SUBLANES, C), 1)
    srow = lax.broadcasted_iota(jnp.int32, (SUBLANES, C_DK), 0)
    blocks = []
    for i8 in range(C // SUBLANES):
        r0 = i8 * SUBLANES
        kb = k[r0:r0 + SUBLANES, :]
        bb = b[r0:r0 + SUBLANES, :]
        acc = jnp.zeros((SUBLANES, C), F32)
        for t in range(SUBLANES):
            dec = jnp.exp(jnp.minimum(b[r0 + t:r0 + t + 1, :] - bb, 0.0))
            p = jnp.where(srow <= t, q[r0 + t:r0 + t + 1, :] * kb * dec, 0.0)
            col = jnp.sum(p, axis=-1, keepdims=True)
            acc = jnp.where(lane_t == r0 + t, col, acc)
        blocks.append(acc)
    st = st + (jnp.concatenate(blocks, axis=0) if len(blocks) > 1 else blocks[0])

    intra = lax.dot_general(st.astype(BF16), vb, (((0,), (0,)), ((), ())),
                            preferred_element_type=F32)

    b_last = b[C - 1:C, :]
    eye = (lax.broadcasted_iota(jnp.int32, (C_DK, C_DK), 0)
           == lax.broadcasted_iota(jnp.int32, (C_DK, C_DK), 1))
    dcol = jnp.sum(jnp.where(eye, jnp.broadcast_to(jnp.exp(b_last), (C_DK, C_DK)), 0.0),
                   axis=-1, keepdims=True)
    kd = k * jnp.exp(b_last - b)
    S_new = dcol * S + lax.dot_general(kd.astype(BF16), vb, (((0,), (0,)), ((), ())),
                                       preferred_element_type=F32)
    return inter + intra, S_new


def _gated_out(o, og, gz):
    ms = jnp.mean(o * o, axis=-1, keepdims=True)
    return (o * lax.rsqrt(ms + EPS) * og) * (gz * _sigmoid(gz))


def _hgrn_prompt_kernel(q_ref, f_ref, i_ref, g_ref, lb_ref, og_ref, o_ref, s_ref, S):
    c = pl.program_id(2)

    @pl.when(c == 0)
    def _():
        S[...] = jnp.zeros_like(S)

    lb = lb_ref[...]
    og = og_ref[...]

    Sv = S[...]
    for n in range(GLA_STEP // GLA_CHUNK):
        rs = slice(n * GLA_CHUNK, (n + 1) * GLA_CHUNK)
        o, Sv = _gla_chunk(q_ref[rs, :], f_ref[rs, :], i_ref[rs, :], lb, Sv)
        o_ref[rs, :] = _gated_out(o, og, g_ref[rs, :]).astype(BF16)
    S[...] = Sv

    @pl.when(c == pl.num_programs(2) - 1)
    def _():
        s_ref[0, 0] = S[...]


def _hgrn_prompt(proj, B, L, T, lb, og):
    ns = L // GLA_STEP
    H = C_HEADS
    col = lambda off: pl.BlockSpec((GLA_STEP, LANES), lambda b, h, c: (b * ns + c, off + h))
    return pl.pallas_call(
        _hgrn_prompt_kernel,
        grid=(B, H, ns),
        in_specs=[col(0), col(H), col(2 * H), col(3 * H),
                  pl.BlockSpec((1, LANES), lambda b, h, c: (0, h)),
                  pl.BlockSpec((1, LANES), lambda b, h, c: (0, 0))],
        out_specs=[pl.BlockSpec((GLA_STEP, LANES), lambda b, h, c: (b * ns + c, h)),
                   pl.BlockSpec((1, 1, C_DK, C_DV), lambda b, h, c: (b, h, 0, 0))],
        out_shape=[jax.ShapeDtypeStruct((B * L, D_MODEL), BF16),
                   jax.ShapeDtypeStruct((B, H, C_DK, C_DV), F32)],
        scratch_shapes=[pltpu.VMEM((C_DK, C_DV), F32)],
        compiler_params=_cparams(("arbitrary", "arbitrary", "arbitrary")),
        name="hgrn_prompt",
    )(proj, proj, proj, proj, lb, og)


def _hgrn_sample_kernel(p_ref, lb_ref, og_ref, s0_ref, o_ref, s_ref):
    Ld = p_ref.shape[0] // SAMPLE_GB
    og = og_ref[...]
    for bb in range(SAMPLE_GB):
        rs = slice(bb * Ld, (bb + 1) * Ld)
        for h in range(C_HEADS):
            cs = lambda part: slice((part * C_HEADS + h) * LANES, (part * C_HEADS + h + 1) * LANES)
            o, S_new = _gla_chunk(p_ref[rs, cs(0)], p_ref[rs, cs(1)], p_ref[rs, cs(2)],
                                  lb_ref[:, h * LANES:(h + 1) * LANES], s0_ref[bb, h])
            s_ref[bb, h] = S_new
            o_ref[rs, h * LANES:(h + 1) * LANES] = _gated_out(o, og, p_ref[rs, cs(3)]).astype(BF16)


def _hgrn_sample(proj, Bd, Ld, row0, lb, og, s0):
    GB = SAMPLE_GB
    rows = GB * Ld
    rb0 = row0 // rows
    H = C_HEADS
    return pl.pallas_call(
        _hgrn_sample_kernel,
        grid=(Bd // GB,),
        in_specs=[pl.BlockSpec((rows, 4 * C_HK), lambda i: (rb0 + i, 0)),
                  pl.BlockSpec((1, C_HK), lambda i: (0, 0)),
                  pl.BlockSpec((1, LANES), lambda i: (0, 0)),
                  pl.BlockSpec((GB, H, C_DK, C_DV), lambda i: (i, 0, 0, 0))],
        out_specs=[pl.BlockSpec((rows, D_MODEL), lambda i: (i, 0)),
                   pl.BlockSpec((GB, H, C_DK, C_DV), lambda i: (i, 0, 0, 0))],
        out_shape=[jax.ShapeDtypeStruct((Bd * Ld, D_MODEL), BF16),
                   jax.ShapeDtypeStruct((Bd, H, C_DK, C_DV), F32)],
        compiler_params=_cparams(("arbitrary",)),
        name="hgrn_sample",
    )(proj, lb, og, s0)


def _out_proj_kernel(n_prompt_tiles, mp_ref, ms_ref, w_ref, x_ref, g_ref, wr_ref, br_ref,
                     x1_ref, h_ref, lg_ref):
    m = jnp.where(pl.program_id(0) < n_prompt_tiles, mp_ref[...], ms_ref[...])
    x1 = x_ref[...] + jnp.dot(m, w_ref[...], preferred_element_type=F32)
    x1_ref[...] = x1
    ms = jnp.mean(x1 * x1, axis=-1, keepdims=True)
    h = x1 * lax.rsqrt(ms + EPS) * g_ref[...]
    h_ref[...] = h
    lg_ref[...] = jnp.dot(h, wr_ref[...], preferred_element_type=F32,
                          precision=lax.Precision.HIGHEST) + br_ref[...]


def _out_proj(mix_p, mix_s, w, x, g, wr, br, tm):
    T, D = x.shape
    K = mix_p.shape[1]
    npt = mix_p.shape[0] // tm
    row = lambda n: pl.BlockSpec((tm, n), lambda i: (i, 0))
    full = lambda a, b: pl.BlockSpec((a, b), lambda i: (0, 0))
    return pl.pallas_call(
        functools.partial(_out_proj_kernel, npt),
        grid=(T // tm,),
        in_specs=[pl.BlockSpec((tm, K), lambda i: (jnp.minimum(i, npt - 1), 0)),
                  pl.BlockSpec((tm, K), lambda i: (jnp.maximum(i - npt, 0), 0)),
                  full(K, D), row(D), full(1, D), full(D, LOGIT_W), full(1, LOGIT_W)],
        out_specs=[row(D), row(D), row(LOGIT_W)],
        out_shape=[jax.ShapeDtypeStruct((T, D), F32), jax.ShapeDtypeStruct((T, D), F32),
                   jax.ShapeDtypeStruct((T, LOGIT_W), F32)],
        compiler_params=_cparams(("arbitrary",)),
        name="out_proj",
    )(mix_p, mix_s, w, x, g.reshape(1, D), wr, br)


def _route_kernel(lg_ref, slot_ref, gate_ref, cnt_ref, carry, total):
    phase = pl.program_id(0)
    step = pl.program_id(1)
    tm = lg_ref.shape[0]
    lg = lg_ref[...]
    lane = lax.broadcasted_iota(jnp.int32, lg.shape, 1)
    neg = -jnp.inf

    is_g = lane < N_GROUPS
    glog = jnp.where(is_g, lg, neg)
    gmax = jnp.max(glog, axis=-1, keepdims=True)
    g_idx = jnp.min(jnp.where(glog == gmax, lane, LOGIT_W), axis=-1, keepdims=True)
    g_p = 1.0 / jnp.sum(jnp.exp(glog - gmax), axis=-1, keepdims=True)

    e0 = N_GROUPS + g_idx * EXP_PER_GROUP
    in_grp = (lane >= e0) & (lane < e0 + EXP_PER_GROUP)
    elog = jnp.where(in_grp, lg, neg)
    emax = jnp.max(elog, axis=-1, keepdims=True)
    ee = jnp.exp(elog - emax)
    prob = ee / jnp.sum(ee, axis=-1, keepdims=True)
    prob = jnp.where(in_grp, prob, -1.0)
    p1 = jnp.max(prob, axis=-1, keepdims=True)
    i1 = jnp.min(jnp.where(prob == p1, lane, LOGIT_W), axis=-1, keepdims=True)
    prob2 = jnp.where(lane == i1, -1.0, prob)
    p2 = jnp.max(prob2, axis=-1, keepdims=True)
    i2 = jnp.min(jnp.where(prob2 == p2, lane, LOGIT_W), axis=-1, keepdims=True)
    psum = p1 + p2
    w1 = p1 / psum * g_p
    w2 = p2 / psum * g_p

    oh1 = lane == (i1 - N_GROUPS)
    oh2 = lane == (i2 - N_GROUPS)
    both = jnp.where(oh1 | oh2, 1.0, 0.0)

    @pl.when((phase == 0) & (step == 0))
    def _():
        total[...] = jnp.zeros_like(total)

    @pl.when(phase == 0)
    def _():
        total[...] += jnp.sum(both, axis=0, keepdims=True)

    @pl.when((phase == 1) & (step == 0))
    def _():
        carry[...] = jnp.zeros_like(carry)

    @pl.when(phase == 1)
    def _():
        cnt = total[...]
        padded = jnp.floor((cnt + (MOE_BLOCK - 1)) * (1.0 / MOE_BLOCK)) * MOE_BLOCK
        ri = lax.broadcasted_iota(jnp.int32, (LOGIT_W, LOGIT_W), 0)
        ci = lax.broadcasted_iota(jnp.int32, (LOGIT_W, LOGIT_W), 1)
        pcol = jnp.sum(jnp.where(ri == ci, jnp.broadcast_to(padded, (LOGIT_W, LOGIT_W)), 0.0),
                       axis=1, keepdims=True)
        pstart = jnp.sum(jnp.where(ri < ci, pcol, 0.0), axis=0, keepdims=True)
        tr = lax.broadcasted_iota(jnp.int32, (tm, tm), 0)
        tc = lax.broadcasted_iota(jnp.int32, (tm, tm), 1)
        tri = jnp.where(tc < tr, 1.0, 0.0).astype(BF16)
        before = jnp.dot(tri, both.astype(BF16), preferred_element_type=F32) + carry[...] + pstart
        s1 = jnp.sum(jnp.where(oh1, before, 0.0), axis=-1, keepdims=True)
        s2 = jnp.sum(jnp.where(oh2, before, 0.0), axis=-1, keepdims=True)
        carry[...] += jnp.sum(both, axis=0, keepdims=True)
        slot_ref[...] = jnp.where(lane == 0, s1, jnp.where(lane == 1, s2, 0.0)).astype(jnp.int32)
        gate_ref[...] = jnp.where(lane == 0, w1, jnp.where(lane == 1, w2, 0.0))
        cnt_ref[...] = cnt


def _route(logits):
    T = logits.shape[0]
    tm = ROUTE_TM
    return pl.pallas_call(
        _route_kernel,
        grid=(2, T // tm),
        in_specs=[pl.BlockSpec((tm, LOGIT_W), lambda p, i: (i, 0))],
        out_specs=[pl.BlockSpec((tm, LOGIT_W), lambda p, i: (i * p, 0)),
                   pl.BlockSpec((tm, LOGIT_W), lambda p, i: (i * p, 0)),
                   pl.BlockSpec((1, LOGIT_W), lambda p, i: (0, 0))],
        out_shape=[jax.ShapeDtypeStruct((T, LOGIT_W), jnp.int32),
                   jax.ShapeDtypeStruct((T, LOGIT_W), F32),
                   jax.ShapeDtypeStruct((1, LOGIT_W), F32)],
        scratch_shapes=[pltpu.VMEM((1, LOGIT_W), F32), pltpu.VMEM((1, LOGIT_W), F32)],
        compiler_params=_cparams(("arbitrary", "arbitrary")),
        name="route",
    )(logits)


DISPATCH_TM = 512


def _dispatch_kernel(T, slot_ref, h_ref, xs_in_ref, xs_ref, sem):
    del xs_in_ref
    base = pl.program_id(0) * DISPATCH_TM

    def copy(r, k):
        return pltpu.make_async_copy(h_ref.at[pl.ds(r, 1)],
                                     xs_ref.at[pl.ds(slot_ref[k * T + base + r], 1)], sem)

    def start(r, c):
        copy(r, 0).start()
        copy(r, 1).start()
        return c

    def wait(r, c):
        copy(r, 0).wait()
        copy(r, 1).wait()
        return c

    lax.fori_loop(0, DISPATCH_TM, start, 0, unroll=8)
    lax.fori_loop(0, DISPATCH_TM, wait, 0, unroll=8)


def _dispatch(slots, h, n_slots):
    T, D = h.shape
    xs0 = jnp.zeros((n_slots, D), F32)
    grid_spec = pltpu.PrefetchScalarGridSpec(
        num_scalar_prefetch=1,
        grid=(T // DISPATCH_TM,),
        in_specs=[pl.BlockSpec((DISPATCH_TM, D), lambda i, s: (i, 0)),
                  pl.BlockSpec(memory_space=pl.ANY)],
        out_specs=pl.BlockSpec(memory_space=pl.ANY),
        scratch_shapes=[pltpu.SemaphoreType.DMA(())],
    )
    return pl.pallas_call(
        functools.partial(_dispatch_kernel, T),
        grid_spec=grid_spec,
        out_shape=jax.ShapeDtypeStruct((n_slots, D), F32),
        input_output_aliases={2: 0},
        compiler_params=_cparams(("arbitrary",)),
        name="moe_dispatch",
    )(slots, h, xs0)


def _ffn_kernel(be_ref, x_ref, w1_ref, w3_ref, w2_ref, y_ref):
    del be_ref
    x = x_ref[...].astype(BF16)
    a = jnp.dot(x, w1_ref[0, 0].astype(BF16), preferred_element_type=F32)
    c = jnp.dot(x, w3_ref[0, 0].astype(BF16), preferred_element_type=F32)
    hid = (a * _sigmoid(a)) * c
    y_ref[...] = jnp.dot(hid.astype(BF16), w2_ref[0, 0].astype(BF16), preferred_element_type=F32)


def _ffn(blk_e, xs, w1, w3, w2, layer):
    n_slots, D = xs.shape
    nblk = n_slots // MOE_BLOCK
    grid_spec = pltpu.PrefetchScalarGridSpec(
        num_scalar_prefetch=1,
        grid=(nblk,),
        in_specs=[pl.BlockSpec((MOE_BLOCK, D), lambda b, e: (b, 0)),
                  pl.BlockSpec((1, 1, D, D_EXPERT), lambda b, e: (layer, e[b], 0, 0)),
                  pl.BlockSpec((1, 1, D, D_EXPERT), lambda b, e: (layer, e[b], 0, 0)),
                  pl.BlockSpec((1, 1, D_EXPERT, D), lambda b, e: (layer, e[b], 0, 0))],
        out_specs=pl.BlockSpec((MOE_BLOCK, D), lambda b, e: (b, 0)),
    )
    return pl.pallas_call(
        _ffn_kernel,
        grid_spec=grid_spec,
        out_shape=jax.ShapeDtypeStruct((n_slots, D), F32),
        compiler_params=_cparams(("arbitrary",)),
        name="moe_ffn",
    )(blk_e, xs, w1, w3, w2)


COMBINE_TM = 256


def _combine_kernel(T, slot_ref, x_ref, gate_ref, ys_ref, o_ref, buf, sem):
    i = pl.program_id(0)
    base = i * COMBINE_TM

    def copy(r, k):
        return pltpu.make_async_copy(ys_ref.at[pl.ds(slot_ref[k * T + base + r], 1)],
                                     buf.at[k, pl.ds(r, 1)], sem)

    def start(r, c):
        copy(r, 0).start()
        copy(r, 1).start()
        return c

    def wait(r, c):
        copy(r, 0).wait()
        copy(r, 1).wait()
        return c

    lax.fori_loop(0, COMBINE_TM, start, 0, unroll=8)
    lax.fori_loop(0, COMBINE_TM, wait, 0, unroll=8)
    g = gate_ref[...]
    o_ref[...] = x_ref[...] + (buf[0] * g[:, 0:1] + buf[1] * g[:, 1:2])


def _combine(slots, x1, gates, ys):
    T, D = x1.shape
    tm = COMBINE_TM
    grid_spec = pltpu.PrefetchScalarGridSpec(
        num_scalar_prefetch=1,
        grid=(T // tm,),
        in_specs=[pl.BlockSpec((tm, D), lambda i, s: (i, 0)),
                  pl.BlockSpec((tm, LOGIT_W), lambda i, s: (i, 0)),
                  pl.BlockSpec(memory_space=pl.ANY)],
        out_specs=pl.BlockSpec((tm, D), lambda i, s: (i, 0)),
        scratch_shapes=[pltpu.VMEM((2, tm, D), F32), pltpu.SemaphoreType.DMA(())],
    )
    return pl.pallas_call(
        functools.partial(_combine_kernel, T),
        grid_spec=grid_spec,
        out_shape=jax.ShapeDtypeStruct((T, D), F32),
        compiler_params=_cparams(("arbitrary",)),
        name="moe_combine",
    )(slots, x1, gates, ys)


def _moe(x1, h, logits, w1, w3, w2, layer):
    T = x1.shape[0]
    slot_l, gate_l, cnt = _route(logits)
    slots = jnp.concatenate([slot_l[:, 0], slot_l[:, 1]])
    n_blocks = -(-(2 * T) // MOE_BLOCK) + N_EXPERTS
    counts = cnt[0, :N_EXPERTS].astype(jnp.int32)
    pend = jnp.cumsum((counts + MOE_BLOCK - 1) // MOE_BLOCK * MOE_BLOCK)
    starts = jnp.arange(n_blocks, dtype=jnp.int32) * MOE_BLOCK
    blk_e = jnp.minimum(jnp.sum((pend[None, :] <= starts[:, None]).astype(jnp.int32), axis=1),
                        N_EXPERTS - 1)
    xs = _dispatch(slots, h, n_blocks * MOE_BLOCK)
    ys = _ffn(blk_e, xs, w1, w3, w2, layer)
    return _combine(slots, x1, gate_l, ys)


def _router_weights(w_grp, b_grp, w_exp, b_exp):
    D = w_grp.shape[0]
    pad = LOGIT_W - N_GROUPS - N_EXPERTS
    wr = jnp.concatenate([w_grp, w_exp, jnp.zeros((D, pad), F32)], axis=1)
    br = jnp.concatenate([b_grp, b_exp, jnp.zeros((pad,), F32)]).reshape(1, LOGIT_W)
    return wr, br


def kernel(x_prompt, x_sample, cache_conv, cache_k, cache_v, state_hgrn, norm_mix, norm_ffn,
           ev_w_in, ev_conv, ev_q_norm, ev_k_norm, ev_sinks, ev_w_out,
           od_w_in, od_lb, od_o_norm, od_w_out,
           moe_w_grp, moe_b_grp, moe_w_exp, moe_b_exp, moe_w1, moe_w3, moe_w2):
    B, L, D = x_prompt.shape
    Bd, Ld, _ = x_sample.shape
    Tp = B * L
    T = Tp + Bd * Ld
    x = jnp.concatenate([x_prompt.reshape(Tp, D), x_sample.reshape(Bd * Ld, D)], axis=0)

    proj = _norm_matmul(x, norm_mix[0], ev_w_in[0].astype(BF16), 512)
    qg2 = jnp.tile(ev_q_norm[0], 2).reshape(1, LANES)
    kg2 = jnp.tile(ev_k_norm[0], 2).reshape(1, LANES)
    mix_p, k_p, v_p, conv_p = _even_prompt(proj, B, L, T, ev_conv[0], qg2, kg2, ev_sinks[0])
    mix_s, k_s, v_s, conv_s = _even_sample(
        proj, Bd, Ld, Tp, ev_conv[0], qg2, kg2, ev_sinks[0], cache_conv[0],
        cache_k[0].reshape(Bd, WINDOW, LANES), cache_v[0].reshape(Bd, WINDOW, LANES))
    wr, br = _router_weights(moe_w_grp[0], moe_b_grp[0], moe_w_exp[0], moe_b_exp[0])
    x1, h, logits = _out_proj(mix_p, mix_s, ev_w_out[0].astype(BF16), x, norm_ffn[0], wr, br, 512)
    x = _moe(x1, h, logits, moe_w1, moe_w3, moe_w2, 0)

    lb_all = jnp.cumsum(jax.nn.softmax(od_lb.astype(F32), axis=0), axis=0)
    lb = (lb_all - lb_all[0])[1].reshape(1, C_HK)
    og = od_o_norm[0].reshape(1, C_DV)
    proj = _norm_matmul(x, norm_mix[1], od_w_in[0].astype(BF16), 256)
    o_p, s_p = _hgrn_prompt(proj, B, L, T, lb, og)
    o_s, s_s = _hgrn_sample(proj, Bd, Ld, Tp, lb, og, state_hgrn[0])
    wr, br = _router_weights(moe_w_grp[1], moe_b_grp[1], moe_w_exp[1], moe_b_exp[1])
    x1, h, logits = _out_proj(o_p, o_s, od_w_out[0].astype(BF16), x, norm_ffn[1], wr, br, 512)
    x = _moe(x1, h, logits, moe_w1, moe_w3, moe_w2, 1)

    y_prompt = x[:Tp].reshape(B, L, D)
    y_sample = x[Tp:].reshape(Bd, Ld, D)
    return (y_prompt, y_sample,
            conv_p[None], k_p.reshape(1, B, WINDOW, N_KV, HEAD_DIM), v_p.reshape(1, B, WINDOW, N_KV, HEAD_DIM),
            s_p[None],
            conv_s[None], k_s.reshape(1, Bd, WINDOW, N_KV, HEAD_DIM), v_s.reshape(1, Bd, WINDOW, N_KV, HEAD_DIM),
            s_s[None])
```

```python
import functools

import numpy as np
import jax
import jax.numpy as jnp
from jax import lax
from jax.experimental import pallas as pl
from jax.experimental.pallas import tpu as pltpu

F32 = jnp.float32
BF16 = jnp.bfloat16

D_MODEL = 1024
PAST_LEN = 16384
D_CONV = 512
CONV_W = 3
HEAD_DIM = 64
N_Q = 8
N_KV = 2
GQA_G = N_Q // N_KV
WINDOW = 128
ROPE_THETA = 10000.0
D_IN_EVEN = 3 * D_CONV + (N_Q + 2 * N_KV) * HEAD_DIM
C_HEADS = 8
C_DK = 128
C_DV = 128
C_HK = C_HEADS * C_DK
N_GROUPS = 8
EXP_PER_GROUP = 8
N_EXPERTS = N_GROUPS * EXP_PER_GROUP
D_EXPERT = 512
MOE_BLOCK = 128
EPS = 1e-6

LANES = 128
SUBLANES = 8
VMEM_LIMIT = 56 * 1024 * 1024

GLA_CHUNK = 128
GLA_STEP = 256
SAMPLE_GB = 2
ROUTE_TM = 512
LOGIT_W = 128


def _cparams(sem):
    return pltpu.CompilerParams(dimension_semantics=sem, vmem_limit_bytes=VMEM_LIMIT)


def _sigmoid(x):
    return 1.0 / (1.0 + jnp.exp(-x))


def _row_source(src, tm):
    if isinstance(src, tuple):
        a, b = src
        n = a.shape[1]
        npt = a.shape[0] // tm
        specs = [pl.BlockSpec((tm, n), lambda i, *_: (jnp.minimum(i, npt - 1), 0)),
                 pl.BlockSpec((tm, n), lambda i, *_: (jnp.maximum(i - npt, 0), 0))]
        return specs, [a, b], npt, a.shape[0] + b.shape[0]
    return [pl.BlockSpec((tm, src.shape[1]), lambda i, *_: (i, 0))], [src], 0, src.shape[0]


def _read_rows(refs, npt):
    if len(refs) == 2:
        return jnp.where(pl.program_id(0) < npt, refs[0][...], refs[1][...])
    return refs[0][...]


def _norm_matmul_kernel(n_src, npt, *refs):
    x = _read_rows(refs[:n_src], npt)
    g_ref, w_ref, o_ref = refs[n_src:]
    ms = jnp.mean(x * x, axis=-1, keepdims=True)
    h = (x * lax.rsqrt(ms + EPS) * g_ref[...]).astype(BF16)
    o_ref[...] = jnp.dot(h, w_ref[...], preferred_element_type=F32)


def _norm_matmul(x, g, w, tm):
    D, N = w.shape
    specs, arrays, npt, T = _row_source(x, tm)
    return pl.pallas_call(
        functools.partial(_norm_matmul_kernel, len(arrays), npt),
        grid=(T // tm,),
        in_specs=specs + [pl.BlockSpec((1, D), lambda i: (0, 0)),
                          pl.BlockSpec((D, N), lambda i: (0, 0))],
        out_specs=pl.BlockSpec((tm, N), lambda i: (i, 0)),
        out_shape=jax.ShapeDtypeStruct((T, N), F32),
        compiler_params=_cparams(("arbitrary",)),
        name="norm_matmul",
    )(*arrays, g.reshape(1, D), w)


def _rope_tables(pos):
    inv = ROPE_THETA ** (-np.arange(0, HEAD_DIM, 2, dtype=np.float64) / HEAD_DIM)
    ang = np.asarray(pos, np.float64)[:, None] * inv[None, :]
    cos = np.cos(ang)
    sin = np.sin(ang)
    cos_t = np.concatenate([cos, cos, cos, cos], axis=1)
    sin_t = np.concatenate([-sin, sin, -sin, sin], axis=1)
    return jnp.asarray(cos_t, F32), jnp.asarray(sin_t, F32)


def _headnorm_rope(x, g2, cos, sin):
    lane = lax.broadcasted_iota(jnp.int32, x.shape, 1)
    lo = lane < HEAD_DIM
    x2 = x * x
    s_lo = jnp.sum(jnp.where(lo, x2, 0.0), axis=-1, keepdims=True)
    s_hi = jnp.sum(jnp.where(lo, 0.0, x2), axis=-1, keepdims=True)
    ms = jnp.where(lo, s_lo, s_hi) * (1.0 / HEAD_DIM)
    y = x * lax.rsqrt(ms + EPS) * g2
    first_half = (lane & (HEAD_DIM // 2)) == 0
    swapped = jnp.where(first_half, pltpu.roll(y, LANES - HEAD_DIM // 2, 1),
                        pltpu.roll(y, HEAD_DIM // 2, 1))
    return y * cos + swapped * sin


def _gated_conv(gb, u, c2, c1, cw):
    R = u.shape[0]
    row = lax.broadcasted_iota(jnp.int32, u.shape, 0)
    u1 = jnp.where(row == 0, c1, pltpu.roll(u, 1, 0))
    u2 = jnp.where(row == 0, c2, jnp.where(row == 1, c1, pltpu.roll(u, 2, 0)))
    del R
    return gb * (cw[0:1, :] * u2 + cw[1:2, :] * u1 + cw[2:3, :] * u)


def _band_attention(qs, kk, vv, valid, sinkv):
    s = lax.dot_general(qs.astype(BF16), kk.astype(BF16), (((1,), (1,)), ((), ())),
                        preferred_element_type=F32) * (HEAD_DIM ** -0.5)
    s = jnp.where(valid, s, -jnp.inf)
    m = jnp.maximum(jnp.max(s, axis=-1, keepdims=True), sinkv)
    e = jnp.exp(s - m)
    den = jnp.sum(e, axis=-1, keepdims=True) + jnp.exp(sinkv - m)
    p = e / den
    return jnp.dot(p.astype(BF16), vv.astype(BF16), preferred_element_type=F32)


def _sink_column(sinks_ref, hk, rows_per_head):
    R = GQA_G * rows_per_head
    row = lax.broadcasted_iota(jnp.int32, (R, 1), 0)
    col = jnp.full((R, 1), sinks_ref[hk * GQA_G + GQA_G - 1], F32)
    for j in range(GQA_G - 2, -1, -1):
        col = jnp.where(row < (j + 1) * rows_per_head, sinks_ref[hk * GQA_G + j], col)
    return col


def _even_prompt_kernel(sinks_ref, proj_ref, cos_ref, sin_ref, cw_ref, qg_ref, kg_ref,
                        mix_ref, kl_ref, vl_ref, cl_ref, kprev, vprev, ucar):
    blk = pl.program_id(1)
    W = WINDOW

    @pl.when(blk == 0)
    def _():
        kprev[...] = jnp.zeros_like(kprev)
        vprev[...] = jnp.zeros_like(vprev)
        ucar[...] = jnp.zeros_like(ucar)

    gb = proj_ref[:, 0:D_CONV]
    u = proj_ref[:, D_CONV:2 * D_CONV] * proj_ref[:, 2 * D_CONV:3 * D_CONV]
    car = ucar[...]
    a_out = _gated_conv(gb, u, car[0:1, :], car[1:2, :], cw_ref[...])
    ucar[0:2, :] = u[W - 2:W, :]
    mix_ref[:, 0:D_CONV] = a_out.astype(BF16)

    cos = cos_ref[...]
    sin = sin_ref[...]
    q0 = 3 * D_CONV
    k0 = q0 + N_Q * HEAD_DIM
    v0 = k0 + N_KV * HEAD_DIM
    k_r = _headnorm_rope(proj_ref[:, k0:k0 + LANES], kg_ref[...], cos, sin)
    v_r = proj_ref[:, v0:v0 + LANES]
    q_r = [_headnorm_rope(proj_ref[:, q0 + LANES * j:q0 + LANES * (j + 1)], qg_ref[...], cos, sin)
           for j in range(N_Q * HEAD_DIM // LANES)]
    k_p = kprev[...]
    v_p = vprev[...]

    R = GQA_G * W
    i = lax.broadcasted_iota(jnp.int32, (R, 2 * W), 0) & (W - 1)
    j = lax.broadcasted_iota(jnp.int32, (R, 2 * W), 1)
    diff = i + W - j
    valid = (diff >= 0) & (diff <= W) & ((blk > 0) | (j >= W))

    for hk in range(N_KV):
        ls = slice(hk * HEAD_DIM, (hk + 1) * HEAD_DIM)
        kk = jnp.concatenate([k_p[:, ls], k_r[:, ls]], axis=0)
        vv = jnp.concatenate([v_p[:, ls], v_r[:, ls]], axis=0)
        heads = []
        for g in range(GQA_G):
            h = hk * GQA_G + g
            tile = q_r[h // 2]
            heads.append(tile[:, (h % 2) * HEAD_DIM:(h % 2 + 1) * HEAD_DIM])
        qs = jnp.concatenate(heads, axis=0)
        o = _band_attention(qs, kk, vv, valid, _sink_column(sinks_ref, hk, W))
        for g in range(GQA_G):
            h = hk * GQA_G + g
            mix_ref[:, D_CONV + h * HEAD_DIM:D_CONV + (h + 1) * HEAD_DIM] = \
                o[g * W:(g + 1) * W, :].astype(BF16)

    kprev[...] = k_r
    vprev[...] = v_r

    @pl.when(blk == pl.num_programs(1) - 1)
    def _():
        kl_ref[0] = k_r
        vl_ref[0] = v_r
        cl_ref[0] = u[W - 2:W, :]


def _even_prompt(proj, B, L, T, cw, qg2, kg2, sinks):
    nb = L // WINDOW
    cos, sin = _rope_tables(np.arange(L))
    full = lambda shape: pl.BlockSpec(shape, lambda b, i, *_: tuple(0 for _ in shape))
    grid_spec = pltpu.PrefetchScalarGridSpec(
        num_scalar_prefetch=1,
        grid=(B, nb),
        in_specs=[pl.BlockSpec((WINDOW, D_IN_EVEN), lambda b, i, s: (b * nb + i, 0)),
                  pl.BlockSpec((WINDOW, LANES), lambda b, i, s: (i, 0)),
                  pl.BlockSpec((WINDOW, LANES), lambda b, i, s: (i, 0)),
                  full((CONV_W, D_CONV)), full((1, LANES)), full((1, LANES))],
        out_specs=[pl.BlockSpec((WINDOW, D_MODEL), lambda b, i, s: (b * nb + i, 0)),
                   pl.BlockSpec((1, WINDOW, LANES), lambda b, i, s: (b, 0, 0)),
                   pl.BlockSpec((1, WINDOW, LANES), lambda b, i, s: (b, 0, 0)),
                   pl.BlockSpec((1, CONV_W - 1, D_CONV), lambda b, i, s: (b, 0, 0))],
        scratch_shapes=[pltpu.VMEM((WINDOW, LANES), F32), pltpu.VMEM((WINDOW, LANES), F32),
                        pltpu.VMEM((SUBLANES, D_CONV), F32)],
    )
    return pl.pallas_call(
        _even_prompt_kernel,
        grid_spec=grid_spec,
        out_shape=[jax.ShapeDtypeStruct((B * L, D_MODEL), BF16),
                   jax.ShapeDtypeStruct((B, WINDOW, LANES), F32),
                   jax.ShapeDtypeStruct((B, WINDOW, LANES), F32),
                   jax.ShapeDtypeStruct((B, CONV_W - 1, D_CONV), F32)],
        compiler_params=_cparams(("arbitrary", "arbitrary")),
        name="even_prompt",
    )(sinks, proj, cos, sin, cw, qg2, kg2)


def _even_sample_kernel(sinks_ref, proj_ref, cos_ref, sin_ref, cw_ref, qg_ref, kg_ref,
                        cc_ref, ck_ref, cv_ref, mix_ref, ko_ref, vo_ref, co_ref):
    W = WINDOW
    Ld = cos_ref.shape[0]
    cos = cos_ref[...]
    sin = sin_ref[...]
    q0 = 3 * D_CONV
    k0 = q0 + N_Q * HEAD_DIM
    v0 = k0 + N_KV * HEAD_DIM
    R = GQA_G * Ld
    i = lax.broadcasted_iota(jnp.int32, (R, 2 * W), 0) % Ld
    j = lax.broadcasted_iota(jnp.int32, (R, 2 * W), 1)
    diff = i + W - j
    valid = (diff >= 0) & (diff <= W)
    zpad = jnp.zeros((W - Ld, HEAD_DIM), F32)

    for bb in range(SAMPLE_GB):
        rs = slice(bb * Ld, (bb + 1) * Ld)
        gb = proj_ref[rs, 0:D_CONV]
        u = proj_ref[rs, D_CONV:2 * D_CONV] * proj_ref[rs, 2 * D_CONV:3 * D_CONV]
        car = cc_ref[bb]
        a_out = _gated_conv(gb, u, car[0:1, :], car[1:2, :], cw_ref[...])
        co_ref[bb] = u[Ld - 2:Ld, :]
        mix_ref[rs, 0:D_CONV] = a_out.astype(BF16)

        k_r = _headnorm_rope(proj_ref[rs, k0:k0 + LANES], kg_ref[...], cos, sin)
        v_r = proj_ref[rs, v0:v0 + LANES]
        q_r = [_headnorm_rope(proj_ref[rs, q0 + LANES * t:q0 + LANES * (t + 1)], qg_ref[...], cos, sin)
               for t in range(N_Q * HEAD_DIM // LANES)]
        k_c = ck_ref[bb]
        v_c = cv_ref[bb]
        ko_ref[bb, 0:W - Ld, :] = k_c[Ld:W, :]
        ko_ref[bb, W - Ld:W, :] = k_r
        vo_ref[bb, 0:W - Ld, :] = v_c[Ld:W, :]
        vo_ref[bb, W - Ld:W, :] = v_r

        for hk in range(N_KV):
            ls = slice(hk * HEAD_DIM, (hk + 1) * HEAD_DIM)
            kk = jnp.concatenate([k_c[:, ls], k_r[:, ls], zpad], axis=0)
            vv = jnp.concatenate([v_c[:, ls], v_r[:, ls], zpad], axis=0)
            heads = []
            for g in range(GQA_G):
                h = hk * GQA_G + g
                tile = q_r[h // 2]
                heads.append(tile[:, (h % 2) * HEAD_DIM:(h % 2 + 1) * HEAD_DIM])
            qs = jnp.concatenate(heads, axis=0)
            o = _band_attention(qs, kk, vv, valid, _sink_column(sinks_ref, hk, Ld))
            for g in range(GQA_G):
                h = hk * GQA_G + g
                mix_ref[rs, D_CONV + h * HEAD_DIM:D_CONV + (h + 1) * HEAD_DIM] = \
                    o[g * Ld:(g + 1) * Ld, :].astype(BF16)


def _even_sample(proj, Bd, Ld, row0, cw, qg2, kg2, sinks, cache_conv, cache_k, cache_v):
    GB = SAMPLE_GB
    rows = GB * Ld
    rb0 = row0 // rows
    cos, sin = _rope_tables(PAST_LEN + np.arange(Ld))
    full = lambda shape: pl.BlockSpec(shape, lambda i, *_: tuple(0 for _ in shape))
    grid_spec = pltpu.PrefetchScalarGridSpec(
        num_scalar_prefetch=1,
        grid=(Bd // GB,),
        in_specs=[pl.BlockSpec((rows, D_IN_EVEN), lambda i, s: (rb0 + i, 0)),
                  full((Ld, LANES)), full((Ld, LANES)),
                  full((CONV_W, D_CONV)), full((1, LANES)), full((1, LANES)),
                  pl.BlockSpec((GB, CONV_W - 1, D_CONV), lambda i, s: (i, 0, 0)),
                  pl.BlockSpec((GB, WINDOW, LANES), lambda i, s: (i, 0, 0)),
                  pl.BlockSpec((GB, WINDOW, LANES), lambda i, s: (i, 0, 0))],
        out_specs=[pl.BlockSpec((rows, D_MODEL), lambda i, s: (i, 0)),
                   pl.BlockSpec((GB, WINDOW, LANES), lambda i, s: (i, 0, 0)),
                   pl.BlockSpec((GB, WINDOW, LANES), lambda i, s: (i, 0, 0)),
                   pl.BlockSpec((GB, CONV_W - 1, D_CONV), lambda i, s: (i, 0, 0))],
    )
    return pl.pallas_call(
        _even_sample_kernel,
        grid_spec=grid_spec,
        out_shape=[jax.ShapeDtypeStruct((Bd * Ld, D_MODEL), BF16),
                   jax.ShapeDtypeStruct((Bd, WINDOW, LANES), F32),
                   jax.ShapeDtypeStruct((Bd, WINDOW, LANES), F32),
                   jax.ShapeDtypeStruct((Bd, CONV_W - 1, D_CONV), F32)],
        compiler_params=_cparams(("arbitrary",)),
        name="even_sample",
    )(sinks, proj, cos, sin, cw, qg2, kg2, cache_conv, cache_k, cache_v)


def _cumsum_rows(x):
    C = x.shape[0]
    row = lax.broadcasted_iota(jnp.int32, x.shape, 0)
    s = 1
    while s < C:
        x = x + jnp.where(row >= s, pltpu.roll(x, s, 0), 0.0)
        s *= 2
    return x


def _group_ref(b, m, row):
    C, D = b.shape
    if 2 * m >= SUBLANES:
        n = C // (2 * m)
        b3 = b.reshape(n, 2 * m, D)
        return jnp.broadcast_to(b3[:, m - 1:m, :], (n, 2 * m, D)).reshape(C, D)
    r = row & (2 * m - 1)
    out = b
    for off in range(2 * m):
        if off == m - 1:
            continue
        shift = (off - (m - 1)) % C
        out = jnp.where(r == off, pltpu.roll(b, shift, 0), out)
    return out


def _gla_chunk(qz, fz, v, lb, S):
    C = qz.shape[0]
    q = qz * _sigmoid(qz)
    f = lb + (1.0 - lb) * _sigmoid(fz)
    k = 1.0 - f
    b = _cumsum_rows(jnp.log(f))
    vb = v.astype(BF16)

    inter = jnp.dot((q * jnp.exp(b)).astype(BF16), S.astype(BF16), preferred_element_type=F32)

    row = lax.broadcasted_iota(jnp.int32, (C, C_DK), 0)
    si = lax.broadcasted_iota(jnp.int32, (C, C), 0)
    ti = lax.broadcasted_iota(jnp.int32, (C, C), 1)
    nt = (((1,), (1,)), ((), ()))
    st = jnp.where(si == ti, lax.dot_general(k.astype(BF16), q.astype(BF16), nt,
                                             preferred_element_type=F32), 0.0)
    m = C // 2
    while m >= 1:
        rho = _group_ref(b, m, row)
        upper = (row & m) != 0
        d = b - rho
        x = (jnp.where(upper, q, k) * jnp.exp(jnp.where(upper, d, -d))).astype(BF16)
        gram = lax.dot_general(x, x, nt, preferred_element_type=F32)
        sh = (2 * m).bit_length() - 1
        pair = ((si >> sh) == (ti >> sh)) & ((si & m) == 0) & ((ti & m) != 0)
        st = st + jnp.where(pair, gram, 0.0)
        m //= 2

    intra = lax.dot_general(st.astype(BF16), vb, (((0,), (0,)), ((), ())),
                            preferred_element_type=F32)

    b_last = b[C - 1:C, :]
    eye = (lax.broadcasted_iota(jnp.int32, (C_DK, C_DK), 0)
           == lax.broadcasted_iota(jnp.int32, (C_DK, C_DK), 1))
    dcol = jnp.sum(jnp.where(eye, jnp.broadcast_to(jnp.exp(b_last), (C_DK, C_DK)), 0.0),
                   axis=-1, keepdims=True)
    kd = k * jnp.exp(b_last - b)
    S_new = dcol * S + lax.dot_general(kd.astype(BF16), vb, (((0,), (0,)), ((), ())),
                                       preferred_element_type=F32)
    return inter + intra, S_new


def _gated_out(o, og, gz):
    ms = jnp.mean(o * o, axis=-1, keepdims=True)
    return (o * lax.rsqrt(ms + EPS) * og) * (gz * _sigmoid(gz))


def _hgrn_prompt_kernel(q_ref, f_ref, i_ref, g_ref, lb_ref, og_ref, o_ref, s_ref, S):
    c = pl.program_id(2)

    @pl.when(c == 0)
    def _():
        S[...] = jnp.zeros_like(S)

    lb = lb_ref[...]
    og = og_ref[...]

    Sv = S[...]
    for n in range(GLA_STEP // GLA_CHUNK):
        rs = slice(n * GLA_CHUNK, (n + 1) * GLA_CHUNK)
        o, Sv = _gla_chunk(q_ref[rs, :], f_ref[rs, :], i_ref[rs, :], lb, Sv)
        o_ref[rs, :] = _gated_out(o, og, g_ref[rs, :]).astype(BF16)
    S[...] = Sv

    @pl.when(c == pl.num_programs(2) - 1)
    def _():
        s_ref[0, 0] = S[...]


def _hgrn_prompt(proj, B, L, T, lb, og):
    ns = L // GLA_STEP
    H = C_HEADS
    col = lambda off: pl.BlockSpec((GLA_STEP, LANES), lambda b, h, c: (b * ns + c, off + h))
    return pl.pallas_call(
        _hgrn_prompt_kernel,
        grid=(B, H, ns),
        in_specs=[col(0), col(H), col(2 * H), col(3 * H),
                  pl.BlockSpec((1, LANES), lambda b, h, c: (0, h)),
                  pl.BlockSpec((1, LANES), lambda b, h, c: (0, 0))],
        out_specs=[pl.BlockSpec((GLA_STEP, LANES), lambda b, h, c: (b * ns + c, h)),
                   pl.BlockSpec((1, 1, C_DK, C_DV), lambda b, h, c: (b, h, 0, 0))],
        out_shape=[jax.ShapeDtypeStruct((B * L, D_MODEL), BF16),
                   jax.ShapeDtypeStruct((B, H, C_DK, C_DV), F32)],
        scratch_shapes=[pltpu.VMEM((C_DK, C_DV), F32)],
        compiler_params=_cparams(("arbitrary", "arbitrary", "arbitrary")),
        name="hgrn_prompt",
    )(proj, proj, proj, proj, lb, og)


def _hgrn_sample_kernel(p_ref, lb_ref, og_ref, s0_ref, o_ref, s_ref):
    Ld = p_ref.shape[0] // SAMPLE_GB
    og = og_ref[...]
    for bb in range(SAMPLE_GB):
        rs = slice(bb * Ld, (bb + 1) * Ld)
        for h in range(C_HEADS):
            cs = lambda part: slice((part * C_HEADS + h) * LANES, (part * C_HEADS + h + 1) * LANES)
            o, S_new = _gla_chunk(p_ref[rs, cs(0)], p_ref[rs, cs(1)], p_ref[rs, cs(2)],
                                  lb_ref[:, h * LANES:(h + 1) * LANES], s0_ref[bb, h])
            s_ref[bb, h] = S_new
            o_ref[rs, h * LANES:(h + 1) * LANES] = _gated_out(o, og, p_ref[rs, cs(3)]).astype(BF16)


def _hgrn_sample(proj, Bd, Ld, row0, lb, og, s0):
    GB = SAMPLE_GB
    rows = GB * Ld
    rb0 = row0 // rows
    H = C_HEADS
    return pl.pallas_call(
        _hgrn_sample_kernel,
        grid=(Bd // GB,),
        in_specs=[pl.BlockSpec((rows, 4 * C_HK), lambda i: (rb0 + i, 0)),
                  pl.BlockSpec((1, C_HK), lambda i: (0, 0)),
                  pl.BlockSpec((1, LANES), lambda i: (0, 0)),
                  pl.BlockSpec((GB, H, C_DK, C_DV), lambda i: (i, 0, 0, 0))],
        out_specs=[pl.BlockSpec((rows, D_MODEL), lambda i: (i, 0)),
                   pl.BlockSpec((GB, H, C_DK, C_DV), lambda i: (i, 0, 0, 0))],
        out_shape=[jax.ShapeDtypeStruct((Bd * Ld, D_MODEL), BF16),
                   jax.ShapeDtypeStruct((Bd, H, C_DK, C_DV), F32)],
        compiler_params=_cparams(("arbitrary",)),
        name="hgrn_sample",
    )(proj, lb, og, s0)


def _out_proj_kernel(n_mix, n_x, npt, *refs):
    m = _read_rows(refs[:n_mix], npt)
    x = _read_rows(refs[n_mix:n_mix + n_x], npt)
    w_ref, g_ref, wr_ref, br_ref, x1_ref, h_ref, lg_ref = refs[n_mix + n_x:]
    x1 = x + jnp.dot(m, w_ref[...], preferred_element_type=F32)
    x1_ref[...] = x1
    ms = jnp.mean(x1 * x1, axis=-1, keepdims=True)
    h = x1 * lax.rsqrt(ms + EPS) * g_ref[...]
    h_ref[...] = h
    lg_ref[...] = jnp.dot(h.astype(BF16), wr_ref[...], preferred_element_type=F32) + br_ref[...]


def _out_proj(mix, w, x, g, wr, br, tm):
    K, D = w.shape
    m_specs, m_arrays, npt, T = _row_source(mix, tm)
    x_specs, x_arrays, npt_x, _ = _row_source(x, tm)
    assert npt_x in (0, npt)
    row = lambda n: pl.BlockSpec((tm, n), lambda i: (i, 0))
    full = lambda a, b: pl.BlockSpec((a, b), lambda i: (0, 0))
    return pl.pallas_call(
        functools.partial(_out_proj_kernel, len(m_arrays), len(x_arrays), npt),
        grid=(T // tm,),
        in_specs=m_specs + x_specs + [full(K, D), full(1, D), full(D, LOGIT_W), full(1, LOGIT_W)],
        out_specs=[row(D), row(D), row(LOGIT_W)],
        out_shape=[jax.ShapeDtypeStruct((T, D), F32), jax.ShapeDtypeStruct((T, D), F32),
                   jax.ShapeDtypeStruct((T, LOGIT_W), F32)],
        compiler_params=_cparams(("arbitrary",)),
        name="out_proj",
    )(*m_arrays, *x_arrays, w, g.reshape(1, D), wr, br)


def _route_kernel(lg_ref, slot_ref, gate_ref, cnt_ref, carry, total):
    phase = pl.program_id(0)
    step = pl.program_id(1)
    tm = lg_ref.shape[0]
    lg = lg_ref[...]
    lane = lax.broadcasted_iota(jnp.int32, lg.shape, 1)
    neg = -jnp.inf

    is_g = lane < N_GROUPS
    glog = jnp.where(is_g, lg, neg)
    gmax = jnp.max(glog, axis=-1, keepdims=True)
    g_idx = jnp.min(jnp.where(glog == gmax, lane, LOGIT_W), axis=-1, keepdims=True)
    g_p = 1.0 / jnp.sum(jnp.exp(glog - gmax), axis=-1, keepdims=True)

    e0 = N_GROUPS + g_idx * EXP_PER_GROUP
    in_grp = (lane >= e0) & (lane < e0 + EXP_PER_GROUP)
    elog = jnp.where(in_grp, lg, neg)
    emax = jnp.max(elog, axis=-1, keepdims=True)
    ee = jnp.exp(elog - emax)
    prob = ee / jnp.sum(ee, axis=-1, keepdims=True)
    prob = jnp.where(in_grp, prob, -1.0)
    p1 = jnp.max(prob, axis=-1, keepdims=True)
    i1 = jnp.min(jnp.where(prob == p1, lane, LOGIT_W), axis=-1, keepdims=True)
    prob2 = jnp.where(lane == i1, -1.0, prob)
    p2 = jnp.max(prob2, axis=-1, keepdims=True)
    i2 = jnp.min(jnp.where(prob2 == p2, lane, LOGIT_W), axis=-1, keepdims=True)
    psum = p1 + p2
    w1 = p1 / psum * g_p
    w2 = p2 / psum * g_p

    oh1 = lane == (i1 - N_GROUPS)
    oh2 = lane == (i2 - N_GROUPS)
    both = jnp.where(oh1 | oh2, 1.0, 0.0)

    @pl.when((phase == 0) & (step == 0))
    def _():
        total[...] = jnp.zeros_like(total)

    @pl.when(phase == 0)
    def _():
        total[...] += jnp.sum(both, axis=0, keepdims=True)

    @pl.when((phase == 1) & (step == 0))
    def _():
        carry[...] = jnp.zeros_like(carry)

    @pl.when(phase == 1)
    def _():
        cnt = total[...]
        padded = jnp.floor((cnt + (MOE_BLOCK - 1)) * (1.0 / MOE_BLOCK)) * MOE_BLOCK
        ri = lax.broadcasted_iota(jnp.int32, (LOGIT_W, LOGIT_W), 0)
        ci = lax.broadcasted_iota(jnp.int32, (LOGIT_W, LOGIT_W), 1)
        pcol = jnp.sum(jnp.where(ri == ci, jnp.broadcast_to(padded, (LOGIT_W, LOGIT_W)), 0.0),
                       axis=1, keepdims=True)
        pstart = jnp.sum(jnp.where(ri < ci, pcol, 0.0), axis=0, keepdims=True)
        tr = lax.broadcasted_iota(jnp.int32, (tm, tm), 0)
        tc = lax.broadcasted_iota(jnp.int32, (tm, tm), 1)
        tri = jnp.where(tc < tr, 1.0, 0.0).astype(BF16)
        before = jnp.dot(tri, both.astype(BF16), preferred_element_type=F32) + carry[...] + pstart
        s1 = jnp.sum(jnp.where(oh1, before, 0.0), axis=-1, keepdims=True)
        s2 = jnp.sum(jnp.where(oh2, before, 0.0), axis=-1, keepdims=True)
        carry[...] += jnp.sum(both, axis=0, keepdims=True)
        slot_ref[...] = jnp.where(lane == 0, s1, jnp.where(lane == 1, s2, 0.0)).astype(jnp.int32)
        gate_ref[...] = jnp.where(lane == 0, w1, jnp.where(lane == 1, w2, 0.0))
        cnt_ref[...] = cnt


def _route(logits):
    T = logits.shape[0]
    tm = ROUTE_TM
    return pl.pallas_call(
        _route_kernel,
        grid=(2, T // tm),
        in_specs=[pl.BlockSpec((tm, LOGIT_W), lambda p, i: (i, 0))],
        out_specs=[pl.BlockSpec((tm, LOGIT_W), lambda p, i: (i * p, 0)),
                   pl.BlockSpec((tm, LOGIT_W), lambda p, i: (i * p, 0)),
                   pl.BlockSpec((1, LOGIT_W), lambda p, i: (0, 0))],
        out_shape=[jax.ShapeDtypeStruct((T, LOGIT_W), jnp.int32),
                   jax.ShapeDtypeStruct((T, LOGIT_W), F32),
                   jax.ShapeDtypeStruct((1, LOGIT_W), F32)],
        scratch_shapes=[pltpu.VMEM((1, LOGIT_W), F32), pltpu.VMEM((1, LOGIT_W), F32)],
        compiler_params=_cparams(("arbitrary", "arbitrary")),
        name="route",
    )(logits)


DISPATCH_TM = 512


def _dispatch_kernel(T, slot_ref, h_ref, xs_in_ref, xs_ref, sem):
    del xs_in_ref
    base = pl.program_id(0) * DISPATCH_TM

    def copy(r, k):
        return pltpu.make_async_copy(h_ref.at[pl.ds(r, 1)],
                                     xs_ref.at[pl.ds(slot_ref[k * T + base + r], 1)], sem)

    def start(r, c):
        copy(r, 0).start()
        copy(r, 1).start()
        return c

    def wait(r, c):
        copy(r, 0).wait()
        copy(r, 1).wait()
        return c

    lax.fori_loop(0, DISPATCH_TM, start, 0, unroll=8)
    lax.fori_loop(0, DISPATCH_TM, wait, 0, unroll=8)


def _dispatch(slots, h, n_slots):
    T, D = h.shape
    xs0 = jnp.zeros((n_slots, D), F32)
    grid_spec = pltpu.PrefetchScalarGridSpec(
        num_scalar_prefetch=1,
        grid=(T // DISPATCH_TM,),
        in_specs=[pl.BlockSpec((DISPATCH_TM, D), lambda i, s: (i, 0)),
                  pl.BlockSpec(memory_space=pl.ANY)],
        out_specs=pl.BlockSpec(memory_space=pl.ANY),
        scratch_shapes=[pltpu.SemaphoreType.DMA(())],
    )
    return pl.pallas_call(
        functools.partial(_dispatch_kernel, T),
        grid_spec=grid_spec,
        out_shape=jax.ShapeDtypeStruct((n_slots, D), F32),
        input_output_aliases={2: 0},
        compiler_params=_cparams(("arbitrary",)),
        name="moe_dispatch",
    )(slots, h, xs0)


def _ffn_kernel(nblk, meta_ref, x_ref, w1_ref, w3_ref, w2_ref, y_ref, wb1, wb3, wb2):
    b = pl.program_id(0)
    used = b < meta_ref[nblk]
    new_expert = (b == 0) | (meta_ref[b] != meta_ref[jnp.maximum(b - 1, 0)])

    @pl.when(used & new_expert)
    def _():
        wb1[...] = w1_ref[0, 0].astype(BF16)
        wb3[...] = w3_ref[0, 0].astype(BF16)
        wb2[...] = w2_ref[0, 0].astype(BF16)

    @pl.when(used)
    def _():
        x = x_ref[...].astype(BF16)
        a = jnp.dot(x, wb1[...], preferred_element_type=F32)
        c = jnp.dot(x, wb3[...], preferred_element_type=F32)
        hid = (a * _sigmoid(a)) * c
        y_ref[...] = jnp.dot(hid.astype(BF16), wb2[...], preferred_element_type=F32)

    @pl.when(jnp.logical_not(used))
    def _():
        y_ref[...] = jnp.zeros_like(y_ref)


def _ffn(meta, xs, w1, w3, w2, layer):
    n_slots, D = xs.shape
    nblk = n_slots // MOE_BLOCK
    grid_spec = pltpu.PrefetchScalarGridSpec(
        num_scalar_prefetch=1,
        grid=(nblk,),
        in_specs=[pl.BlockSpec((MOE_BLOCK, D), lambda b, e: (b, 0)),
                  pl.BlockSpec((1, 1, D, D_EXPERT), lambda b, e: (layer, e[b], 0, 0)),
                  pl.BlockSpec((1, 1, D, D_EXPERT), lambda b, e: (layer, e[b], 0, 0)),
                  pl.BlockSpec((1, 1, D_EXPERT, D), lambda b, e: (layer, e[b], 0, 0))],
        out_specs=pl.BlockSpec((MOE_BLOCK, D), lambda b, e: (b, 0)),
        scratch_shapes=[pltpu.VMEM((D, D_EXPERT), BF16), pltpu.VMEM((D, D_EXPERT), BF16),
                        pltpu.VMEM((D_EXPERT, D), BF16)],
    )
    return pl.pallas_call(
        functools.partial(_ffn_kernel, nblk),
        grid_spec=grid_spec,
        out_shape=jax.ShapeDtypeStruct((n_slots, D), F32),
        compiler_params=_cparams(("arbitrary",)),
        name="moe_ffn",
    )(meta, xs, w1, w3, w2)


COMBINE_TM = 256


def _combine_kernel(T, npt, slot_ref, x_ref, gate_ref, ys_ref, *refs):
    out_refs, (buf, sem) = refs[:-2], refs[-2:]
    i = pl.program_id(0)
    base = i * COMBINE_TM

    def copy(r, k):
        return pltpu.make_async_copy(ys_ref.at[pl.ds(slot_ref[k * T + base + r], 1)],
                                     buf.at[k, pl.ds(r, 1)], sem)

    def start(r, c):
        copy(r, 0).start()
        copy(r, 1).start()
        return c

    def wait(r, c):
        copy(r, 0).wait()
        copy(r, 1).wait()
        return c

    lax.fori_loop(0, COMBINE_TM, start, 0, unroll=8)
    lax.fori_loop(0, COMBINE_TM, wait, 0, unroll=8)
    g = gate_ref[...]
    y = x_ref[...] + (buf[0] * g[:, 0:1] + buf[1] * g[:, 1:2])
    if npt == 0:
        out_refs[0][...] = y
    else:
        @pl.when(i < npt)
        def _():
            out_refs[0][...] = y

        @pl.when(i >= npt)
        def _():
            out_refs[1][...] = y


def _combine(slots, x1, gates, ys, split_rows=0):
    T, D = x1.shape
    tm = COMBINE_TM
    npt = split_rows // tm
    if npt == 0:
        out_specs = pl.BlockSpec((tm, D), lambda i, s: (i, 0))
        out_shape = jax.ShapeDtypeStruct((T, D), F32)
    else:
        out_specs = [pl.BlockSpec((tm, D), lambda i, s: (jnp.minimum(i, npt - 1), 0)),
                     pl.BlockSpec((tm, D), lambda i, s: (jnp.maximum(i - npt, 0), 0))]
        out_shape = [jax.ShapeDtypeStruct((split_rows, D), F32),
                     jax.ShapeDtypeStruct((T - split_rows, D), F32)]
    grid_spec = pltpu.PrefetchScalarGridSpec(
        num_scalar_prefetch=1,
        grid=(T // tm,),
        in_specs=[pl.BlockSpec((tm, D), lambda i, s: (i, 0)),
                  pl.BlockSpec((tm, LOGIT_W), lambda i, s: (i, 0)),
                  pl.BlockSpec(memory_space=pl.ANY)],
        out_specs=out_specs,
        scratch_shapes=[pltpu.VMEM((2, tm, D), F32), pltpu.SemaphoreType.DMA(())],
    )
    return pl.pallas_call(
        functools.partial(_combine_kernel, T, npt),
        grid_spec=grid_spec,
        out_shape=out_shape,
        compiler_params=_cparams(("arbitrary",)),
        name="moe_combine",
    )(slots, x1, gates, ys)


def _moe(x1, h, logits, w1, w3, w2, layer, split_rows=0):
    T = x1.shape[0]
    slot_l, gate_l, cnt = _route(logits)
    slots = jnp.concatenate([slot_l[:, 0], slot_l[:, 1]])
    n_blocks = -(-(2 * T) // MOE_BLOCK) + N_EXPERTS
    counts = cnt[0, :N_EXPERTS].astype(jnp.int32)
    pend = jnp.cumsum((counts + MOE_BLOCK - 1) // MOE_BLOCK * MOE_BLOCK)
    starts = jnp.arange(n_blocks, dtype=jnp.int32) * MOE_BLOCK
    blk_e = jnp.minimum(jnp.sum((pend[None, :] <= starts[:, None]).astype(jnp.int32), axis=1),
                        N_EXPERTS - 1)
    meta = jnp.concatenate([blk_e, pend[-1:] // MOE_BLOCK]).astype(jnp.int32)
    xs = _dispatch(slots, h, n_blocks * MOE_BLOCK)
    ys = _ffn(meta, xs, w1, w3, w2, layer)
    return _combine(slots, x1, gate_l, ys, split_rows)


def _router_weights(w_grp, b_grp, w_exp, b_exp):
    D = w_grp.shape[0]
    pad = LOGIT_W - N_GROUPS - N_EXPERTS
    wr = jnp.concatenate([w_grp, w_exp, jnp.zeros((D, pad), F32)], axis=1).astype(BF16)
    br = jnp.concatenate([b_grp, b_exp, jnp.zeros((pad,), F32)]).reshape(1, LOGIT_W)
    return wr, br


def kernel(x_prompt, x_sample, cache_conv, cache_k, cache_v, state_hgrn, norm_mix, norm_ffn,
           ev_w_in, ev_conv, ev_q_norm, ev_k_norm, ev_sinks, ev_w_out,
           od_w_in, od_lb, od_o_norm, od_w_out,
           moe_w_grp, moe_b_grp, moe_w_exp, moe_b_exp, moe_w1, moe_w3, moe_w2):
    B, L, D = x_prompt.shape
    Bd, Ld, _ = x_sample.shape
    Tp = B * L
    T = Tp + Bd * Ld
    x = (x_prompt.reshape(Tp, D), x_sample.reshape(Bd * Ld, D))

    proj = _norm_matmul(x, norm_mix[0], ev_w_in[0].astype(BF16), 512)
    qg2 = jnp.tile(ev_q_norm[0], 2).reshape(1, LANES)
    kg2 = jnp.tile(ev_k_norm[0], 2).reshape(1, LANES)
    mix_p, k_p, v_p, conv_p = _even_prompt(proj, B, L, T, ev_conv[0], qg2, kg2, ev_sinks[0])
    mix_s, k_s, v_s, conv_s = _even_sample(
        proj, Bd, Ld, Tp, ev_conv[0], qg2, kg2, ev_sinks[0], cache_conv[0],
        cache_k[0].reshape(Bd, WINDOW, LANES), cache_v[0].reshape(Bd, WINDOW, LANES))
    wr, br = _router_weights(moe_w_grp[0], moe_b_grp[0], moe_w_exp[0], moe_b_exp[0])
    x1, h, logits = _out_proj((mix_p, mix_s), ev_w_out[0].astype(BF16), x, norm_ffn[0], wr, br, 512)
    x = _moe(x1, h, logits, moe_w1, moe_w3, moe_w2, 0)

    lb_all = jnp.cumsum(jax.nn.softmax(od_lb.astype(F32), axis=0), axis=0)
    lb = (lb_all - lb_all[0])[1].reshape(1, C_HK)
    og = od_o_norm[0].reshape(1, C_DV)
    proj = _norm_matmul(x, norm_mix[1], od_w_in[0].astype(BF16), 256)
    o_p, s_p = _hgrn_prompt(proj, B, L, T, lb, og)
    o_s, s_s = _hgrn_sample(proj, Bd, Ld, Tp, lb, og, state_hgrn[0])
    wr, br = _router_weights(moe_w_grp[1], moe_b_grp[1], moe_w_exp[1], moe_b_exp[1])
    x1, h, logits = _out_proj((o_p, o_s), od_w_out[0].astype(BF16), x, norm_ffn[1], wr, br, 512)
    y_p, y_s = _moe(x1, h, logits, moe_w1, moe_w3, moe_w2, 1, split_rows=Tp)

    y_prompt = y_p.reshape(B, L, D)
    y_sample = y_s.reshape(Bd, Ld, D)
    return (y_prompt, y_sample,
            conv_p[None], k_p.reshape(1, B, WINDOW, N_KV, HEAD_DIM), v_p.reshape(1, B, WINDOW, N_KV, HEAD_DIM),
            s_p[None],
            conv_s[None], k_s.reshape(1, Bd, WINDOW, N_KV, HEAD_DIM), v_s.reshape(1, Bd, WINDOW, N_KV, HEAD_DIM),
            s_s[None])
```

```python
import functools

import numpy as np
import jax
import jax.numpy as jnp
from jax import lax
from jax.experimental import pallas as pl
from jax.experimental.pallas import tpu as pltpu

F32 = jnp.float32
BF16 = jnp.bfloat16

D_MODEL = 1024
PAST_LEN = 16384
D_CONV = 512
CONV_W = 3
HEAD_DIM = 64
N_Q = 8
N_KV = 2
GQA_G = N_Q // N_KV
WINDOW = 128
ROPE_THETA = 10000.0
D_IN_EVEN = 3 * D_CONV + (N_Q + 2 * N_KV) * HEAD_DIM
C_HEADS = 8
C_DK = 128
C_DV = 128
C_HK = C_HEADS * C_DK
N_GROUPS = 8
EXP_PER_GROUP = 8
N_EXPERTS = N_GROUPS * EXP_PER_GROUP
D_EXPERT = 512
MOE_BLOCK = 256
EPS = 1e-6

LANES = 128
SUBLANES = 8
VMEM_LIMIT = 56 * 1024 * 1024

GLA_CHUNK = 128
GLA_STEP = 256
SAMPLE_GB = 2
ROUTE_TM = 512
LOGIT_W = 128


def _cparams(sem):
    return pltpu.CompilerParams(dimension_semantics=sem, vmem_limit_bytes=VMEM_LIMIT)


def _sigmoid(x):
    return 1.0 / (1.0 + jnp.exp(-x))


def _row_source(src, tm):
    if isinstance(src, tuple):
        a, b = src
        n = a.shape[1]
        npt = a.shape[0] // tm
        specs = [pl.BlockSpec((tm, n), lambda i, *_: (jnp.minimum(i, npt - 1), 0)),
                 pl.BlockSpec((tm, n), lambda i, *_: (jnp.maximum(i - npt, 0), 0))]
        return specs, [a, b], npt, a.shape[0] + b.shape[0]
    return [pl.BlockSpec((tm, src.shape[1]), lambda i, *_: (i, 0))], [src], 0, src.shape[0]


def _read_rows(refs, npt):
    if len(refs) == 2:
        return jnp.where(pl.program_id(0) < npt, refs[0][...], refs[1][...])
    return refs[0][...]


def _norm_matmul_kernel(n_src, npt, *refs):
    x = _read_rows(refs[:n_src], npt)
    g_ref, w_ref, o_ref = refs[n_src:]
    ms = jnp.mean(x * x, axis=-1, keepdims=True)
    h = (x * lax.rsqrt(ms + EPS) * g_ref[...]).astype(BF16)
    o_ref[...] = jnp.dot(h, w_ref[...], preferred_element_type=F32)


def _norm_matmul(x, g, w, tm):
    D, N = w.shape
    specs, arrays, npt, T = _row_source(x, tm)
    return pl.pallas_call(
        functools.partial(_norm_matmul_kernel, len(arrays), npt),
        grid=(T // tm,),
        in_specs=specs + [pl.BlockSpec((1, D), lambda i: (0, 0)),
                          pl.BlockSpec((D, N), lambda i: (0, 0))],
        out_specs=pl.BlockSpec((tm, N), lambda i: (i, 0)),
        out_shape=jax.ShapeDtypeStruct((T, N), F32),
        compiler_params=_cparams(("arbitrary",)),
        name="norm_matmul",
    )(*arrays, g.reshape(1, D), w)


def _rope_tables(pos):
    inv = ROPE_THETA ** (-jnp.arange(0, HEAD_DIM, 2, dtype=F32) / HEAD_DIM)
    ang = pos.astype(F32)[:, None] * inv[None, :]
    cos = jnp.cos(ang)
    sin = jnp.sin(ang)
    return (jnp.concatenate([cos, cos, cos, cos], axis=1),
            jnp.concatenate([-sin, sin, -sin, sin], axis=1))


def _headnorm_rope(x, g2, cos, sin):
    lane = lax.broadcasted_iota(jnp.int32, x.shape, 1)
    lo = lane < HEAD_DIM
    x2 = x * x
    s_lo = jnp.sum(jnp.where(lo, x2, 0.0), axis=-1, keepdims=True)
    s_hi = jnp.sum(jnp.where(lo, 0.0, x2), axis=-1, keepdims=True)
    ms = jnp.where(lo, s_lo, s_hi) * (1.0 / HEAD_DIM)
    y = x * lax.rsqrt(ms + EPS) * g2
    first_half = (lane & (HEAD_DIM // 2)) == 0
    swapped = jnp.where(first_half, pltpu.roll(y, LANES - HEAD_DIM // 2, 1),
                        pltpu.roll(y, HEAD_DIM // 2, 1))
    return y * cos + swapped * sin


def _gated_conv(gb, u, c2, c1, cw):
    R = u.shape[0]
    row = lax.broadcasted_iota(jnp.int32, u.shape, 0)
    u1 = jnp.where(row == 0, c1, pltpu.roll(u, 1, 0))
    u2 = jnp.where(row == 0, c2, jnp.where(row == 1, c1, pltpu.roll(u, 2, 0)))
    del R
    return gb * (cw[0:1, :] * u2 + cw[1:2, :] * u1 + cw[2:3, :] * u)


def _band_attention(qs, kk, vv, valid, sinkv):
    s = lax.dot_general(qs.astype(BF16), kk.astype(BF16), (((1,), (1,)), ((), ())),
                        preferred_element_type=F32) * (HEAD_DIM ** -0.5)
    s = jnp.where(valid, s, -jnp.inf)
    m = jnp.maximum(jnp.max(s, axis=-1, keepdims=True), sinkv)
    e = jnp.exp(s - m)
    den = jnp.sum(e, axis=-1, keepdims=True) + jnp.exp(sinkv - m)
    p = e / den
    return jnp.dot(p.astype(BF16), vv.astype(BF16), preferred_element_type=F32)


def _sink_column(sinks_ref, hk, rows_per_head):
    R = GQA_G * rows_per_head
    row = lax.broadcasted_iota(jnp.int32, (R, 1), 0)
    col = jnp.full((R, 1), sinks_ref[hk * GQA_G + GQA_G - 1], F32)
    for j in range(GQA_G - 2, -1, -1):
        col = jnp.where(row < (j + 1) * rows_per_head, sinks_ref[hk * GQA_G + j], col)
    return col


def _even_prompt_kernel(sinks_ref, proj_ref, cos_ref, sin_ref, cw_ref, qg_ref, kg_ref,
                        mix_ref, kl_ref, vl_ref, cl_ref, kprev, vprev, ucar):
    blk = pl.program_id(1)
    W = WINDOW

    @pl.when(blk == 0)
    def _():
        kprev[...] = jnp.zeros_like(kprev)
        vprev[...] = jnp.zeros_like(vprev)
        ucar[...] = jnp.zeros_like(ucar)

    gb = proj_ref[:, 0:D_CONV]
    u = proj_ref[:, D_CONV:2 * D_CONV] * proj_ref[:, 2 * D_CONV:3 * D_CONV]
    car = ucar[...]
    a_out = _gated_conv(gb, u, car[0:1, :], car[1:2, :], cw_ref[...])
    ucar[0:2, :] = u[W - 2:W, :]
    mix_ref[:, 0:D_CONV] = a_out.astype(BF16)

    cos = cos_ref[...]
    sin = sin_ref[...]
    q0 = 3 * D_CONV
    k0 = q0 + N_Q * HEAD_DIM
    v0 = k0 + N_KV * HEAD_DIM
    k_r = _headnorm_rope(proj_ref[:, k0:k0 + LANES], kg_ref[...], cos, sin)
    v_r = proj_ref[:, v0:v0 + LANES]
    q_r = [_headnorm_rope(proj_ref[:, q0 + LANES * j:q0 + LANES * (j + 1)], qg_ref[...], cos, sin)
           for j in range(N_Q * HEAD_DIM // LANES)]
    k_p = kprev[...]
    v_p = vprev[...]

    R = GQA_G * W
    i = lax.broadcasted_iota(jnp.int32, (R, 2 * W), 0) & (W - 1)
    j = lax.broadcasted_iota(jnp.int32, (R, 2 * W), 1)
    diff = i + W - j
    valid = (diff >= 0) & (diff <= W) & ((blk > 0) | (j >= W))

    for hk in range(N_KV):
        ls = slice(hk * HEAD_DIM, (hk + 1) * HEAD_DIM)
        kk = jnp.concatenate([k_p[:, ls], k_r[:, ls]], axis=0)
        vv = jnp.concatenate([v_p[:, ls], v_r[:, ls]], axis=0)
        heads = []
        for g in range(GQA_G):
            h = hk * GQA_G + g
            tile = q_r[h // 2]
            heads.append(tile[:, (h % 2) * HEAD_DIM:(h % 2 + 1) * HEAD_DIM])
        qs = jnp.concatenate(heads, axis=0)
        o = _band_attention(qs, kk, vv, valid, _sink_column(sinks_ref, hk, W))
        for g in range(GQA_G):
            h = hk * GQA_G + g
            mix_ref[:, D_CONV + h * HEAD_DIM:D_CONV + (h + 1) * HEAD_DIM] = \
                o[g * W:(g + 1) * W, :].astype(BF16)

    kprev[...] = k_r
    vprev[...] = v_r

    @pl.when(blk == pl.num_programs(1) - 1)
    def _():
        kl_ref[0] = k_r
        vl_ref[0] = v_r
        cl_ref[0] = u[W - 2:W, :]


def _even_prompt(proj, B, L, T, cw, qg2, kg2, sinks):
    nb = L // WINDOW
    cos, sin = _rope_tables(jnp.arange(L, dtype=jnp.int32))
    full = lambda shape: pl.BlockSpec(shape, lambda b, i, *_: tuple(0 for _ in shape))
    grid_spec = pltpu.PrefetchScalarGridSpec(
        num_scalar_prefetch=1,
        grid=(B, nb),
        in_specs=[pl.BlockSpec((WINDOW, D_IN_EVEN), lambda b, i, s: (b * nb + i, 0)),
                  pl.BlockSpec((WINDOW, LANES), lambda b, i, s: (i, 0)),
                  pl.BlockSpec((WINDOW, LANES), lambda b, i, s: (i, 0)),
                  full((CONV_W, D_CONV)), full((1, LANES)), full((1, LANES))],
        out_specs=[pl.BlockSpec((WINDOW, D_MODEL), lambda b, i, s: (b * nb + i, 0)),
                   pl.BlockSpec((1, WINDOW, LANES), lambda b, i, s: (b, 0, 0)),
                   pl.BlockSpec((1, WINDOW, LANES), lambda b, i, s: (b, 0, 0)),
                   pl.BlockSpec((1, CONV_W - 1, D_CONV), lambda b, i, s: (b, 0, 0))],
        scratch_shapes=[pltpu.VMEM((WINDOW, LANES), F32), pltpu.VMEM((WINDOW, LANES), F32),
                        pltpu.VMEM((SUBLANES, D_CONV), F32)],
    )
    return pl.pallas_call(
        _even_prompt_kernel,
        grid_spec=grid_spec,
        out_shape=[jax.ShapeDtypeStruct((B * L, D_MODEL), BF16),
                   jax.ShapeDtypeStruct((B, WINDOW, LANES), F32),
                   jax.ShapeDtypeStruct((B, WINDOW, LANES), F32),
                   jax.ShapeDtypeStruct((B, CONV_W - 1, D_CONV), F32)],
        compiler_params=_cparams(("arbitrary", "arbitrary")),
        name="even_prompt",
    )(sinks, proj, cos, sin, cw, qg2, kg2)


def _even_sample_kernel(sinks_ref, proj_ref, cos_ref, sin_ref, cw_ref, qg_ref, kg_ref,
                        cc_ref, ck_ref, cv_ref, mix_ref, ko_ref, vo_ref, co_ref):
    W = WINDOW
    Ld = cos_ref.shape[0]
    cos = cos_ref[...]
    sin = sin_ref[...]
    q0 = 3 * D_CONV
    k0 = q0 + N_Q * HEAD_DIM
    v0 = k0 + N_KV * HEAD_DIM
    R = GQA_G * Ld
    i = lax.broadcasted_iota(jnp.int32, (R, 2 * W), 0) % Ld
    j = lax.broadcasted_iota(jnp.int32, (R, 2 * W), 1)
    diff = i + W - j
    valid = (diff >= 0) & (diff <= W)
    zpad = jnp.zeros((W - Ld, HEAD_DIM), F32)

    for bb in range(SAMPLE_GB):
        rs = slice(bb * Ld, (bb + 1) * Ld)
        gb = proj_ref[rs, 0:D_CONV]
        u = proj_ref[rs, D_CONV:2 * D_CONV] * proj_ref[rs, 2 * D_CONV:3 * D_CONV]
        car = cc_ref[bb]
        a_out = _gated_conv(gb, u, car[0:1, :], car[1:2, :], cw_ref[...])
        co_ref[bb] = u[Ld - 2:Ld, :]
        mix_ref[rs, 0:D_CONV] = a_out.astype(BF16)

        k_r = _headnorm_rope(proj_ref[rs, k0:k0 + LANES], kg_ref[...], cos, sin)
        v_r = proj_ref[rs, v0:v0 + LANES]
        q_r = [_headnorm_rope(proj_ref[rs, q0 + LANES * t:q0 + LANES * (t + 1)], qg_ref[...], cos, sin)
               for t in range(N_Q * HEAD_DIM // LANES)]
        k_c = ck_ref[bb]
        v_c = cv_ref[bb]
        ko_ref[bb, 0:W - Ld, :] = k_c[Ld:W, :]
        ko_ref[bb, W - Ld:W, :] = k_r
        vo_ref[bb, 0:W - Ld, :] = v_c[Ld:W, :]
        vo_ref[bb, W - Ld:W, :] = v_r

        for hk in range(N_KV):
            ls = slice(hk * HEAD_DIM, (hk + 1) * HEAD_DIM)
            kk = jnp.concatenate([k_c[:, ls], k_r[:, ls], zpad], axis=0)
            vv = jnp.concatenate([v_c[:, ls], v_r[:, ls], zpad], axis=0)
            heads = []
            for g in range(GQA_G):
                h = hk * GQA_G + g
                tile = q_r[h // 2]
                heads.append(tile[:, (h % 2) * HEAD_DIM:(h % 2 + 1) * HEAD_DIM])
            qs = jnp.concatenate(heads, axis=0)
            o = _band_attention(qs, kk, vv, valid, _sink_column(sinks_ref, hk, Ld))
            for g in range(GQA_G):
                h = hk * GQA_G + g
                mix_ref[rs, D_CONV + h * HEAD_DIM:D_CONV + (h + 1) * HEAD_DIM] = \
                    o[g * Ld:(g + 1) * Ld, :].astype(BF16)


def _even_sample(proj, Bd, Ld, row0, cw, qg2, kg2, sinks, cache_conv, cache_k, cache_v):
    GB = SAMPLE_GB
    rows = GB * Ld
    rb0 = row0 // rows
    cos, sin = _rope_tables(PAST_LEN + jnp.arange(Ld, dtype=jnp.int32))
    full = lambda shape: pl.BlockSpec(shape, lambda i, *_: tuple(0 for _ in shape))
    grid_spec = pltpu.PrefetchScalarGridSpec(
        num_scalar_prefetch=1,
        grid=(Bd // GB,),
        in_specs=[pl.BlockSpec((rows, D_IN_EVEN), lambda i, s: (rb0 + i, 0)),
                  full((Ld, LANES)), full((Ld, LANES)),
                  full((CONV_W, D_CONV)), full((1, LANES)), full((1, LANES)),
                  pl.BlockSpec((GB, CONV_W - 1, D_CONV), lambda i, s: (i, 0, 0)),
                  pl.BlockSpec((GB, WINDOW, LANES), lambda i, s: (i, 0, 0)),
                  pl.BlockSpec((GB, WINDOW, LANES), lambda i, s: (i, 0, 0))],
        out_specs=[pl.BlockSpec((rows, D_MODEL), lambda i, s: (i, 0)),
                   pl.BlockSpec((GB, WINDOW, LANES), lambda i, s: (i, 0, 0)),
                   pl.BlockSpec((GB, WINDOW, LANES), lambda i, s: (i, 0, 0)),
                   pl.BlockSpec((GB, CONV_W - 1, D_CONV), lambda i, s: (i, 0, 0))],
    )
    return pl.pallas_call(
        _even_sample_kernel,
        grid_spec=grid_spec,
        out_shape=[jax.ShapeDtypeStruct((Bd * Ld, D_MODEL), BF16),
                   jax.ShapeDtypeStruct((Bd, WINDOW, LANES), F32),
                   jax.ShapeDtypeStruct((Bd, WINDOW, LANES), F32),
                   jax.ShapeDtypeStruct((Bd, CONV_W - 1, D_CONV), F32)],
        compiler_params=_cparams(("arbitrary",)),
        name="even_sample",
    )(sinks, proj, cos, sin, cw, qg2, kg2, cache_conv, cache_k, cache_v)


def _cumsum_rows(x):
    C = x.shape[0]
    row = lax.broadcasted_iota(jnp.int32, x.shape, 0)
    s = 1
    while s < C:
        x = x + jnp.where(row >= s, pltpu.roll(x, s, 0), 0.0)
        s *= 2
    return x


def _group_ref(b, m, row):
    C, D = b.shape
    if 2 * m >= SUBLANES:
        n = C // (2 * m)
        b3 = b.reshape(n, 2 * m, D)
        return jnp.broadcast_to(b3[:, m - 1:m, :], (n, 2 * m, D)).reshape(C, D)
    r = row & (2 * m - 1)
    out = b
    for off in range(2 * m):
        if off == m - 1:
            continue
        shift = (off - (m - 1)) % C
        out = jnp.where(r == off, pltpu.roll(b, shift, 0), out)
    return out


def _gla_chunk(qz, fz, v, lb, S):
    C = qz.shape[0]
    q = qz * _sigmoid(qz)
    f = lb + (1.0 - lb) * _sigmoid(fz)
    k = 1.0 - f
    b = _cumsum_rows(jnp.log(f))
    vb = v.astype(BF16)

    inter = jnp.dot((q * jnp.exp(b)).astype(BF16), S.astype(BF16), preferred_element_type=F32)

    row = lax.broadcasted_iota(jnp.int32, (C, C_DK), 0)
    si = lax.broadcasted_iota(jnp.int32, (C, C), 0)
    ti = lax.broadcasted_iota(jnp.int32, (C, C), 1)
    nt = (((1,), (1,)), ((), ()))
    if C == SUBLANES:
        st = jnp.zeros((C, C), F32)
        for t in range(C):
            p = jnp.where(row <= t, q[t:t + 1, :] * k * jnp.exp(jnp.minimum(b[t:t + 1, :] - b, 0.0)), 0.0)
            st = jnp.where(ti == t, jnp.sum(p, axis=-1, keepdims=True), st)
        m = 0
    else:
        st = jnp.where(si == ti, lax.dot_general(k.astype(BF16), q.astype(BF16), nt,
                                                 preferred_element_type=F32), 0.0)
        m = C // 2
    while m >= 1:
        rho = _group_ref(b, m, row)
        upper = (row & m) != 0
        d = b - rho
        x = (jnp.where(upper, q, k) * jnp.exp(jnp.where(upper, d, -d))).astype(BF16)
        gram = lax.dot_general(x, x, nt, preferred_element_type=F32)
        sh = (2 * m).bit_length() - 1
        pair = ((si >> sh) == (ti >> sh)) & ((si & m) == 0) & ((ti & m) != 0)
        st = st + jnp.where(pair, gram, 0.0)
        m //= 2

    intra = lax.dot_general(st.astype(BF16), vb, (((0,), (0,)), ((), ())),
                            preferred_element_type=F32)

    b_last = b[C - 1:C, :]
    eye = (lax.broadcasted_iota(jnp.int32, (C_DK, C_DK), 0)
           == lax.broadcasted_iota(jnp.int32, (C_DK, C_DK), 1))
    dcol = jnp.sum(jnp.where(eye, jnp.broadcast_to(jnp.exp(b_last), (C_DK, C_DK)), 0.0),
                   axis=-1, keepdims=True)
    kd = k * jnp.exp(b_last - b)
    S_new = dcol * S + lax.dot_general(kd.astype(BF16), vb, (((0,), (0,)), ((), ())),
                                       preferred_element_type=F32)
    return inter + intra, S_new


def _gated_out(o, og, gz):
    ms = jnp.mean(o * o, axis=-1, keepdims=True)
    return (o * lax.rsqrt(ms + EPS) * og) * (gz * _sigmoid(gz))


def _hgrn_prompt_kernel(q_ref, f_ref, i_ref, g_ref, lb_ref, og_ref, o_ref, s_ref, S):
    c = pl.program_id(2)

    @pl.when(c == 0)
    def _():
        S[...] = jnp.zeros_like(S)

    lb = lb_ref[...]
    og = og_ref[...]

    Sv = S[...]
    for n in range(GLA_STEP // GLA_CHUNK):
        rs = slice(n * GLA_CHUNK, (n + 1) * GLA_CHUNK)
        o, Sv = _gla_chunk(q_ref[rs, :], f_ref[rs, :], i_ref[rs, :], lb, Sv)
        o_ref[rs, :] = _gated_out(o, og, g_ref[rs, :]).astype(BF16)
    S[...] = Sv

    @pl.when(c == pl.num_programs(2) - 1)
    def _():
        s_ref[0, 0] = S[...]


def _hgrn_prompt(proj, B, L, T, lb, og):
    ns = L // GLA_STEP
    H = C_HEADS
    col = lambda off: pl.BlockSpec((GLA_STEP, LANES), lambda b, h, c: (b * ns + c, off + h))
    return pl.pallas_call(
        _hgrn_prompt_kernel,
        grid=(B, H, ns),
        in_specs=[col(0), col(H), col(2 * H), col(3 * H),
                  pl.BlockSpec((1, LANES), lambda b, h, c: (0, h)),
                  pl.BlockSpec((1, LANES), lambda b, h, c: (0, 0))],
        out_specs=[pl.BlockSpec((GLA_STEP, LANES), lambda b, h, c: (b * ns + c, h)),
                   pl.BlockSpec((1, 1, C_DK, C_DV), lambda b, h, c: (b, h, 0, 0))],
        out_shape=[jax.ShapeDtypeStruct((B * L, D_MODEL), BF16),
                   jax.ShapeDtypeStruct((B, H, C_DK, C_DV), F32)],
        scratch_shapes=[pltpu.VMEM((C_DK, C_DV), F32)],
        compiler_params=_cparams(("arbitrary", "arbitrary", "arbitrary")),
        name="hgrn_prompt",
    )(proj, proj, proj, proj, lb, og)


def _hgrn_sample_kernel(p_ref, lb_ref, og_ref, s0_ref, o_ref, s_ref):
    Ld = p_ref.shape[0] // SAMPLE_GB
    og = og_ref[...]
    for bb in range(SAMPLE_GB):
        rs = slice(bb * Ld, (bb + 1) * Ld)
        for h in range(C_HEADS):
            cs = lambda part: slice((part * C_HEADS + h) * LANES, (part * C_HEADS + h + 1) * LANES)
            o, S_new = _gla_chunk(p_ref[rs, cs(0)], p_ref[rs, cs(1)], p_ref[rs, cs(2)],
                                  lb_ref[:, h * LANES:(h + 1) * LANES], s0_ref[bb, h])
            s_ref[bb, h] = S_new
            o_ref[rs, h * LANES:(h + 1) * LANES] = _gated_out(o, og, p_ref[rs, cs(3)]).astype(BF16)


def _hgrn_sample(proj, Bd, Ld, row0, lb, og, s0):
    GB = SAMPLE_GB
    rows = GB * Ld
    rb0 = row0 // rows
    H = C_HEADS
    return pl.pallas_call(
        _hgrn_sample_kernel,
        grid=(Bd // GB,),
        in_specs=[pl.BlockSpec((rows, 4 * C_HK), lambda i: (rb0 + i, 0)),
                  pl.BlockSpec((1, C_HK), lambda i: (0, 0)),
                  pl.BlockSpec((1, LANES), lambda i: (0, 0)),
                  pl.BlockSpec((GB, H, C_DK, C_DV), lambda i: (i, 0, 0, 0))],
        out_specs=[pl.BlockSpec((rows, D_MODEL), lambda i: (i, 0)),
                   pl.BlockSpec((GB, H, C_DK, C_DV), lambda i: (i, 0, 0, 0))],
        out_shape=[jax.ShapeDtypeStruct((Bd * Ld, D_MODEL), BF16),
                   jax.ShapeDtypeStruct((Bd, H, C_DK, C_DV), F32)],
        compiler_params=_cparams(("arbitrary",)),
        name="hgrn_sample",
    )(proj, lb, og, s0)


def _out_proj_kernel(n_mix, n_x, npt, *refs):
    m = _read_rows(refs[:n_mix], npt)
    x = _read_rows(refs[n_mix:n_mix + n_x], npt)
    w_ref, g_ref, wr_ref, br_ref, x1_ref, h_ref, lg_ref = refs[n_mix + n_x:]
    x1 = x + jnp.dot(m, w_ref[...], preferred_element_type=F32)
    x1_ref[...] = x1
    ms = jnp.mean(x1 * x1, axis=-1, keepdims=True)
    h = x1 * lax.rsqrt(ms + EPS) * g_ref[...]
    h_ref[...] = h
    lg_ref[...] = jnp.dot(h.astype(BF16), wr_ref[...], preferred_element_type=F32) + br_ref[...]


def _out_proj(mix, w, x, g, wr, br, tm):
    K, D = w.shape
    m_specs, m_arrays, npt, T = _row_source(mix, tm)
    x_specs, x_arrays, npt_x, _ = _row_source(x, tm)
    assert npt_x in (0, npt)
    row = lambda n: pl.BlockSpec((tm, n), lambda i: (i, 0))
    full = lambda a, b: pl.BlockSpec((a, b), lambda i: (0, 0))
    return pl.pallas_call(
        functools.partial(_out_proj_kernel, len(m_arrays), len(x_arrays), npt),
        grid=(T // tm,),
        in_specs=m_specs + x_specs + [full(K, D), full(1, D), full(D, LOGIT_W), full(1, LOGIT_W)],
        out_specs=[row(D), row(D), row(LOGIT_W)],
        out_shape=[jax.ShapeDtypeStruct((T, D), F32), jax.ShapeDtypeStruct((T, D), F32),
                   jax.ShapeDtypeStruct((T, LOGIT_W), F32)],
        compiler_params=_cparams(("arbitrary",)),
        name="out_proj",
    )(*m_arrays, *x_arrays, w, g.reshape(1, D), wr, br)


def _route_kernel(lg_ref, slot_ref, gate_ref, cnt_ref, carry, total):
    phase = pl.program_id(0)
    step = pl.program_id(1)
    tm = lg_ref.shape[0]
    lg = lg_ref[...]
    lane = lax.broadcasted_iota(jnp.int32, lg.shape, 1)
    neg = -jnp.inf

    is_g = lane < N_GROUPS
    glog = jnp.where(is_g, lg, neg)
    gmax = jnp.max(glog, axis=-1, keepdims=True)
    g_idx = jnp.min(jnp.where(glog == gmax, lane, LOGIT_W), axis=-1, keepdims=True)
    g_p = 1.0 / jnp.sum(jnp.exp(glog - gmax), axis=-1, keepdims=True)

    e0 = N_GROUPS + g_idx * EXP_PER_GROUP
    in_grp = (lane >= e0) & (lane < e0 + EXP_PER_GROUP)
    elog = jnp.where(in_grp, lg, neg)
    emax = jnp.max(elog, axis=-1, keepdims=True)
    ee = jnp.exp(elog - emax)
    prob = ee / jnp.sum(ee, axis=-1, keepdims=True)
    prob = jnp.where(in_grp, prob, -1.0)
    p1 = jnp.max(prob, axis=-1, keepdims=True)
    i1 = jnp.min(jnp.where(prob == p1, lane, LOGIT_W), axis=-1, keepdims=True)
    prob2 = jnp.where(lane == i1, -1.0, prob)
    p2 = jnp.max(prob2, axis=-1, keepdims=True)
    i2 = jnp.min(jnp.where(prob2 == p2, lane, LOGIT_W), axis=-1, keepdims=True)
    psum = p1 + p2
    w1 = p1 / psum * g_p
    w2 = p2 / psum * g_p

    oh1 = lane == (i1 - N_GROUPS)
    oh2 = lane == (i2 - N_GROUPS)
    both = jnp.where(oh1 | oh2, 1.0, 0.0)

    @pl.when((phase == 0) & (step == 0))
    def _():
        total[...] = jnp.zeros_like(total)

    @pl.when(phase == 0)
    def _():
        total[...] += jnp.sum(both, axis=0, keepdims=True)

    @pl.when((phase == 1) & (step == 0))
    def _():
        carry[...] = jnp.zeros_like(carry)

    @pl.when(phase == 1)
    def _():
        cnt = total[...]
        padded = jnp.floor((cnt + (MOE_BLOCK - 1)) * (1.0 / MOE_BLOCK)) * MOE_BLOCK
        ri = lax.broadcasted_iota(jnp.int32, (LOGIT_W, LOGIT_W), 0)
        ci = lax.broadcasted_iota(jnp.int32, (LOGIT_W, LOGIT_W), 1)
        pcol = jnp.sum(jnp.where(ri == ci, jnp.broadcast_to(padded, (LOGIT_W, LOGIT_W)), 0.0),
                       axis=1, keepdims=True)
        pstart = jnp.sum(jnp.where(ri < ci, pcol, 0.0), axis=0, keepdims=True)
        tr = lax.broadcasted_iota(jnp.int32, (tm, tm), 0)
        tc = lax.broadcasted_iota(jnp.int32, (tm, tm), 1)
        tri = jnp.where(tc < tr, 1.0, 0.0).astype(BF16)
        before = jnp.dot(tri, both.astype(BF16), preferred_element_type=F32) + carry[...] + pstart
        s1 = jnp.sum(jnp.where(oh1, before, 0.0), axis=-1, keepdims=True)
        s2 = jnp.sum(jnp.where(oh2, before, 0.0), axis=-1, keepdims=True)
        carry[...] += jnp.sum(both, axis=0, keepdims=True)
        slot_ref[...] = jnp.where(lane == 0, s1, jnp.where(lane == 1, s2, 0.0)).astype(jnp.int32)
        gate_ref[...] = jnp.where(lane == 0, w1, jnp.where(lane == 1, w2, 0.0))
        cnt_ref[...] = cnt


def _route(logits):
    T = logits.shape[0]
    tm = ROUTE_TM
    return pl.pallas_call(
        _route_kernel,
        grid=(2, T // tm),
        in_specs=[pl.BlockSpec((tm, LOGIT_W), lambda p, i: (i, 0))],
        out_specs=[pl.BlockSpec((tm, LOGIT_W), lambda p, i: (i * p, 0)),
                   pl.BlockSpec((tm, LOGIT_W), lambda p, i: (i * p, 0)),
                   pl.BlockSpec((1, LOGIT_W), lambda p, i: (0, 0))],
        out_shape=[jax.ShapeDtypeStruct((T, LOGIT_W), jnp.int32),
                   jax.ShapeDtypeStruct((T, LOGIT_W), F32),
                   jax.ShapeDtypeStruct((1, LOGIT_W), F32)],
        scratch_shapes=[pltpu.VMEM((1, LOGIT_W), F32), pltpu.VMEM((1, LOGIT_W), F32)],
        compiler_params=_cparams(("arbitrary", "arbitrary")),
        name="route",
    )(logits)


DISPATCH_TM = 512


def _dispatch_kernel(T, slot_ref, h_ref, xs_in_ref, xs_ref, sem):
    del xs_in_ref
    base = pl.program_id(0) * DISPATCH_TM

    def copy(r8, u, k):
        r = r8 + u
        return pltpu.make_async_copy(h_ref.at[pl.ds(r, 1)],
                                     xs_ref.at[pl.ds(slot_ref[k * T + base + r], 1)], sem)

    def start(j, c):
        r8 = pl.multiple_of(j * SUBLANES, SUBLANES)
        for u in range(SUBLANES):
            copy(r8, u, 0).start()
            copy(r8, u, 1).start()
        return c

    def wait(j, c):
        r8 = pl.multiple_of(j * SUBLANES, SUBLANES)
        for u in range(SUBLANES):
            copy(r8, u, 0).wait()
            copy(r8, u, 1).wait()
        return c

    lax.fori_loop(0, DISPATCH_TM // SUBLANES, start, 0)
    lax.fori_loop(0, DISPATCH_TM // SUBLANES, wait, 0)


def _dispatch(slots, h, n_slots):
    T, D = h.shape
    xs0 = jnp.zeros((n_slots, D), F32)
    grid_spec = pltpu.PrefetchScalarGridSpec(
        num_scalar_prefetch=1,
        grid=(T // DISPATCH_TM,),
        in_specs=[pl.BlockSpec((DISPATCH_TM, D), lambda i, s: (i, 0)),
                  pl.BlockSpec(memory_space=pl.ANY)],
        out_specs=pl.BlockSpec(memory_space=pl.ANY),
        scratch_shapes=[pltpu.SemaphoreType.DMA(())],
    )
    return pl.pallas_call(
        functools.partial(_dispatch_kernel, T),
        grid_spec=grid_spec,
        out_shape=jax.ShapeDtypeStruct((n_slots, D), F32),
        input_output_aliases={2: 0},
        compiler_params=_cparams(("arbitrary",)),
        name="moe_dispatch",
    )(slots, h, xs0)


def _ffn_kernel(nblk, meta_ref, x_ref, w1_ref, w3_ref, w2_ref, y_ref, wb1, wb3, wb2):
    b = pl.program_id(0)
    used = b < meta_ref[nblk]
    new_expert = (b == 0) | (meta_ref[b] != meta_ref[jnp.maximum(b - 1, 0)])

    @pl.when(used & new_expert)
    def _():
        wb1[...] = w1_ref[0, 0].astype(BF16)
        wb3[...] = w3_ref[0, 0].astype(BF16)
        wb2[...] = w2_ref[0, 0].astype(BF16)

    @pl.when(used)
    def _():
        x = x_ref[...].astype(BF16)
        a = jnp.dot(x, wb1[...], preferred_element_type=F32)
        c = jnp.dot(x, wb3[...], preferred_element_type=F32)
        hid = (a * _sigmoid(a)) * c
        y_ref[...] = jnp.dot(hid.astype(BF16), wb2[...], preferred_element_type=F32)

    @pl.when(jnp.logical_not(used))
    def _():
        y_ref[...] = jnp.zeros_like(y_ref)


def _ffn(meta, xs, w1, w3, w2, layer):
    n_slots, D = xs.shape
    nblk = n_slots // MOE_BLOCK
    grid_spec = pltpu.PrefetchScalarGridSpec(
        num_scalar_prefetch=1,
        grid=(nblk,),
        in_specs=[pl.BlockSpec((MOE_BLOCK, D), lambda b, e: (b, 0)),
                  pl.BlockSpec((1, 1, D, D_EXPERT), lambda b, e: (layer, e[b], 0, 0)),
                  pl.BlockSpec((1, 1, D, D_EXPERT), lambda b, e: (layer, e[b], 0, 0)),
                  pl.BlockSpec((1, 1, D_EXPERT, D), lambda b, e: (layer, e[b], 0, 0))],
        out_specs=pl.BlockSpec((MOE_BLOCK, D), lambda b, e: (b, 0)),
        scratch_shapes=[pltpu.VMEM((D, D_EXPERT), BF16), pltpu.VMEM((D, D_EXPERT), BF16),
                        pltpu.VMEM((D_EXPERT, D), BF16)],
    )
    return pl.pallas_call(
        functools.partial(_ffn_kernel, nblk),
        grid_spec=grid_spec,
        out_shape=jax.ShapeDtypeStruct((n_slots, D), F32),
        compiler_params=_cparams(("arbitrary",)),
        name="moe_ffn",
    )(meta, xs, w1, w3, w2)


COMBINE_TM = 256


def _combine_kernel(T, npt, slot_ref, x_ref, gate_ref, ys_ref, *refs):
    out_refs, (buf, sem) = refs[:-2], refs[-2:]
    i = pl.program_id(0)
    base = i * COMBINE_TM

    def copy(r8, u, k):
        r = r8 + u
        return pltpu.make_async_copy(ys_ref.at[pl.ds(slot_ref[k * T + base + r], 1)],
                                     buf.at[k, pl.ds(r, 1)], sem)

    def start(j, c):
        r8 = pl.multiple_of(j * SUBLANES, SUBLANES)
        for u in range(SUBLANES):
            copy(r8, u, 0).start()
            copy(r8, u, 1).start()
        return c

    def wait(j, c):
        r8 = pl.multiple_of(j * SUBLANES, SUBLANES)
        for u in range(SUBLANES):
            copy(r8, u, 0).wait()
            copy(r8, u, 1).wait()
        return c

    lax.fori_loop(0, COMBINE_TM // SUBLANES, start, 0)
    lax.fori_loop(0, COMBINE_TM // SUBLANES, wait, 0)
    g = gate_ref[...]
    y = x_ref[...] + (buf[0] * g[:, 0:1] + buf[1] * g[:, 1:2])
    if npt == 0:
        out_refs[0][...] = y
    else:
        @pl.when(i < npt)
        def _():
            out_refs[0][...] = y

        @pl.when(i >= npt)
        def _():
            out_refs[1][...] = y


def _combine(slots, x1, gates, ys, split_rows=0):
    T, D = x1.shape
    tm = COMBINE_TM
    npt = split_rows // tm
    if npt == 0:
        out_specs = pl.BlockSpec((tm, D), lambda i, s: (i, 0))
        out_shape = jax.ShapeDtypeStruct((T, D), F32)
    else:
        out_specs = [pl.BlockSpec((tm, D), lambda i, s: (jnp.minimum(i, npt - 1), 0)),
                     pl.BlockSpec((tm, D), lambda i, s: (jnp.maximum(i - npt, 0), 0))]
        out_shape = [jax.ShapeDtypeStruct((split_rows, D), F32),
                     jax.ShapeDtypeStruct((T - split_rows, D), F32)]
    grid_spec = pltpu.PrefetchScalarGridSpec(
        num_scalar_prefetch=1,
        grid=(T // tm,),
        in_specs=[pl.BlockSpec((tm, D), lambda i, s: (i, 0)),
                  pl.BlockSpec((tm, LOGIT_W), lambda i, s: (i, 0)),
                  pl.BlockSpec(memory_space=pl.ANY)],
        out_specs=out_specs,
        scratch_shapes=[pltpu.VMEM((2, tm, D), F32), pltpu.SemaphoreType.DMA(())],
    )
    return pl.pallas_call(
        functools.partial(_combine_kernel, T, npt),
        grid_spec=grid_spec,
        out_shape=out_shape,
        compiler_params=_cparams(("arbitrary",)),
        name="moe_combine",
    )(slots, x1, gates, ys)


def _moe(x1, h, logits, w1, w3, w2, layer, split_rows=0):
    T = x1.shape[0]
    slot_l, gate_l, cnt = _route(logits)
    slots = jnp.concatenate([slot_l[:, 0], slot_l[:, 1]])
    n_blocks = -(-(2 * T) // MOE_BLOCK) + N_EXPERTS
    counts = cnt[0, :N_EXPERTS].astype(jnp.int32)
    pend = jnp.cumsum((counts + MOE_BLOCK - 1) // MOE_BLOCK * MOE_BLOCK)
    starts = jnp.arange(n_blocks, dtype=jnp.int32) * MOE_BLOCK
    blk_e = jnp.minimum(jnp.sum((pend[None, :] <= starts[:, None]).astype(jnp.int32), axis=1),
                        N_EXPERTS - 1)
    meta = jnp.concatenate([blk_e, pend[-1:] // MOE_BLOCK]).astype(jnp.int32)
    xs = _dispatch(slots, h, n_blocks * MOE_BLOCK)
    ys = _ffn(meta, xs, w1, w3, w2, layer)
    return _combine(slots, x1, gate_l, ys, split_rows)


def _router_weights(w_grp, b_grp, w_exp, b_exp):
    D = w_grp.shape[0]
    pad = LOGIT_W - N_GROUPS - N_EXPERTS
    wr = jnp.concatenate([w_grp, w_exp, jnp.zeros((D, pad), F32)], axis=1).astype(BF16)
    br = jnp.concatenate([b_grp, b_exp, jnp.zeros((pad,), F32)]).reshape(1, LOGIT_W)
    return wr, br


def kernel(x_prompt, x_sample, cache_conv, cache_k, cache_v, state_hgrn, norm_mix, norm_ffn,
           ev_w_in, ev_conv, ev_q_norm, ev_k_norm, ev_sinks, ev_w_out,
           od_w_in, od_lb, od_o_norm, od_w_out,
           moe_w_grp, moe_b_grp, moe_w_exp, moe_b_exp, moe_w1, moe_w3, moe_w2):
    B, L, D = x_prompt.shape
    Bd, Ld, _ = x_sample.shape
    Tp = B * L
    T = Tp + Bd * Ld
    x = (x_prompt.reshape(Tp, D), x_sample.reshape(Bd * Ld, D))

    proj = _norm_matmul(x, norm_mix[0], ev_w_in[0].astype(BF16), 512)
    qg2 = jnp.tile(ev_q_norm[0], 2).reshape(1, LANES)
    kg2 = jnp.tile(ev_k_norm[0], 2).reshape(1, LANES)
    mix_p, k_p, v_p, conv_p = _even_prompt(proj, B, L, T, ev_conv[0], qg2, kg2, ev_sinks[0])
    mix_s, k_s, v_s, conv_s = _even_sample(
        proj, Bd, Ld, Tp, ev_conv[0], qg2, kg2, ev_sinks[0], cache_conv[0],
        cache_k[0].reshape(Bd, WINDOW, LANES), cache_v[0].reshape(Bd, WINDOW, LANES))
    wr, br = _router_weights(moe_w_grp[0], moe_b_grp[0], moe_w_exp[0], moe_b_exp[0])
    x1, h, logits = _out_proj((mix_p, mix_s), ev_w_out[0].astype(BF16), x, norm_ffn[0], wr, br, 512)
    x = _moe(x1, h, logits, moe_w1, moe_w3, moe_w2, 0)

    lb_all = jnp.cumsum(jax.nn.softmax(od_lb.astype(F32), axis=0), axis=0)
    lb = (lb_all - lb_all[0])[1].reshape(1, C_HK)
    og = od_o_norm[0].reshape(1, C_DV)
    proj = _norm_matmul(x, norm_mix[1], od_w_in[0].astype(BF16), 256)
    o_p, s_p = _hgrn_prompt(proj, B, L, T, lb, og)
    o_s, s_s = _hgrn_sample(proj, Bd, Ld, Tp, lb, og, state_hgrn[0])
    wr, br = _router_weights(moe_w_grp[1], moe_b_grp[1], moe_w_exp[1], moe_b_exp[1])
    x1, h, logits = _out_proj((o_p, o_s), od_w_out[0].astype(BF16), x, norm_ffn[1], wr, br, 512)
    y_p, y_s = _moe(x1, h, logits, moe_w1, moe_w3, moe_w2, 1, split_rows=Tp)

    y_prompt = y_p.reshape(B, L, D)
    y_sample = y_s.reshape(Bd, Ld, D)
    return (y_prompt, y_sample,
            conv_p[None], k_p.reshape(1, B, WINDOW, N_KV, HEAD_DIM), v_p.reshape(1, B, WINDOW, N_KV, HEAD_DIM),
            s_p[None],
            conv_s[None], k_s.reshape(1, Bd, WINDOW, N_KV, HEAD_DIM), v_s.reshape(1, Bd, WINDOW, N_KV, HEAD_DIM),
            s_s[None])
```

```python
import functools

import numpy as np
import jax
import jax.numpy as jnp
from jax import lax
from jax.experimental import pallas as pl
from jax.experimental.pallas import tpu as pltpu

F32 = jnp.float32
BF16 = jnp.bfloat16

D_MODEL = 1024
PAST_LEN = 16384
D_CONV = 512
CONV_W = 3
HEAD_DIM = 64
N_Q = 8
N_KV = 2
GQA_G = N_Q // N_KV
WINDOW = 128
ROPE_THETA = 10000.0
D_IN_EVEN = 3 * D_CONV + (N_Q + 2 * N_KV) * HEAD_DIM
C_HEADS = 8
C_DK = 128
C_DV = 128
C_HK = C_HEADS * C_DK
N_GROUPS = 8
EXP_PER_GROUP = 8
N_EXPERTS = N_GROUPS * EXP_PER_GROUP
D_EXPERT = 512
MOE_BLOCK = 256
EPS = 1e-6

LANES = 128
SUBLANES = 8
VMEM_LIMIT = 56 * 1024 * 1024

GLA_CHUNK = 128
GLA_STEP = 512
SAMPLE_GB = 2
ROUTE_TM = 512
LOGIT_W = 128


def _cparams(sem):
    return pltpu.CompilerParams(dimension_semantics=sem, vmem_limit_bytes=VMEM_LIMIT)


def _sigmoid(x):
    return 1.0 / (1.0 + jnp.exp(-x))


def _silu(x):
    return x * (0.5 * jnp.tanh(0.5 * x) + 0.5)


def _row_source(src, tm):
    if isinstance(src, tuple):
        a, b = src
        n = a.shape[1]
        npt = a.shape[0] // tm
        specs = [pl.BlockSpec((tm, n), lambda i, *_: (jnp.minimum(i, npt - 1), 0)),
                 pl.BlockSpec((tm, n), lambda i, *_: (jnp.maximum(i - npt, 0), 0))]
        return specs, [a, b], npt, a.shape[0] + b.shape[0]
    return [pl.BlockSpec((tm, src.shape[1]), lambda i, *_: (i, 0))], [src], 0, src.shape[0]


def _read_rows(refs, npt):
    if len(refs) == 2:
        return jnp.where(pl.program_id(0) < npt, refs[0][...], refs[1][...])
    return refs[0][...]


def _norm_matmul_kernel(n_src, npt, *refs):
    x = _read_rows(refs[:n_src], npt)
    g_ref, w_ref, o_ref = refs[n_src:]
    ms = jnp.mean(x * x, axis=-1, keepdims=True)
    h = (x * lax.rsqrt(ms + EPS) * g_ref[...]).astype(BF16)
    o_ref[...] = jnp.dot(h, w_ref[...], preferred_element_type=F32)


def _norm_matmul(x, g, w, tm):
    D, N = w.shape
    specs, arrays, npt, T = _row_source(x, tm)
    return pl.pallas_call(
        functools.partial(_norm_matmul_kernel, len(arrays), npt),
        grid=(T // tm,),
        in_specs=specs + [pl.BlockSpec((1, D), lambda i: (0, 0)),
                          pl.BlockSpec((D, N), lambda i: (0, 0))],
        out_specs=pl.BlockSpec((tm, N), lambda i: (i, 0)),
        out_shape=jax.ShapeDtypeStruct((T, N), F32),
        compiler_params=_cparams(("arbitrary",)),
        name="norm_matmul",
    )(*arrays, g.reshape(1, D), w)


def _rope_tables(pos):
    inv = ROPE_THETA ** (-jnp.arange(0, HEAD_DIM, 2, dtype=F32) / HEAD_DIM)
    ang = pos.astype(F32)[:, None] * inv[None, :]
    cos = jnp.cos(ang)
    sin = jnp.sin(ang)
    return (jnp.concatenate([cos, cos, cos, cos], axis=1),
            jnp.concatenate([-sin, sin, -sin, sin], axis=1))


def _headnorm_rope(x, g2, cos, sin):
    lane = lax.broadcasted_iota(jnp.int32, x.shape, 1)
    lo = lane < HEAD_DIM
    x2 = x * x
    s_lo = jnp.sum(jnp.where(lo, x2, 0.0), axis=-1, keepdims=True)
    s_hi = jnp.sum(jnp.where(lo, 0.0, x2), axis=-1, keepdims=True)
    ms = jnp.where(lo, s_lo, s_hi) * (1.0 / HEAD_DIM)
    y = x * lax.rsqrt(ms + EPS) * g2
    first_half = (lane & (HEAD_DIM // 2)) == 0
    swapped = jnp.where(first_half, pltpu.roll(y, LANES - HEAD_DIM // 2, 1),
                        pltpu.roll(y, HEAD_DIM // 2, 1))
    return y * cos + swapped * sin


def _gated_conv(gb, u, c2, c1, cw):
    R = u.shape[0]
    row = lax.broadcasted_iota(jnp.int32, u.shape, 0)
    u1 = jnp.where(row == 0, c1, pltpu.roll(u, 1, 0))
    u2 = jnp.where(row == 0, c2, jnp.where(row == 1, c1, pltpu.roll(u, 2, 0)))
    del R
    return gb * (cw[0:1, :] * u2 + cw[1:2, :] * u1 + cw[2:3, :] * u)


def _band_attention(qs, kk, vv, valid, sinkv):
    s = lax.dot_general(qs.astype(BF16), kk.astype(BF16), (((1,), (1,)), ((), ())),
                        preferred_element_type=F32) * (HEAD_DIM ** -0.5)
    s = jnp.where(valid, s, -jnp.inf)
    m = jnp.maximum(jnp.max(s, axis=-1, keepdims=True), sinkv)
    e = jnp.exp(s - m)
    den = jnp.sum(e, axis=-1, keepdims=True) + jnp.exp(sinkv - m)
    p = e / den
    return jnp.dot(p.astype(BF16), vv.astype(BF16), preferred_element_type=F32)


def _sink_column(sinks_ref, hk, rows_per_head):
    R = GQA_G * rows_per_head
    row = lax.broadcasted_iota(jnp.int32, (R, 1), 0)
    col = jnp.full((R, 1), sinks_ref[hk * GQA_G + GQA_G - 1], F32)
    for j in range(GQA_G - 2, -1, -1):
        col = jnp.where(row < (j + 1) * rows_per_head, sinks_ref[hk * GQA_G + j], col)
    return col


def _even_prompt_kernel(sinks_ref, proj_ref, cos_ref, sin_ref, cw_ref, qg_ref, kg_ref,
                        mix_ref, kl_ref, vl_ref, cl_ref, kprev, vprev, ucar):
    blk = pl.program_id(1)
    W = WINDOW

    @pl.when(blk == 0)
    def _():
        kprev[...] = jnp.zeros_like(kprev)
        vprev[...] = jnp.zeros_like(vprev)
        ucar[...] = jnp.zeros_like(ucar)

    gb = proj_ref[:, 0:D_CONV]
    u = proj_ref[:, D_CONV:2 * D_CONV] * proj_ref[:, 2 * D_CONV:3 * D_CONV]
    car = ucar[...]
    a_out = _gated_conv(gb, u, car[0:1, :], car[1:2, :], cw_ref[...])
    ucar[0:2, :] = u[W - 2:W, :]
    mix_ref[:, 0:D_CONV] = a_out.astype(BF16)

    cos = cos_ref[...]
    sin = sin_ref[...]
    q0 = 3 * D_CONV
    k0 = q0 + N_Q * HEAD_DIM
    v0 = k0 + N_KV * HEAD_DIM
    k_r = _headnorm_rope(proj_ref[:, k0:k0 + LANES], kg_ref[...], cos, sin)
    v_r = proj_ref[:, v0:v0 + LANES]
    q_r = [_headnorm_rope(proj_ref[:, q0 + LANES * j:q0 + LANES * (j + 1)], qg_ref[...], cos, sin)
           for j in range(N_Q * HEAD_DIM // LANES)]
    k_p = kprev[...]
    v_p = vprev[...]

    R = GQA_G * W
    i = lax.broadcasted_iota(jnp.int32, (R, 2 * W), 0) & (W - 1)
    j = lax.broadcasted_iota(jnp.int32, (R, 2 * W), 1)
    diff = i + W - j
    valid = (diff >= 0) & (diff <= W) & ((blk > 0) | (j >= W))

    for hk in range(N_KV):
        ls = slice(hk * HEAD_DIM, (hk + 1) * HEAD_DIM)
        kk = jnp.concatenate([k_p[:, ls], k_r[:, ls]], axis=0)
        vv = jnp.concatenate([v_p[:, ls], v_r[:, ls]], axis=0)
        heads = []
        for g in range(GQA_G):
            h = hk * GQA_G + g
            tile = q_r[h // 2]
            heads.append(tile[:, (h % 2) * HEAD_DIM:(h % 2 + 1) * HEAD_DIM])
        qs = jnp.concatenate(heads, axis=0)
        o = _band_attention(qs, kk, vv, valid, _sink_column(sinks_ref, hk, W))
        for g in range(GQA_G):
            h = hk * GQA_G + g
            mix_ref[:, D_CONV + h * HEAD_DIM:D_CONV + (h + 1) * HEAD_DIM] = \
                o[g * W:(g + 1) * W, :].astype(BF16)

    kprev[...] = k_r
    vprev[...] = v_r

    @pl.when(blk == pl.num_programs(1) - 1)
    def _():
        kl_ref[0] = k_r
        vl_ref[0] = v_r
        cl_ref[0] = u[W - 2:W, :]


def _even_prompt(proj, B, L, T, cw, qg2, kg2, sinks):
    nb = L // WINDOW
    cos, sin = _rope_tables(jnp.arange(L, dtype=jnp.int32))
    full = lambda shape: pl.BlockSpec(shape, lambda b, i, *_: tuple(0 for _ in shape))
    grid_spec = pltpu.PrefetchScalarGridSpec(
        num_scalar_prefetch=1,
        grid=(B, nb),
        in_specs=[pl.BlockSpec((WINDOW, D_IN_EVEN), lambda b, i, s: (b * nb + i, 0)),
                  pl.BlockSpec((WINDOW, LANES), lambda b, i, s: (i, 0)),
                  pl.BlockSpec((WINDOW, LANES), lambda b, i, s: (i, 0)),
                  full((CONV_W, D_CONV)), full((1, LANES)), full((1, LANES))],
        out_specs=[pl.BlockSpec((WINDOW, D_MODEL), lambda b, i, s: (b * nb + i, 0)),
                   pl.BlockSpec((1, WINDOW, LANES), lambda b, i, s: (b, 0, 0)),
                   pl.BlockSpec((1, WINDOW, LANES), lambda b, i, s: (b, 0, 0)),
                   pl.BlockSpec((1, CONV_W - 1, D_CONV), lambda b, i, s: (b, 0, 0))],
        scratch_shapes=[pltpu.VMEM((WINDOW, LANES), F32), pltpu.VMEM((WINDOW, LANES), F32),
                        pltpu.VMEM((SUBLANES, D_CONV), F32)],
    )
    return pl.pallas_call(
        _even_prompt_kernel,
        grid_spec=grid_spec,
        out_shape=[jax.ShapeDtypeStruct((B * L, D_MODEL), BF16),
                   jax.ShapeDtypeStruct((B, WINDOW, LANES), F32),
                   jax.ShapeDtypeStruct((B, WINDOW, LANES), F32),
                   jax.ShapeDtypeStruct((B, CONV_W - 1, D_CONV), F32)],
        compiler_params=_cparams(("arbitrary", "arbitrary")),
        name="even_prompt",
    )(sinks, proj, cos, sin, cw, qg2, kg2)


def _even_sample_kernel(sinks_ref, proj_ref, cos_ref, sin_ref, cw_ref, qg_ref, kg_ref,
                        cc_ref, ck_ref, cv_ref, mix_ref, ko_ref, vo_ref, co_ref):
    W = WINDOW
    Ld = cos_ref.shape[0]
    cos = cos_ref[...]
    sin = sin_ref[...]
    q0 = 3 * D_CONV
    k0 = q0 + N_Q * HEAD_DIM
    v0 = k0 + N_KV * HEAD_DIM
    R = GQA_G * Ld
    i = lax.broadcasted_iota(jnp.int32, (R, 2 * W), 0) % Ld
    j = lax.broadcasted_iota(jnp.int32, (R, 2 * W), 1)
    diff = i + W - j
    valid = (diff >= 0) & (diff <= W)
    zpad = jnp.zeros((W - Ld, HEAD_DIM), F32)

    for bb in range(SAMPLE_GB):
        rs = slice(bb * Ld, (bb + 1) * Ld)
        gb = proj_ref[rs, 0:D_CONV]
        u = proj_ref[rs, D_CONV:2 * D_CONV] * proj_ref[rs, 2 * D_CONV:3 * D_CONV]
        car = cc_ref[bb]
        a_out = _gated_conv(gb, u, car[0:1, :], car[1:2, :], cw_ref[...])
        co_ref[bb] = u[Ld - 2:Ld, :]
        mix_ref[rs, 0:D_CONV] = a_out.astype(BF16)

        k_r = _headnorm_rope(proj_ref[rs, k0:k0 + LANES], kg_ref[...], cos, sin)
        v_r = proj_ref[rs, v0:v0 + LANES]
        q_r = [_headnorm_rope(proj_ref[rs, q0 + LANES * t:q0 + LANES * (t + 1)], qg_ref[...], cos, sin)
               for t in range(N_Q * HEAD_DIM // LANES)]
        k_c = ck_ref[bb]
        v_c = cv_ref[bb]
        ko_ref[bb, 0:W - Ld, :] = k_c[Ld:W, :]
        ko_ref[bb, W - Ld:W, :] = k_r
        vo_ref[bb, 0:W - Ld, :] = v_c[Ld:W, :]
        vo_ref[bb, W - Ld:W, :] = v_r

        for hk in range(N_KV):
            ls = slice(hk * HEAD_DIM, (hk + 1) * HEAD_DIM)
            kk = jnp.concatenate([k_c[:, ls], k_r[:, ls], zpad], axis=0)
            vv = jnp.concatenate([v_c[:, ls], v_r[:, ls], zpad], axis=0)
            heads = []
            for g in range(GQA_G):
                h = hk * GQA_G + g
                tile = q_r[h // 2]
                heads.append(tile[:, (h % 2) * HEAD_DIM:(h % 2 + 1) * HEAD_DIM])
            qs = jnp.concatenate(heads, axis=0)
            o = _band_attention(qs, kk, vv, valid, _sink_column(sinks_ref, hk, Ld))
            for g in range(GQA_G):
                h = hk * GQA_G + g
                mix_ref[rs, D_CONV + h * HEAD_DIM:D_CONV + (h + 1) * HEAD_DIM] = \
                    o[g * Ld:(g + 1) * Ld, :].astype(BF16)


def _even_sample(proj, Bd, Ld, row0, cw, qg2, kg2, sinks, cache_conv, cache_k, cache_v):
    GB = SAMPLE_GB
    rows = GB * Ld
    rb0 = row0 // rows
    cos, sin = _rope_tables(PAST_LEN + jnp.arange(Ld, dtype=jnp.int32))
    full = lambda shape: pl.BlockSpec(shape, lambda i, *_: tuple(0 for _ in shape))
    grid_spec = pltpu.PrefetchScalarGridSpec(
        num_scalar_prefetch=1,
        grid=(Bd // GB,),
        in_specs=[pl.BlockSpec((rows, D_IN_EVEN), lambda i, s: (rb0 + i, 0)),
                  full((Ld, LANES)), full((Ld, LANES)),
                  full((CONV_W, D_CONV)), full((1, LANES)), full((1, LANES)),
                  pl.BlockSpec((GB, CONV_W - 1, D_CONV), lambda i, s: (i, 0, 0)),
                  pl.BlockSpec((GB, WINDOW, LANES), lambda i, s: (i, 0, 0)),
                  pl.BlockSpec((GB, WINDOW, LANES), lambda i, s: (i, 0, 0))],
        out_specs=[pl.BlockSpec((rows, D_MODEL), lambda i, s: (i, 0)),
                   pl.BlockSpec((GB, WINDOW, LANES), lambda i, s: (i, 0, 0)),
                   pl.BlockSpec((GB, WINDOW, LANES), lambda i, s: (i, 0, 0)),
                   pl.BlockSpec((GB, CONV_W - 1, D_CONV), lambda i, s: (i, 0, 0))],
    )
    return pl.pallas_call(
        _even_sample_kernel,
        grid_spec=grid_spec,
        out_shape=[jax.ShapeDtypeStruct((Bd * Ld, D_MODEL), BF16),
                   jax.ShapeDtypeStruct((Bd, WINDOW, LANES), F32),
                   jax.ShapeDtypeStruct((Bd, WINDOW, LANES), F32),
                   jax.ShapeDtypeStruct((Bd, CONV_W - 1, D_CONV), F32)],
        compiler_params=_cparams(("arbitrary",)),
        name="even_sample",
    )(sinks, proj, cos, sin, cw, qg2, kg2, cache_conv, cache_k, cache_v)


def _cumsum_rows(x):
    C = x.shape[0]
    row = lax.broadcasted_iota(jnp.int32, x.shape, 0)
    s = 1
    while s < C:
        x = x + jnp.where(row >= s, pltpu.roll(x, s, 0), 0.0)
        s *= 2
    return x


def _group_ref(b, m, row):
    C, D = b.shape
    if 2 * m >= SUBLANES:
        n = C // (2 * m)
        b3 = b.reshape(n, 2 * m, D)
        return jnp.broadcast_to(b3[:, m - 1:m, :], (n, 2 * m, D)).reshape(C, D)
    r = row & (2 * m - 1)
    out = b
    for off in range(2 * m):
        if off == m - 1:
            continue
        shift = (off - (m - 1)) % C
        out = jnp.where(r == off, pltpu.roll(b, shift, 0), out)
    return out


def _gla_consts(C):
    s = np.arange(C)[:, None]
    t = np.arange(C)[None, :]
    masks = [s == t]
    m = C // 2
    while m >= 1:
        masks.append((s // (2 * m) == t // (2 * m)) & ((s & m) == 0) & ((t & m) != 0))
        m //= 2
    return jnp.asarray(np.stack(masks).astype(np.float32))


def _gla_chunk(qz, fz, v, lb, S, masks=None):
    C = qz.shape[0]
    q = _silu(qz)
    f = lb + (1.0 - lb) * _sigmoid(fz)
    k = 1.0 - f
    b = _cumsum_rows(jnp.log(f))
    vb = v.astype(BF16)

    inter = jnp.dot((q * jnp.exp(b)).astype(BF16), S.astype(BF16), preferred_element_type=F32)

    row = lax.broadcasted_iota(jnp.int32, (C, C_DK), 0)
    si = lax.broadcasted_iota(jnp.int32, (C, C), 0)
    ti = lax.broadcasted_iota(jnp.int32, (C, C), 1)
    nt = (((1,), (1,)), ((), ()))
    if C == SUBLANES:
        st = jnp.zeros((C, C), F32)
        for t in range(C):
            p = jnp.where(row <= t, q[t:t + 1, :] * k * jnp.exp(jnp.minimum(b[t:t + 1, :] - b, 0.0)), 0.0)
            st = jnp.where(ti == t, jnp.sum(p, axis=-1, keepdims=True), st)
        m = 0
    else:
        st = masks[0] * lax.dot_general(k.astype(BF16), q.astype(BF16), nt,
                                        preferred_element_type=F32)
        m = C // 2
    level = 1
    while m >= 1:
        rho = _group_ref(b, m, row)
        upper = (row & m) != 0
        d = b - rho
        x = (jnp.where(upper, q, k) * jnp.exp(jnp.where(upper, d, -d))).astype(BF16)
        st = st + masks[level] * lax.dot_general(x, x, nt, preferred_element_type=F32)
        level += 1
        m //= 2

    intra = lax.dot_general(st.astype(BF16), vb, (((0,), (0,)), ((), ())),
                            preferred_element_type=F32)

    b_last = b[C - 1:C, :]
    eye = (lax.broadcasted_iota(jnp.int32, (C_DK, C_DK), 0)
           == lax.broadcasted_iota(jnp.int32, (C_DK, C_DK), 1))
    dcol = jnp.sum(jnp.where(eye, jnp.broadcast_to(jnp.exp(b_last), (C_DK, C_DK)), 0.0),
                   axis=-1, keepdims=True)
    kd = k * jnp.exp(b_last - b)
    S_new = dcol * S + lax.dot_general(kd.astype(BF16), vb, (((0,), (0,)), ((), ())),
                                       preferred_element_type=F32)
    return inter + intra, S_new


def _gated_out(o, og, gz):
    ms = jnp.mean(o * o, axis=-1, keepdims=True)
    return (o * lax.rsqrt(ms + EPS) * og) * _silu(gz)


def _hgrn_prompt_kernel(q_ref, f_ref, i_ref, g_ref, lb_ref, og_ref, mask_ref, o_ref, s_ref, S):
    c = pl.program_id(2)

    @pl.when(c == 0)
    def _():
        S[...] = jnp.zeros_like(S)

    lb = lb_ref[...]
    og = og_ref[...]

    Sv = S[...]
    for n in range(GLA_STEP // GLA_CHUNK):
        rs = slice(n * GLA_CHUNK, (n + 1) * GLA_CHUNK)
        o, Sv = _gla_chunk(q_ref[rs, :], f_ref[rs, :], i_ref[rs, :], lb, Sv, mask_ref)
        o_ref[rs, :] = _gated_out(o, og, g_ref[rs, :]).astype(BF16)
    S[...] = Sv

    @pl.when(c == pl.num_programs(2) - 1)
    def _():
        s_ref[0, 0] = S[...]


def _hgrn_prompt(proj, B, L, T, lb, og):
    ns = L // GLA_STEP
    H = C_HEADS
    col = lambda off: pl.BlockSpec((GLA_STEP, LANES), lambda b, h, c: (b * ns + c, off + h))
    masks = _gla_consts(GLA_CHUNK)
    return pl.pallas_call(
        _hgrn_prompt_kernel,
        grid=(B, H, ns),
        in_specs=[col(0), col(H), col(2 * H), col(3 * H),
                  pl.BlockSpec((1, LANES), lambda b, h, c: (0, h)),
                  pl.BlockSpec((1, LANES), lambda b, h, c: (0, 0)),
                  pl.BlockSpec(masks.shape, lambda b, h, c: (0, 0, 0))],
        out_specs=[pl.BlockSpec((GLA_STEP, LANES), lambda b, h, c: (b * ns + c, h)),
                   pl.BlockSpec((1, 1, C_DK, C_DV), lambda b, h, c: (b, h, 0, 0))],
        out_shape=[jax.ShapeDtypeStruct((B * L, D_MODEL), BF16),
                   jax.ShapeDtypeStruct((B, H, C_DK, C_DV), F32)],
        scratch_shapes=[pltpu.VMEM((C_DK, C_DV), F32)],
        compiler_params=_cparams(("arbitrary", "arbitrary", "arbitrary")),
        name="hgrn_prompt",
    )(proj, proj, proj, proj, lb, og, masks)


def _hgrn_sample_kernel(p_ref, lb_ref, og_ref, s0_ref, o_ref, s_ref):
    Ld = p_ref.shape[0] // SAMPLE_GB
    og = og_ref[...]
    for bb in range(SAMPLE_GB):
        rs = slice(bb * Ld, (bb + 1) * Ld)
        for h in range(C_HEADS):
            cs = lambda part: slice((part * C_HEADS + h) * LANES, (part * C_HEADS + h + 1) * LANES)
            o, S_new = _gla_chunk(p_ref[rs, cs(0)], p_ref[rs, cs(1)], p_ref[rs, cs(2)],
                                  lb_ref[:, h * LANES:(h + 1) * LANES], s0_ref[bb, h])
            s_ref[bb, h] = S_new
            o_ref[rs, h * LANES:(h + 1) * LANES] = _gated_out(o, og, p_ref[rs, cs(3)]).astype(BF16)


def _hgrn_sample(proj, Bd, Ld, row0, lb, og, s0):
    GB = SAMPLE_GB
    rows = GB * Ld
    rb0 = row0 // rows
    H = C_HEADS
    return pl.pallas_call(
        _hgrn_sample_kernel,
        grid=(Bd // GB,),
        in_specs=[pl.BlockSpec((rows, 4 * C_HK), lambda i: (rb0 + i, 0)),
                  pl.BlockSpec((1, C_HK), lambda i: (0, 0)),
                  pl.BlockSpec((1, LANES), lambda i: (0, 0)),
                  pl.BlockSpec((GB, H, C_DK, C_DV), lambda i: (i, 0, 0, 0))],
        out_specs=[pl.BlockSpec((rows, D_MODEL), lambda i: (i, 0)),
                   pl.BlockSpec((GB, H, C_DK, C_DV), lambda i: (i, 0, 0, 0))],
        out_shape=[jax.ShapeDtypeStruct((Bd * Ld, D_MODEL), BF16),
                   jax.ShapeDtypeStruct((Bd, H, C_DK, C_DV), F32)],
        compiler_params=_cparams(("arbitrary",)),
        name="hgrn_sample",
    )(proj, lb, og, s0)


def _out_proj_kernel(n_mix, n_x, npt, *refs):
    m = _read_rows(refs[:n_mix], npt)
    x = _read_rows(refs[n_mix:n_mix + n_x], npt)
    w_ref, g_ref, wr_ref, br_ref, x1_ref, h_ref, lg_ref = refs[n_mix + n_x:]
    x1 = x + jnp.dot(m, w_ref[...], preferred_element_type=F32)
    x1_ref[...] = x1
    ms = jnp.mean(x1 * x1, axis=-1, keepdims=True)
    h = x1 * lax.rsqrt(ms + EPS) * g_ref[...]
    h_ref[...] = h
    lg_ref[...] = jnp.dot(h.astype(BF16), wr_ref[...], preferred_element_type=F32) + br_ref[...]


def _out_proj(mix, w, x, g, wr, br, tm):
    K, D = w.shape
    m_specs, m_arrays, npt, T = _row_source(mix, tm)
    x_specs, x_arrays, npt_x, _ = _row_source(x, tm)
    assert npt_x in (0, npt)
    row = lambda n: pl.BlockSpec((tm, n), lambda i: (i, 0))
    full = lambda a, b: pl.BlockSpec((a, b), lambda i: (0, 0))
    return pl.pallas_call(
        functools.partial(_out_proj_kernel, len(m_arrays), len(x_arrays), npt),
        grid=(T // tm,),
        in_specs=m_specs + x_specs + [full(K, D), full(1, D), full(D, LOGIT_W), full(1, LOGIT_W)],
        out_specs=[row(D), row(D), row(LOGIT_W)],
        out_shape=[jax.ShapeDtypeStruct((T, D), F32), jax.ShapeDtypeStruct((T, D), F32),
                   jax.ShapeDtypeStruct((T, LOGIT_W), F32)],
        compiler_params=_cparams(("arbitrary",)),
        name="out_proj",
    )(*m_arrays, *x_arrays, w, g.reshape(1, D), wr, br)


def _route_kernel(lg_ref, slot_ref, gate_ref, cnt_ref, carry, total):
    phase = pl.program_id(0)
    step = pl.program_id(1)
    tm = lg_ref.shape[0]
    lg = lg_ref[...]
    lane = lax.broadcasted_iota(jnp.int32, lg.shape, 1)
    neg = -jnp.inf

    is_g = lane < N_GROUPS
    glog = jnp.where(is_g, lg, neg)
    gmax = jnp.max(glog, axis=-1, keepdims=True)
    g_idx = jnp.min(jnp.where(glog == gmax, lane, LOGIT_W), axis=-1, keepdims=True)
    g_p = 1.0 / jnp.sum(jnp.exp(glog - gmax), axis=-1, keepdims=True)

    e0 = N_GROUPS + g_idx * EXP_PER_GROUP
    in_grp = (lane >= e0) & (lane < e0 + EXP_PER_GROUP)
    elog = jnp.where(in_grp, lg, neg)
    emax = jnp.max(elog, axis=-1, keepdims=True)
    ee = jnp.exp(elog - emax)
    prob = ee / jnp.sum(ee, axis=-1, keepdims=True)
    prob = jnp.where(in_grp, prob, -1.0)
    p1 = jnp.max(prob, axis=-1, keepdims=True)
    i1 = jnp.min(jnp.where(prob == p1, lane, LOGIT_W), axis=-1, keepdims=True)
    prob2 = jnp.where(lane == i1, -1.0, prob)
    p2 = jnp.max(prob2, axis=-1, keepdims=True)
    i2 = jnp.min(jnp.where(prob2 == p2, lane, LOGIT_W), axis=-1, keepdims=True)
    psum = p1 + p2
    w1 = p1 / psum * g_p
    w2 = p2 / psum * g_p

    oh1 = lane == (i1 - N_GROUPS)
    oh2 = lane == (i2 - N_GROUPS)
    both = jnp.where(oh1 | oh2, 1.0, 0.0)

    @pl.when((phase == 0) & (step == 0))
    def _():
        total[...] = jnp.zeros_like(total)

    @pl.when(phase == 0)
    def _():
        total[...] += jnp.sum(both, axis=0, keepdims=True)

    @pl.when((phase == 1) & (step == 0))
    def _():
        carry[...] = jnp.zeros_like(carry)

    @pl.when(phase == 1)
    def _():
        cnt = total[...]
        padded = jnp.floor((cnt + (MOE_BLOCK - 1)) * (1.0 / MOE_BLOCK)) * MOE_BLOCK
        ri = lax.broadcasted_iota(jnp.int32, (LOGIT_W, LOGIT_W), 0)
        ci = lax.broadcasted_iota(jnp.int32, (LOGIT_W, LOGIT_W), 1)
        pcol = jnp.sum(jnp.where(ri == ci, jnp.broadcast_to(padded, (LOGIT_W, LOGIT_W)), 0.0),
                       axis=1, keepdims=True)
        pstart = jnp.sum(jnp.where(ri < ci, pcol, 0.0), axis=0, keepdims=True)
        tr = lax.broadcasted_iota(jnp.int32, (tm, tm), 0)
        tc = lax.broadcasted_iota(jnp.int32, (tm, tm), 1)
        tri = jnp.where(tc < tr, 1.0, 0.0).astype(BF16)
        before = jnp.dot(tri, both.astype(BF16), preferred_element_type=F32) + carry[...] + pstart
        s1 = jnp.sum(jnp.where(oh1, before, 0.0), axis=-1, keepdims=True)
        s2 = jnp.sum(jnp.where(oh2, before, 0.0), axis=-1, keepdims=True)
        carry[...] += jnp.sum(both, axis=0, keepdims=True)
        slot_ref[...] = jnp.where(lane == 0, s1, jnp.where(lane == 1, s2, 0.0)).astype(jnp.int32)
        gate_ref[...] = jnp.where(lane == 0, w1, jnp.where(lane == 1, w2, 0.0))
        cnt_ref[...] = cnt


def _route(logits):
    T = logits.shape[0]
    tm = ROUTE_TM
    return pl.pallas_call(
        _route_kernel,
        grid=(2, T // tm),
        in_specs=[pl.BlockSpec((tm, LOGIT_W), lambda p, i: (i, 0))],
        out_specs=[pl.BlockSpec((tm, LOGIT_W), lambda p, i: (i * p, 0)),
                   pl.BlockSpec((tm, LOGIT_W), lambda p, i: (i * p, 0)),
                   pl.BlockSpec((1, LOGIT_W), lambda p, i: (0, 0))],
        out_shape=[jax.ShapeDtypeStruct((T, LOGIT_W), jnp.int32),
                   jax.ShapeDtypeStruct((T, LOGIT_W), F32),
                   jax.ShapeDtypeStruct((1, LOGIT_W), F32)],
        scratch_shapes=[pltpu.VMEM((1, LOGIT_W), F32), pltpu.VMEM((1, LOGIT_W), F32)],
        compiler_params=_cparams(("arbitrary", "arbitrary")),
        name="route",
    )(logits)


def _ffn_kernel(nblk, meta_ref, tok_ref, h_ref, w1_ref, w3_ref, w2_ref, y_ref,
                xbuf, sems, wb1, wb3, wb2):
    b = pl.program_id(0)
    n_used = meta_ref[nblk]
    used = b < n_used
    new_expert = (b == 0) | (meta_ref[b] != meta_ref[jnp.maximum(b - 1, 0)])

    def row_copy(blk, buf, r):
        return pltpu.make_async_copy(h_ref.at[pl.ds(tok_ref[blk * MOE_BLOCK + r], 1)],
                                     xbuf.at[buf, pl.ds(r, 1)], sems.at[buf])

    def gather(blk, buf, wait):
        def body(j, c):
            r8 = pl.multiple_of(j * SUBLANES, SUBLANES)
            for u in range(SUBLANES):
                cp = row_copy(blk, buf, r8 + u)
                if wait:
                    cp.wait()
                else:
                    cp.start()
            return c
        lax.fori_loop(0, MOE_BLOCK // SUBLANES, body, 0)

    @pl.when(used & (b == 0))
    def _():
        gather(b, 0, False)

    @pl.when(b + 1 < n_used)
    def _():
        gather(b + 1, (b + 1) % 2, False)

    @pl.when(used)
    def _():
        gather(b, b % 2, True)

    @pl.when(used & new_expert)
    def _():
        wb1[...] = w1_ref[0, 0].astype(BF16)
        wb3[...] = w3_ref[0, 0].astype(BF16)
        wb2[...] = w2_ref[0, 0].astype(BF16)

    @pl.when(used)
    def _():
        x = xbuf[b % 2].astype(BF16)
        a = jnp.dot(x, wb1[...], preferred_element_type=F32)
        c = jnp.dot(x, wb3[...], preferred_element_type=F32)
        hid = (a * _sigmoid(a)) * c
        y_ref[...] = jnp.dot(hid.astype(BF16), wb2[...], preferred_element_type=F32)

    @pl.when(jnp.logical_not(used))
    def _():
        y_ref[...] = jnp.zeros_like(y_ref)


def _ffn(meta, slot_tok, h, w1, w3, w2, layer):
    D = h.shape[1]
    n_slots = slot_tok.shape[0]
    nblk = n_slots // MOE_BLOCK
    grid_spec = pltpu.PrefetchScalarGridSpec(
        num_scalar_prefetch=2,
        grid=(nblk,),
        in_specs=[pl.BlockSpec(memory_space=pl.ANY),
                  pl.BlockSpec((1, 1, D, D_EXPERT), lambda b, e, t: (layer, e[b], 0, 0)),
                  pl.BlockSpec((1, 1, D, D_EXPERT), lambda b, e, t: (layer, e[b], 0, 0)),
                  pl.BlockSpec((1, 1, D_EXPERT, D), lambda b, e, t: (layer, e[b], 0, 0))],
        out_specs=pl.BlockSpec((MOE_BLOCK, D), lambda b, e, t: (b, 0)),
        scratch_shapes=[pltpu.VMEM((2, MOE_BLOCK, D), F32), pltpu.SemaphoreType.DMA((2,)),
                        pltpu.VMEM((D, D_EXPERT), BF16), pltpu.VMEM((D, D_EXPERT), BF16),
                        pltpu.VMEM((D_EXPERT, D), BF16)],
    )
    return pl.pallas_call(
        functools.partial(_ffn_kernel, nblk),
        grid_spec=grid_spec,
        out_shape=jax.ShapeDtypeStruct((n_slots, D), F32),
        compiler_params=_cparams(("arbitrary",)),
        name="moe_ffn",
    )(meta, slot_tok, h, w1, w3, w2)


COMBINE_TM = 256


def _combine_kernel(T, npt, slot_ref, x_ref, gate_ref, ys_ref, *refs):
    out_refs, (buf, sem) = refs[:-2], refs[-2:]
    i = pl.program_id(0)
    base = i * COMBINE_TM

    def copy(r8, u, k):
        r = r8 + u
        return pltpu.make_async_copy(ys_ref.at[pl.ds(slot_ref[k * T + base + r], 1)],
                                     buf.at[k, pl.ds(r, 1)], sem)

    def start(j, c):
        r8 = pl.multiple_of(j * SUBLANES, SUBLANES)
        for u in range(SUBLANES):
            copy(r8, u, 0).start()
            copy(r8, u, 1).start()
        return c

    def wait(j, c):
        r8 = pl.multiple_of(j * SUBLANES, SUBLANES)
        for u in range(SUBLANES):
            copy(r8, u, 0).wait()
            copy(r8, u, 1).wait()
        return c

    lax.fori_loop(0, COMBINE_TM // SUBLANES, start, 0)
    lax.fori_loop(0, COMBINE_TM // SUBLANES, wait, 0)
    g = gate_ref[...]
    y = x_ref[...] + (buf[0] * g[:, 0:1] + buf[1] * g[:, 1:2])
    if npt == 0:
        out_refs[0][...] = y
    else:
        @pl.when(i < npt)
        def _():
            out_refs[0][...] = y

        @pl.when(i >= npt)
        def _():
            out_refs[1][...] = y


def _combine(slots, x1, gates, ys, split_rows=0):
    T, D = x1.shape
    tm = COMBINE_TM
    npt = split_rows // tm
    if npt == 0:
        out_specs = pl.BlockSpec((tm, D), lambda i, s: (i, 0))
        out_shape = jax.ShapeDtypeStruct((T, D), F32)
    else:
        out_specs = [pl.BlockSpec((tm, D), lambda i, s: (jnp.minimum(i, npt - 1), 0)),
                     pl.BlockSpec((tm, D), lambda i, s: (jnp.maximum(i - npt, 0), 0))]
        out_shape = [jax.ShapeDtypeStruct((split_rows, D), F32),
                     jax.ShapeDtypeStruct((T - split_rows, D), F32)]
    grid_spec = pltpu.PrefetchScalarGridSpec(
        num_scalar_prefetch=1,
        grid=(T // tm,),
        in_specs=[pl.BlockSpec((tm, D), lambda i, s: (i, 0)),
                  pl.BlockSpec((tm, LOGIT_W), lambda i, s: (i, 0)),
                  pl.BlockSpec(memory_space=pl.ANY)],
        out_specs=out_specs,
        scratch_shapes=[pltpu.VMEM((2, tm, D), F32), pltpu.SemaphoreType.DMA(())],
    )
    return pl.pallas_call(
        functools.partial(_combine_kernel, T, npt),
        grid_spec=grid_spec,
        out_shape=out_shape,
        compiler_params=_cparams(("arbitrary",)),
        name="moe_combine",
    )(slots, x1, gates, ys)


def _moe(x1, h, logits, w1, w3, w2, layer, split_rows=0):
    T = x1.shape[0]
    slot_l, gate_l, cnt = _route(logits)
    slots = jnp.concatenate([slot_l[:, 0], slot_l[:, 1]])
    n_blocks = -(-(2 * T) // MOE_BLOCK) + N_EXPERTS
    counts = cnt[0, :N_EXPERTS].astype(jnp.int32)
    pend = jnp.cumsum((counts + MOE_BLOCK - 1) // MOE_BLOCK * MOE_BLOCK)
    starts = jnp.arange(n_blocks, dtype=jnp.int32) * MOE_BLOCK
    blk_e = jnp.minimum(jnp.sum((pend[None, :] <= starts[:, None]).astype(jnp.int32), axis=1),
                        N_EXPERTS - 1)
    meta = jnp.concatenate([blk_e, pend[-1:] // MOE_BLOCK]).astype(jnp.int32)
    tok = jnp.arange(T, dtype=jnp.int32)
    slot_tok = jnp.zeros((n_blocks * MOE_BLOCK,), jnp.int32).at[slots].set(
        jnp.concatenate([tok, tok]), unique_indices=True)
    ys = _ffn(meta, slot_tok, h, w1, w3, w2, layer)
    return _combine(slots, x1, gate_l, ys, split_rows)


def _router_weights(w_grp, b_grp, w_exp, b_exp):
    D = w_grp.shape[0]
    pad = LOGIT_W - N_GROUPS - N_EXPERTS
    wr = jnp.concatenate([w_grp, w_exp, jnp.zeros((D, pad), F32)], axis=1).astype(BF16)
    br = jnp.concatenate([b_grp, b_exp, jnp.zeros((pad,), F32)]).reshape(1, LOGIT_W)
    return wr, br


def kernel(x_prompt, x_sample, cache_conv, cache_k, cache_v, state_hgrn, norm_mix, norm_ffn,
           ev_w_in, ev_conv, ev_q_norm, ev_k_norm, ev_sinks, ev_w_out,
           od_w_in, od_lb, od_o_norm, od_w_out,
           moe_w_grp, moe_b_grp, moe_w_exp, moe_b_exp, moe_w1, moe_w3, moe_w2):
    B, L, D = x_prompt.shape
    Bd, Ld, _ = x_sample.shape
    Tp = B * L
    T = Tp + Bd * Ld
    x = (x_prompt.reshape(Tp, D), x_sample.reshape(Bd * Ld, D))

    proj = _norm_matmul(x, norm_mix[0], ev_w_in[0].astype(BF16), 512)
    qg2 = jnp.tile(ev_q_norm[0], 2).reshape(1, LANES)
    kg2 = jnp.tile(ev_k_norm[0], 2).reshape(1, LANES)
    mix_p, k_p, v_p, conv_p = _even_prompt(proj, B, L, T, ev_conv[0], qg2, kg2, ev_sinks[0])
    mix_s, k_s, v_s, conv_s = _even_sample(
        proj, Bd, Ld, Tp, ev_conv[0], qg2, kg2, ev_sinks[0], cache_conv[0],
        cache_k[0].reshape(Bd, WINDOW, LANES), cache_v[0].reshape(Bd, WINDOW, LANES))
    wr, br = _router_weights(moe_w_grp[0], moe_b_grp[0], moe_w_exp[0], moe_b_exp[0])
    x1, h, logits = _out_proj((mix_p, mix_s), ev_w_out[0].astype(BF16), x, norm_ffn[0], wr, br, 512)
    x = _moe(x1, h, logits, moe_w1, moe_w3, moe_w2, 0)

    lb_all = jnp.cumsum(jax.nn.softmax(od_lb.astype(F32), axis=0), axis=0)
    lb = (lb_all - lb_all[0])[1].reshape(1, C_HK)
    og = od_o_norm[0].reshape(1, C_DV)
    proj = _norm_matmul(x, norm_mix[1], od_w_in[0].astype(BF16), 256)
    o_p, s_p = _hgrn_prompt(proj, B, L, T, lb, og)
    o_s, s_s = _hgrn_sample(proj, Bd, Ld, Tp, lb, og, state_hgrn[0])
    wr, br = _router_weights(moe_w_grp[1], moe_b_grp[1], moe_w_exp[1], moe_b_exp[1])
    x1, h, logits = _out_proj((o_p, o_s), od_w_out[0].astype(BF16), x, norm_ffn[1], wr, br, 512)
    y_p, y_s = _moe(x1, h, logits, moe_w1, moe_w3, moe_w2, 1, split_rows=Tp)

    y_prompt = y_p.reshape(B, L, D)
    y_sample = y_s.reshape(Bd, Ld, D)
    return (y_prompt, y_sample,
            conv_p[None], k_p.reshape(1, B, WINDOW, N_KV, HEAD_DIM), v_p.reshape(1, B, WINDOW, N_KV, HEAD_DIM),
            s_p[None],
            conv_s[None], k_s.reshape(1, Bd, WINDOW, N_KV, HEAD_DIM), v_s.reshape(1, Bd, WINDOW, N_KV, HEAD_DIM),
            s_s[None])
```

```python
import functools

import numpy as np
import jax
import jax.numpy as jnp
from jax import lax
from jax.experimental import pallas as pl
from jax.experimental.pallas import tpu as pltpu

F32 = jnp.float32
BF16 = jnp.bfloat16

D_MODEL = 1024
PAST_LEN = 16384
D_CONV = 512
CONV_W = 3
HEAD_DIM = 64
N_Q = 8
N_KV = 2
GQA_G = N_Q // N_KV
WINDOW = 128
ROPE_THETA = 10000.0
D_IN_EVEN = 3 * D_CONV + (N_Q + 2 * N_KV) * HEAD_DIM
C_HEADS = 8
C_DK = 128
C_DV = 128
C_HK = C_HEADS * C_DK
N_GROUPS = 8
EXP_PER_GROUP = 8
N_EXPERTS = N_GROUPS * EXP_PER_GROUP
D_EXPERT = 512
MOE_BLOCK = 256
EPS = 1e-6

LANES = 128
SUBLANES = 8
VMEM_LIMIT = 56 * 1024 * 1024

GLA_CHUNK = 128
GLA_STEP = 512
SAMPLE_GB = 2
ROUTE_TM = 512
LOGIT_W = 128


def _cparams(sem):
    return pltpu.CompilerParams(dimension_semantics=sem, vmem_limit_bytes=VMEM_LIMIT)


def _sigmoid(x):
    return 1.0 / (1.0 + jnp.exp(-x))


def _silu(x):
    return x * (0.5 * jnp.tanh(0.5 * x) + 0.5)


def _row_source(src, tm):
    if isinstance(src, tuple):
        a, b = src
        n = a.shape[1]
        npt = a.shape[0] // tm
        specs = [pl.BlockSpec((tm, n), lambda i, *_: (jnp.minimum(i, npt - 1), 0)),
                 pl.BlockSpec((tm, n), lambda i, *_: (jnp.maximum(i - npt, 0), 0))]
        return specs, [a, b], npt, a.shape[0] + b.shape[0]
    return [pl.BlockSpec((tm, src.shape[1]), lambda i, *_: (i, 0))], [src], 0, src.shape[0]


def _read_rows(refs, npt):
    if len(refs) == 2:
        return jnp.where(pl.program_id(0) < npt, refs[0][...], refs[1][...])
    return refs[0][...]


def _norm_matmul_kernel(n_src, npt, *refs):
    x = _read_rows(refs[:n_src], npt)
    g_ref, w_ref, o_ref = refs[n_src:]
    ms = jnp.mean(x * x, axis=-1, keepdims=True)
    h = (x * lax.rsqrt(ms + EPS) * g_ref[...]).astype(BF16)
    o_ref[...] = jnp.dot(h, w_ref[...], preferred_element_type=F32)


def _norm_matmul(x, g, w, tm):
    D, N = w.shape
    specs, arrays, npt, T = _row_source(x, tm)
    return pl.pallas_call(
        functools.partial(_norm_matmul_kernel, len(arrays), npt),
        grid=(T // tm,),
        in_specs=specs + [pl.BlockSpec((1, D), lambda i: (0, 0)),
                          pl.BlockSpec((D, N), lambda i: (0, 0))],
        out_specs=pl.BlockSpec((tm, N), lambda i: (i, 0)),
        out_shape=jax.ShapeDtypeStruct((T, N), F32),
        compiler_params=_cparams(("arbitrary",)),
        name="norm_matmul",
    )(*arrays, g.reshape(1, D), w)


def _rope_tables(pos):
    inv = ROPE_THETA ** (-jnp.arange(0, HEAD_DIM, 2, dtype=F32) / HEAD_DIM)
    ang = pos.astype(F32)[:, None] * inv[None, :]
    cos = jnp.cos(ang)
    sin = jnp.sin(ang)
    return (jnp.concatenate([cos, cos, cos, cos], axis=1),
            jnp.concatenate([-sin, sin, -sin, sin], axis=1))


def _headnorm_rope(x, g2, cos, sin):
    lane = lax.broadcasted_iota(jnp.int32, x.shape, 1)
    lo = lane < HEAD_DIM
    x2 = x * x
    s_lo = jnp.sum(jnp.where(lo, x2, 0.0), axis=-1, keepdims=True)
    s_hi = jnp.sum(jnp.where(lo, 0.0, x2), axis=-1, keepdims=True)
    ms = jnp.where(lo, s_lo, s_hi) * (1.0 / HEAD_DIM)
    y = x * lax.rsqrt(ms + EPS) * g2
    first_half = (lane & (HEAD_DIM // 2)) == 0
    swapped = jnp.where(first_half, pltpu.roll(y, LANES - HEAD_DIM // 2, 1),
                        pltpu.roll(y, HEAD_DIM // 2, 1))
    return y * cos + swapped * sin


def _gated_conv(gb, u, c2, c1, cw):
    R = u.shape[0]
    row = lax.broadcasted_iota(jnp.int32, u.shape, 0)
    u1 = jnp.where(row == 0, c1, pltpu.roll(u, 1, 0))
    u2 = jnp.where(row == 0, c2, jnp.where(row == 1, c1, pltpu.roll(u, 2, 0)))
    del R
    return gb * (cw[0:1, :] * u2 + cw[1:2, :] * u1 + cw[2:3, :] * u)


def _band_attention(qs, kk, vv, valid, sinkv):
    s = lax.dot_general(qs.astype(BF16), kk.astype(BF16), (((1,), (1,)), ((), ())),
                        preferred_element_type=F32) * (HEAD_DIM ** -0.5)
    s = jnp.where(valid, s, -jnp.inf)
    m = jnp.maximum(jnp.max(s, axis=-1, keepdims=True), sinkv)
    e = jnp.exp(s - m)
    den = jnp.sum(e, axis=-1, keepdims=True) + jnp.exp(sinkv - m)
    p = e / den
    return jnp.dot(p.astype(BF16), vv.astype(BF16), preferred_element_type=F32)


def _sink_column(sinks_ref, hk, rows_per_head):
    R = GQA_G * rows_per_head
    row = lax.broadcasted_iota(jnp.int32, (R, 1), 0)
    col = jnp.full((R, 1), sinks_ref[hk * GQA_G + GQA_G - 1], F32)
    for j in range(GQA_G - 2, -1, -1):
        col = jnp.where(row < (j + 1) * rows_per_head, sinks_ref[hk * GQA_G + j], col)
    return col


def _even_prompt_kernel(sinks_ref, proj_ref, cos_ref, sin_ref, cw_ref, qg_ref, kg_ref,
                        mix_ref, kl_ref, vl_ref, cl_ref, kprev, vprev, ucar):
    blk = pl.program_id(1)
    W = WINDOW

    @pl.when(blk == 0)
    def _():
        kprev[...] = jnp.zeros_like(kprev)
        vprev[...] = jnp.zeros_like(vprev)
        ucar[...] = jnp.zeros_like(ucar)

    gb = proj_ref[:, 0:D_CONV]
    u = proj_ref[:, D_CONV:2 * D_CONV] * proj_ref[:, 2 * D_CONV:3 * D_CONV]
    car = ucar[...]
    a_out = _gated_conv(gb, u, car[0:1, :], car[1:2, :], cw_ref[...])
    ucar[0:2, :] = u[W - 2:W, :]
    mix_ref[:, 0:D_CONV] = a_out.astype(BF16)

    cos = cos_ref[...]
    sin = sin_ref[...]
    q0 = 3 * D_CONV
    k0 = q0 + N_Q * HEAD_DIM
    v0 = k0 + N_KV * HEAD_DIM
    k_r = _headnorm_rope(proj_ref[:, k0:k0 + LANES], kg_ref[...], cos, sin)
    v_r = proj_ref[:, v0:v0 + LANES]
    q_r = [_headnorm_rope(proj_ref[:, q0 + LANES * j:q0 + LANES * (j + 1)], qg_ref[...], cos, sin)
           for j in range(N_Q * HEAD_DIM // LANES)]
    k_p = kprev[...]
    v_p = vprev[...]

    R = GQA_G * W
    i = lax.broadcasted_iota(jnp.int32, (R, 2 * W), 0) & (W - 1)
    j = lax.broadcasted_iota(jnp.int32, (R, 2 * W), 1)
    diff = i + W - j
    valid = (diff >= 0) & (diff <= W) & ((blk > 0) | (j >= W))

    for hk in range(N_KV):
        ls = slice(hk * HEAD_DIM, (hk + 1) * HEAD_DIM)
        kk = jnp.concatenate([k_p[:, ls], k_r[:, ls]], axis=0)
        vv = jnp.concatenate([v_p[:, ls], v_r[:, ls]], axis=0)
        heads = []
        for g in range(GQA_G):
            h = hk * GQA_G + g
            tile = q_r[h // 2]
            heads.append(tile[:, (h % 2) * HEAD_DIM:(h % 2 + 1) * HEAD_DIM])
        qs = jnp.concatenate(heads, axis=0)
        o = _band_attention(qs, kk, vv, valid, _sink_column(sinks_ref, hk, W))
        for g in range(GQA_G):
            h = hk * GQA_G + g
            mix_ref[:, D_CONV + h * HEAD_DIM:D_CONV + (h + 1) * HEAD_DIM] = \
                o[g * W:(g + 1) * W, :].astype(BF16)

    kprev[...] = k_r
    vprev[...] = v_r

    @pl.when(blk == pl.num_programs(1) - 1)
    def _():
        kl_ref[0] = k_r
        vl_ref[0] = v_r
        cl_ref[0] = u[W - 2:W, :]


def _even_prompt(proj, B, L, T, cw, qg2, kg2, sinks):
    nb = L // WINDOW
    cos, sin = _rope_tables(jnp.arange(L, dtype=jnp.int32))
    full = lambda shape: pl.BlockSpec(shape, lambda b, i, *_: tuple(0 for _ in shape))
    grid_spec = pltpu.PrefetchScalarGridSpec(
        num_scalar_prefetch=1,
        grid=(B, nb),
        in_specs=[pl.BlockSpec((WINDOW, D_IN_EVEN), lambda b, i, s: (b * nb + i, 0)),
                  pl.BlockSpec((WINDOW, LANES), lambda b, i, s: (i, 0)),
                  pl.BlockSpec((WINDOW, LANES), lambda b, i, s: (i, 0)),
                  full((CONV_W, D_CONV)), full((1, LANES)), full((1, LANES))],
        out_specs=[pl.BlockSpec((WINDOW, D_MODEL), lambda b, i, s: (b * nb + i, 0)),
                   pl.BlockSpec((1, WINDOW, LANES), lambda b, i, s: (b, 0, 0)),
                   pl.BlockSpec((1, WINDOW, LANES), lambda b, i, s: (b, 0, 0)),
                   pl.BlockSpec((1, CONV_W - 1, D_CONV), lambda b, i, s: (b, 0, 0))],
        scratch_shapes=[pltpu.VMEM((WINDOW, LANES), F32), pltpu.VMEM((WINDOW, LANES), F32),
                        pltpu.VMEM((SUBLANES, D_CONV), F32)],
    )
    return pl.pallas_call(
        _even_prompt_kernel,
        grid_spec=grid_spec,
        out_shape=[jax.ShapeDtypeStruct((B * L, D_MODEL), BF16),
                   jax.ShapeDtypeStruct((B, WINDOW, LANES), F32),
                   jax.ShapeDtypeStruct((B, WINDOW, LANES), F32),
                   jax.ShapeDtypeStruct((B, CONV_W - 1, D_CONV), F32)],
        compiler_params=_cparams(("arbitrary", "arbitrary")),
        name="even_prompt",
    )(sinks, proj, cos, sin, cw, qg2, kg2)


def _even_sample_kernel(sinks_ref, proj_ref, cos_ref, sin_ref, cw_ref, qg_ref, kg_ref,
                        cc_ref, ck_ref, cv_ref, mix_ref, ko_ref, vo_ref, co_ref):
    W = WINDOW
    Ld = cos_ref.shape[0]
    cos = cos_ref[...]
    sin = sin_ref[...]
    q0 = 3 * D_CONV
    k0 = q0 + N_Q * HEAD_DIM
    v0 = k0 + N_KV * HEAD_DIM
    R = GQA_G * Ld
    i = lax.broadcasted_iota(jnp.int32, (R, 2 * W), 0) % Ld
    j = lax.broadcasted_iota(jnp.int32, (R, 2 * W), 1)
    diff = i + W - j
    valid = (diff >= 0) & (diff <= W)
    zpad = jnp.zeros((W - Ld, HEAD_DIM), F32)

    for bb in range(SAMPLE_GB):
        rs = slice(bb * Ld, (bb + 1) * Ld)
        gb = proj_ref[rs, 0:D_CONV]
        u = proj_ref[rs, D_CONV:2 * D_CONV] * proj_ref[rs, 2 * D_CONV:3 * D_CONV]
        car = cc_ref[bb]
        a_out = _gated_conv(gb, u, car[0:1, :], car[1:2, :], cw_ref[...])
        co_ref[bb] = u[Ld - 2:Ld, :]
        mix_ref[rs, 0:D_CONV] = a_out.astype(BF16)

        k_r = _headnorm_rope(proj_ref[rs, k0:k0 + LANES], kg_ref[...], cos, sin)
        v_r = proj_ref[rs, v0:v0 + LANES]
        q_r = [_headnorm_rope(proj_ref[rs, q0 + LANES * t:q0 + LANES * (t + 1)], qg_ref[...], cos, sin)
               for t in range(N_Q * HEAD_DIM // LANES)]
        k_c = ck_ref[bb]
        v_c = cv_ref[bb]
        ko_ref[bb, 0:W - Ld, :] = k_c[Ld:W, :]
        ko_ref[bb, W - Ld:W, :] = k_r
        vo_ref[bb, 0:W - Ld, :] = v_c[Ld:W, :]
        vo_ref[bb, W - Ld:W, :] = v_r

        for hk in range(N_KV):
            ls = slice(hk * HEAD_DIM, (hk + 1) * HEAD_DIM)
            kk = jnp.concatenate([k_c[:, ls], k_r[:, ls], zpad], axis=0)
            vv = jnp.concatenate([v_c[:, ls], v_r[:, ls], zpad], axis=0)
            heads = []
            for g in range(GQA_G):
                h = hk * GQA_G + g
                tile = q_r[h // 2]
                heads.append(tile[:, (h % 2) * HEAD_DIM:(h % 2 + 1) * HEAD_DIM])
            qs = jnp.concatenate(heads, axis=0)
            o = _band_attention(qs, kk, vv, valid, _sink_column(sinks_ref, hk, Ld))
            for g in range(GQA_G):
                h = hk * GQA_G + g
                mix_ref[rs, D_CONV + h * HEAD_DIM:D_CONV + (h + 1) * HEAD_DIM] = \
                    o[g * Ld:(g + 1) * Ld, :].astype(BF16)


def _even_sample(proj, Bd, Ld, row0, cw, qg2, kg2, sinks, cache_conv, cache_k, cache_v):
    GB = SAMPLE_GB
    rows = GB * Ld
    rb0 = row0 // rows
    cos, sin = _rope_tables(PAST_LEN + jnp.arange(Ld, dtype=jnp.int32))
    full = lambda shape: pl.BlockSpec(shape, lambda i, *_: tuple(0 for _ in shape))
    grid_spec = pltpu.PrefetchScalarGridSpec(
        num_scalar_prefetch=1,
        grid=(Bd // GB,),
        in_specs=[pl.BlockSpec((rows, D_IN_EVEN), lambda i, s: (rb0 + i, 0)),
                  full((Ld, LANES)), full((Ld, LANES)),
                  full((CONV_W, D_CONV)), full((1, LANES)), full((1, LANES)),
                  pl.BlockSpec((GB, CONV_W - 1, D_CONV), lambda i, s: (i, 0, 0)),
                  pl.BlockSpec((GB, WINDOW, LANES), lambda i, s: (i, 0, 0)),
                  pl.BlockSpec((GB, WINDOW, LANES), lambda i, s: (i, 0, 0))],
        out_specs=[pl.BlockSpec((rows, D_MODEL), lambda i, s: (i, 0)),
                   pl.BlockSpec((GB, WINDOW, LANES), lambda i, s: (i, 0, 0)),
                   pl.BlockSpec((GB, WINDOW, LANES), lambda i, s: (i, 0, 0)),
                   pl.BlockSpec((GB, CONV_W - 1, D_CONV), lambda i, s: (i, 0, 0))],
    )
    return pl.pallas_call(
        _even_sample_kernel,
        grid_spec=grid_spec,
        out_shape=[jax.ShapeDtypeStruct((Bd * Ld, D_MODEL), BF16),
                   jax.ShapeDtypeStruct((Bd, WINDOW, LANES), F32),
                   jax.ShapeDtypeStruct((Bd, WINDOW, LANES), F32),
                   jax.ShapeDtypeStruct((Bd, CONV_W - 1, D_CONV), F32)],
        compiler_params=_cparams(("arbitrary",)),
        name="even_sample",
    )(sinks, proj, cos, sin, cw, qg2, kg2, cache_conv, cache_k, cache_v)


def _cumsum_rows(x):
    C = x.shape[0]
    row = lax.broadcasted_iota(jnp.int32, x.shape, 0)
    s = 1
    while s < C:
        x = x + jnp.where(row >= s, pltpu.roll(x, s, 0), 0.0)
        s *= 2
    return x


def _group_ref(b, m, row):
    C, D = b.shape
    if 2 * m >= SUBLANES:
        n = C // (2 * m)
        b3 = b.reshape(n, 2 * m, D)
        return jnp.broadcast_to(b3[:, m - 1:m, :], (n, 2 * m, D)).reshape(C, D)
    r = row & (2 * m - 1)
    out = b
    for off in range(2 * m):
        if off == m - 1:
            continue
        shift = (off - (m - 1)) % C
        out = jnp.where(r == off, pltpu.roll(b, shift, 0), out)
    return out


def _gla_consts(C):
    s = np.arange(C)[:, None]
    t = np.arange(C)[None, :]
    masks = [s == t]
    m = C // 2
    while m >= 1:
        masks.append((s // (2 * m) == t // (2 * m)) & ((s & m) == 0) & ((t & m) != 0))
        m //= 2
    return jnp.asarray(np.stack(masks).astype(np.float32))


def _gla_chunk(qz, fz, v, lb, S, masks=None):
    C = qz.shape[0]
    q = _silu(qz)
    f = lb + (1.0 - lb) * _sigmoid(fz)
    k = 1.0 - f
    b = _cumsum_rows(jnp.log(f))
    vb = v.astype(BF16)

    inter = jnp.dot((q * jnp.exp(b)).astype(BF16), S.astype(BF16), preferred_element_type=F32)

    row = lax.broadcasted_iota(jnp.int32, (C, C_DK), 0)
    si = lax.broadcasted_iota(jnp.int32, (C, C), 0)
    ti = lax.broadcasted_iota(jnp.int32, (C, C), 1)
    nt = (((1,), (1,)), ((), ()))
    if C == SUBLANES:
        st = jnp.zeros((C, C), F32)
        for t in range(C):
            p = jnp.where(row <= t, q[t:t + 1, :] * k * jnp.exp(jnp.minimum(b[t:t + 1, :] - b, 0.0)), 0.0)
            st = jnp.where(ti == t, jnp.sum(p, axis=-1, keepdims=True), st)
        m = 0
    else:
        st = masks[0] * lax.dot_general(k.astype(BF16), q.astype(BF16), nt,
                                        preferred_element_type=F32)
        m = C // 2
    level = 1
    while m >= 1:
        rho = _group_ref(b, m, row)
        upper = (row & m) != 0
        d = b - rho
        x = (jnp.where(upper, q, k) * jnp.exp(jnp.where(upper, d, -d))).astype(BF16)
        st = st + masks[level] * lax.dot_general(x, x, nt, preferred_element_type=F32)
        level += 1
        m //= 2

    intra = lax.dot_general(st.astype(BF16), vb, (((0,), (0,)), ((), ())),
                            preferred_element_type=F32)

    b_last = b[C - 1:C, :]
    eye = (lax.broadcasted_iota(jnp.int32, (C_DK, C_DK), 0)
           == lax.broadcasted_iota(jnp.int32, (C_DK, C_DK), 1))
    dcol = jnp.sum(jnp.where(eye, jnp.broadcast_to(jnp.exp(b_last), (C_DK, C_DK)), 0.0),
                   axis=-1, keepdims=True)
    kd = k * jnp.exp(b_last - b)
    S_new = dcol * S + lax.dot_general(kd.astype(BF16), vb, (((0,), (0,)), ((), ())),
                                       preferred_element_type=F32)
    return inter + intra, S_new


def _gated_out(o, og, gz):
    ms = jnp.mean(o * o, axis=-1, keepdims=True)
    return (o * lax.rsqrt(ms + EPS) * og) * _silu(gz)


def _hgrn_prompt_kernel(q_ref, f_ref, i_ref, g_ref, lb_ref, og_ref, mask_ref, o_ref, s_ref, S):
    c = pl.program_id(2)

    @pl.when(c == 0)
    def _():
        S[...] = jnp.zeros_like(S)

    lb = lb_ref[...]
    og = og_ref[...]

    Sv = S[...]
    for n in range(GLA_STEP // GLA_CHUNK):
        rs = slice(n * GLA_CHUNK, (n + 1) * GLA_CHUNK)
        o, Sv = _gla_chunk(q_ref[rs, :], f_ref[rs, :], i_ref[rs, :], lb, Sv, mask_ref)
        o_ref[rs, :] = _gated_out(o, og, g_ref[rs, :]).astype(BF16)
    S[...] = Sv

    @pl.when(c == pl.num_programs(2) - 1)
    def _():
        s_ref[0, 0] = S[...]


def _hgrn_prompt(proj, B, L, T, lb, og):
    ns = L // GLA_STEP
    H = C_HEADS
    col = lambda off: pl.BlockSpec((GLA_STEP, LANES), lambda b, h, c: (b * ns + c, off + h))
    masks = _gla_consts(GLA_CHUNK)
    return pl.pallas_call(
        _hgrn_prompt_kernel,
        grid=(B, H, ns),
        in_specs=[col(0), col(H), col(2 * H), col(3 * H),
                  pl.BlockSpec((1, LANES), lambda b, h, c: (0, h)),
                  pl.BlockSpec((1, LANES), lambda b, h, c: (0, 0)),
                  pl.BlockSpec(masks.shape, lambda b, h, c: (0, 0, 0))],
        out_specs=[pl.BlockSpec((GLA_STEP, LANES), lambda b, h, c: (b * ns + c, h)),
                   pl.BlockSpec((1, 1, C_DK, C_DV), lambda b, h, c: (b, h, 0, 0))],
        out_shape=[jax.ShapeDtypeStruct((B * L, D_MODEL), BF16),
                   jax.ShapeDtypeStruct((B, H, C_DK, C_DV), F32)],
        scratch_shapes=[pltpu.VMEM((C_DK, C_DV), F32)],
        compiler_params=_cparams(("arbitrary", "arbitrary", "arbitrary")),
        name="hgrn_prompt",
    )(proj, proj, proj, proj, lb, og, masks)


def _hgrn_sample_kernel(p_ref, lb_ref, og_ref, s0_ref, o_ref, s_ref):
    Ld = p_ref.shape[0] // SAMPLE_GB
    og = og_ref[...]
    for bb in range(SAMPLE_GB):
        rs = slice(bb * Ld, (bb + 1) * Ld)
        for h in range(C_HEADS):
            cs = lambda part: slice((part * C_HEADS + h) * LANES, (part * C_HEADS + h + 1) * LANES)
            o, S_new = _gla_chunk(p_ref[rs, cs(0)], p_ref[rs, cs(1)], p_ref[rs, cs(2)],
                                  lb_ref[:, h * LANES:(h + 1) * LANES], s0_ref[bb, h])
            s_ref[bb, h] = S_new
            o_ref[rs, h * LANES:(h + 1) * LANES] = _gated_out(o, og, p_ref[rs, cs(3)]).astype(BF16)


def _hgrn_sample(proj, Bd, Ld, row0, lb, og, s0):
    GB = SAMPLE_GB
    rows = GB * Ld
    rb0 = row0 // rows
    H = C_HEADS
    return pl.pallas_call(
        _hgrn_sample_kernel,
        grid=(Bd // GB,),
        in_specs=[pl.BlockSpec((rows, 4 * C_HK), lambda i: (rb0 + i, 0)),
                  pl.BlockSpec((1, C_HK), lambda i: (0, 0)),
                  pl.BlockSpec((1, LANES), lambda i: (0, 0)),
                  pl.BlockSpec((GB, H, C_DK, C_DV), lambda i: (i, 0, 0, 0))],
        out_specs=[pl.BlockSpec((rows, D_MODEL), lambda i: (i, 0)),
                   pl.BlockSpec((GB, H, C_DK, C_DV), lambda i: (i, 0, 0, 0))],
        out_shape=[jax.ShapeDtypeStruct((Bd * Ld, D_MODEL), BF16),
                   jax.ShapeDtypeStruct((Bd, H, C_DK, C_DV), F32)],
        compiler_params=_cparams(("arbitrary",)),
        name="hgrn_sample",
    )(proj, lb, og, s0)


def _out_proj_kernel(n_mix, n_x, npt, *refs):
    m = _read_rows(refs[:n_mix], npt)
    x = _read_rows(refs[n_mix:n_mix + n_x], npt)
    w_ref, g_ref, wr_ref, br_ref, x1_ref, h_ref, lg_ref = refs[n_mix + n_x:]
    x1 = x + jnp.dot(m, w_ref[...], preferred_element_type=F32)
    x1_ref[...] = x1
    ms = jnp.mean(x1 * x1, axis=-1, keepdims=True)
    h = x1 * lax.rsqrt(ms + EPS) * g_ref[...]
    h_ref[...] = h
    lg_ref[...] = lax.dot_general(wr_ref[...], h.astype(BF16), (((1,), (1,)), ((), ())),
                                  preferred_element_type=F32) + br_ref[...]


def _out_proj(mix, w, x, g, wr, br, tm):
    K, D = w.shape
    m_specs, m_arrays, npt, T = _row_source(mix, tm)
    x_specs, x_arrays, npt_x, _ = _row_source(x, tm)
    assert npt_x in (0, npt)
    row = lambda n: pl.BlockSpec((tm, n), lambda i: (i, 0))
    full = lambda a, b: pl.BlockSpec((a, b), lambda i: (0, 0))
    return pl.pallas_call(
        functools.partial(_out_proj_kernel, len(m_arrays), len(x_arrays), npt),
        grid=(T // tm,),
        in_specs=m_specs + x_specs + [full(K, D), full(1, D), full(LOGIT_W, D), full(LOGIT_W, 1)],
        out_specs=[row(D), row(D), pl.BlockSpec((LOGIT_W, tm), lambda i: (0, i))],
        out_shape=[jax.ShapeDtypeStruct((T, D), F32), jax.ShapeDtypeStruct((T, D), F32),
                   jax.ShapeDtypeStruct((LOGIT_W, T), F32)],
        compiler_params=_cparams(("arbitrary",)),
        name="out_proj",
    )(*m_arrays, *x_arrays, w, g.reshape(1, D), wr, br)


def _route_kernel(lg_ref, slot_ref, gate_ref, cnt_ref, carry, total, tri):
    phase = pl.program_id(0)
    step = pl.program_id(1)
    tm = lg_ref.shape[1]
    G, E = N_GROUPS, EXP_PER_GROUP
    row8 = lax.broadcasted_iota(jnp.int32, (G, tm), 0)

    glog = lg_ref[0:G, :]
    gmax = jnp.max(glog, axis=0, keepdims=True)
    g_idx = jnp.min(jnp.where(glog == gmax, row8, G), axis=0, keepdims=True)
    g_p = 1.0 / jnp.sum(jnp.exp(glog - gmax), axis=0, keepdims=True)

    elog = jnp.zeros((E, tm), F32)
    for g in range(G):
        elog = jnp.where(g_idx == g, lg_ref[G + g * E:G + (g + 1) * E, :], elog)
    emax = jnp.max(elog, axis=0, keepdims=True)
    ee = jnp.exp(elog - emax)
    prob = ee / jnp.sum(ee, axis=0, keepdims=True)
    p1 = jnp.max(prob, axis=0, keepdims=True)
    i1 = jnp.min(jnp.where(prob == p1, row8, E), axis=0, keepdims=True)
    prob2 = jnp.where(row8 == i1, -1.0, prob)
    p2 = jnp.max(prob2, axis=0, keepdims=True)
    i2 = jnp.min(jnp.where(prob2 == p2, row8, E), axis=0, keepdims=True)
    psum = p1 + p2
    w1 = p1 / psum * g_p
    w2 = p2 / psum * g_p

    erow = lax.broadcasted_iota(jnp.int32, (N_EXPERTS, tm), 0)
    oh1 = erow == g_idx * E + i1
    oh2 = erow == g_idx * E + i2
    both = jnp.where(oh1 | oh2, 1.0, 0.0)
    tile_cnt = jnp.sum(both, axis=1, keepdims=True)

    @pl.when((phase == 0) & (step == 0))
    def _():
        total[...] = jnp.zeros_like(total)
        r = lax.broadcasted_iota(jnp.int32, (tm, tm), 0)
        c = lax.broadcasted_iota(jnp.int32, (tm, tm), 1)
        tri[...] = jnp.where(r < c, 1.0, 0.0).astype(BF16)

    @pl.when(phase == 0)
    def _():
        total[...] += tile_cnt

    @pl.when((phase == 1) & (step == 0))
    def _():
        carry[...] = jnp.zeros_like(carry)

    @pl.when(phase == 1)
    def _():
        cnt = total[...]
        padded = jnp.floor((cnt + (MOE_BLOCK - 1)) * (1.0 / MOE_BLOCK)) * MOE_BLOCK
        pb = jnp.broadcast_to(padded, (N_EXPERTS, LANES))
        pstart = (_cumsum_rows(pb) - pb)[:, 0:1]
        before = (jnp.dot(both.astype(BF16), tri[...], preferred_element_type=F32)
                  + (carry[...] + pstart))
        s1 = jnp.sum(jnp.where(oh1, before, 0.0), axis=0, keepdims=True)
        s2 = jnp.sum(jnp.where(oh2, before, 0.0), axis=0, keepdims=True)
        carry[...] += tile_cnt
        slot_ref[...] = jnp.concatenate([s1, s2], axis=0).astype(jnp.int32)
        grow = lax.broadcasted_iota(jnp.int32, (LOGIT_W, tm), 0)
        gate_ref[...] = jnp.where(grow == 0, w1, jnp.where(grow == 1, w2, 0.0)).T
        cnt_ref[...] = jnp.broadcast_to(cnt, (N_EXPERTS, LANES))


def _route(logits_t):
    T = logits_t.shape[1]
    tm = ROUTE_TM
    return pl.pallas_call(
        _route_kernel,
        grid=(2, T // tm),
        in_specs=[pl.BlockSpec((LOGIT_W, tm), lambda p, i: (0, i))],
        out_specs=[pl.BlockSpec((2, tm), lambda p, i: (0, i * p)),
                   pl.BlockSpec((tm, LOGIT_W), lambda p, i: (i * p, 0)),
                   pl.BlockSpec((N_EXPERTS, LANES), lambda p, i: (0, 0))],
        out_shape=[jax.ShapeDtypeStruct((2, T), jnp.int32),
                   jax.ShapeDtypeStruct((T, LOGIT_W), F32),
                   jax.ShapeDtypeStruct((N_EXPERTS, LANES), F32)],
        scratch_shapes=[pltpu.VMEM((N_EXPERTS, 1), F32), pltpu.VMEM((N_EXPERTS, 1), F32),
                        pltpu.VMEM((tm, tm), BF16)],
        compiler_params=_cparams(("arbitrary", "arbitrary")),
        name="route",
    )(logits_t)


DISPATCH_TM = 512


def _dispatch_kernel(T, slot_ref, h_ref, xs_in_ref, xs_ref, sem):
    del xs_in_ref
    base = pl.program_id(0) * DISPATCH_TM

    def copy(r, k):
        return pltpu.make_async_copy(h_ref.at[pl.ds(r, 1)],
                                     xs_ref.at[pl.ds(slot_ref[k * T + base + r], 1)], sem)

    def start(r, c):
        copy(r, 0).start()
        copy(r, 1).start()
        return c

    def wait(r, c):
        copy(r, 0).wait()
        copy(r, 1).wait()
        return c

    lax.fori_loop(0, DISPATCH_TM, start, 0, unroll=8)
    lax.fori_loop(0, DISPATCH_TM, wait, 0, unroll=8)


def _dispatch(slots, h, n_slots):
    T, D = h.shape
    xs0 = jnp.zeros((n_slots, D), F32)
    grid_spec = pltpu.PrefetchScalarGridSpec(
        num_scalar_prefetch=1,
        grid=(T // DISPATCH_TM,),
        in_specs=[pl.BlockSpec((DISPATCH_TM, D), lambda i, s: (i, 0)),
                  pl.BlockSpec(memory_space=pl.ANY)],
        out_specs=pl.BlockSpec(memory_space=pl.ANY),
        scratch_shapes=[pltpu.SemaphoreType.DMA(())],
    )
    return pl.pallas_call(
        functools.partial(_dispatch_kernel, T),
        grid_spec=grid_spec,
        out_shape=jax.ShapeDtypeStruct((n_slots, D), F32),
        input_output_aliases={2: 0},
        compiler_params=_cparams(("arbitrary",)),
        name="moe_dispatch",
    )(slots, h, xs0)


def _ffn_kernel(nblk, meta_ref, x_ref, w1_ref, w3_ref, w2_ref, y_ref, wb1, wb3, wb2):
    b = pl.program_id(0)
    used = b < meta_ref[nblk]
    new_expert = (b == 0) | (meta_ref[b] != meta_ref[jnp.maximum(b - 1, 0)])

    @pl.when(used & new_expert)
    def _():
        wb1[...] = w1_ref[0, 0].astype(BF16)
        wb3[...] = w3_ref[0, 0].astype(BF16)
        wb2[...] = w2_ref[0, 0].astype(BF16)

    @pl.when(used)
    def _():
        x = x_ref[...].astype(BF16)
        a = jnp.dot(x, wb1[...], preferred_element_type=F32)
        c = jnp.dot(x, wb3[...], preferred_element_type=F32)
        hid = (a * _sigmoid(a)) * c
        y_ref[...] = jnp.dot(hid.astype(BF16), wb2[...], preferred_element_type=F32)

    @pl.when(jnp.logical_not(used))
    def _():
        y_ref[...] = jnp.zeros_like(y_ref)


def _ffn(meta, xs, w1, w3, w2, layer):
    n_slots, D = xs.shape
    nblk = n_slots // MOE_BLOCK
    grid_spec = pltpu.PrefetchScalarGridSpec(
        num_scalar_prefetch=1,
        grid=(nblk,),
        in_specs=[pl.BlockSpec((MOE_BLOCK, D), lambda b, e: (b, 0)),
                  pl.BlockSpec((1, 1, D, D_EXPERT), lambda b, e: (layer, e[b], 0, 0)),
                  pl.BlockSpec((1, 1, D, D_EXPERT), lambda b, e: (layer, e[b], 0, 0)),
                  pl.BlockSpec((1, 1, D_EXPERT, D), lambda b, e: (layer, e[b], 0, 0))],
        out_specs=pl.BlockSpec((MOE_BLOCK, D), lambda b, e: (b, 0)),
        scratch_shapes=[pltpu.VMEM((D, D_EXPERT), BF16), pltpu.VMEM((D, D_EXPERT), BF16),
                        pltpu.VMEM((D_EXPERT, D), BF16)],
    )
    return pl.pallas_call(
        functools.partial(_ffn_kernel, nblk),
        grid_spec=grid_spec,
        out_shape=jax.ShapeDtypeStruct((n_slots, D), F32),
        compiler_params=_cparams(("arbitrary",)),
        name="moe_ffn",
    )(meta, xs, w1, w3, w2)


COMBINE_TM = 256


def _combine_kernel(T, npt, slot_ref, x_ref, gate_ref, ys_ref, *refs):
    out_refs, (buf, sem) = refs[:-2], refs[-2:]
    i = pl.program_id(0)
    base = i * COMBINE_TM

    def copy(r8, u, k):
        r = r8 + u
        return pltpu.make_async_copy(ys_ref.at[pl.ds(slot_ref[k * T + base + r], 1)],
                                     buf.at[k, pl.ds(r, 1)], sem)

    def start(j, c):
        r8 = pl.multiple_of(j * SUBLANES, SUBLANES)
        for u in range(SUBLANES):
            copy(r8, u, 0).start()
            copy(r8, u, 1).start()
        return c

    def wait(j, c):
        r8 = pl.multiple_of(j * SUBLANES, SUBLANES)
        for u in range(SUBLANES):
            copy(r8, u, 0).wait()
            copy(r8, u, 1).wait()
        return c

    lax.fori_loop(0, COMBINE_TM // SUBLANES, start, 0)
    lax.fori_loop(0, COMBINE_TM // SUBLANES, wait, 0)
    g = gate_ref[...]
    y = x_ref[...] + (buf[0] * g[:, 0:1] + buf[1] * g[:, 1:2])
    if npt == 0:
        out_refs[0][...] = y
    else:
        @pl.when(i < npt)
        def _():
            out_refs[0][...] = y

        @pl.when(i >= npt)
        def _():
            out_refs[1][...] = y


def _combine(slots, x1, gates, ys, split_rows=0):
    T, D = x1.shape
    tm = COMBINE_TM
    npt = split_rows // tm
    if npt == 0:
        out_specs = pl.BlockSpec((tm, D), lambda i, s: (i, 0))
        out_shape = jax.ShapeDtypeStruct((T, D), F32)
    else:
        out_specs = [pl.BlockSpec((tm, D), lambda i, s: (jnp.minimum(i, npt - 1), 0)),
                     pl.BlockSpec((tm, D), lambda i, s: (jnp.maximum(i - npt, 0), 0))]
        out_shape = [jax.ShapeDtypeStruct((split_rows, D), F32),
                     jax.ShapeDtypeStruct((T - split_rows, D), F32)]
    grid_spec = pltpu.PrefetchScalarGridSpec(
        num_scalar_prefetch=1,
        grid=(T // tm,),
        in_specs=[pl.BlockSpec((tm, D), lambda i, s: (i, 0)),
                  pl.BlockSpec((tm, LOGIT_W), lambda i, s: (i, 0)),
                  pl.BlockSpec(memory_space=pl.ANY)],
        out_specs=out_specs,
        scratch_shapes=[pltpu.VMEM((2, tm, D), F32), pltpu.SemaphoreType.DMA(())],
    )
    return pl.pallas_call(
        functools.partial(_combine_kernel, T, npt),
        grid_spec=grid_spec,
        out_shape=out_shape,
        compiler_params=_cparams(("arbitrary",)),
        name="moe_combine",
    )(slots, x1, gates, ys)


def _moe(x1, h, logits, w1, w3, w2, layer, split_rows=0):
    T = x1.shape[0]
    slot_l, gate_l, cnt = _route(logits)
    slots = slot_l.reshape(2 * T)
    n_blocks = -(-(2 * T) // MOE_BLOCK) + N_EXPERTS
    counts = cnt[:, 0].astype(jnp.int32)
    pend = jnp.cumsum((counts + MOE_BLOCK - 1) // MOE_BLOCK * MOE_BLOCK)
    starts = jnp.arange(n_blocks, dtype=jnp.int32) * MOE_BLOCK
    blk_e = jnp.minimum(jnp.sum((pend[None, :] <= starts[:, None]).astype(jnp.int32), axis=1),
                        N_EXPERTS - 1)
    meta = jnp.concatenate([blk_e, pend[-1:] // MOE_BLOCK]).astype(jnp.int32)
    xs = _dispatch(slots, h, n_blocks * MOE_BLOCK)
    ys = _ffn(meta, xs, w1, w3, w2, layer)
    return _combine(slots, x1, gate_l, ys, split_rows)


def _router_weights(w_grp, b_grp, w_exp, b_exp):
    D = w_grp.shape[0]
    pad = LOGIT_W - N_GROUPS - N_EXPERTS
    wr = jnp.concatenate([w_grp, w_exp, jnp.zeros((D, pad), F32)], axis=1).astype(BF16)
    br = jnp.concatenate([b_grp, b_exp, jnp.zeros((pad,), F32)]).reshape(LOGIT_W, 1)
    return wr.T, br


def kernel(x_prompt, x_sample, cache_conv, cache_k, cache_v, state_hgrn, norm_mix, norm_ffn,
           ev_w_in, ev_conv, ev_q_norm, ev_k_norm, ev_sinks, ev_w_out,
           od_w_in, od_lb, od_o_norm, od_w_out,
           moe_w_grp, moe_b_grp, moe_w_exp, moe_b_exp, moe_w1, moe_w3, moe_w2):
    B, L, D = x_prompt.shape
    Bd, Ld, _ = x_sample.shape
    Tp = B * L
    T = Tp + Bd * Ld
    x = (x_prompt.reshape(Tp, D), x_sample.reshape(Bd * Ld, D))

    proj = _norm_matmul(x, norm_mix[0], ev_w_in[0].astype(BF16), 512)
    qg2 = jnp.tile(ev_q_norm[0], 2).reshape(1, LANES)
    kg2 = jnp.tile(ev_k_norm[0], 2).reshape(1, LANES)
    mix_p, k_p, v_p, conv_p = _even_prompt(proj, B, L, T, ev_conv[0], qg2, kg2, ev_sinks[0])
    mix_s, k_s, v_s, conv_s = _even_sample(
        proj, Bd, Ld, Tp, ev_conv[0], qg2, kg2, ev_sinks[0], cache_conv[0],
        cache_k[0].reshape(Bd, WINDOW, LANES), cache_v[0].reshape(Bd, WINDOW, LANES))
    wr, br = _router_weights(moe_w_grp[0], moe_b_grp[0], moe_w_exp[0], moe_b_exp[0])
    x1, h, logits = _out_proj((mix_p, mix_s), ev_w_out[0].astype(BF16), x, norm_ffn[0], wr, br, 512)
    x = _moe(x1, h, logits, moe_w1, moe_w3, moe_w2, 0)

    lb_all = jnp.cumsum(jax.nn.softmax(od_lb.astype(F32), axis=0), axis=0)
    lb = (lb_all - lb_all[0])[1].reshape(1, C_HK)
    og = od_o_norm[0].reshape(1, C_DV)
    proj = _norm_matmul(x, norm_mix[1], od_w_in[0].astype(BF16), 256)
    o_p, s_p = _hgrn_prompt(proj, B, L, T, lb, og)
    o_s, s_s = _hgrn_sample(proj, Bd, Ld, Tp, lb, og, state_hgrn[0])
    wr, br = _router_weights(moe_w_grp[1], moe_b_grp[1], moe_w_exp[1], moe_b_exp[1])
    x1, h, logits = _out_proj((o_p, o_s), od_w_out[0].astype(BF16), x, norm_ffn[1], wr, br, 512)
    y_p, y_s = _moe(x1, h, logits, moe_w1, moe_w3, moe_w2, 1, split_rows=Tp)

    y_prompt = y_p.reshape(B, L, D)
    y_sample = y_s.reshape(Bd, Ld, D)
    return (y_prompt, y_sample,
            conv_p[None], k_p.reshape(1, B, WINDOW, N_KV, HEAD_DIM), v_p.reshape(1, B, WINDOW, N_KV, HEAD_DIM),
            s_p[None],
            conv_s[None], k_s.reshape(1, Bd, WINDOW, N_KV, HEAD_DIM), v_s.reshape(1, Bd, WINDOW, N_KV, HEAD_DIM),
            s_s[None])
```

```python
import functools

import numpy as np
import jax
import jax.numpy as jnp
from jax import lax
from jax.experimental import pallas as pl
from jax.experimental.pallas import tpu as pltpu

F32 = jnp.float32
BF16 = jnp.bfloat16

D_MODEL = 1024
PAST_LEN = 16384
D_CONV = 512
CONV_W = 3
HEAD_DIM = 64
N_Q = 8
N_KV = 2
GQA_G = N_Q // N_KV
WINDOW = 128
ROPE_THETA = 10000.0
D_IN_EVEN = 3 * D_CONV + (N_Q + 2 * N_KV) * HEAD_DIM
C_HEADS = 8
C_DK = 128
C_DV = 128
C_HK = C_HEADS * C_DK
N_GROUPS = 8
EXP_PER_GROUP = 8
N_EXPERTS = N_GROUPS * EXP_PER_GROUP
D_EXPERT = 512
MOE_BLOCK = 256
EPS = 1e-6

LANES = 128
SUBLANES = 8
VMEM_LIMIT = 56 * 1024 * 1024

GLA_CHUNK = 128
GLA_STEP = 512
SAMPLE_GB = 2
ROUTE_TM = 512
LOGIT_W = 128


def _cparams(sem):
    return pltpu.CompilerParams(dimension_semantics=sem, vmem_limit_bytes=VMEM_LIMIT)


def _sigmoid(x):
    return 1.0 / (1.0 + jnp.exp(-x))


def _silu(x):
    return x * (0.5 * jnp.tanh(0.5 * x) + 0.5)


def _row_source(src, tm):
    if isinstance(src, tuple):
        a, b = src
        n = a.shape[1]
        npt = a.shape[0] // tm
        specs = [pl.BlockSpec((tm, n), lambda i, *_: (jnp.minimum(i, npt - 1), 0)),
                 pl.BlockSpec((tm, n), lambda i, *_: (jnp.maximum(i - npt, 0), 0))]
        return specs, [a, b], npt, a.shape[0] + b.shape[0]
    return [pl.BlockSpec((tm, src.shape[1]), lambda i, *_: (i, 0))], [src], 0, src.shape[0]


def _read_rows(refs, npt):
    if len(refs) == 2:
        return jnp.where(pl.program_id(0) < npt, refs[0][...], refs[1][...])
    return refs[0][...]


def _norm_matmul_kernel(n_src, npt, *refs):
    x = _read_rows(refs[:n_src], npt)
    g_ref, w_ref, o_ref = refs[n_src:]
    ms = jnp.mean(x * x, axis=-1, keepdims=True)
    h = (x * lax.rsqrt(ms + EPS) * g_ref[...]).astype(BF16)
    o_ref[...] = jnp.dot(h, w_ref[...], preferred_element_type=F32)


def _norm_matmul(x, g, w, tm):
    D, N = w.shape
    specs, arrays, npt, T = _row_source(x, tm)
    return pl.pallas_call(
        functools.partial(_norm_matmul_kernel, len(arrays), npt),
        grid=(T // tm,),
        in_specs=specs + [pl.BlockSpec((1, D), lambda i: (0, 0)),
                          pl.BlockSpec((D, N), lambda i: (0, 0))],
        out_specs=pl.BlockSpec((tm, N), lambda i: (i, 0)),
        out_shape=jax.ShapeDtypeStruct((T, N), F32),
        compiler_params=_cparams(("arbitrary",)),
        name="norm_matmul",
    )(*arrays, g.reshape(1, D), w)


def _rope_tables(pos):
    inv = ROPE_THETA ** (-jnp.arange(0, HEAD_DIM, 2, dtype=F32) / HEAD_DIM)
    ang = pos.astype(F32)[:, None] * inv[None, :]
    cos = jnp.cos(ang)
    sin = jnp.sin(ang)
    return (jnp.concatenate([cos, cos, cos, cos], axis=1),
            jnp.concatenate([-sin, sin, -sin, sin], axis=1))


def _headnorm_rope(x, g2, cos, sin):
    lane = lax.broadcasted_iota(jnp.int32, x.shape, 1)
    lo = lane < HEAD_DIM
    x2 = x * x
    s_lo = jnp.sum(jnp.where(lo, x2, 0.0), axis=-1, keepdims=True)
    s_hi = jnp.sum(jnp.where(lo, 0.0, x2), axis=-1, keepdims=True)
    ms = jnp.where(lo, s_lo, s_hi) * (1.0 / HEAD_DIM)
    y = x * lax.rsqrt(ms + EPS) * g2
    first_half = (lane & (HEAD_DIM // 2)) == 0
    swapped = jnp.where(first_half, pltpu.roll(y, LANES - HEAD_DIM // 2, 1),
                        pltpu.roll(y, HEAD_DIM // 2, 1))
    return y * cos + swapped * sin


def _gated_conv(gb, u, c2, c1, cw):
    R = u.shape[0]
    row = lax.broadcasted_iota(jnp.int32, u.shape, 0)
    u1 = jnp.where(row == 0, c1, pltpu.roll(u, 1, 0))
    u2 = jnp.where(row == 0, c2, jnp.where(row == 1, c1, pltpu.roll(u, 2, 0)))
    del R
    return gb * (cw[0:1, :] * u2 + cw[1:2, :] * u1 + cw[2:3, :] * u)


def _band_attention(qs, kk, vv, valid, sinkv):
    s = lax.dot_general(qs.astype(BF16), kk.astype(BF16), (((1,), (1,)), ((), ())),
                        preferred_element_type=F32) * (HEAD_DIM ** -0.5)
    s = jnp.where(valid, s, -jnp.inf)
    m = jnp.maximum(jnp.max(s, axis=-1, keepdims=True), sinkv)
    e = jnp.exp(s - m)
    den = jnp.sum(e, axis=-1, keepdims=True) + jnp.exp(sinkv - m)
    p = e / den
    return jnp.dot(p.astype(BF16), vv.astype(BF16), preferred_element_type=F32)


def _sink_column(sinks_ref, hk, rows_per_head):
    R = GQA_G * rows_per_head
    row = lax.broadcasted_iota(jnp.int32, (R, 1), 0)
    col = jnp.full((R, 1), sinks_ref[hk * GQA_G + GQA_G - 1], F32)
    for j in range(GQA_G - 2, -1, -1):
        col = jnp.where(row < (j + 1) * rows_per_head, sinks_ref[hk * GQA_G + j], col)
    return col


def _even_prompt_kernel(sinks_ref, proj_ref, cos_ref, sin_ref, cw_ref, qg_ref, kg_ref,
                        mix_ref, kl_ref, vl_ref, cl_ref, kprev, vprev, ucar):
    blk = pl.program_id(1)
    W = WINDOW

    @pl.when(blk == 0)
    def _():
        kprev[...] = jnp.zeros_like(kprev)
        vprev[...] = jnp.zeros_like(vprev)
        ucar[...] = jnp.zeros_like(ucar)

    gb = proj_ref[:, 0:D_CONV]
    u = proj_ref[:, D_CONV:2 * D_CONV] * proj_ref[:, 2 * D_CONV:3 * D_CONV]
    car = ucar[...]
    a_out = _gated_conv(gb, u, car[0:1, :], car[1:2, :], cw_ref[...])
    ucar[0:2, :] = u[W - 2:W, :]
    mix_ref[:, 0:D_CONV] = a_out.astype(BF16)

    cos = cos_ref[...]
    sin = sin_ref[...]
    q0 = 3 * D_CONV
    k0 = q0 + N_Q * HEAD_DIM
    v0 = k0 + N_KV * HEAD_DIM
    k_r = _headnorm_rope(proj_ref[:, k0:k0 + LANES], kg_ref[...], cos, sin)
    v_r = proj_ref[:, v0:v0 + LANES]
    q_r = [_headnorm_rope(proj_ref[:, q0 + LANES * j:q0 + LANES * (j + 1)], qg_ref[...], cos, sin)
           for j in range(N_Q * HEAD_DIM // LANES)]
    k_p = kprev[...]
    v_p = vprev[...]

    R = GQA_G * W
    i = lax.broadcasted_iota(jnp.int32, (R, 2 * W), 0) & (W - 1)
    j = lax.broadcasted_iota(jnp.int32, (R, 2 * W), 1)
    diff = i + W - j
    valid = (diff >= 0) & (diff <= W) & ((blk > 0) | (j >= W))

    for hk in range(N_KV):
        ls = slice(hk * HEAD_DIM, (hk + 1) * HEAD_DIM)
        kk = jnp.concatenate([k_p[:, ls], k_r[:, ls]], axis=0)
        vv = jnp.concatenate([v_p[:, ls], v_r[:, ls]], axis=0)
        heads = []
        for g in range(GQA_G):
            h = hk * GQA_G + g
            tile = q_r[h // 2]
            heads.append(tile[:, (h % 2) * HEAD_DIM:(h % 2 + 1) * HEAD_DIM])
        qs = jnp.concatenate(heads, axis=0)
        o = _band_attention(qs, kk, vv, valid, _sink_column(sinks_ref, hk, W))
        for g in range(GQA_G):
            h = hk * GQA_G + g
            mix_ref[:, D_CONV + h * HEAD_DIM:D_CONV + (h + 1) * HEAD_DIM] = \
                o[g * W:(g + 1) * W, :].astype(BF16)

    kprev[...] = k_r
    vprev[...] = v_r

    @pl.when(blk == pl.num_programs(1) - 1)
    def _():
        kl_ref[0] = k_r
        vl_ref[0] = v_r
        cl_ref[0] = u[W - 2:W, :]


def _even_prompt(proj, B, L, T, cw, qg2, kg2, sinks):
    nb = L // WINDOW
    cos, sin = _rope_tables(jnp.arange(L, dtype=jnp.int32))
    full = lambda shape: pl.BlockSpec(shape, lambda b, i, *_: tuple(0 for _ in shape))
    grid_spec = pltpu.PrefetchScalarGridSpec(
        num_scalar_prefetch=1,
        grid=(B, nb),
        in_specs=[pl.BlockSpec((WINDOW, D_IN_EVEN), lambda b, i, s: (b * nb + i, 0)),
                  pl.BlockSpec((WINDOW, LANES), lambda b, i, s: (i, 0)),
                  pl.BlockSpec((WINDOW, LANES), lambda b, i, s: (i, 0)),
                  full((CONV_W, D_CONV)), full((1, LANES)), full((1, LANES))],
        out_specs=[pl.BlockSpec((WINDOW, D_MODEL), lambda b, i, s: (b * nb + i, 0)),
                   pl.BlockSpec((1, WINDOW, LANES), lambda b, i, s: (b, 0, 0)),
                   pl.BlockSpec((1, WINDOW, LANES), lambda b, i, s: (b, 0, 0)),
                   pl.BlockSpec((1, CONV_W - 1, D_CONV), lambda b, i, s: (b, 0, 0))],
        scratch_shapes=[pltpu.VMEM((WINDOW, LANES), F32), pltpu.VMEM((WINDOW, LANES), F32),
                        pltpu.VMEM((SUBLANES, D_CONV), F32)],
    )
    return pl.pallas_call(
        _even_prompt_kernel,
        grid_spec=grid_spec,
        out_shape=[jax.ShapeDtypeStruct((B * L, D_MODEL), BF16),
                   jax.ShapeDtypeStruct((B, WINDOW, LANES), F32),
                   jax.ShapeDtypeStruct((B, WINDOW, LANES), F32),
                   jax.ShapeDtypeStruct((B, CONV_W - 1, D_CONV), F32)],
        compiler_params=_cparams(("arbitrary", "arbitrary")),
        name="even_prompt",
    )(sinks, proj, cos, sin, cw, qg2, kg2)


def _even_sample_kernel(sinks_ref, proj_ref, cos_ref, sin_ref, cw_ref, qg_ref, kg_ref,
                        cc_ref, ck_ref, cv_ref, mix_ref, ko_ref, vo_ref, co_ref):
    W = WINDOW
    Ld = cos_ref.shape[0]
    cos = cos_ref[...]
    sin = sin_ref[...]
    q0 = 3 * D_CONV
    k0 = q0 + N_Q * HEAD_DIM
    v0 = k0 + N_KV * HEAD_DIM
    R = GQA_G * Ld
    i = lax.broadcasted_iota(jnp.int32, (R, 2 * W), 0) % Ld
    j = lax.broadcasted_iota(jnp.int32, (R, 2 * W), 1)
    diff = i + W - j
    valid = (diff >= 0) & (diff <= W)
    zpad = jnp.zeros((W - Ld, HEAD_DIM), F32)

    for bb in range(SAMPLE_GB):
        rs = slice(bb * Ld, (bb + 1) * Ld)
        gb = proj_ref[rs, 0:D_CONV]
        u = proj_ref[rs, D_CONV:2 * D_CONV] * proj_ref[rs, 2 * D_CONV:3 * D_CONV]
        car = cc_ref[bb]
        a_out = _gated_conv(gb, u, car[0:1, :], car[1:2, :], cw_ref[...])
        co_ref[bb] = u[Ld - 2:Ld, :]
        mix_ref[rs, 0:D_CONV] = a_out.astype(BF16)

        k_r = _headnorm_rope(proj_ref[rs, k0:k0 + LANES], kg_ref[...], cos, sin)
        v_r = proj_ref[rs, v0:v0 + LANES]
        q_r = [_headnorm_rope(proj_ref[rs, q0 + LANES * t:q0 + LANES * (t + 1)], qg_ref[...], cos, sin)
               for t in range(N_Q * HEAD_DIM // LANES)]
        k_c = ck_ref[bb]
        v_c = cv_ref[bb]
        ko_ref[bb, 0:W - Ld, :] = k_c[Ld:W, :]
        ko_ref[bb, W - Ld:W, :] = k_r
        vo_ref[bb, 0:W - Ld, :] = v_c[Ld:W, :]
        vo_ref[bb, W - Ld:W, :] = v_r

        for hk in range(N_KV):
            ls = slice(hk * HEAD_DIM, (hk + 1) * HEAD_DIM)
            kk = jnp.concatenate([k_c[:, ls], k_r[:, ls], zpad], axis=0)
            vv = jnp.concatenate([v_c[:, ls], v_r[:, ls], zpad], axis=0)
            heads = []
            for g in range(GQA_G):
                h = hk * GQA_G + g
                tile = q_r[h // 2]
                heads.append(tile[:, (h % 2) * HEAD_DIM:(h % 2 + 1) * HEAD_DIM])
            qs = jnp.concatenate(heads, axis=0)
            o = _band_attention(qs, kk, vv, valid, _sink_column(sinks_ref, hk, Ld))
            for g in range(GQA_G):
                h = hk * GQA_G + g
                mix_ref[rs, D_CONV + h * HEAD_DIM:D_CONV + (h + 1) * HEAD_DIM] = \
                    o[g * Ld:(g + 1) * Ld, :].astype(BF16)


def _even_sample(proj, Bd, Ld, row0, cw, qg2, kg2, sinks, cache_conv, cache_k, cache_v):
    GB = SAMPLE_GB
    rows = GB * Ld
    rb0 = row0 // rows
    cos, sin = _rope_tables(PAST_LEN + jnp.arange(Ld, dtype=jnp.int32))
    full = lambda shape: pl.BlockSpec(shape, lambda i, *_: tuple(0 for _ in shape))
    grid_spec = pltpu.PrefetchScalarGridSpec(
        num_scalar_prefetch=1,
        grid=(Bd // GB,),
        in_specs=[pl.BlockSpec((rows, D_IN_EVEN), lambda i, s: (rb0 + i, 0)),
                  full((Ld, LANES)), full((Ld, LANES)),
                  full((CONV_W, D_CONV)), full((1, LANES)), full((1, LANES)),
                  pl.BlockSpec((GB, CONV_W - 1, D_CONV), lambda i, s: (i, 0, 0)),
                  pl.BlockSpec((GB, WINDOW, LANES), lambda i, s: (i, 0, 0)),
                  pl.BlockSpec((GB, WINDOW, LANES), lambda i, s: (i, 0, 0))],
        out_specs=[pl.BlockSpec((rows, D_MODEL), lambda i, s: (i, 0)),
                   pl.BlockSpec((GB, WINDOW, LANES), lambda i, s: (i, 0, 0)),
                   pl.BlockSpec((GB, WINDOW, LANES), lambda i, s: (i, 0, 0)),
                   pl.BlockSpec((GB, CONV_W - 1, D_CONV), lambda i, s: (i, 0, 0))],
    )
    return pl.pallas_call(
        _even_sample_kernel,
        grid_spec=grid_spec,
        out_shape=[jax.ShapeDtypeStruct((Bd * Ld, D_MODEL), BF16),
                   jax.ShapeDtypeStruct((Bd, WINDOW, LANES), F32),
                   jax.ShapeDtypeStruct((Bd, WINDOW, LANES), F32),
                   jax.ShapeDtypeStruct((Bd, CONV_W - 1, D_CONV), F32)],
        compiler_params=_cparams(("arbitrary",)),
        name="even_sample",
    )(sinks, proj, cos, sin, cw, qg2, kg2, cache_conv, cache_k, cache_v)


def _cumsum_rows(x):
    C = x.shape[0]
    row = lax.broadcasted_iota(jnp.int32, x.shape, 0)
    s = 1
    while s < C:
        x = x + jnp.where(row >= s, pltpu.roll(x, s, 0), 0.0)
        s *= 2
    return x


def _group_ref(b, m, row):
    C, D = b.shape
    if 2 * m >= SUBLANES:
        n = C // (2 * m)
        b3 = b.reshape(n, 2 * m, D)
        return jnp.broadcast_to(b3[:, m - 1:m, :], (n, 2 * m, D)).reshape(C, D)
    r = row & (2 * m - 1)
    out = b
    for off in range(2 * m):
        if off == m - 1:
            continue
        shift = (off - (m - 1)) % C
        out = jnp.where(r == off, pltpu.roll(b, shift, 0), out)
    return out


def _gla_consts(C):
    s = np.arange(C)[:, None]
    t = np.arange(C)[None, :]
    masks = [s == t]
    m = C // 2
    while m >= 1:
        masks.append((s // (2 * m) == t // (2 * m)) & ((s & m) == 0) & ((t & m) != 0))
        m //= 2
    return jnp.asarray(np.stack(masks).astype(np.float32))


def _gla_chunk(qz, fz, v, lb, S, masks=None):
    C = qz.shape[0]
    q = _silu(qz)
    f = lb + (1.0 - lb) * _sigmoid(fz)
    k = 1.0 - f
    b = _cumsum_rows(jnp.log(f))
    vb = v.astype(BF16)

    inter = jnp.dot((q * jnp.exp(b)).astype(BF16), S.astype(BF16), preferred_element_type=F32)

    row = lax.broadcasted_iota(jnp.int32, (C, C_DK), 0)
    si = lax.broadcasted_iota(jnp.int32, (C, C), 0)
    ti = lax.broadcasted_iota(jnp.int32, (C, C), 1)
    nt = (((1,), (1,)), ((), ()))
    if C == SUBLANES:
        st = jnp.zeros((C, C), F32)
        for t in range(C):
            p = jnp.where(row <= t, q[t:t + 1, :] * k * jnp.exp(jnp.minimum(b[t:t + 1, :] - b, 0.0)), 0.0)
            st = jnp.where(ti == t, jnp.sum(p, axis=-1, keepdims=True), st)
        m = 0
    else:
        st = masks[0] * lax.dot_general(k.astype(BF16), q.astype(BF16), nt,
                                        preferred_element_type=F32)
        m = C // 2
    level = 1
    while m >= 1:
        rho = _group_ref(b, m, row)
        upper = (row & m) != 0
        d = b - rho
        x = (jnp.where(upper, q, k) * jnp.exp(jnp.where(upper, d, -d))).astype(BF16)
        st = st + masks[level] * lax.dot_general(x, x, nt, preferred_element_type=F32)
        level += 1
        m //= 2

    intra = lax.dot_general(st.astype(BF16), vb, (((0,), (0,)), ((), ())),
                            preferred_element_type=F32)

    b_last = b[C - 1:C, :]
    eye = (lax.broadcasted_iota(jnp.int32, (C_DK, C_DK), 0)
           == lax.broadcasted_iota(jnp.int32, (C_DK, C_DK), 1))
    dcol = jnp.sum(jnp.where(eye, jnp.broadcast_to(jnp.exp(b_last), (C_DK, C_DK)), 0.0),
                   axis=-1, keepdims=True)
    kd = k * jnp.exp(b_last - b)
    S_new = dcol * S + lax.dot_general(kd.astype(BF16), vb, (((0,), (0,)), ((), ())),
                                       preferred_element_type=F32)
    return inter + intra, S_new


def _gated_out(o, og, gz):
    ms = jnp.mean(o * o, axis=-1, keepdims=True)
    return (o * lax.rsqrt(ms + EPS) * og) * _silu(gz)


def _hgrn_prompt_kernel(q_ref, f_ref, i_ref, g_ref, lb_ref, og_ref, mask_ref, o_ref, s_ref, S):
    c = pl.program_id(2)

    @pl.when(c == 0)
    def _():
        S[...] = jnp.zeros_like(S)

    lb = lb_ref[...]
    og = og_ref[...]

    Sv = S[...]
    for n in range(GLA_STEP // GLA_CHUNK):
        rs = slice(n * GLA_CHUNK, (n + 1) * GLA_CHUNK)
        o, Sv = _gla_chunk(q_ref[rs, :], f_ref[rs, :], i_ref[rs, :], lb, Sv, mask_ref)
        o_ref[rs, :] = _gated_out(o, og, g_ref[rs, :]).astype(BF16)
    S[...] = Sv

    @pl.when(c == pl.num_programs(2) - 1)
    def _():
        s_ref[0, 0] = S[...]


def _hgrn_prompt(proj, B, L, T, lb, og):
    ns = L // GLA_STEP
    H = C_HEADS
    col = lambda off: pl.BlockSpec((GLA_STEP, LANES), lambda b, h, c: (b * ns + c, off + h))
    masks = _gla_consts(GLA_CHUNK)
    return pl.pallas_call(
        _hgrn_prompt_kernel,
        grid=(B, H, ns),
        in_specs=[col(0), col(H), col(2 * H), col(3 * H),
                  pl.BlockSpec((1, LANES), lambda b, h, c: (0, h)),
                  pl.BlockSpec((1, LANES), lambda b, h, c: (0, 0)),
                  pl.BlockSpec(masks.shape, lambda b, h, c: (0, 0, 0))],
        out_specs=[pl.BlockSpec((GLA_STEP, LANES), lambda b, h, c: (b * ns + c, h)),
                   pl.BlockSpec((1, 1, C_DK, C_DV), lambda b, h, c: (b, h, 0, 0))],
        out_shape=[jax.ShapeDtypeStruct((B * L, D_MODEL), BF16),
                   jax.ShapeDtypeStruct((B, H, C_DK, C_DV), F32)],
        scratch_shapes=[pltpu.VMEM((C_DK, C_DV), F32)],
        compiler_params=_cparams(("arbitrary", "arbitrary", "arbitrary")),
        name="hgrn_prompt",
    )(proj, proj, proj, proj, lb, og, masks)


def _hgrn_sample_kernel(p_ref, lb_ref, og_ref, s0_ref, o_ref, s_ref):
    Ld = p_ref.shape[0] // SAMPLE_GB
    og = og_ref[...]
    for bb in range(SAMPLE_GB):
        rs = slice(bb * Ld, (bb + 1) * Ld)
        for h in range(C_HEADS):
            cs = lambda part: slice((part * C_HEADS + h) * LANES, (part * C_HEADS + h + 1) * LANES)
            o, S_new = _gla_chunk(p_ref[rs, cs(0)], p_ref[rs, cs(1)], p_ref[rs, cs(2)],
                                  lb_ref[:, h * LANES:(h + 1) * LANES], s0_ref[bb, h])
            s_ref[bb, h] = S_new
            o_ref[rs, h * LANES:(h + 1) * LANES] = _gated_out(o, og, p_ref[rs, cs(3)]).astype(BF16)


def _hgrn_sample(proj, Bd, Ld, row0, lb, og, s0):
    GB = SAMPLE_GB
    rows = GB * Ld
    rb0 = row0 // rows
    H = C_HEADS
    return pl.pallas_call(
        _hgrn_sample_kernel,
        grid=(Bd // GB,),
        in_specs=[pl.BlockSpec((rows, 4 * C_HK), lambda i: (rb0 + i, 0)),
                  pl.BlockSpec((1, C_HK), lambda i: (0, 0)),
                  pl.BlockSpec((1, LANES), lambda i: (0, 0)),
                  pl.BlockSpec((GB, H, C_DK, C_DV), lambda i: (i, 0, 0, 0))],
        out_specs=[pl.BlockSpec((rows, D_MODEL), lambda i: (i, 0)),
                   pl.BlockSpec((GB, H, C_DK, C_DV), lambda i: (i, 0, 0, 0))],
        out_shape=[jax.ShapeDtypeStruct((Bd * Ld, D_MODEL), BF16),
                   jax.ShapeDtypeStruct((Bd, H, C_DK, C_DV), F32)],
        compiler_params=_cparams(("arbitrary",)),
        name="hgrn_sample",
    )(proj, lb, og, s0)


def _pack_bf16_pairs(xb):
    n = xb.shape[1] // 2
    lo = lax.bitcast_convert_type(xb[:, :n].astype(F32), jnp.uint32)
    hi = lax.bitcast_convert_type(xb[:, n:].astype(F32), jnp.uint32)
    return (lo >> 16) | (hi & jnp.uint32(0xFFFF0000))


def _unpack_bf16_pairs(w):
    lo = lax.bitcast_convert_type(w << 16, F32)
    hi = lax.bitcast_convert_type(w & jnp.uint32(0xFFFF0000), F32)
    return jnp.concatenate([lo, hi], axis=1).astype(BF16)


def _out_proj_kernel(n_mix, n_x, npt, *refs):
    m = _read_rows(refs[:n_mix], npt)
    x = _read_rows(refs[n_mix:n_mix + n_x], npt)
    w_ref, g_ref, wr_ref, br_ref, x1_ref, h_ref, lg_ref = refs[n_mix + n_x:]
    x1 = x + jnp.dot(m, w_ref[...], preferred_element_type=F32)
    x1_ref[...] = x1
    ms = jnp.mean(x1 * x1, axis=-1, keepdims=True)
    hb = (x1 * lax.rsqrt(ms + EPS) * g_ref[...]).astype(BF16)
    h_ref[...] = _pack_bf16_pairs(hb)
    lg_ref[...] = lax.dot_general(wr_ref[...], hb, (((1,), (1,)), ((), ())),
                                  preferred_element_type=F32) + br_ref[...]


def _out_proj(mix, w, x, g, wr, br, tm):
    K, D = w.shape
    m_specs, m_arrays, npt, T = _row_source(mix, tm)
    x_specs, x_arrays, npt_x, _ = _row_source(x, tm)
    assert npt_x in (0, npt)
    row = lambda n: pl.BlockSpec((tm, n), lambda i: (i, 0))
    full = lambda a, b: pl.BlockSpec((a, b), lambda i: (0, 0))
    return pl.pallas_call(
        functools.partial(_out_proj_kernel, len(m_arrays), len(x_arrays), npt),
        grid=(T // tm,),
        in_specs=m_specs + x_specs + [full(K, D), full(1, D), full(LOGIT_W, D), full(LOGIT_W, 1)],
        out_specs=[row(D), row(D // 2), pl.BlockSpec((LOGIT_W, tm), lambda i: (0, i))],
        out_shape=[jax.ShapeDtypeStruct((T, D), F32), jax.ShapeDtypeStruct((T, D // 2), jnp.uint32),
                   jax.ShapeDtypeStruct((LOGIT_W, T), F32)],
        compiler_params=_cparams(("arbitrary",)),
        name="out_proj",
    )(*m_arrays, *x_arrays, w, g.reshape(1, D), wr, br)


def _route_kernel(lg_ref, slot_ref, gate_ref, cnt_ref, carry, total, tri):
    phase = pl.program_id(0)
    step = pl.program_id(1)
    tm = lg_ref.shape[1]
    G, E = N_GROUPS, EXP_PER_GROUP
    row8 = lax.broadcasted_iota(jnp.int32, (G, tm), 0)

    glog = lg_ref[0:G, :]
    gmax = jnp.max(glog, axis=0, keepdims=True)
    g_idx = jnp.min(jnp.where(glog == gmax, row8, G), axis=0, keepdims=True)
    g_p = 1.0 / jnp.sum(jnp.exp(glog - gmax), axis=0, keepdims=True)

    elog = jnp.zeros((E, tm), F32)
    for g in range(G):
        elog = jnp.where(g_idx == g, lg_ref[G + g * E:G + (g + 1) * E, :], elog)
    emax = jnp.max(elog, axis=0, keepdims=True)
    ee = jnp.exp(elog - emax)
    prob = ee / jnp.sum(ee, axis=0, keepdims=True)
    p1 = jnp.max(prob, axis=0, keepdims=True)
    i1 = jnp.min(jnp.where(prob == p1, row8, E), axis=0, keepdims=True)
    prob2 = jnp.where(row8 == i1, -1.0, prob)
    p2 = jnp.max(prob2, axis=0, keepdims=True)
    i2 = jnp.min(jnp.where(prob2 == p2, row8, E), axis=0, keepdims=True)
    psum = p1 + p2
    w1 = p1 / psum * g_p
    w2 = p2 / psum * g_p

    erow = lax.broadcasted_iota(jnp.int32, (N_EXPERTS, tm), 0)
    oh1 = erow == g_idx * E + i1
    oh2 = erow == g_idx * E + i2
    both = jnp.where(oh1 | oh2, 1.0, 0.0)
    tile_cnt = jnp.sum(both, axis=1, keepdims=True)

    @pl.when((phase == 0) & (step == 0))
    def _():
        total[...] = jnp.zeros_like(total)
        r = lax.broadcasted_iota(jnp.int32, (tm, tm), 0)
        c = lax.broadcasted_iota(jnp.int32, (tm, tm), 1)
        tri[...] = jnp.where(r < c, 1.0, 0.0).astype(BF16)

    @pl.when(phase == 0)
    def _():
        total[...] += tile_cnt

    @pl.when((phase == 1) & (step == 0))
    def _():
        carry[...] = jnp.zeros_like(carry)

    @pl.when(phase == 1)
    def _():
        cnt = total[...]
        padded = jnp.floor((cnt + (MOE_BLOCK - 1)) * (1.0 / MOE_BLOCK)) * MOE_BLOCK
        pb = jnp.broadcast_to(padded, (N_EXPERTS, LANES))
        pstart = (_cumsum_rows(pb) - pb)[:, 0:1]
        before = (jnp.dot(both.astype(BF16), tri[...], preferred_element_type=F32)
                  + (carry[...] + pstart))
        s1 = jnp.sum(jnp.where(oh1, before, 0.0), axis=0, keepdims=True)
        s2 = jnp.sum(jnp.where(oh2, before, 0.0), axis=0, keepdims=True)
        carry[...] += tile_cnt
        slot_ref[...] = jnp.concatenate([s1, s2], axis=0).astype(jnp.int32)
        grow = lax.broadcasted_iota(jnp.int32, (LOGIT_W, tm), 0)
        gate_ref[...] = jnp.where(grow == 0, w1, jnp.where(grow == 1, w2, 0.0)).T
        cnt_ref[...] = jnp.broadcast_to(cnt, (N_EXPERTS, LANES))


def _route(logits_t):
    T = logits_t.shape[1]
    tm = ROUTE_TM
    return pl.pallas_call(
        _route_kernel,
        grid=(2, T // tm),
        in_specs=[pl.BlockSpec((LOGIT_W, tm), lambda p, i: (0, i))],
        out_specs=[pl.BlockSpec((2, tm), lambda p, i: (0, i * p)),
                   pl.BlockSpec((tm, LOGIT_W), lambda p, i: (i * p, 0)),
                   pl.BlockSpec((N_EXPERTS, LANES), lambda p, i: (0, 0))],
        out_shape=[jax.ShapeDtypeStruct((2, T), jnp.int32),
                   jax.ShapeDtypeStruct((T, LOGIT_W), F32),
                   jax.ShapeDtypeStruct((N_EXPERTS, LANES), F32)],
        scratch_shapes=[pltpu.VMEM((N_EXPERTS, 1), F32), pltpu.VMEM((N_EXPERTS, 1), F32),
                        pltpu.VMEM((tm, tm), BF16)],
        compiler_params=_cparams(("arbitrary", "arbitrary")),
        name="route",
    )(logits_t)


DISPATCH_TM = 512


def _dispatch_kernel(T, slot_ref, h_ref, xs_in_ref, xs_ref, sem):
    del xs_in_ref
    base = pl.program_id(0) * DISPATCH_TM

    def copy(r, k):
        return pltpu.make_async_copy(h_ref.at[pl.ds(r, 1)],
                                     xs_ref.at[pl.ds(slot_ref[k * T + base + r], 1)], sem)

    def start(r, c):
        copy(r, 0).start()
        copy(r, 1).start()
        return c

    def wait(r, c):
        copy(r, 0).wait()
        copy(r, 1).wait()
        return c

    lax.fori_loop(0, DISPATCH_TM, start, 0, unroll=8)
    lax.fori_loop(0, DISPATCH_TM, wait, 0, unroll=8)


def _dispatch(slots, h, n_slots):
    T, D = h.shape
    xs0 = jnp.zeros((n_slots, D), h.dtype)
    grid_spec = pltpu.PrefetchScalarGridSpec(
        num_scalar_prefetch=1,
        grid=(T // DISPATCH_TM,),
        in_specs=[pl.BlockSpec((DISPATCH_TM, D), lambda i, s: (i, 0)),
                  pl.BlockSpec(memory_space=pl.ANY)],
        out_specs=pl.BlockSpec(memory_space=pl.ANY),
        scratch_shapes=[pltpu.SemaphoreType.DMA(())],
    )
    return pl.pallas_call(
        functools.partial(_dispatch_kernel, T),
        grid_spec=grid_spec,
        out_shape=jax.ShapeDtypeStruct((n_slots, D), h.dtype),
        input_output_aliases={2: 0},
        compiler_params=_cparams(("arbitrary",)),
        name="moe_dispatch",
    )(slots, h, xs0)


def _ffn_kernel(nblk, meta_ref, x_ref, w1_ref, w3_ref, w2_ref, y_ref, wb1, wb3, wb2):
    b = pl.program_id(0)
    used = b < meta_ref[nblk]
    new_expert = (b == 0) | (meta_ref[b] != meta_ref[jnp.maximum(b - 1, 0)])

    @pl.when(used & new_expert)
    def _():
        wb1[...] = w1_ref[0, 0].astype(BF16)
        wb3[...] = w3_ref[0, 0].astype(BF16)
        wb2[...] = w2_ref[0, 0].astype(BF16)

    @pl.when(used)
    def _():
        x = _unpack_bf16_pairs(x_ref[...])
        a = jnp.dot(x, wb1[...], preferred_element_type=F32)
        c = jnp.dot(x, wb3[...], preferred_element_type=F32)
        hid = (a * _sigmoid(a)) * c
        y_ref[...] = jnp.dot(hid.astype(BF16), wb2[...], preferred_element_type=F32)

    @pl.when(jnp.logical_not(used))
    def _():
        y_ref[...] = jnp.zeros_like(y_ref)


def _ffn(meta, xs, w1, w3, w2, layer):
    n_slots = xs.shape[0]
    D = w1.shape[2]
    nblk = n_slots // MOE_BLOCK
    grid_spec = pltpu.PrefetchScalarGridSpec(
        num_scalar_prefetch=1,
        grid=(nblk,),
        in_specs=[pl.BlockSpec((MOE_BLOCK, xs.shape[1]), lambda b, e: (b, 0)),
                  pl.BlockSpec((1, 1, D, D_EXPERT), lambda b, e: (layer, e[b], 0, 0)),
                  pl.BlockSpec((1, 1, D, D_EXPERT), lambda b, e: (layer, e[b], 0, 0)),
                  pl.BlockSpec((1, 1, D_EXPERT, D), lambda b, e: (layer, e[b], 0, 0))],
        out_specs=pl.BlockSpec((MOE_BLOCK, D), lambda b, e: (b, 0)),
        scratch_shapes=[pltpu.VMEM((D, D_EXPERT), BF16), pltpu.VMEM((D, D_EXPERT), BF16),
                        pltpu.VMEM((D_EXPERT, D), BF16)],
    )
    return pl.pallas_call(
        functools.partial(_ffn_kernel, nblk),
        grid_spec=grid_spec,
        out_shape=jax.ShapeDtypeStruct((n_slots, D), F32),
        compiler_params=_cparams(("arbitrary",)),
        name="moe_ffn",
    )(meta, xs, w1, w3, w2)


COMBINE_TM = 512
COMBINE_CHUNK = 128


def _combine_kernel(T, npt, slot_ref, x_ref, gate_ref, ys_ref, *refs):
    out_refs, (buf, sems) = refs[:-2], refs[-2:]
    i = pl.program_id(0)
    base = i * COMBINE_TM
    n_chunks = COMBINE_TM // COMBINE_CHUNK

    def copy(r, k, q):
        return pltpu.make_async_copy(ys_ref.at[pl.ds(slot_ref[k * T + base + r], 1)],
                                     buf.at[k, pl.ds(r, 1)], sems.at[q])

    def for_chunk_rows(q, fn):
        def body(r, c):
            fn(copy(r, 0, q))
            fn(copy(r, 1, q))
            return c
        lax.fori_loop(q * COMBINE_CHUNK, (q + 1) * COMBINE_CHUNK, body, 0, unroll=8)

    for q in range(n_chunks):
        for_chunk_rows(q, lambda cp: cp.start())

    def emit(rs, y):
        if npt == 0:
            out_refs[0][rs, :] = y
        else:
            @pl.when(i < npt)
            def _():
                out_refs[0][rs, :] = y

            @pl.when(i >= npt)
            def _():
                out_refs[1][rs, :] = y

    for q in range(n_chunks):
        for_chunk_rows(q, lambda cp: cp.wait())
        rs = slice(q * COMBINE_CHUNK, (q + 1) * COMBINE_CHUNK)
        g = gate_ref[rs, :]
        emit(rs, x_ref[rs, :] + (buf[0, rs, :] * g[:, 0:1] + buf[1, rs, :] * g[:, 1:2]))


def _combine(slots, x1, gates, ys, split_rows=0):
    T, D = x1.shape
    tm = COMBINE_TM
    npt = split_rows // tm
    if npt == 0:
        out_specs = pl.BlockSpec((tm, D), lambda i, s: (i, 0))
        out_shape = jax.ShapeDtypeStruct((T, D), F32)
    else:
        out_specs = [pl.BlockSpec((tm, D), lambda i, s: (jnp.minimum(i, npt - 1), 0)),
                     pl.BlockSpec((tm, D), lambda i, s: (jnp.maximum(i - npt, 0), 0))]
        out_shape = [jax.ShapeDtypeStruct((split_rows, D), F32),
                     jax.ShapeDtypeStruct((T - split_rows, D), F32)]
    grid_spec = pltpu.PrefetchScalarGridSpec(
        num_scalar_prefetch=1,
        grid=(T // tm,),
        in_specs=[pl.BlockSpec((tm, D), lambda i, s: (i, 0)),
                  pl.BlockSpec((tm, LOGIT_W), lambda i, s: (i, 0)),
                  pl.BlockSpec(memory_space=pl.ANY)],
        out_specs=out_specs,
        scratch_shapes=[pltpu.VMEM((2, tm, D), F32),
                        pltpu.SemaphoreType.DMA((COMBINE_TM // COMBINE_CHUNK,))],
    )
    return pl.pallas_call(
        functools.partial(_combine_kernel, T, npt),
        grid_spec=grid_spec,
        out_shape=out_shape,
        compiler_params=_cparams(("arbitrary",)),
        name="moe_combine",
    )(slots, x1, gates, ys)


def _moe(x1, h, logits, w1, w3, w2, layer, split_rows=0):
    T = x1.shape[0]
    slot_l, gate_l, cnt = _route(logits)
    slots = slot_l.reshape(2 * T)
    n_blocks = -(-(2 * T) // MOE_BLOCK) + N_EXPERTS
    counts = cnt[:, 0].astype(jnp.int32)
    pend = jnp.cumsum((counts + MOE_BLOCK - 1) // MOE_BLOCK * MOE_BLOCK)
    starts = jnp.arange(n_blocks, dtype=jnp.int32) * MOE_BLOCK
    blk_e = jnp.minimum(jnp.sum((pend[None, :] <= starts[:, None]).astype(jnp.int32), axis=1),
                        N_EXPERTS - 1)
    meta = jnp.concatenate([blk_e, pend[-1:] // MOE_BLOCK]).astype(jnp.int32)
    xs = _dispatch(slots, h, n_blocks * MOE_BLOCK)
    ys = _ffn(meta, xs, w1, w3, w2, layer)
    return _combine(slots, x1, gate_l, ys, split_rows)


def _router_weights(w_grp, b_grp, w_exp, b_exp):
    D = w_grp.shape[0]
    pad = LOGIT_W - N_GROUPS - N_EXPERTS
    wr = jnp.concatenate([w_grp, w_exp, jnp.zeros((D, pad), F32)], axis=1).astype(BF16)
    br = jnp.concatenate([b_grp, b_exp, jnp.zeros((pad,), F32)]).reshape(LOGIT_W, 1)
    return wr.T, br


def kernel(x_prompt, x_sample, cache_conv, cache_k, cache_v, state_hgrn, norm_mix, norm_ffn,
           ev_w_in, ev_conv, ev_q_norm, ev_k_norm, ev_sinks, ev_w_out,
           od_w_in, od_lb, od_o_norm, od_w_out,
           moe_w_grp, moe_b_grp, moe_w_exp, moe_b_exp, moe_w1, moe_w3, moe_w2):
    B, L, D = x_prompt.shape
    Bd, Ld, _ = x_sample.shape
    Tp = B * L
    T = Tp + Bd * Ld
    x = (x_prompt.reshape(Tp, D), x_sample.reshape(Bd * Ld, D))

    proj = _norm_matmul(x, norm_mix[0], ev_w_in[0].astype(BF16), 512)
    qg2 = jnp.tile(ev_q_norm[0], 2).reshape(1, LANES)
    kg2 = jnp.tile(ev_k_norm[0], 2).reshape(1, LANES)
    mix_p, k_p, v_p, conv_p = _even_prompt(proj, B, L, T, ev_conv[0], qg2, kg2, ev_sinks[0])
    mix_s, k_s, v_s, conv_s = _even_sample(
        proj, Bd, Ld, Tp, ev_conv[0], qg2, kg2, ev_sinks[0], cache_conv[0],
        cache_k[0].reshape(Bd, WINDOW, LANES), cache_v[0].reshape(Bd, WINDOW, LANES))
    wr, br = _router_weights(moe_w_grp[0], moe_b_grp[0], moe_w_exp[0], moe_b_exp[0])
    x1, h, logits = _out_proj((mix_p, mix_s), ev_w_out[0].astype(BF16), x, norm_ffn[0], wr, br, 512)
    x = _moe(x1, h, logits, moe_w1, moe_w3, moe_w2, 0)

    lb_all = jnp.cumsum(jax.nn.softmax(od_lb.astype(F32), axis=0), axis=0)
    lb = (lb_all - lb_all[0])[1].reshape(1, C_HK)
    og = od_o_norm[0].reshape(1, C_DV)
    proj = _norm_matmul(x, norm_mix[1], od_w_in[0].astype(BF16), 256)
    o_p, s_p = _hgrn_prompt(proj, B, L, T, lb, og)
    o_s, s_s = _hgrn_sample(proj, Bd, Ld, Tp, lb, og, state_hgrn[0])
    wr, br = _router_weights(moe_w_grp[1], moe_b_grp[1], moe_w_exp[1], moe_b_exp[1])
    x1, h, logits = _out_proj((o_p, o_s), od_w_out[0].astype(BF16), x, norm_ffn[1], wr, br, 512)
    y_p, y_s = _moe(x1, h, logits, moe_w1, moe_w3, moe_w2, 1, split_rows=Tp)

    y_prompt = y_p.reshape(B, L, D)
    y_sample = y_s.reshape(Bd, Ld, D)
    return (y_prompt, y_sample,
            conv_p[None], k_p.reshape(1, B, WINDOW, N_KV, HEAD_DIM), v_p.reshape(1, B, WINDOW, N_KV, HEAD_DIM),
            s_p[None],
            conv_s[None], k_s.reshape(1, Bd, WINDOW, N_KV, HEAD_DIM), v_s.reshape(1, Bd, WINDOW, N_KV, HEAD_DIM),
            s_s[None])
```

```python
import functools

import numpy as np
import jax
import jax.numpy as jnp
from jax import lax
from jax.experimental import pallas as pl
from jax.experimental.pallas import tpu as pltpu

F32 = jnp.float32
BF16 = jnp.bfloat16

D_MODEL = 1024
PAST_LEN = 16384
D_CONV = 512
CONV_W = 3
HEAD_DIM = 64
N_Q = 8
N_KV = 2
GQA_G = N_Q // N_KV
WINDOW = 128
ROPE_THETA = 10000.0
D_IN_EVEN = 3 * D_CONV + (N_Q + 2 * N_KV) * HEAD_DIM
C_HEADS = 8
C_DK = 128
C_DV = 128
C_HK = C_HEADS * C_DK
N_GROUPS = 8
EXP_PER_GROUP = 8
N_EXPERTS = N_GROUPS * EXP_PER_GROUP
D_EXPERT = 512
MOE_BLOCK = 256
EPS = 1e-6

LANES = 128
SUBLANES = 8
VMEM_LIMIT = 56 * 1024 * 1024

GLA_CHUNK = 128
GLA_STEP = 512
SAMPLE_GB = 2
ROUTE_TM = 512
LOGIT_W = 128


def _cparams(sem):
    return pltpu.CompilerParams(dimension_semantics=sem, vmem_limit_bytes=VMEM_LIMIT)


def _sigmoid(x):
    return 1.0 / (1.0 + jnp.exp(-x))


def _silu(x):
    return x * (0.5 * jnp.tanh(0.5 * x) + 0.5)


def _row_source(src, tm):
    if isinstance(src, tuple):
        a, b = src
        n = a.shape[1]
        npt = a.shape[0] // tm
        specs = [pl.BlockSpec((tm, n), lambda i, *_: (jnp.minimum(i, npt - 1), 0)),
                 pl.BlockSpec((tm, n), lambda i, *_: (jnp.maximum(i - npt, 0), 0))]
        return specs, [a, b], npt, a.shape[0] + b.shape[0]
    return [pl.BlockSpec((tm, src.shape[1]), lambda i, *_: (i, 0))], [src], 0, src.shape[0]


def _read_rows(refs, npt):
    if len(refs) == 2:
        return jnp.where(pl.program_id(0) < npt, refs[0][...], refs[1][...])
    return refs[0][...]


def _norm_matmul_kernel(n_src, npt, *refs):
    x = _read_rows(refs[:n_src], npt)
    g_ref, w_ref, o_ref = refs[n_src:]
    ms = jnp.mean(x * x, axis=-1, keepdims=True)
    h = (x * lax.rsqrt(ms + EPS) * g_ref[...]).astype(BF16)
    o_ref[...] = jnp.dot(h, w_ref[...], preferred_element_type=F32)


def _norm_matmul(x, g, w, tm):
    D, N = w.shape
    specs, arrays, npt, T = _row_source(x, tm)
    return pl.pallas_call(
        functools.partial(_norm_matmul_kernel, len(arrays), npt),
        grid=(T // tm,),
        in_specs=specs + [pl.BlockSpec((1, D), lambda i: (0, 0)),
                          pl.BlockSpec((D, N), lambda i: (0, 0))],
        out_specs=pl.BlockSpec((tm, N), lambda i: (i, 0)),
        out_shape=jax.ShapeDtypeStruct((T, N), F32),
        compiler_params=_cparams(("arbitrary",)),
        name="norm_matmul",
    )(*arrays, g.reshape(1, D), w)


def _rope_tables(pos):
    inv = ROPE_THETA ** (-jnp.arange(0, HEAD_DIM, 2, dtype=F32) / HEAD_DIM)
    ang = pos.astype(F32)[:, None] * inv[None, :]
    cos = jnp.cos(ang)
    sin = jnp.sin(ang)
    return (jnp.concatenate([cos, cos, cos, cos], axis=1),
            jnp.concatenate([-sin, sin, -sin, sin], axis=1))


def _headnorm_rope(x, g2, cos, sin):
    lane = lax.broadcasted_iota(jnp.int32, x.shape, 1)
    lo = lane < HEAD_DIM
    x2 = x * x
    s_lo = jnp.sum(jnp.where(lo, x2, 0.0), axis=-1, keepdims=True)
    s_hi = jnp.sum(jnp.where(lo, 0.0, x2), axis=-1, keepdims=True)
    ms = jnp.where(lo, s_lo, s_hi) * (1.0 / HEAD_DIM)
    y = x * lax.rsqrt(ms + EPS) * g2
    first_half = (lane & (HEAD_DIM // 2)) == 0
    swapped = jnp.where(first_half, pltpu.roll(y, LANES - HEAD_DIM // 2, 1),
                        pltpu.roll(y, HEAD_DIM // 2, 1))
    return y * cos + swapped * sin


def _gated_conv(gb, u, c2, c1, cw):
    R = u.shape[0]
    row = lax.broadcasted_iota(jnp.int32, u.shape, 0)
    u1 = jnp.where(row == 0, c1, pltpu.roll(u, 1, 0))
    u2 = jnp.where(row == 0, c2, jnp.where(row == 1, c1, pltpu.roll(u, 2, 0)))
    del R
    return gb * (cw[0:1, :] * u2 + cw[1:2, :] * u1 + cw[2:3, :] * u)


def _band_attention(qs, kk, vv, valid, sinkv):
    s = lax.dot_general(qs.astype(BF16), kk.astype(BF16), (((1,), (1,)), ((), ())),
                        preferred_element_type=F32) * (HEAD_DIM ** -0.5)
    s = jnp.where(valid, s, -jnp.inf)
    m = jnp.maximum(jnp.max(s, axis=-1, keepdims=True), sinkv)
    e = jnp.exp(s - m)
    den = jnp.sum(e, axis=-1, keepdims=True) + jnp.exp(sinkv - m)
    p = e / den
    return jnp.dot(p.astype(BF16), vv.astype(BF16), preferred_element_type=F32)


def _sink_column(sinks_ref, hk, rows_per_head):
    R = GQA_G * rows_per_head
    row = lax.broadcasted_iota(jnp.int32, (R, 1), 0)
    col = jnp.full((R, 1), sinks_ref[hk * GQA_G + GQA_G - 1], F32)
    for j in range(GQA_G - 2, -1, -1):
        col = jnp.where(row < (j + 1) * rows_per_head, sinks_ref[hk * GQA_G + j], col)
    return col


def _even_prompt_kernel(sinks_ref, proj_ref, cos_ref, sin_ref, cw_ref, qg_ref, kg_ref,
                        mix_ref, kl_ref, vl_ref, cl_ref, kprev, vprev, ucar):
    blk = pl.program_id(1)
    W = WINDOW

    @pl.when(blk == 0)
    def _():
        kprev[...] = jnp.zeros_like(kprev)
        vprev[...] = jnp.zeros_like(vprev)
        ucar[...] = jnp.zeros_like(ucar)

    gb = proj_ref[:, 0:D_CONV]
    u = proj_ref[:, D_CONV:2 * D_CONV] * proj_ref[:, 2 * D_CONV:3 * D_CONV]
    car = ucar[...]
    a_out = _gated_conv(gb, u, car[0:1, :], car[1:2, :], cw_ref[...])
    ucar[0:2, :] = u[W - 2:W, :]
    mix_ref[:, 0:D_CONV] = a_out.astype(BF16)

    cos = cos_ref[...]
    sin = sin_ref[...]
    q0 = 3 * D_CONV
    k0 = q0 + N_Q * HEAD_DIM
    v0 = k0 + N_KV * HEAD_DIM
    k_r = _headnorm_rope(proj_ref[:, k0:k0 + LANES], kg_ref[...], cos, sin)
    v_r = proj_ref[:, v0:v0 + LANES]
    q_r = [_headnorm_rope(proj_ref[:, q0 + LANES * j:q0 + LANES * (j + 1)], qg_ref[...], cos, sin)
           for j in range(N_Q * HEAD_DIM // LANES)]
    k_p = kprev[...]
    v_p = vprev[...]

    R = GQA_G * W
    i = lax.broadcasted_iota(jnp.int32, (R, 2 * W), 0) & (W - 1)
    j = lax.broadcasted_iota(jnp.int32, (R, 2 * W), 1)
    diff = i + W - j
    valid = (diff >= 0) & (diff <= W) & ((blk > 0) | (j >= W))

    for hk in range(N_KV):
        ls = slice(hk * HEAD_DIM, (hk + 1) * HEAD_DIM)
        kk = jnp.concatenate([k_p[:, ls], k_r[:, ls]], axis=0)
        vv = jnp.concatenate([v_p[:, ls], v_r[:, ls]], axis=0)
        heads = []
        for g in range(GQA_G):
            h = hk * GQA_G + g
            tile = q_r[h // 2]
            heads.append(tile[:, (h % 2) * HEAD_DIM:(h % 2 + 1) * HEAD_DIM])
        qs = jnp.concatenate(heads, axis=0)
        o = _band_attention(qs, kk, vv, valid, _sink_column(sinks_ref, hk, W))
        for g in range(GQA_G):
            h = hk * GQA_G + g
            mix_ref[:, D_CONV + h * HEAD_DIM:D_CONV + (h + 1) * HEAD_DIM] = \
                o[g * W:(g + 1) * W, :].astype(BF16)

    kprev[...] = k_r
    vprev[...] = v_r

    @pl.when(blk == pl.num_programs(1) - 1)
    def _():
        kl_ref[0] = k_r
        vl_ref[0] = v_r
        cl_ref[0] = u[W - 2:W, :]


def _even_prompt(proj, B, L, T, cw, qg2, kg2, sinks):
    nb = L // WINDOW
    cos, sin = _rope_tables(jnp.arange(L, dtype=jnp.int32))
    full = lambda shape: pl.BlockSpec(shape, lambda b, i, *_: tuple(0 for _ in shape))
    grid_spec = pltpu.PrefetchScalarGridSpec(
        num_scalar_prefetch=1,
        grid=(B, nb),
        in_specs=[pl.BlockSpec((WINDOW, D_IN_EVEN), lambda b, i, s: (b * nb + i, 0)),
                  pl.BlockSpec((WINDOW, LANES), lambda b, i, s: (i, 0)),
                  pl.BlockSpec((WINDOW, LANES), lambda b, i, s: (i, 0)),
                  full((CONV_W, D_CONV)), full((1, LANES)), full((1, LANES))],
        out_specs=[pl.BlockSpec((WINDOW, D_MODEL), lambda b, i, s: (b * nb + i, 0)),
                   pl.BlockSpec((1, WINDOW, LANES), lambda b, i, s: (b, 0, 0)),
                   pl.BlockSpec((1, WINDOW, LANES), lambda b, i, s: (b, 0, 0)),
                   pl.BlockSpec((1, CONV_W - 1, D_CONV), lambda b, i, s: (b, 0, 0))],
        scratch_shapes=[pltpu.VMEM((WINDOW, LANES), F32), pltpu.VMEM((WINDOW, LANES), F32),
                        pltpu.VMEM((SUBLANES, D_CONV), F32)],
    )
    return pl.pallas_call(
        _even_prompt_kernel,
        grid_spec=grid_spec,
        out_shape=[jax.ShapeDtypeStruct((B * L, D_MODEL), BF16),
                   jax.ShapeDtypeStruct((B, WINDOW, LANES), F32),
                   jax.ShapeDtypeStruct((B, WINDOW, LANES), F32),
                   jax.ShapeDtypeStruct((B, CONV_W - 1, D_CONV), F32)],
        compiler_params=_cparams(("arbitrary", "arbitrary")),
        name="even_prompt",
    )(sinks, proj, cos, sin, cw, qg2, kg2)


def _even_sample_kernel(sinks_ref, proj_ref, cos_ref, sin_ref, cw_ref, qg_ref, kg_ref,
                        cc_ref, ck_ref, cv_ref, mix_ref, ko_ref, vo_ref, co_ref):
    W = WINDOW
    Ld = cos_ref.shape[0]
    cos = cos_ref[...]
    sin = sin_ref[...]
    q0 = 3 * D_CONV
    k0 = q0 + N_Q * HEAD_DIM
    v0 = k0 + N_KV * HEAD_DIM
    R = GQA_G * Ld
    i = lax.broadcasted_iota(jnp.int32, (R, 2 * W), 0) % Ld
    j = lax.broadcasted_iota(jnp.int32, (R, 2 * W), 1)
    diff = i + W - j
    valid = (diff >= 0) & (diff <= W)
    zpad = jnp.zeros((W - Ld, HEAD_DIM), F32)

    for bb in range(SAMPLE_GB):
        rs = slice(bb * Ld, (bb + 1) * Ld)
        gb = proj_ref[rs, 0:D_CONV]
        u = proj_ref[rs, D_CONV:2 * D_CONV] * proj_ref[rs, 2 * D_CONV:3 * D_CONV]
        car = cc_ref[bb]
        a_out = _gated_conv(gb, u, car[0:1, :], car[1:2, :], cw_ref[...])
        co_ref[bb] = u[Ld - 2:Ld, :]
        mix_ref[rs, 0:D_CONV] = a_out.astype(BF16)

        k_r = _headnorm_rope(proj_ref[rs, k0:k0 + LANES], kg_ref[...], cos, sin)
        v_r = proj_ref[rs, v0:v0 + LANES]
        q_r = [_headnorm_rope(proj_ref[rs, q0 + LANES * t:q0 + LANES * (t + 1)], qg_ref[...], cos, sin)
               for t in range(N_Q * HEAD_DIM // LANES)]
        k_c = ck_ref[bb]
        v_c = cv_ref[bb]
        ko_ref[bb, 0:W - Ld, :] = k_c[Ld:W, :]
        ko_ref[bb, W - Ld:W, :] = k_r
        vo_ref[bb, 0:W - Ld, :] = v_c[Ld:W, :]
        vo_ref[bb, W - Ld:W, :] = v_r

        for hk in range(N_KV):
            ls = slice(hk * HEAD_DIM, (hk + 1) * HEAD_DIM)
            kk = jnp.concatenate([k_c[:, ls], k_r[:, ls], zpad], axis=0)
            vv = jnp.concatenate([v_c[:, ls], v_r[:, ls], zpad], axis=0)
            heads = []
            for g in range(GQA_G):
                h = hk * GQA_G + g
                tile = q_r[h // 2]
                heads.append(tile[:, (h % 2) * HEAD_DIM:(h % 2 + 1) * HEAD_DIM])
            qs = jnp.concatenate(heads, axis=0)
            o = _band_attention(qs, kk, vv, valid, _sink_column(sinks_ref, hk, Ld))
            for g in range(GQA_G):
                h = hk * GQA_G + g
                mix_ref[rs, D_CONV + h * HEAD_DIM:D_CONV + (h + 1) * HEAD_DIM] = \
                    o[g * Ld:(g + 1) * Ld, :].astype(BF16)


def _even_sample(proj, Bd, Ld, row0, cw, qg2, kg2, sinks, cache_conv, cache_k, cache_v):
    GB = SAMPLE_GB
    rows = GB * Ld
    rb0 = row0 // rows
    cos, sin = _rope_tables(PAST_LEN + jnp.arange(Ld, dtype=jnp.int32))
    full = lambda shape: pl.BlockSpec(shape, lambda i, *_: tuple(0 for _ in shape))
    grid_spec = pltpu.PrefetchScalarGridSpec(
        num_scalar_prefetch=1,
        grid=(Bd // GB,),
        in_specs=[pl.BlockSpec((rows, D_IN_EVEN), lambda i, s: (rb0 + i, 0)),
                  full((Ld, LANES)), full((Ld, LANES)),
                  full((CONV_W, D_CONV)), full((1, LANES)), full((1, LANES)),
                  pl.BlockSpec((GB, CONV_W - 1, D_CONV), lambda i, s: (i, 0, 0)),
                  pl.BlockSpec((GB, WINDOW, LANES), lambda i, s: (i, 0, 0)),
                  pl.BlockSpec((GB, WINDOW, LANES), lambda i, s: (i, 0, 0))],
        out_specs=[pl.BlockSpec((rows, D_MODEL), lambda i, s: (i, 0)),
                   pl.BlockSpec((GB, WINDOW, LANES), lambda i, s: (i, 0, 0)),
                   pl.BlockSpec((GB, WINDOW, LANES), lambda i, s: (i, 0, 0)),
                   pl.BlockSpec((GB, CONV_W - 1, D_CONV), lambda i, s: (i, 0, 0))],
    )
    return pl.pallas_call(
        _even_sample_kernel,
        grid_spec=grid_spec,
        out_shape=[jax.ShapeDtypeStruct((Bd * Ld, D_MODEL), BF16),
                   jax.ShapeDtypeStruct((Bd, WINDOW, LANES), F32),
                   jax.ShapeDtypeStruct((Bd, WINDOW, LANES), F32),
                   jax.ShapeDtypeStruct((Bd, CONV_W - 1, D_CONV), F32)],
        compiler_params=_cparams(("arbitrary",)),
        name="even_sample",
    )(sinks, proj, cos, sin, cw, qg2, kg2, cache_conv, cache_k, cache_v)


def _cumsum_rows(x):
    C = x.shape[0]
    row = lax.broadcasted_iota(jnp.int32, x.shape, 0)
    s = 1
    while s < C:
        x = x + jnp.where(row >= s, pltpu.roll(x, s, 0), 0.0)
        s *= 2
    return x


def _group_ref(b, m, row):
    C, D = b.shape
    if 2 * m >= SUBLANES:
        n = C // (2 * m)
        b3 = b.reshape(n, 2 * m, D)
        return jnp.broadcast_to(b3[:, m - 1:m, :], (n, 2 * m, D)).reshape(C, D)
    r = row & (2 * m - 1)
    out = b
    for off in range(2 * m):
        if off == m - 1:
            continue
        shift = (off - (m - 1)) % C
        out = jnp.where(r == off, pltpu.roll(b, shift, 0), out)
    return out


def _gla_consts(C):
    s = np.arange(C)[:, None]
    t = np.arange(C)[None, :]
    masks = [s == t]
    m = C // 2
    while m >= 1:
        masks.append((s // (2 * m) == t // (2 * m)) & ((s & m) == 0) & ((t & m) != 0))
        m //= 2
    return jnp.asarray(np.stack(masks).astype(np.float32))


def _gla_chunk(qz, fz, v, lb, S, masks=None):
    C = qz.shape[0]
    q = _silu(qz)
    f = lb + (1.0 - lb) * _sigmoid(fz)
    k = 1.0 - f
    b = _cumsum_rows(jnp.log(f))
    vb = v.astype(BF16)

    inter = jnp.dot((q * jnp.exp(b)).astype(BF16), S.astype(BF16), preferred_element_type=F32)

    row = lax.broadcasted_iota(jnp.int32, (C, C_DK), 0)
    si = lax.broadcasted_iota(jnp.int32, (C, C), 0)
    ti = lax.broadcasted_iota(jnp.int32, (C, C), 1)
    nt = (((1,), (1,)), ((), ()))
    if C == SUBLANES:
        st = jnp.zeros((C, C), F32)
        for t in range(C):
            p = jnp.where(row <= t, q[t:t + 1, :] * k * jnp.exp(jnp.minimum(b[t:t + 1, :] - b, 0.0)), 0.0)
            st = jnp.where(ti == t, jnp.sum(p, axis=-1, keepdims=True), st)
        m = 0
    else:
        st = masks[0] * lax.dot_general(k.astype(BF16), q.astype(BF16), nt,
                                        preferred_element_type=F32)
        m = C // 2
    level = 1
    while m >= 1:
        rho = _group_ref(b, m, row)
        upper = (row & m) != 0
        d = b - rho
        x = (jnp.where(upper, q, k) * jnp.exp(jnp.where(upper, d, -d))).astype(BF16)
        st = st + masks[level] * lax.dot_general(x, x, nt, preferred_element_type=F32)
        level += 1
        m //= 2

    intra = lax.dot_general(st.astype(BF16), vb, (((0,), (0,)), ((), ())),
                            preferred_element_type=F32)

    b_last = b[C - 1:C, :]
    eye = (lax.broadcasted_iota(jnp.int32, (C_DK, C_DK), 0)
           == lax.broadcasted_iota(jnp.int32, (C_DK, C_DK), 1))
    dcol = jnp.sum(jnp.where(eye, jnp.broadcast_to(jnp.exp(b_last), (C_DK, C_DK)), 0.0),
                   axis=-1, keepdims=True)
    kd = k * jnp.exp(b_last - b)
    S_new = dcol * S + lax.dot_general(kd.astype(BF16), vb, (((0,), (0,)), ((), ())),
                                       preferred_element_type=F32)
    return inter + intra, S_new


def _gated_out(o, og, gz):
    ms = jnp.mean(o * o, axis=-1, keepdims=True)
    return (o * lax.rsqrt(ms + EPS) * og) * _silu(gz)


def _hgrn_prompt_kernel(q_ref, f_ref, i_ref, g_ref, lb_ref, og_ref, mask_ref, o_ref, s_ref, S):
    c = pl.program_id(2)

    @pl.when(c == 0)
    def _():
        S[...] = jnp.zeros_like(S)

    lb = lb_ref[...]
    og = og_ref[...]

    Sv = S[...]
    for n in range(GLA_STEP // GLA_CHUNK):
        rs = slice(n * GLA_CHUNK, (n + 1) * GLA_CHUNK)
        o, Sv = _gla_chunk(q_ref[rs, :], f_ref[rs, :], i_ref[rs, :], lb, Sv, mask_ref)
        o_ref[rs, :] = _gated_out(o, og, g_ref[rs, :]).astype(BF16)
    S[...] = Sv

    @pl.when(c == pl.num_programs(2) - 1)
    def _():
        s_ref[0, 0] = S[...]


def _hgrn_prompt(proj, B, L, T, lb, og):
    ns = L // GLA_STEP
    H = C_HEADS
    col = lambda off: pl.BlockSpec((GLA_STEP, LANES), lambda b, h, c: (b * ns + c, off + h))
    masks = _gla_consts(GLA_CHUNK)
    return pl.pallas_call(
        _hgrn_prompt_kernel,
        grid=(B, H, ns),
        in_specs=[col(0), col(H), col(2 * H), col(3 * H),
                  pl.BlockSpec((1, LANES), lambda b, h, c: (0, h)),
                  pl.BlockSpec((1, LANES), lambda b, h, c: (0, 0)),
                  pl.BlockSpec(masks.shape, lambda b, h, c: (0, 0, 0))],
        out_specs=[pl.BlockSpec((GLA_STEP, LANES), lambda b, h, c: (b * ns + c, h)),
                   pl.BlockSpec((1, 1, C_DK, C_DV), lambda b, h, c: (b, h, 0, 0))],
        out_shape=[jax.ShapeDtypeStruct((B * L, D_MODEL), BF16),
                   jax.ShapeDtypeStruct((B, H, C_DK, C_DV), F32)],
        scratch_shapes=[pltpu.VMEM((C_DK, C_DV), F32)],
        compiler_params=_cparams(("arbitrary", "arbitrary", "arbitrary")),
        name="hgrn_prompt",
    )(proj, proj, proj, proj, lb, og, masks)


def _hgrn_sample_kernel(p_ref, lb_ref, og_ref, s0_ref, o_ref, s_ref):
    Ld = p_ref.shape[0] // SAMPLE_GB
    og = og_ref[...]
    for bb in range(SAMPLE_GB):
        rs = slice(bb * Ld, (bb + 1) * Ld)
        for h in range(C_HEADS):
            cs = lambda part: slice((part * C_HEADS + h) * LANES, (part * C_HEADS + h + 1) * LANES)
            o, S_new = _gla_chunk(p_ref[rs, cs(0)], p_ref[rs, cs(1)], p_ref[rs, cs(2)],
                                  lb_ref[:, h * LANES:(h + 1) * LANES], s0_ref[bb, h])
            s_ref[bb, h] = S_new
            o_ref[rs, h * LANES:(h + 1) * LANES] = _gated_out(o, og, p_ref[rs, cs(3)]).astype(BF16)


def _hgrn_sample(proj, Bd, Ld, row0, lb, og, s0):
    GB = SAMPLE_GB
    rows = GB * Ld
    rb0 = row0 // rows
    H = C_HEADS
    return pl.pallas_call(
        _hgrn_sample_kernel,
        grid=(Bd // GB,),
        in_specs=[pl.BlockSpec((rows, 4 * C_HK), lambda i: (rb0 + i, 0)),
                  pl.BlockSpec((1, C_HK), lambda i: (0, 0)),
                  pl.BlockSpec((1, LANES), lambda i: (0, 0)),
                  pl.BlockSpec((GB, H, C_DK, C_DV), lambda i: (i, 0, 0, 0))],
        out_specs=[pl.BlockSpec((rows, D_MODEL), lambda i: (i, 0)),
                   pl.BlockSpec((GB, H, C_DK, C_DV), lambda i: (i, 0, 0, 0))],
        out_shape=[jax.ShapeDtypeStruct((Bd * Ld, D_MODEL), BF16),
                   jax.ShapeDtypeStruct((Bd, H, C_DK, C_DV), F32)],
        compiler_params=_cparams(("arbitrary",)),
        name="hgrn_sample",
    )(proj, lb, og, s0)


def _pack_bf16_pairs(xb):
    n = xb.shape[1] // 2
    lo = lax.bitcast_convert_type(xb[:, :n].astype(F32), jnp.uint32)
    hi = lax.bitcast_convert_type(xb[:, n:].astype(F32), jnp.uint32)
    return (lo >> 16) | (hi & jnp.uint32(0xFFFF0000))


def _unpack_bf16_pairs(w):
    lo = lax.bitcast_convert_type(w << 16, F32)
    hi = lax.bitcast_convert_type(w & jnp.uint32(0xFFFF0000), F32)
    return jnp.concatenate([lo, hi], axis=1).astype(BF16)


def _out_proj_kernel(n_mix, n_x, npt, *refs):
    m = _read_rows(refs[:n_mix], npt)
    x = _read_rows(refs[n_mix:n_mix + n_x], npt)
    w_ref, g_ref, wr_ref, br_ref, x1_ref, h_ref, lg_ref = refs[n_mix + n_x:]
    x1 = x + jnp.dot(m, w_ref[...], preferred_element_type=F32)
    x1_ref[...] = x1
    ms = jnp.mean(x1 * x1, axis=-1, keepdims=True)
    hb = (x1 * lax.rsqrt(ms + EPS) * g_ref[...]).astype(BF16)
    h_ref[...] = _pack_bf16_pairs(hb)
    lg_ref[...] = lax.dot_general(wr_ref[...], hb, (((1,), (1,)), ((), ())),
                                  preferred_element_type=F32) + br_ref[...]


def _out_proj(mix, w, x, g, wr, br, tm):
    K, D = w.shape
    m_specs, m_arrays, npt, T = _row_source(mix, tm)
    x_specs, x_arrays, npt_x, _ = _row_source(x, tm)
    assert npt_x in (0, npt)
    row = lambda n: pl.BlockSpec((tm, n), lambda i: (i, 0))
    full = lambda a, b: pl.BlockSpec((a, b), lambda i: (0, 0))
    return pl.pallas_call(
        functools.partial(_out_proj_kernel, len(m_arrays), len(x_arrays), npt),
        grid=(T // tm,),
        in_specs=m_specs + x_specs + [full(K, D), full(1, D), full(LOGIT_W, D), full(LOGIT_W, 1)],
        out_specs=[row(D), row(D // 2), pl.BlockSpec((LOGIT_W, tm), lambda i: (0, i))],
        out_shape=[jax.ShapeDtypeStruct((T, D), F32), jax.ShapeDtypeStruct((T, D // 2), jnp.uint32),
                   jax.ShapeDtypeStruct((LOGIT_W, T), F32)],
        compiler_params=_cparams(("arbitrary",)),
        name="out_proj",
    )(*m_arrays, *x_arrays, w, g.reshape(1, D), wr, br)


def _route_kernel(lg_ref, slot_ref, gate_ref, cnt_ref, carry, total, tri):
    phase = pl.program_id(0)
    step = pl.program_id(1)
    tm = lg_ref.shape[1]
    G, E = N_GROUPS, EXP_PER_GROUP
    row8 = lax.broadcasted_iota(jnp.int32, (G, tm), 0)

    glog = lg_ref[0:G, :]
    gmax = jnp.max(glog, axis=0, keepdims=True)
    g_idx = jnp.min(jnp.where(glog == gmax, row8, G), axis=0, keepdims=True)
    g_p = 1.0 / jnp.sum(jnp.exp(glog - gmax), axis=0, keepdims=True)

    elog = jnp.zeros((E, tm), F32)
    for g in range(G):
        elog = jnp.where(g_idx == g, lg_ref[G + g * E:G + (g + 1) * E, :], elog)
    emax = jnp.max(elog, axis=0, keepdims=True)
    ee = jnp.exp(elog - emax)
    prob = ee / jnp.sum(ee, axis=0, keepdims=True)
    p1 = jnp.max(prob, axis=0, keepdims=True)
    i1 = jnp.min(jnp.where(prob == p1, row8, E), axis=0, keepdims=True)
    prob2 = jnp.where(row8 == i1, -1.0, prob)
    p2 = jnp.max(prob2, axis=0, keepdims=True)
    i2 = jnp.min(jnp.where(prob2 == p2, row8, E), axis=0, keepdims=True)
    psum = p1 + p2
    w1 = p1 / psum * g_p
    w2 = p2 / psum * g_p

    erow = lax.broadcasted_iota(jnp.int32, (N_EXPERTS, tm), 0)
    oh1 = erow == g_idx * E + i1
    oh2 = erow == g_idx * E + i2
    both = jnp.where(oh1 | oh2, 1.0, 0.0)
    tile_cnt = jnp.sum(both, axis=1, keepdims=True)

    @pl.when((phase == 0) & (step == 0))
    def _():
        total[...] = jnp.zeros_like(total)
        r = lax.broadcasted_iota(jnp.int32, (tm, tm), 0)
        c = lax.broadcasted_iota(jnp.int32, (tm, tm), 1)
        tri[...] = jnp.where(r < c, 1.0, 0.0).astype(BF16)

    @pl.when(phase == 0)
    def _():
        total[...] += tile_cnt

    @pl.when((phase == 1) & (step == 0))
    def _():
        carry[...] = jnp.zeros_like(carry)

    @pl.when(phase == 1)
    def _():
        cnt = total[...]
        padded = jnp.floor((cnt + (MOE_BLOCK - 1)) * (1.0 / MOE_BLOCK)) * MOE_BLOCK
        pb = jnp.broadcast_to(padded, (N_EXPERTS, LANES))
        pstart = (_cumsum_rows(pb) - pb)[:, 0:1]
        before = (jnp.dot(both.astype(BF16), tri[...], preferred_element_type=F32)
                  + (carry[...] + pstart))
        s1 = jnp.sum(jnp.where(oh1, before, 0.0), axis=0, keepdims=True)
        s2 = jnp.sum(jnp.where(oh2, before, 0.0), axis=0, keepdims=True)
        carry[...] += tile_cnt
        slot_ref[...] = jnp.concatenate([s1, s2], axis=0).astype(jnp.int32)
        grow = lax.broadcasted_iota(jnp.int32, (LOGIT_W, tm), 0)
        gate_ref[...] = jnp.where(grow == 0, w1, jnp.where(grow == 1, w2, 0.0)).T
        cnt_ref[...] = jnp.broadcast_to(cnt, (N_EXPERTS, LANES))


def _route(logits_t):
    T = logits_t.shape[1]
    tm = ROUTE_TM
    return pl.pallas_call(
        _route_kernel,
        grid=(2, T // tm),
        in_specs=[pl.BlockSpec((LOGIT_W, tm), lambda p, i: (0, i))],
        out_specs=[pl.BlockSpec((2, tm), lambda p, i: (0, i * p)),
                   pl.BlockSpec((tm, LOGIT_W), lambda p, i: (i * p, 0)),
                   pl.BlockSpec((N_EXPERTS, LANES), lambda p, i: (0, 0))],
        out_shape=[jax.ShapeDtypeStruct((2, T), jnp.int32),
                   jax.ShapeDtypeStruct((T, LOGIT_W), F32),
                   jax.ShapeDtypeStruct((N_EXPERTS, LANES), F32)],
        scratch_shapes=[pltpu.VMEM((N_EXPERTS, 1), F32), pltpu.VMEM((N_EXPERTS, 1), F32),
                        pltpu.VMEM((tm, tm), BF16)],
        compiler_params=_cparams(("arbitrary", "arbitrary")),
        name="route",
    )(logits_t)


DISPATCH_TM = 512


def _dispatch_kernel(T, slot_ref, h_ref, xs_in_ref, xs_ref, sem):
    del xs_in_ref
    base = pl.program_id(0) * DISPATCH_TM

    def copy(r, k):
        return pltpu.make_async_copy(h_ref.at[pl.ds(r, 1)],
                                     xs_ref.at[pl.ds(slot_ref[k * T + base + r], 1)], sem)

    def start(r, c):
        copy(r, 0).start()
        copy(r, 1).start()
        return c

    def wait(r, c):
        copy(r, 0).wait()
        copy(r, 1).wait()
        return c

    lax.fori_loop(0, DISPATCH_TM, start, 0, unroll=8)
    lax.fori_loop(0, DISPATCH_TM, wait, 0, unroll=8)


def _dispatch(slots, h, n_slots):
    T, D = h.shape
    xs0 = jnp.zeros((n_slots, D), h.dtype)
    grid_spec = pltpu.PrefetchScalarGridSpec(
        num_scalar_prefetch=1,
        grid=(T // DISPATCH_TM,),
        in_specs=[pl.BlockSpec((DISPATCH_TM, D), lambda i, s: (i, 0)),
                  pl.BlockSpec(memory_space=pl.ANY)],
        out_specs=pl.BlockSpec(memory_space=pl.ANY),
        scratch_shapes=[pltpu.SemaphoreType.DMA(())],
    )
    return pl.pallas_call(
        functools.partial(_dispatch_kernel, T),
        grid_spec=grid_spec,
        out_shape=jax.ShapeDtypeStruct((n_slots, D), h.dtype),
        input_output_aliases={2: 0},
        compiler_params=_cparams(("arbitrary",)),
        name="moe_dispatch",
    )(slots, h, xs0)


def _ffn_kernel(nblk, layer, meta_ref, x_ref, w1_hbm, w3_hbm, w2_hbm, y_ref,
                wf1, wf3, wf2, sems, wb1, wb3, wb2):
    b = pl.program_id(0)
    used = b < meta_ref[nblk]
    e = meta_ref[b]
    new_expert = (b == 0) | (e != meta_ref[jnp.maximum(b - 1, 0)])
    slot = meta_ref[nblk + 1 + e] % 2
    nxt = meta_ref[nblk + 1 + N_EXPERTS + e]

    def fetch(expert, s):
        return (pltpu.make_async_copy(w1_hbm.at[layer, expert], wf1.at[s], sems.at[s, 0]),
                pltpu.make_async_copy(w3_hbm.at[layer, expert], wf3.at[s], sems.at[s, 1]),
                pltpu.make_async_copy(w2_hbm.at[layer, expert], wf2.at[s], sems.at[s, 2]))

    @pl.when(used & (b == 0))
    def _():
        for cp in fetch(e, slot):
            cp.start()

    @pl.when(used & new_expert)
    def _():
        for cp in fetch(e, slot):
            cp.wait()

        @pl.when(nxt >= 0)
        def _():
            for cp in fetch(nxt, 1 - slot):
                cp.start()

        wb1[...] = wf1[slot].astype(BF16)
        wb3[...] = wf3[slot].astype(BF16)
        wb2[...] = wf2[slot].astype(BF16)

    @pl.when(used)
    def _():
        x = _unpack_bf16_pairs(x_ref[...])
        a = jnp.dot(x, wb1[...], preferred_element_type=F32)
        c = jnp.dot(x, wb3[...], preferred_element_type=F32)
        hid = (a * _sigmoid(a)) * c
        y_ref[...] = jnp.dot(hid.astype(BF16), wb2[...], preferred_element_type=F32)

    @pl.when(jnp.logical_not(used))
    def _():
        y_ref[...] = jnp.zeros_like(y_ref)


def _ffn(meta, xs, w1, w3, w2, layer):
    n_slots = xs.shape[0]
    D = w1.shape[2]
    nblk = n_slots // MOE_BLOCK
    grid_spec = pltpu.PrefetchScalarGridSpec(
        num_scalar_prefetch=1,
        grid=(nblk,),
        in_specs=[pl.BlockSpec((MOE_BLOCK, xs.shape[1]), lambda b, e: (b, 0)),
                  pl.BlockSpec(memory_space=pl.ANY), pl.BlockSpec(memory_space=pl.ANY),
                  pl.BlockSpec(memory_space=pl.ANY)],
        out_specs=pl.BlockSpec((MOE_BLOCK, D), lambda b, e: (b, 0)),
        scratch_shapes=[pltpu.VMEM((2, D, D_EXPERT), F32), pltpu.VMEM((2, D, D_EXPERT), F32),
                        pltpu.VMEM((2, D_EXPERT, D), F32), pltpu.SemaphoreType.DMA((2, 3)),
                        pltpu.VMEM((D, D_EXPERT), BF16), pltpu.VMEM((D, D_EXPERT), BF16),
                        pltpu.VMEM((D_EXPERT, D), BF16)],
    )
    return pl.pallas_call(
        functools.partial(_ffn_kernel, nblk, layer),
        grid_spec=grid_spec,
        out_shape=jax.ShapeDtypeStruct((n_slots, D), F32),
        compiler_params=_cparams(("arbitrary",)),
        name="moe_ffn",
    )(meta, xs, w1, w3, w2)


COMBINE_TM = 512
COMBINE_CHUNK = 128


def _combine_kernel(T, npt, slot_ref, x_ref, gate_ref, ys_ref, *refs):
    out_refs, (buf, sems) = refs[:-2], refs[-2:]
    i = pl.program_id(0)
    base = i * COMBINE_TM
    n_chunks = COMBINE_TM // COMBINE_CHUNK

    def copy(r, k, q):
        return pltpu.make_async_copy(ys_ref.at[pl.ds(slot_ref[k * T + base + r], 1)],
                                     buf.at[k, pl.ds(r, 1)], sems.at[q])

    def for_chunk_rows(q, fn):
        def body(r, c):
            fn(copy(r, 0, q))
            fn(copy(r, 1, q))
            return c
        lax.fori_loop(q * COMBINE_CHUNK, (q + 1) * COMBINE_CHUNK, body, 0, unroll=8)

    for q in range(n_chunks):
        for_chunk_rows(q, lambda cp: cp.start())

    def emit(rs, y):
        if npt == 0:
            out_refs[0][rs, :] = y
        else:
            @pl.when(i < npt)
            def _():
                out_refs[0][rs, :] = y

            @pl.when(i >= npt)
            def _():
                out_refs[1][rs, :] = y

    for q in range(n_chunks):
        for_chunk_rows(q, lambda cp: cp.wait())
        rs = slice(q * COMBINE_CHUNK, (q + 1) * COMBINE_CHUNK)
        g = gate_ref[rs, :]
        emit(rs, x_ref[rs, :] + (buf[0, rs, :] * g[:, 0:1] + buf[1, rs, :] * g[:, 1:2]))


def _combine(slots, x1, gates, ys, split_rows=0):
    T, D = x1.shape
    tm = COMBINE_TM
    npt = split_rows // tm
    if npt == 0:
        out_specs = pl.BlockSpec((tm, D), lambda i, s: (i, 0))
        out_shape = jax.ShapeDtypeStruct((T, D), F32)
    else:
        out_specs = [pl.BlockSpec((tm, D), lambda i, s: (jnp.minimum(i, npt - 1), 0)),
                     pl.BlockSpec((tm, D), lambda i, s: (jnp.maximum(i - npt, 0), 0))]
        out_shape = [jax.ShapeDtypeStruct((split_rows, D), F32),
                     jax.ShapeDtypeStruct((T - split_rows, D), F32)]
    grid_spec = pltpu.PrefetchScalarGridSpec(
        num_scalar_prefetch=1,
        grid=(T // tm,),
        in_specs=[pl.BlockSpec((tm, D), lambda i, s: (i, 0)),
                  pl.BlockSpec((tm, LOGIT_W), lambda i, s: (i, 0)),
                  pl.BlockSpec(memory_space=pl.ANY)],
        out_specs=out_specs,
        scratch_shapes=[pltpu.VMEM((2, tm, D), F32),
                        pltpu.SemaphoreType.DMA((COMBINE_TM // COMBINE_CHUNK,))],
    )
    return pl.pallas_call(
        functools.partial(_combine_kernel, T, npt),
        grid_spec=grid_spec,
        out_shape=out_shape,
        compiler_params=_cparams(("arbitrary",)),
        name="moe_combine",
    )(slots, x1, gates, ys)


def _moe(x1, h, logits, w1, w3, w2, layer, split_rows=0):
    T = x1.shape[0]
    slot_l, gate_l, cnt = _route(logits)
    slots = slot_l.reshape(2 * T)
    n_blocks = -(-(2 * T) // MOE_BLOCK) + N_EXPERTS
    counts = cnt[:, 0].astype(jnp.int32)
    pend = jnp.cumsum((counts + MOE_BLOCK - 1) // MOE_BLOCK * MOE_BLOCK)
    starts = jnp.arange(n_blocks, dtype=jnp.int32) * MOE_BLOCK
    blk_e = jnp.minimum(jnp.sum((pend[None, :] <= starts[:, None]).astype(jnp.int32), axis=1),
                        N_EXPERTS - 1)
    present = (counts > 0).astype(jnp.int32)
    rank = jnp.cumsum(present) - present
    ids = jnp.where(present > 0, jnp.arange(N_EXPERTS, dtype=jnp.int32), N_EXPERTS)
    after = jnp.concatenate([lax.cummin(ids[::-1])[::-1][1:], jnp.full((1,), N_EXPERTS, jnp.int32)])
    nxt = jnp.where(after < N_EXPERTS, after, -1)
    meta = jnp.concatenate([blk_e, pend[-1:] // MOE_BLOCK, rank, nxt]).astype(jnp.int32)
    xs = _dispatch(slots, h, n_blocks * MOE_BLOCK)
    ys = _ffn(meta, xs, w1, w3, w2, layer)
    return _combine(slots, x1, gate_l, ys, split_rows)


def _router_weights(w_grp, b_grp, w_exp, b_exp):
    D = w_grp.shape[0]
    pad = LOGIT_W - N_GROUPS - N_EXPERTS
    wr = jnp.concatenate([w_grp, w_exp, jnp.zeros((D, pad), F32)], axis=1).astype(BF16)
    br = jnp.concatenate([b_grp, b_exp, jnp.zeros((pad,), F32)]).reshape(LOGIT_W, 1)
    return wr.T, br


def kernel(x_prompt, x_sample, cache_conv, cache_k, cache_v, state_hgrn, norm_mix, norm_ffn,
           ev_w_in, ev_conv, ev_q_norm, ev_k_norm, ev_sinks, ev_w_out,
           od_w_in, od_lb, od_o_norm, od_w_out,
           moe_w_grp, moe_b_grp, moe_w_exp, moe_b_exp, moe_w1, moe_w3, moe_w2):
    B, L, D = x_prompt.shape
    Bd, Ld, _ = x_sample.shape
    Tp = B * L
    T = Tp + Bd * Ld
    x = (x_prompt.reshape(Tp, D), x_sample.reshape(Bd * Ld, D))

    proj = _norm_matmul(x, norm_mix[0], ev_w_in[0].astype(BF16), 512)
    qg2 = jnp.tile(ev_q_norm[0], 2).reshape(1, LANES)
    kg2 = jnp.tile(ev_k_norm[0], 2).reshape(1, LANES)
    mix_p, k_p, v_p, conv_p = _even_prompt(proj, B, L, T, ev_conv[0], qg2, kg2, ev_sinks[0])
    mix_s, k_s, v_s, conv_s = _even_sample(
        proj, Bd, Ld, Tp, ev_conv[0], qg2, kg2, ev_sinks[0], cache_conv[0],
        cache_k[0].reshape(Bd, WINDOW, LANES), cache_v[0].reshape(Bd, WINDOW, LANES))
    wr, br = _router_weights(moe_w_grp[0], moe_b_grp[0], moe_w_exp[0], moe_b_exp[0])
    x1, h, logits = _out_proj((mix_p, mix_s), ev_w_out[0].astype(BF16), x, norm_ffn[0], wr, br, 512)
    x = _moe(x1, h, logits, moe_w1, moe_w3, moe_w2, 0)

    lb_all = jnp.cumsum(jax.nn.softmax(od_lb.astype(F32), axis=0), axis=0)
    lb = (lb_all - lb_all[0])[1].reshape(1, C_HK)
    og = od_o_norm[0].reshape(1, C_DV)
    proj = _norm_matmul(x, norm_mix[1], od_w_in[0].astype(BF16), 256)
    o_p, s_p = _hgrn_prompt(proj, B, L, T, lb, og)
    o_s, s_s = _hgrn_sample(proj, Bd, Ld, Tp, lb, og, state_hgrn[0])
    wr, br = _router_weights(moe_w_grp[1], moe_b_grp[1], moe_w_exp[1], moe_b_exp[1])
    x1, h, logits = _out_proj((o_p, o_s), od_w_out[0].astype(BF16), x, norm_ffn[1], wr, br, 512)
    y_p, y_s = _moe(x1, h, logits, moe_w1, moe_w3, moe_w2, 1, split_rows=Tp)

    y_prompt = y_p.reshape(B, L, D)
    y_sample = y_s.reshape(Bd, Ld, D)
    return (y_prompt, y_sample,
            conv_p[None], k_p.reshape(1, B, WINDOW, N_KV, HEAD_DIM), v_p.reshape(1, B, WINDOW, N_KV, HEAD_DIM),
            s_p[None],
            conv_s[None], k_s.reshape(1, Bd, WINDOW, N_KV, HEAD_DIM), v_s.reshape(1, Bd, WINDOW, N_KV, HEAD_DIM),
            s_s[None])
```

```python
import functools

import numpy as np
import jax
import jax.numpy as jnp
from jax import lax
from jax.experimental import pallas as pl
from jax.experimental.pallas import tpu as pltpu

F32 = jnp.float32
BF16 = jnp.bfloat16

D_MODEL = 1024
PAST_LEN = 16384
D_CONV = 512
CONV_W = 3
HEAD_DIM = 64
N_Q = 8
N_KV = 2
GQA_G = N_Q // N_KV
WINDOW = 128
ROPE_THETA = 10000.0
D_IN_EVEN = 3 * D_CONV + (N_Q + 2 * N_KV) * HEAD_DIM
C_HEADS = 8
C_DK = 128
C_DV = 128
C_HK = C_HEADS * C_DK
N_GROUPS = 8
EXP_PER_GROUP = 8
N_EXPERTS = N_GROUPS * EXP_PER_GROUP
D_EXPERT = 512
MOE_BLOCK = 256
EPS = 1e-6

LANES = 128
SUBLANES = 8
VMEM_LIMIT = 56 * 1024 * 1024

GLA_CHUNK = 128
GLA_STEP = 512
SAMPLE_GB = 2
ROUTE_TM = 512
LOGIT_W = 128


def _cparams(sem):
    return pltpu.CompilerParams(dimension_semantics=sem, vmem_limit_bytes=VMEM_LIMIT)


def _sigmoid(x):
    return 1.0 / (1.0 + jnp.exp(-x))


def _silu(x):
    return x * (0.5 * jnp.tanh(0.5 * x) + 0.5)


def _row_source(src, tm):
    if isinstance(src, tuple):
        a, b = src
        n = a.shape[1]
        npt = a.shape[0] // tm
        specs = [pl.BlockSpec((tm, n), lambda i, *_: (jnp.minimum(i, npt - 1), 0)),
                 pl.BlockSpec((tm, n), lambda i, *_: (jnp.maximum(i - npt, 0), 0))]
        return specs, [a, b], npt, a.shape[0] + b.shape[0]
    return [pl.BlockSpec((tm, src.shape[1]), lambda i, *_: (i, 0))], [src], 0, src.shape[0]


def _read_rows(refs, npt):
    if len(refs) == 2:
        return jnp.where(pl.program_id(0) < npt, refs[0][...], refs[1][...])
    return refs[0][...]


def _norm_matmul_kernel(n_src, npt, *refs):
    x = _read_rows(refs[:n_src], npt)
    g_ref, w_ref, o_ref = refs[n_src:]
    ms = jnp.mean(x * x, axis=-1, keepdims=True)
    h = (x * lax.rsqrt(ms + EPS) * g_ref[...]).astype(BF16)
    o_ref[...] = jnp.dot(h, w_ref[...], preferred_element_type=F32)


def _norm_matmul(x, g, w, tm):
    D, N = w.shape
    specs, arrays, npt, T = _row_source(x, tm)
    return pl.pallas_call(
        functools.partial(_norm_matmul_kernel, len(arrays), npt),
        grid=(T // tm,),
        in_specs=specs + [pl.BlockSpec((1, D), lambda i: (0, 0)),
                          pl.BlockSpec((D, N), lambda i: (0, 0), pipeline_mode=pl.Buffered(1))],
        out_specs=pl.BlockSpec((tm, N), lambda i: (i, 0)),
        out_shape=jax.ShapeDtypeStruct((T, N), F32),
        compiler_params=_cparams(("arbitrary",)),
        name="norm_matmul",
    )(*arrays, g.reshape(1, D), w)


def _rope_tables(pos):
    inv = ROPE_THETA ** (-jnp.arange(0, HEAD_DIM, 2, dtype=F32) / HEAD_DIM)
    ang = pos.astype(F32)[:, None] * inv[None, :]
    cos = jnp.cos(ang)
    sin = jnp.sin(ang)
    return (jnp.concatenate([cos, cos, cos, cos], axis=1),
            jnp.concatenate([-sin, sin, -sin, sin], axis=1))


def _headnorm_rope(x, g2, cos, sin):
    lane = lax.broadcasted_iota(jnp.int32, x.shape, 1)
    lo = lane < HEAD_DIM
    x2 = x * x
    s_lo = jnp.sum(jnp.where(lo, x2, 0.0), axis=-1, keepdims=True)
    s_hi = jnp.sum(jnp.where(lo, 0.0, x2), axis=-1, keepdims=True)
    ms = jnp.where(lo, s_lo, s_hi) * (1.0 / HEAD_DIM)
    y = x * lax.rsqrt(ms + EPS) * g2
    first_half = (lane & (HEAD_DIM // 2)) == 0
    swapped = jnp.where(first_half, pltpu.roll(y, LANES - HEAD_DIM // 2, 1),
                        pltpu.roll(y, HEAD_DIM // 2, 1))
    return y * cos + swapped * sin


def _gated_conv(gb, u, c2, c1, cw):
    R = u.shape[0]
    row = lax.broadcasted_iota(jnp.int32, u.shape, 0)
    u1 = jnp.where(row == 0, c1, pltpu.roll(u, 1, 0))
    u2 = jnp.where(row == 0, c2, jnp.where(row == 1, c1, pltpu.roll(u, 2, 0)))
    del R
    return gb * (cw[0:1, :] * u2 + cw[1:2, :] * u1 + cw[2:3, :] * u)


def _band_attention(qs, kk, vv, valid, sinkv):
    s = lax.dot_general(qs.astype(BF16), kk.astype(BF16), (((1,), (1,)), ((), ())),
                        preferred_element_type=F32) * (HEAD_DIM ** -0.5)
    s = jnp.where(valid, s, -jnp.inf)
    m = jnp.maximum(jnp.max(s, axis=-1, keepdims=True), sinkv)
    e = jnp.exp(s - m)
    den = jnp.sum(e, axis=-1, keepdims=True) + jnp.exp(sinkv - m)
    p = e / den
    return jnp.dot(p.astype(BF16), vv.astype(BF16), preferred_element_type=F32)


def _sink_column(sinks_ref, hk, rows_per_head):
    R = GQA_G * rows_per_head
    row = lax.broadcasted_iota(jnp.int32, (R, 1), 0)
    col = jnp.full((R, 1), sinks_ref[hk * GQA_G + GQA_G - 1], F32)
    for j in range(GQA_G - 2, -1, -1):
        col = jnp.where(row < (j + 1) * rows_per_head, sinks_ref[hk * GQA_G + j], col)
    return col


def _even_prompt_kernel(sinks_ref, proj_ref, cos_ref, sin_ref, cw_ref, qg_ref, kg_ref,
                        mix_ref, kl_ref, vl_ref, cl_ref, kprev, vprev, ucar):
    blk = pl.program_id(1)
    W = WINDOW

    @pl.when(blk == 0)
    def _():
        kprev[...] = jnp.zeros_like(kprev)
        vprev[...] = jnp.zeros_like(vprev)
        ucar[...] = jnp.zeros_like(ucar)

    gb = proj_ref[:, 0:D_CONV]
    u = proj_ref[:, D_CONV:2 * D_CONV] * proj_ref[:, 2 * D_CONV:3 * D_CONV]
    car = ucar[...]
    a_out = _gated_conv(gb, u, car[0:1, :], car[1:2, :], cw_ref[...])
    ucar[0:2, :] = u[W - 2:W, :]
    mix_ref[:, 0:D_CONV] = a_out.astype(BF16)

    cos = cos_ref[...]
    sin = sin_ref[...]
    q0 = 3 * D_CONV
    k0 = q0 + N_Q * HEAD_DIM
    v0 = k0 + N_KV * HEAD_DIM
    k_r = _headnorm_rope(proj_ref[:, k0:k0 + LANES], kg_ref[...], cos, sin)
    v_r = proj_ref[:, v0:v0 + LANES]
    q_r = [_headnorm_rope(proj_ref[:, q0 + LANES * j:q0 + LANES * (j + 1)], qg_ref[...], cos, sin)
           for j in range(N_Q * HEAD_DIM // LANES)]
    k_p = kprev[...]
    v_p = vprev[...]

    R = GQA_G * W
    i = lax.broadcasted_iota(jnp.int32, (R, 2 * W), 0) & (W - 1)
    j = lax.broadcasted_iota(jnp.int32, (R, 2 * W), 1)
    diff = i + W - j
    valid = (diff >= 0) & (diff <= W) & ((blk > 0) | (j >= W))

    for hk in range(N_KV):
        ls = slice(hk * HEAD_DIM, (hk + 1) * HEAD_DIM)
        kk = jnp.concatenate([k_p[:, ls], k_r[:, ls]], axis=0)
        vv = jnp.concatenate([v_p[:, ls], v_r[:, ls]], axis=0)
        heads = []
        for g in range(GQA_G):
            h = hk * GQA_G + g
            tile = q_r[h // 2]
            heads.append(tile[:, (h % 2) * HEAD_DIM:(h % 2 + 1) * HEAD_DIM])
        qs = jnp.concatenate(heads, axis=0)
        o = _band_attention(qs, kk, vv, valid, _sink_column(sinks_ref, hk, W))
        for g in range(GQA_G):
            h = hk * GQA_G + g
            mix_ref[:, D_CONV + h * HEAD_DIM:D_CONV + (h + 1) * HEAD_DIM] = \
                o[g * W:(g + 1) * W, :].astype(BF16)

    kprev[...] = k_r
    vprev[...] = v_r

    @pl.when(blk == pl.num_programs(1) - 1)
    def _():
        kl_ref[0] = k_r
        vl_ref[0] = v_r
        cl_ref[0] = u[W - 2:W, :]


def _even_prompt(proj, B, L, T, cw, qg2, kg2, sinks):
    nb = L // WINDOW
    cos, sin = _rope_tables(jnp.arange(L, dtype=jnp.int32))
    full = lambda shape: pl.BlockSpec(shape, lambda b, i, *_: tuple(0 for _ in shape))
    grid_spec = pltpu.PrefetchScalarGridSpec(
        num_scalar_prefetch=1,
        grid=(B, nb),
        in_specs=[pl.BlockSpec((WINDOW, D_IN_EVEN), lambda b, i, s: (b * nb + i, 0)),
                  pl.BlockSpec((WINDOW, LANES), lambda b, i, s: (i, 0)),
                  pl.BlockSpec((WINDOW, LANES), lambda b, i, s: (i, 0)),
                  full((CONV_W, D_CONV)), full((1, LANES)), full((1, LANES))],
        out_specs=[pl.BlockSpec((WINDOW, D_MODEL), lambda b, i, s: (b * nb + i, 0)),
                   pl.BlockSpec((1, WINDOW, LANES), lambda b, i, s: (b, 0, 0)),
                   pl.BlockSpec((1, WINDOW, LANES), lambda b, i, s: (b, 0, 0)),
                   pl.BlockSpec((1, CONV_W - 1, D_CONV), lambda b, i, s: (b, 0, 0))],
        scratch_shapes=[pltpu.VMEM((WINDOW, LANES), F32), pltpu.VMEM((WINDOW, LANES), F32),
                        pltpu.VMEM((SUBLANES, D_CONV), F32)],
    )
    return pl.pallas_call(
        _even_prompt_kernel,
        grid_spec=grid_spec,
        out_shape=[jax.ShapeDtypeStruct((B * L, D_MODEL), BF16),
                   jax.ShapeDtypeStruct((B, WINDOW, LANES), F32),
                   jax.ShapeDtypeStruct((B, WINDOW, LANES), F32),
                   jax.ShapeDtypeStruct((B, CONV_W - 1, D_CONV), F32)],
        compiler_params=_cparams(("arbitrary", "arbitrary")),
        name="even_prompt",
    )(sinks, proj, cos, sin, cw, qg2, kg2)


def _even_sample_kernel(sinks_ref, proj_ref, cos_ref, sin_ref, cw_ref, qg_ref, kg_ref,
                        cc_ref, ck_ref, cv_ref, mix_ref, ko_ref, vo_ref, co_ref):
    W = WINDOW
    Ld = cos_ref.shape[0]
    cos = cos_ref[...]
    sin = sin_ref[...]
    q0 = 3 * D_CONV
    k0 = q0 + N_Q * HEAD_DIM
    v0 = k0 + N_KV * HEAD_DIM
    R = GQA_G * Ld
    i = lax.broadcasted_iota(jnp.int32, (R, 2 * W), 0) % Ld
    j = lax.broadcasted_iota(jnp.int32, (R, 2 * W), 1)
    diff = i + W - j
    valid = (diff >= 0) & (diff <= W)
    zpad = jnp.zeros((W - Ld, HEAD_DIM), F32)

    for bb in range(SAMPLE_GB):
        rs = slice(bb * Ld, (bb + 1) * Ld)
        gb = proj_ref[rs, 0:D_CONV]
        u = proj_ref[rs, D_CONV:2 * D_CONV] * proj_ref[rs, 2 * D_CONV:3 * D_CONV]
        car = cc_ref[bb]
        a_out = _gated_conv(gb, u, car[0:1, :], car[1:2, :], cw_ref[...])
        co_ref[bb] = u[Ld - 2:Ld, :]
        mix_ref[rs, 0:D_CONV] = a_out.astype(BF16)

        k_r = _headnorm_rope(proj_ref[rs, k0:k0 + LANES], kg_ref[...], cos, sin)
        v_r = proj_ref[rs, v0:v0 + LANES]
        q_r = [_headnorm_rope(proj_ref[rs, q0 + LANES * t:q0 + LANES * (t + 1)], qg_ref[...], cos, sin)
               for t in range(N_Q * HEAD_DIM // LANES)]
        k_c = ck_ref[bb]
        v_c = cv_ref[bb]
        ko_ref[bb, 0:W - Ld, :] = k_c[Ld:W, :]
        ko_ref[bb, W - Ld:W, :] = k_r
        vo_ref[bb, 0:W - Ld, :] = v_c[Ld:W, :]
        vo_ref[bb, W - Ld:W, :] = v_r

        for hk in range(N_KV):
            ls = slice(hk * HEAD_DIM, (hk + 1) * HEAD_DIM)
            kk = jnp.concatenate([k_c[:, ls], k_r[:, ls], zpad], axis=0)
            vv = jnp.concatenate([v_c[:, ls], v_r[:, ls], zpad], axis=0)
            heads = []
            for g in range(GQA_G):
                h = hk * GQA_G + g
                tile = q_r[h // 2]
                heads.append(tile[:, (h % 2) * HEAD_DIM:(h % 2 + 1) * HEAD_DIM])
            qs = jnp.concatenate(heads, axis=0)
            o = _band_attention(qs, kk, vv, valid, _sink_column(sinks_ref, hk, Ld))
            for g in range(GQA_G):
                h = hk * GQA_G + g
                mix_ref[rs, D_CONV + h * HEAD_DIM:D_CONV + (h + 1) * HEAD_DIM] = \
                    o[g * Ld:(g + 1) * Ld, :].astype(BF16)


def _even_sample(proj, Bd, Ld, row0, cw, qg2, kg2, sinks, cache_conv, cache_k, cache_v):
    GB = SAMPLE_GB
    rows = GB * Ld
    rb0 = row0 // rows
    cos, sin = _rope_tables(PAST_LEN + jnp.arange(Ld, dtype=jnp.int32))
    full = lambda shape: pl.BlockSpec(shape, lambda i, *_: tuple(0 for _ in shape))
    grid_spec = pltpu.PrefetchScalarGridSpec(
        num_scalar_prefetch=1,
        grid=(Bd // GB,),
        in_specs=[pl.BlockSpec((rows, D_IN_EVEN), lambda i, s: (rb0 + i, 0)),
                  full((Ld, LANES)), full((Ld, LANES)),
                  full((CONV_W, D_CONV)), full((1, LANES)), full((1, LANES)),
                  pl.BlockSpec((GB, CONV_W - 1, D_CONV), lambda i, s: (i, 0, 0)),
                  pl.BlockSpec((GB, WINDOW, LANES), lambda i, s: (i, 0, 0)),
                  pl.BlockSpec((GB, WINDOW, LANES), lambda i, s: (i, 0, 0))],
        out_specs=[pl.BlockSpec((rows, D_MODEL), lambda i, s: (i, 0)),
                   pl.BlockSpec((GB, WINDOW, LANES), lambda i, s: (i, 0, 0)),
                   pl.BlockSpec((GB, WINDOW, LANES), lambda i, s: (i, 0, 0)),
                   pl.BlockSpec((GB, CONV_W - 1, D_CONV), lambda i, s: (i, 0, 0))],
    )
    return pl.pallas_call(
        _even_sample_kernel,
        grid_spec=grid_spec,
        out_shape=[jax.ShapeDtypeStruct((Bd * Ld, D_MODEL), BF16),
                   jax.ShapeDtypeStruct((Bd, WINDOW, LANES), F32),
                   jax.ShapeDtypeStruct((Bd, WINDOW, LANES), F32),
                   jax.ShapeDtypeStruct((Bd, CONV_W - 1, D_CONV), F32)],
        compiler_params=_cparams(("arbitrary",)),
        name="even_sample",
    )(sinks, proj, cos, sin, cw, qg2, kg2, cache_conv, cache_k, cache_v)


def _cumsum_rows(x):
    C = x.shape[0]
    row = lax.broadcasted_iota(jnp.int32, x.shape, 0)
    s = 1
    while s < C:
        x = x + jnp.where(row >= s, pltpu.roll(x, s, 0), 0.0)
        s *= 2
    return x


def _group_ref(b, m, row):
    C, D = b.shape
    if 2 * m >= SUBLANES:
        n = C // (2 * m)
        b3 = b.reshape(n, 2 * m, D)
        return jnp.broadcast_to(b3[:, m - 1:m, :], (n, 2 * m, D)).reshape(C, D)
    r = row & (2 * m - 1)
    out = b
    for off in range(2 * m):
        if off == m - 1:
            continue
        shift = (off - (m - 1)) % C
        out = jnp.where(r == off, pltpu.roll(b, shift, 0), out)
    return out


def _gla_consts(C):
    s = np.arange(C)[:, None]
    t = np.arange(C)[None, :]
    masks = [s == t]
    m = C // 2
    while m >= 1:
        masks.append((s // (2 * m) == t // (2 * m)) & ((s & m) == 0) & ((t & m) != 0))
        m //= 2
    return jnp.asarray(np.stack(masks).astype(np.float32))


def _gla_chunk(qz, fz, v, lb, S, masks=None):
    C = qz.shape[0]
    q = _silu(qz)
    f = lb + (1.0 - lb) * _sigmoid(fz)
    k = 1.0 - f
    b = _cumsum_rows(jnp.log(f))
    vb = v.astype(BF16)

    inter = jnp.dot((q * jnp.exp(b)).astype(BF16), S.astype(BF16), preferred_element_type=F32)

    row = lax.broadcasted_iota(jnp.int32, (C, C_DK), 0)
    nt = (((1,), (1,)), ((), ()))
    if C == SUBLANES:
        vr = vb.astype(F32)
        intra = jnp.zeros((C, C_DV), F32)
        for t in range(C):
            p = jnp.where(row <= t, q[t:t + 1, :] * k * jnp.exp(jnp.minimum(b[t:t + 1, :] - b, 0.0)), 0.0)
            col = jnp.sum(p, axis=-1, keepdims=True).astype(BF16).astype(F32)
            intra = jnp.where(row == t, jnp.sum(col * vr, axis=0, keepdims=True), intra)
    else:
        st = masks[0] * lax.dot_general(k.astype(BF16), q.astype(BF16), nt,
                                        preferred_element_type=F32)
        m = C // 2
        level = 1
        while m >= 1:
            rho = _group_ref(b, m, row)
            upper = (row & m) != 0
            d = b - rho
            x = (jnp.where(upper, q, k) * jnp.exp(jnp.where(upper, d, -d))).astype(BF16)
            st = st + masks[level] * lax.dot_general(x, x, nt, preferred_element_type=F32)
            level += 1
            m //= 2
        intra = lax.dot_general(st.astype(BF16), vb, (((0,), (0,)), ((), ())),
                                preferred_element_type=F32)

    b_last = b[C - 1:C, :]
    eye = (lax.broadcasted_iota(jnp.int32, (C_DK, C_DK), 0)
           == lax.broadcasted_iota(jnp.int32, (C_DK, C_DK), 1))
    dcol = jnp.sum(jnp.where(eye, jnp.broadcast_to(jnp.exp(b_last), (C_DK, C_DK)), 0.0),
                   axis=-1, keepdims=True)
    kd = k * jnp.exp(b_last - b)
    S_new = dcol * S + lax.dot_general(kd.astype(BF16), vb, (((0,), (0,)), ((), ())),
                                       preferred_element_type=F32)
    return inter + intra, S_new


def _gated_out(o, og, gz):
    ms = jnp.mean(o * o, axis=-1, keepdims=True)
    return (o * lax.rsqrt(ms + EPS) * og) * _silu(gz)


def _hgrn_prompt_kernel(q_ref, f_ref, i_ref, g_ref, lb_ref, og_ref, mask_ref, o_ref, s_ref, S):
    c = pl.program_id(2)

    @pl.when(c == 0)
    def _():
        S[...] = jnp.zeros_like(S)

    lb = lb_ref[...]
    og = og_ref[...]

    Sv = S[...]
    for n in range(GLA_STEP // GLA_CHUNK):
        rs = slice(n * GLA_CHUNK, (n + 1) * GLA_CHUNK)
        o, Sv = _gla_chunk(q_ref[rs, :], f_ref[rs, :], i_ref[rs, :], lb, Sv, mask_ref)
        o_ref[rs, :] = _gated_out(o, og, g_ref[rs, :]).astype(BF16)
    S[...] = Sv

    @pl.when(c == pl.num_programs(2) - 1)
    def _():
        s_ref[0, 0] = S[...]


def _hgrn_prompt(proj, B, L, T, lb, og):
    ns = L // GLA_STEP
    H = C_HEADS
    col = lambda off: pl.BlockSpec((GLA_STEP, LANES), lambda b, h, c: (b * ns + c, off + h))
    masks = _gla_consts(GLA_CHUNK)
    return pl.pallas_call(
        _hgrn_prompt_kernel,
        grid=(B, H, ns),
        in_specs=[col(0), col(H), col(2 * H), col(3 * H),
                  pl.BlockSpec((1, LANES), lambda b, h, c: (0, h)),
                  pl.BlockSpec((1, LANES), lambda b, h, c: (0, 0)),
                  pl.BlockSpec(masks.shape, lambda b, h, c: (0, 0, 0))],
        out_specs=[pl.BlockSpec((GLA_STEP, LANES), lambda b, h, c: (b * ns + c, h)),
                   pl.BlockSpec((1, 1, C_DK, C_DV), lambda b, h, c: (b, h, 0, 0))],
        out_shape=[jax.ShapeDtypeStruct((B * L, D_MODEL), BF16),
                   jax.ShapeDtypeStruct((B, H, C_DK, C_DV), F32)],
        scratch_shapes=[pltpu.VMEM((C_DK, C_DV), F32)],
        compiler_params=_cparams(("arbitrary", "arbitrary", "arbitrary")),
        name="hgrn_prompt",
    )(proj, proj, proj, proj, lb, og, masks)


def _hgrn_sample_kernel(p_ref, lb_ref, og_ref, s0_ref, o_ref, s_ref):
    Ld = p_ref.shape[0] // SAMPLE_GB
    og = og_ref[...]
    results = []
    for bb in range(SAMPLE_GB):
        rs = slice(bb * Ld, (bb + 1) * Ld)
        for h in range(C_HEADS):
            cs = lambda part: slice((part * C_HEADS + h) * LANES, (part * C_HEADS + h + 1) * LANES)
            o, S_new = _gla_chunk(p_ref[rs, cs(0)], p_ref[rs, cs(1)], p_ref[rs, cs(2)],
                                  lb_ref[:, h * LANES:(h + 1) * LANES], s0_ref[bb, h])
            results.append((bb, h, rs, S_new, _gated_out(o, og, p_ref[rs, cs(3)]).astype(BF16)))
    for bb, h, rs, S_new, out in results:
        s_ref[bb, h] = S_new
        o_ref[rs, h * LANES:(h + 1) * LANES] = out


def _hgrn_sample(proj, Bd, Ld, row0, lb, og, s0):
    GB = SAMPLE_GB
    rows = GB * Ld
    rb0 = row0 // rows
    H = C_HEADS
    return pl.pallas_call(
        _hgrn_sample_kernel,
        grid=(Bd // GB,),
        in_specs=[pl.BlockSpec((rows, 4 * C_HK), lambda i: (rb0 + i, 0)),
                  pl.BlockSpec((1, C_HK), lambda i: (0, 0)),
                  pl.BlockSpec((1, LANES), lambda i: (0, 0)),
                  pl.BlockSpec((GB, H, C_DK, C_DV), lambda i: (i, 0, 0, 0))],
        out_specs=[pl.BlockSpec((rows, D_MODEL), lambda i: (i, 0)),
                   pl.BlockSpec((GB, H, C_DK, C_DV), lambda i: (i, 0, 0, 0))],
        out_shape=[jax.ShapeDtypeStruct((Bd * Ld, D_MODEL), BF16),
                   jax.ShapeDtypeStruct((Bd, H, C_DK, C_DV), F32)],
        compiler_params=_cparams(("arbitrary",)),
        name="hgrn_sample",
    )(proj, lb, og, s0)


def _pack_bf16_pairs(xb):
    n = xb.shape[1] // 2
    lo = lax.bitcast_convert_type(xb[:, :n].astype(F32), jnp.uint32)
    hi = lax.bitcast_convert_type(xb[:, n:].astype(F32), jnp.uint32)
    return (lo >> 16) | (hi & jnp.uint32(0xFFFF0000))


def _unpack_bf16_pairs(w):
    lo = lax.bitcast_convert_type(w << 16, F32)
    hi = lax.bitcast_convert_type(w & jnp.uint32(0xFFFF0000), F32)
    return jnp.concatenate([lo, hi], axis=1).astype(BF16)


def _out_proj_kernel(n_mix, n_x, npt, *refs):
    m = _read_rows(refs[:n_mix], npt)
    x = _read_rows(refs[n_mix:n_mix + n_x], npt)
    w_ref, g_ref, wr_ref, br_ref, x1_ref, h_ref, lg_ref = refs[n_mix + n_x:]
    x1 = x + jnp.dot(m, w_ref[...], preferred_element_type=F32)
    x1_ref[...] = x1
    ms = jnp.mean(x1 * x1, axis=-1, keepdims=True)
    hb = (x1 * lax.rsqrt(ms + EPS) * g_ref[...]).astype(BF16)
    h_ref[...] = _pack_bf16_pairs(hb)
    lg_ref[...] = lax.dot_general(wr_ref[...], hb, (((1,), (1,)), ((), ())),
                                  preferred_element_type=F32) + br_ref[...]


def _out_proj(mix, w, x, g, wr, br, tm):
    K, D = w.shape
    m_specs, m_arrays, npt, T = _row_source(mix, tm)
    x_specs, x_arrays, npt_x, _ = _row_source(x, tm)
    assert npt_x in (0, npt)
    row = lambda n: pl.BlockSpec((tm, n), lambda i: (i, 0))
    full = lambda a, b: pl.BlockSpec((a, b), lambda i: (0, 0))
    return pl.pallas_call(
        functools.partial(_out_proj_kernel, len(m_arrays), len(x_arrays), npt),
        grid=(T // tm,),
        in_specs=m_specs + x_specs + [full(K, D), full(1, D), full(LOGIT_W, D), full(LOGIT_W, 1)],
        out_specs=[row(D), row(D // 2), pl.BlockSpec((LOGIT_W, tm), lambda i: (0, i))],
        out_shape=[jax.ShapeDtypeStruct((T, D), F32), jax.ShapeDtypeStruct((T, D // 2), jnp.uint32),
                   jax.ShapeDtypeStruct((LOGIT_W, T), F32)],
        compiler_params=_cparams(("arbitrary",)),
        name="out_proj",
    )(*m_arrays, *x_arrays, w, g.reshape(1, D), wr, br)


def _route_kernel(lg_ref, slot_ref, gate_ref, cnt_ref, carry, total, tri):
    phase = pl.program_id(0)
    step = pl.program_id(1)
    tm = lg_ref.shape[1]
    G, E = N_GROUPS, EXP_PER_GROUP
    row8 = lax.broadcasted_iota(jnp.int32, (G, tm), 0)

    glog = lg_ref[0:G, :]
    gmax = jnp.max(glog, axis=0, keepdims=True)
    g_idx = jnp.min(jnp.where(glog == gmax, row8, G), axis=0, keepdims=True)
    g_p = 1.0 / jnp.sum(jnp.exp(glog - gmax), axis=0, keepdims=True)

    elog = jnp.zeros((E, tm), F32)
    for g in range(G):
        elog = jnp.where(g_idx == g, lg_ref[G + g * E:G + (g + 1) * E, :], elog)
    emax = jnp.max(elog, axis=0, keepdims=True)
    ee = jnp.exp(elog - emax)
    prob = ee / jnp.sum(ee, axis=0, keepdims=True)
    p1 = jnp.max(prob, axis=0, keepdims=True)
    i1 = jnp.min(jnp.where(prob == p1, row8, E), axis=0, keepdims=True)
    prob2 = jnp.where(row8 == i1, -1.0, prob)
    p2 = jnp.max(prob2, axis=0, keepdims=True)
    i2 = jnp.min(jnp.where(prob2 == p2, row8, E), axis=0, keepdims=True)
    psum = p1 + p2
    w1 = p1 / psum * g_p
    w2 = p2 / psum * g_p

    erow = lax.broadcasted_iota(jnp.int32, (N_EXPERTS, tm), 0)
    oh1 = erow == g_idx * E + i1
    oh2 = erow == g_idx * E + i2
    both = jnp.where(oh1 | oh2, 1.0, 0.0)
    tile_cnt = jnp.sum(both, axis=1, keepdims=True)

    @pl.when((phase == 0) & (step == 0))
    def _():
        total[...] = jnp.zeros_like(total)
        r = lax.broadcasted_iota(jnp.int32, (tm, tm), 0)
        c = lax.broadcasted_iota(jnp.int32, (tm, tm), 1)
        tri[...] = jnp.where(r < c, 1.0, 0.0).astype(BF16)

    @pl.when(phase == 0)
    def _():
        total[...] += tile_cnt

    @pl.when((phase == 1) & (step == 0))
    def _():
        carry[...] = jnp.zeros_like(carry)

    @pl.when(phase == 1)
    def _():
        cnt = total[...]
        padded = jnp.floor((cnt + (MOE_BLOCK - 1)) * (1.0 / MOE_BLOCK)) * MOE_BLOCK
        pb = jnp.broadcast_to(padded, (N_EXPERTS, LANES))
        pstart = (_cumsum_rows(pb) - pb)[:, 0:1]
        before = (jnp.dot(both.astype(BF16), tri[...], preferred_element_type=F32)
                  + (carry[...] + pstart))
        s1 = jnp.sum(jnp.where(oh1, before, 0.0), axis=0, keepdims=True)
        s2 = jnp.sum(jnp.where(oh2, before, 0.0), axis=0, keepdims=True)
        carry[...] += tile_cnt
        slot_ref[...] = jnp.concatenate([s1, s2], axis=0).astype(jnp.int32)
        grow = lax.broadcasted_iota(jnp.int32, (LOGIT_W, tm), 0)
        gate_ref[...] = jnp.where(grow == 0, w1, jnp.where(grow == 1, w2, 0.0)).T
        cnt_ref[...] = jnp.broadcast_to(cnt, (N_EXPERTS, LANES))


def _route(logits_t):
    T = logits_t.shape[1]
    tm = ROUTE_TM
    return pl.pallas_call(
        _route_kernel,
        grid=(2, T // tm),
        in_specs=[pl.BlockSpec((LOGIT_W, tm), lambda p, i: (0, i))],
        out_specs=[pl.BlockSpec((2, tm), lambda p, i: (0, i * p)),
                   pl.BlockSpec((tm, LOGIT_W), lambda p, i: (i * p, 0)),
                   pl.BlockSpec((N_EXPERTS, LANES), lambda p, i: (0, 0))],
        out_shape=[jax.ShapeDtypeStruct((2, T), jnp.int32),
                   jax.ShapeDtypeStruct((T, LOGIT_W), F32),
                   jax.ShapeDtypeStruct((N_EXPERTS, LANES), F32)],
        scratch_shapes=[pltpu.VMEM((N_EXPERTS, 1), F32), pltpu.VMEM((N_EXPERTS, 1), F32),
                        pltpu.VMEM((tm, tm), BF16)],
        compiler_params=_cparams(("arbitrary", "arbitrary")),
        name="route",
    )(logits_t)


DISPATCH_TM = 512


def _dispatch_kernel(T, slot_ref, h_ref, xs_in_ref, xs_ref, sem):
    del xs_in_ref
    base = pl.program_id(0) * DISPATCH_TM

    def copy(r, k):
        return pltpu.make_async_copy(h_ref.at[pl.ds(r, 1)],
                                     xs_ref.at[pl.ds(slot_ref[k * T + base + r], 1)], sem)

    def start(r, c):
        copy(r, 0).start()
        copy(r, 1).start()
        return c

    def wait(r, c):
        copy(r, 0).wait()
        copy(r, 1).wait()
        return c

    lax.fori_loop(0, DISPATCH_TM, start, 0, unroll=8)
    lax.fori_loop(0, DISPATCH_TM, wait, 0, unroll=8)


def _dispatch(slots, h, n_slots):
    T, D = h.shape
    xs0 = jnp.zeros((n_slots, D), h.dtype)
    grid_spec = pltpu.PrefetchScalarGridSpec(
        num_scalar_prefetch=1,
        grid=(T // DISPATCH_TM,),
        in_specs=[pl.BlockSpec((DISPATCH_TM, D), lambda i, s: (i, 0)),
                  pl.BlockSpec(memory_space=pl.ANY)],
        out_specs=pl.BlockSpec(memory_space=pl.ANY),
        scratch_shapes=[pltpu.SemaphoreType.DMA(())],
    )
    return pl.pallas_call(
        functools.partial(_dispatch_kernel, T),
        grid_spec=grid_spec,
        out_shape=jax.ShapeDtypeStruct((n_slots, D), h.dtype),
        input_output_aliases={2: 0},
        compiler_params=_cparams(("arbitrary",)),
        name="moe_dispatch",
    )(slots, h, xs0)


def _ffn_kernel(nblk, layer, meta_ref, x_ref, w1_hbm, w3_hbm, w2_hbm, y_ref,
                wf1, wf3, wf2, sems, wb1, wb3, wb2):
    b = pl.program_id(0)
    used = b < meta_ref[nblk]
    e = meta_ref[b]
    new_expert = (b == 0) | (e != meta_ref[jnp.maximum(b - 1, 0)])
    slot = meta_ref[nblk + 1 + e] % 2
    nxt = meta_ref[nblk + 1 + N_EXPERTS + e]

    def fetch(expert, s):
        return (pltpu.make_async_copy(w1_hbm.at[layer, expert], wf1.at[s], sems.at[s, 0]),
                pltpu.make_async_copy(w3_hbm.at[layer, expert], wf3.at[s], sems.at[s, 1]),
                pltpu.make_async_copy(w2_hbm.at[layer, expert], wf2.at[s], sems.at[s, 2]))

    @pl.when(used & (b == 0))
    def _():
        for cp in fetch(e, slot):
            cp.start()

    @pl.when(used & new_expert)
    def _():
        for cp in fetch(e, slot):
            cp.wait()

        @pl.when(nxt >= 0)
        def _():
            for cp in fetch(nxt, 1 - slot):
                cp.start()

        wb1[...] = wf1[slot].astype(BF16)
        wb3[...] = wf3[slot].astype(BF16)
        wb2[...] = wf2[slot].astype(BF16)

    @pl.when(used)
    def _():
        x = _unpack_bf16_pairs(x_ref[...])
        a = jnp.dot(x, wb1[...], preferred_element_type=F32)
        c = jnp.dot(x, wb3[...], preferred_element_type=F32)
        hid = (a * _sigmoid(a)) * c
        y_ref[...] = jnp.dot(hid.astype(BF16), wb2[...], preferred_element_type=F32)

    @pl.when(jnp.logical_not(used))
    def _():
        y_ref[...] = jnp.zeros_like(y_ref)


def _ffn(meta, xs, w1, w3, w2, layer):
    n_slots = xs.shape[0]
    D = w1.shape[2]
    nblk = n_slots // MOE_BLOCK
    grid_spec = pltpu.PrefetchScalarGridSpec(
        num_scalar_prefetch=1,
        grid=(nblk,),
        in_specs=[pl.BlockSpec((MOE_BLOCK, xs.shape[1]), lambda b, e: (b, 0)),
                  pl.BlockSpec(memory_space=pl.ANY), pl.BlockSpec(memory_space=pl.ANY),
                  pl.BlockSpec(memory_space=pl.ANY)],
        out_specs=pl.BlockSpec((MOE_BLOCK, D), lambda b, e: (b, 0)),
        scratch_shapes=[pltpu.VMEM((2, D, D_EXPERT), F32), pltpu.VMEM((2, D, D_EXPERT), F32),
                        pltpu.VMEM((2, D_EXPERT, D), F32), pltpu.SemaphoreType.DMA((2, 3)),
                        pltpu.VMEM((D, D_EXPERT), BF16), pltpu.VMEM((D, D_EXPERT), BF16),
                        pltpu.VMEM((D_EXPERT, D), BF16)],
    )
    return pl.pallas_call(
        functools.partial(_ffn_kernel, nblk, layer),
        grid_spec=grid_spec,
        out_shape=jax.ShapeDtypeStruct((n_slots, D), F32),
        compiler_params=_cparams(("arbitrary",)),
        name="moe_ffn",
    )(meta, xs, w1, w3, w2)


COMBINE_TM = 512
COMBINE_CHUNK = 128


def _combine_kernel(T, npt, slot_ref, x_ref, gate_ref, ys_ref, *refs):
    out_refs, (buf, sems) = refs[:-2], refs[-2:]
    i = pl.program_id(0)
    base = i * COMBINE_TM
    n_chunks = COMBINE_TM // COMBINE_CHUNK

    def copy(r, k, q):
        return pltpu.make_async_copy(ys_ref.at[pl.ds(slot_ref[k * T + base + r], 1)],
                                     buf.at[k, pl.ds(r, 1)], sems.at[q])

    def for_chunk_rows(q, fn):
        def body(r, c):
            fn(copy(r, 0, q))
            fn(copy(r, 1, q))
            return c
        lax.fori_loop(q * COMBINE_CHUNK, (q + 1) * COMBINE_CHUNK, body, 0, unroll=8)

    for q in range(n_chunks):
        for_chunk_rows(q, lambda cp: cp.start())

    def emit(rs, y):
        if npt == 0:
            out_refs[0][rs, :] = y
        else:
            @pl.when(i < npt)
            def _():
                out_refs[0][rs, :] = y

            @pl.when(i >= npt)
            def _():
                out_refs[1][rs, :] = y

    for q in range(n_chunks):
        for_chunk_rows(q, lambda cp: cp.wait())
        rs = slice(q * COMBINE_CHUNK, (q + 1) * COMBINE_CHUNK)
        g = gate_ref[rs, :]
        emit(rs, x_ref[rs, :] + (buf[0, rs, :] * g[:, 0:1] + buf[1, rs, :] * g[:, 1:2]))


def _combine(slots, x1, gates, ys, split_rows=0):
    T, D = x1.shape
    tm = COMBINE_TM
    npt = split_rows // tm
    if npt == 0:
        out_specs = pl.BlockSpec((tm, D), lambda i, s: (i, 0))
        out_shape = jax.ShapeDtypeStruct((T, D), F32)
    else:
        out_specs = [pl.BlockSpec((tm, D), lambda i, s: (jnp.minimum(i, npt - 1), 0)),
                     pl.BlockSpec((tm, D), lambda i, s: (jnp.maximum(i - npt, 0), 0))]
        out_shape = [jax.ShapeDtypeStruct((split_rows, D), F32),
                     jax.ShapeDtypeStruct((T - split_rows, D), F32)]
    grid_spec = pltpu.PrefetchScalarGridSpec(
        num_scalar_prefetch=1,
        grid=(T // tm,),
        in_specs=[pl.BlockSpec((tm, D), lambda i, s: (i, 0)),
                  pl.BlockSpec((tm, LOGIT_W), lambda i, s: (i, 0)),
                  pl.BlockSpec(memory_space=pl.ANY)],
        out_specs=out_specs,
        scratch_shapes=[pltpu.VMEM((2, tm, D), F32),
                        pltpu.SemaphoreType.DMA((COMBINE_TM // COMBINE_CHUNK,))],
    )
    return pl.pallas_call(
        functools.partial(_combine_kernel, T, npt),
        grid_spec=grid_spec,
        out_shape=out_shape,
        compiler_params=_cparams(("arbitrary",)),
        name="moe_combine",
    )(slots, x1, gates, ys)


def _moe(x1, h, logits, w1, w3, w2, layer, split_rows=0):
    T = x1.shape[0]
    slot_l, gate_l, cnt = _route(logits)
    slots = slot_l.reshape(2 * T)
    n_blocks = -(-(2 * T) // MOE_BLOCK) + N_EXPERTS
    counts = cnt[:, 0].astype(jnp.int32)
    pend = jnp.cumsum((counts + MOE_BLOCK - 1) // MOE_BLOCK * MOE_BLOCK)
    starts = jnp.arange(n_blocks, dtype=jnp.int32) * MOE_BLOCK
    blk_e = jnp.minimum(jnp.sum((pend[None, :] <= starts[:, None]).astype(jnp.int32), axis=1),
                        N_EXPERTS - 1)
    present = (counts > 0).astype(jnp.int32)
    rank = jnp.cumsum(present) - present
    ids = jnp.where(present > 0, jnp.arange(N_EXPERTS, dtype=jnp.int32), N_EXPERTS)
    after = jnp.concatenate([lax.cummin(ids[::-1])[::-1][1:], jnp.full((1,), N_EXPERTS, jnp.int32)])
    nxt = jnp.where(after < N_EXPERTS, after, -1)
    meta = jnp.concatenate([blk_e, pend[-1:] // MOE_BLOCK, rank, nxt]).astype(jnp.int32)
    xs = _dispatch(slots, h, n_blocks * MOE_BLOCK)
    ys = _ffn(meta, xs, w1, w3, w2, layer)
    return _combine(slots, x1, gate_l, ys, split_rows)


def _router_weights(w_grp, b_grp, w_exp, b_exp):
    D = w_grp.shape[0]
    pad = LOGIT_W - N_GROUPS - N_EXPERTS
    wr = jnp.concatenate([w_grp, w_exp, jnp.zeros((D, pad), F32)], axis=1).astype(BF16)
    br = jnp.concatenate([b_grp, b_exp, jnp.zeros((pad,), F32)]).reshape(LOGIT_W, 1)
    return wr.T, br


def kernel(x_prompt, x_sample, cache_conv, cache_k, cache_v, state_hgrn, norm_mix, norm_ffn,
           ev_w_in, ev_conv, ev_q_norm, ev_k_norm, ev_sinks, ev_w_out,
           od_w_in, od_lb, od_o_norm, od_w_out,
           moe_w_grp, moe_b_grp, moe_w_exp, moe_b_exp, moe_w1, moe_w3, moe_w2):
    B, L, D = x_prompt.shape
    Bd, Ld, _ = x_sample.shape
    Tp = B * L
    T = Tp + Bd * Ld
    x = (x_prompt.reshape(Tp, D), x_sample.reshape(Bd * Ld, D))

    proj = _norm_matmul(x, norm_mix[0], ev_w_in[0].astype(BF16), 512)
    qg2 = jnp.tile(ev_q_norm[0], 2).reshape(1, LANES)
    kg2 = jnp.tile(ev_k_norm[0], 2).reshape(1, LANES)
    mix_p, k_p, v_p, conv_p = _even_prompt(proj, B, L, T, ev_conv[0], qg2, kg2, ev_sinks[0])
    mix_s, k_s, v_s, conv_s = _even_sample(
        proj, Bd, Ld, Tp, ev_conv[0], qg2, kg2, ev_sinks[0], cache_conv[0],
        cache_k[0].reshape(Bd, WINDOW, LANES), cache_v[0].reshape(Bd, WINDOW, LANES))
    wr, br = _router_weights(moe_w_grp[0], moe_b_grp[0], moe_w_exp[0], moe_b_exp[0])
    x1, h, logits = _out_proj((mix_p, mix_s), ev_w_out[0].astype(BF16), x, norm_ffn[0], wr, br, 512)
    x = _moe(x1, h, logits, moe_w1, moe_w3, moe_w2, 0)

    lb_all = jnp.cumsum(jax.nn.softmax(od_lb.astype(F32), axis=0), axis=0)
    lb = (lb_all - lb_all[0])[1].reshape(1, C_HK)
    og = od_o_norm[0].reshape(1, C_DV)
    proj = _norm_matmul(x, norm_mix[1], od_w_in[0].astype(BF16), 512)
    o_p, s_p = _hgrn_prompt(proj, B, L, T, lb, og)
    o_s, s_s = _hgrn_sample(proj, Bd, Ld, Tp, lb, og, state_hgrn[0])
    wr, br = _router_weights(moe_w_grp[1], moe_b_grp[1], moe_w_exp[1], moe_b_exp[1])
    x1, h, logits = _out_proj((o_p, o_s), od_w_out[0].astype(BF16), x, norm_ffn[1], wr, br, 512)
    y_p, y_s = _moe(x1, h, logits, moe_w1, moe_w3, moe_w2, 1, split_rows=Tp)

    y_prompt = y_p.reshape(B, L, D)
    y_sample = y_s.reshape(Bd, Ld, D)
    return (y_prompt, y_sample,
            conv_p[None], k_p.reshape(1, B, WINDOW, N_KV, HEAD_DIM), v_p.reshape(1, B, WINDOW, N_KV, HEAD_DIM),
            s_p[None],
            conv_s[None], k_s.reshape(1, Bd, WINDOW, N_KV, HEAD_DIM), v_s.reshape(1, Bd, WINDOW, N_KV, HEAD_DIM),
            s_s[None])
```

```python
import functools

import numpy as np
import jax
import jax.numpy as jnp
from jax import lax
from jax.experimental import pallas as pl
from jax.experimental.pallas import tpu as pltpu

F32 = jnp.float32
BF16 = jnp.bfloat16

D_MODEL = 1024
PAST_LEN = 16384
D_CONV = 512
CONV_W = 3
HEAD_DIM = 64
N_Q = 8
N_KV = 2
GQA_G = N_Q // N_KV
WINDOW = 128
ROPE_THETA = 10000.0
D_IN_EVEN = 3 * D_CONV + (N_Q + 2 * N_KV) * HEAD_DIM
C_HEADS = 8
C_DK = 128
C_DV = 128
C_HK = C_HEADS * C_DK
N_GROUPS = 8
EXP_PER_GROUP = 8
N_EXPERTS = N_GROUPS * EXP_PER_GROUP
D_EXPERT = 512
MOE_BLOCK = 256
EPS = 1e-6

LANES = 128
SUBLANES = 8
VMEM_LIMIT = 56 * 1024 * 1024

GLA_CHUNK = 128
GLA_STEP = 512
SAMPLE_GB = 2
ROUTE_TM = 512
LOGIT_W = 128


def _cparams(sem):
    return pltpu.CompilerParams(dimension_semantics=sem, vmem_limit_bytes=VMEM_LIMIT)


def _sigmoid(x):
    return 1.0 / (1.0 + jnp.exp(-x))


def _silu(x):
    return x * (0.5 * jnp.tanh(0.5 * x) + 0.5)


def _row_source(src, tm):
    if isinstance(src, tuple):
        a, b = src
        n = a.shape[1]
        npt = a.shape[0] // tm
        specs = [pl.BlockSpec((tm, n), lambda i, *_: (jnp.minimum(i, npt - 1), 0)),
                 pl.BlockSpec((tm, n), lambda i, *_: (jnp.maximum(i - npt, 0), 0))]
        return specs, [a, b], npt, a.shape[0] + b.shape[0]
    return [pl.BlockSpec((tm, src.shape[1]), lambda i, *_: (i, 0))], [src], 0, src.shape[0]


def _read_rows(refs, npt):
    if len(refs) == 2:
        return jnp.where(pl.program_id(0) < npt, refs[0][...], refs[1][...])
    return refs[0][...]


def _norm_matmul_kernel(n_src, npt, *refs):
    x = _read_rows(refs[:n_src], npt)
    g_ref, w_ref, o_ref = refs[n_src:]
    ms = jnp.mean(x * x, axis=-1, keepdims=True)
    h = (x * lax.rsqrt(ms + EPS) * g_ref[...]).astype(BF16)
    o_ref[...] = jnp.dot(h, w_ref[...], preferred_element_type=F32)


def _norm_matmul(x, g, w, tm):
    D, N = w.shape
    specs, arrays, npt, T = _row_source(x, tm)
    return pl.pallas_call(
        functools.partial(_norm_matmul_kernel, len(arrays), npt),
        grid=(T // tm,),
        in_specs=specs + [pl.BlockSpec((1, D), lambda i: (0, 0)),
                          pl.BlockSpec((D, N), lambda i: (0, 0), pipeline_mode=pl.Buffered(1))],
        out_specs=pl.BlockSpec((tm, N), lambda i: (i, 0)),
        out_shape=jax.ShapeDtypeStruct((T, N), F32),
        compiler_params=_cparams(("arbitrary",)),
        name="norm_matmul",
    )(*arrays, g.reshape(1, D), w)


def _rope_tables(pos):
    inv = ROPE_THETA ** (-jnp.arange(0, HEAD_DIM, 2, dtype=F32) / HEAD_DIM)
    ang = pos.astype(F32)[:, None] * inv[None, :]
    cos = jnp.cos(ang)
    sin = jnp.sin(ang)
    return (jnp.concatenate([cos, cos, cos, cos], axis=1),
            jnp.concatenate([-sin, sin, -sin, sin], axis=1))


def _headnorm_rope(x, g2, cos, sin):
    lane = lax.broadcasted_iota(jnp.int32, x.shape, 1)
    lo = lane < HEAD_DIM
    x2 = x * x
    s_lo = jnp.sum(jnp.where(lo, x2, 0.0), axis=-1, keepdims=True)
    s_hi = jnp.sum(jnp.where(lo, 0.0, x2), axis=-1, keepdims=True)
    ms = jnp.where(lo, s_lo, s_hi) * (1.0 / HEAD_DIM)
    y = x * lax.rsqrt(ms + EPS) * g2
    first_half = (lane & (HEAD_DIM // 2)) == 0
    swapped = jnp.where(first_half, pltpu.roll(y, LANES - HEAD_DIM // 2, 1),
                        pltpu.roll(y, HEAD_DIM // 2, 1))
    return y * cos + swapped * sin


def _gated_conv(gb, u, c2, c1, cw):
    R = u.shape[0]
    row = lax.broadcasted_iota(jnp.int32, u.shape, 0)
    u1 = jnp.where(row == 0, c1, pltpu.roll(u, 1, 0))
    u2 = jnp.where(row == 0, c2, jnp.where(row == 1, c1, pltpu.roll(u, 2, 0)))
    del R
    return gb * (cw[0:1, :] * u2 + cw[1:2, :] * u1 + cw[2:3, :] * u)


def _band_attention(qs, kk, vv, valid, sinkv):
    s = lax.dot_general(qs.astype(BF16), kk.astype(BF16), (((1,), (1,)), ((), ())),
                        preferred_element_type=F32) * (HEAD_DIM ** -0.5)
    s = jnp.where(valid, s, -jnp.inf)
    m = jnp.maximum(jnp.max(s, axis=-1, keepdims=True), sinkv)
    e = jnp.exp(s - m)
    den = jnp.sum(e, axis=-1, keepdims=True) + jnp.exp(sinkv - m)
    p = e / den
    return jnp.dot(p.astype(BF16), vv.astype(BF16), preferred_element_type=F32)


def _sink_column(sinks_ref, hk, rows_per_head):
    R = GQA_G * rows_per_head
    row = lax.broadcasted_iota(jnp.int32, (R, 1), 0)
    col = jnp.full((R, 1), sinks_ref[hk * GQA_G + GQA_G - 1], F32)
    for j in range(GQA_G - 2, -1, -1):
        col = jnp.where(row < (j + 1) * rows_per_head, sinks_ref[hk * GQA_G + j], col)
    return col


def _even_prompt_kernel(sinks_ref, proj_ref, cos_ref, sin_ref, cw_ref, qg_ref, kg_ref,
                        mix_ref, kl_ref, vl_ref, cl_ref, kprev, vprev, ucar):
    blk = pl.program_id(1)
    W = WINDOW

    @pl.when(blk == 0)
    def _():
        kprev[...] = jnp.zeros_like(kprev)
        vprev[...] = jnp.zeros_like(vprev)
        ucar[...] = jnp.zeros_like(ucar)

    gb = proj_ref[:, 0:D_CONV]
    u = proj_ref[:, D_CONV:2 * D_CONV] * proj_ref[:, 2 * D_CONV:3 * D_CONV]
    car = ucar[...]
    a_out = _gated_conv(gb, u, car[0:1, :], car[1:2, :], cw_ref[...])
    ucar[0:2, :] = u[W - 2:W, :]
    mix_ref[:, 0:D_CONV] = a_out.astype(BF16)

    cos = cos_ref[...]
    sin = sin_ref[...]
    q0 = 3 * D_CONV
    k0 = q0 + N_Q * HEAD_DIM
    v0 = k0 + N_KV * HEAD_DIM
    k_r = _headnorm_rope(proj_ref[:, k0:k0 + LANES], kg_ref[...], cos, sin)
    v_r = proj_ref[:, v0:v0 + LANES]
    q_r = [_headnorm_rope(proj_ref[:, q0 + LANES * j:q0 + LANES * (j + 1)], qg_ref[...], cos, sin)
           for j in range(N_Q * HEAD_DIM // LANES)]
    k_p = kprev[...]
    v_p = vprev[...]

    R = GQA_G * W
    i = lax.broadcasted_iota(jnp.int32, (R, 2 * W), 0) & (W - 1)
    j = lax.broadcasted_iota(jnp.int32, (R, 2 * W), 1)
    diff = i + W - j
    valid = (diff >= 0) & (diff <= W) & ((blk > 0) | (j >= W))

    for hk in range(N_KV):
        ls = slice(hk * HEAD_DIM, (hk + 1) * HEAD_DIM)
        kk = jnp.concatenate([k_p[:, ls], k_r[:, ls]], axis=0)
        vv = jnp.concatenate([v_p[:, ls], v_r[:, ls]], axis=0)
        heads = []
        for g in range(GQA_G):
            h = hk * GQA_G + g
            tile = q_r[h // 2]
            heads.append(tile[:, (h % 2) * HEAD_DIM:(h % 2 + 1) * HEAD_DIM])
        qs = jnp.concatenate(heads, axis=0)
        o = _band_attention(qs, kk, vv, valid, _sink_column(sinks_ref, hk, W))
        for g in range(GQA_G):
            h = hk * GQA_G + g
            mix_ref[:, D_CONV + h * HEAD_DIM:D_CONV + (h + 1) * HEAD_DIM] = \
                o[g * W:(g + 1) * W, :].astype(BF16)

    kprev[...] = k_r
    vprev[...] = v_r

    @pl.when(blk == pl.num_programs(1) - 1)
    def _():
        kl_ref[0] = k_r
        vl_ref[0] = v_r
        cl_ref[0] = u[W - 2:W, :]


def _even_prompt(proj, B, L, T, cw, qg2, kg2, sinks):
    nb = L // WINDOW
    cos, sin = _rope_tables(jnp.arange(L, dtype=jnp.int32))
    full = lambda shape: pl.BlockSpec(shape, lambda b, i, *_: tuple(0 for _ in shape))
    grid_spec = pltpu.PrefetchScalarGridSpec(
        num_scalar_prefetch=1,
        grid=(B, nb),
        in_specs=[pl.BlockSpec((WINDOW, D_IN_EVEN), lambda b, i, s: (b * nb + i, 0)),
                  pl.BlockSpec((WINDOW, LANES), lambda b, i, s: (i, 0)),
                  pl.BlockSpec((WINDOW, LANES), lambda b, i, s: (i, 0)),
                  full((CONV_W, D_CONV)), full((1, LANES)), full((1, LANES))],
        out_specs=[pl.BlockSpec((WINDOW, D_MODEL), lambda b, i, s: (b * nb + i, 0)),
                   pl.BlockSpec((1, WINDOW, LANES), lambda b, i, s: (b, 0, 0)),
                   pl.BlockSpec((1, WINDOW, LANES), lambda b, i, s: (b, 0, 0)),
                   pl.BlockSpec((1, CONV_W - 1, D_CONV), lambda b, i, s: (b, 0, 0))],
        scratch_shapes=[pltpu.VMEM((WINDOW, LANES), F32), pltpu.VMEM((WINDOW, LANES), F32),
                        pltpu.VMEM((SUBLANES, D_CONV), F32)],
    )
    return pl.pallas_call(
        _even_prompt_kernel,
        grid_spec=grid_spec,
        out_shape=[jax.ShapeDtypeStruct((B * L, D_MODEL), BF16),
                   jax.ShapeDtypeStruct((B, WINDOW, LANES), F32),
                   jax.ShapeDtypeStruct((B, WINDOW, LANES), F32),
                   jax.ShapeDtypeStruct((B, CONV_W - 1, D_CONV), F32)],
        compiler_params=_cparams(("arbitrary", "arbitrary")),
        name="even_prompt",
    )(sinks, proj, cos, sin, cw, qg2, kg2)


def _even_sample_kernel(sinks_ref, proj_ref, cos_ref, sin_ref, cw_ref, qg_ref, kg_ref,
                        cc_ref, ck_ref, cv_ref, mix_ref, ko_ref, vo_ref, co_ref):
    W = WINDOW
    Ld = cos_ref.shape[0]
    cos = cos_ref[...]
    sin = sin_ref[...]
    q0 = 3 * D_CONV
    k0 = q0 + N_Q * HEAD_DIM
    v0 = k0 + N_KV * HEAD_DIM
    R = GQA_G * Ld
    i = lax.broadcasted_iota(jnp.int32, (R, 2 * W), 0) % Ld
    j = lax.broadcasted_iota(jnp.int32, (R, 2 * W), 1)
    diff = i + W - j
    valid = (diff >= 0) & (diff <= W)
    zpad = jnp.zeros((W - Ld, HEAD_DIM), F32)

    for bb in range(SAMPLE_GB):
        rs = slice(bb * Ld, (bb + 1) * Ld)
        gb = proj_ref[rs, 0:D_CONV]
        u = proj_ref[rs, D_CONV:2 * D_CONV] * proj_ref[rs, 2 * D_CONV:3 * D_CONV]
        car = cc_ref[bb]
        a_out = _gated_conv(gb, u, car[0:1, :], car[1:2, :], cw_ref[...])
        co_ref[bb] = u[Ld - 2:Ld, :]
        mix_ref[rs, 0:D_CONV] = a_out.astype(BF16)

        k_r = _headnorm_rope(proj_ref[rs, k0:k0 + LANES], kg_ref[...], cos, sin)
        v_r = proj_ref[rs, v0:v0 + LANES]
        q_r = [_headnorm_rope(proj_ref[rs, q0 + LANES * t:q0 + LANES * (t + 1)], qg_ref[...], cos, sin)
               for t in range(N_Q * HEAD_DIM // LANES)]
        k_c = ck_ref[bb]
        v_c = cv_ref[bb]
        ko_ref[bb, 0:W - Ld, :] = k_c[Ld:W, :]
        ko_ref[bb, W - Ld:W, :] = k_r
        vo_ref[bb, 0:W - Ld, :] = v_c[Ld:W, :]
        vo_ref[bb, W - Ld:W, :] = v_r

        for hk in range(N_KV):
            ls = slice(hk * HEAD_DIM, (hk + 1) * HEAD_DIM)
            kk = jnp.concatenate([k_c[:, ls], k_r[:, ls], zpad], axis=0)
            vv = jnp.concatenate([v_c[:, ls], v_r[:, ls], zpad], axis=0)
            heads = []
            for g in range(GQA_G):
                h = hk * GQA_G + g
                tile = q_r[h // 2]
                heads.append(tile[:, (h % 2) * HEAD_DIM:(h % 2 + 1) * HEAD_DIM])
            qs = jnp.concatenate(heads, axis=0)
            o = _band_attention(qs, kk, vv, valid, _sink_column(sinks_ref, hk, Ld))
            for g in range(GQA_G):
                h = hk * GQA_G + g
                mix_ref[rs, D_CONV + h * HEAD_DIM:D_CONV + (h + 1) * HEAD_DIM] = \
                    o[g * Ld:(g + 1) * Ld, :].astype(BF16)


def _even_sample(proj, Bd, Ld, row0, cw, qg2, kg2, sinks, cache_conv, cache_k, cache_v):
    GB = SAMPLE_GB
    rows = GB * Ld
    rb0 = row0 // rows
    cos, sin = _rope_tables(PAST_LEN + jnp.arange(Ld, dtype=jnp.int32))
    full = lambda shape: pl.BlockSpec(shape, lambda i, *_: tuple(0 for _ in shape))
    grid_spec = pltpu.PrefetchScalarGridSpec(
        num_scalar_prefetch=1,
        grid=(Bd // GB,),
        in_specs=[pl.BlockSpec((rows, D_IN_EVEN), lambda i, s: (rb0 + i, 0)),
                  full((Ld, LANES)), full((Ld, LANES)),
                  full((CONV_W, D_CONV)), full((1, LANES)), full((1, LANES)),
                  pl.BlockSpec((GB, CONV_W - 1, D_CONV), lambda i, s: (i, 0, 0)),
                  pl.BlockSpec((GB, WINDOW, LANES), lambda i, s: (i, 0, 0)),
                  pl.BlockSpec((GB, WINDOW, LANES), lambda i, s: (i, 0, 0))],
        out_specs=[pl.BlockSpec((rows, D_MODEL), lambda i, s: (i, 0)),
                   pl.BlockSpec((GB, WINDOW, LANES), lambda i, s: (i, 0, 0)),
                   pl.BlockSpec((GB, WINDOW, LANES), lambda i, s: (i, 0, 0)),
                   pl.BlockSpec((GB, CONV_W - 1, D_CONV), lambda i, s: (i, 0, 0))],
    )
    return pl.pallas_call(
        _even_sample_kernel,
        grid_spec=grid_spec,
        out_shape=[jax.ShapeDtypeStruct((Bd * Ld, D_MODEL), BF16),
                   jax.ShapeDtypeStruct((Bd, WINDOW, LANES), F32),
                   jax.ShapeDtypeStruct((Bd, WINDOW, LANES), F32),
                   jax.ShapeDtypeStruct((Bd, CONV_W - 1, D_CONV), F32)],
        compiler_params=_cparams(("arbitrary",)),
        name="even_sample",
    )(sinks, proj, cos, sin, cw, qg2, kg2, cache_conv, cache_k, cache_v)


def _cumsum_rows(x):
    C = x.shape[0]
    row = lax.broadcasted_iota(jnp.int32, x.shape, 0)
    s = 1
    while s < C:
        x = x + jnp.where(row >= s, pltpu.roll(x, s, 0), 0.0)
        s *= 2
    return x


def _group_ref(b, m, row):
    C, D = b.shape
    if 2 * m >= SUBLANES:
        n = C // (2 * m)
        b3 = b.reshape(n, 2 * m, D)
        return jnp.broadcast_to(b3[:, m - 1:m, :], (n, 2 * m, D)).reshape(C, D)
    r = row & (2 * m - 1)
    out = b
    for off in range(2 * m):
        if off == m - 1:
            continue
        shift = (off - (m - 1)) % C
        out = jnp.where(r == off, pltpu.roll(b, shift, 0), out)
    return out


def _gla_consts(C):
    t = np.arange(C)[:, None]
    s = np.arange(C)[None, :]
    masks = [s == t]
    m = C // 2
    while m >= 1:
        masks.append((s // (2 * m) == t // (2 * m)) & ((s & m) == 0) & ((t & m) != 0))
        m //= 2
    return jnp.asarray(np.stack(masks).astype(np.float32))


def _gla_chunk(qz, fz, v, lb, S, masks=None):
    C = qz.shape[0]
    q = _silu(qz)
    f = lb + (1.0 - lb) * _sigmoid(fz)
    k = 1.0 - f
    b = _cumsum_rows(jnp.log(f))
    vb = v.astype(BF16)

    inter = jnp.dot((q * jnp.exp(b)).astype(BF16), S.astype(BF16), preferred_element_type=F32)

    row = lax.broadcasted_iota(jnp.int32, (C, C_DK), 0)
    nt = (((1,), (1,)), ((), ()))
    if C == SUBLANES:
        vr = vb.astype(F32)
        intra = jnp.zeros((C, C_DV), F32)
        for t in range(C):
            p = jnp.where(row <= t, q[t:t + 1, :] * k * jnp.exp(jnp.minimum(b[t:t + 1, :] - b, 0.0)), 0.0)
            col = jnp.sum(p, axis=-1, keepdims=True).astype(BF16).astype(F32)
            intra = jnp.where(row == t, jnp.sum(col * vr, axis=0, keepdims=True), intra)
    else:
        sc = masks[0] * lax.dot_general(q.astype(BF16), k.astype(BF16), nt,
                                        preferred_element_type=F32)
        m = C // 2
        level = 1
        while m >= 1:
            rho = _group_ref(b, m, row)
            upper = (row & m) != 0
            d = b - rho
            x = (jnp.where(upper, q, k) * jnp.exp(jnp.where(upper, d, -d))).astype(BF16)
            sc = sc + masks[level] * lax.dot_general(x, x, nt, preferred_element_type=F32)
            level += 1
            m //= 2
        intra = jnp.dot(sc.astype(BF16), vb, preferred_element_type=F32)

    b_last = b[C - 1:C, :]
    eye = (lax.broadcasted_iota(jnp.int32, (C_DK, C_DK), 0)
           == lax.broadcasted_iota(jnp.int32, (C_DK, C_DK), 1))
    dcol = jnp.sum(jnp.where(eye, jnp.broadcast_to(jnp.exp(b_last), (C_DK, C_DK)), 0.0),
                   axis=-1, keepdims=True)
    kd = k * jnp.exp(b_last - b)
    S_new = dcol * S + lax.dot_general(kd.astype(BF16), vb, (((0,), (0,)), ((), ())),
                                       preferred_element_type=F32)
    return inter + intra, S_new


def _gated_out(o, og, gz):
    ms = jnp.mean(o * o, axis=-1, keepdims=True)
    return (o * lax.rsqrt(ms + EPS) * og) * _silu(gz)


def _hgrn_prompt_kernel(q_ref, f_ref, i_ref, g_ref, lb_ref, og_ref, mask_ref, o_ref, s_ref, S):
    c = pl.program_id(2)

    @pl.when(c == 0)
    def _():
        S[...] = jnp.zeros_like(S)

    lb = lb_ref[...]
    og = og_ref[...]

    Sv = S[...]
    for n in range(GLA_STEP // GLA_CHUNK):
        rs = slice(n * GLA_CHUNK, (n + 1) * GLA_CHUNK)
        o, Sv = _gla_chunk(q_ref[rs, :], f_ref[rs, :], i_ref[rs, :], lb, Sv, mask_ref)
        o_ref[rs, :] = _gated_out(o, og, g_ref[rs, :]).astype(BF16)
    S[...] = Sv

    @pl.when(c == pl.num_programs(2) - 1)
    def _():
        s_ref[0, 0] = S[...]


def _hgrn_prompt(proj, B, L, T, lb, og):
    ns = L // GLA_STEP
    H = C_HEADS
    col = lambda off: pl.BlockSpec((GLA_STEP, LANES), lambda b, h, c: (b * ns + c, off + h))
    masks = _gla_consts(GLA_CHUNK)
    return pl.pallas_call(
        _hgrn_prompt_kernel,
        grid=(B, H, ns),
        in_specs=[col(0), col(H), col(2 * H), col(3 * H),
                  pl.BlockSpec((1, LANES), lambda b, h, c: (0, h)),
                  pl.BlockSpec((1, LANES), lambda b, h, c: (0, 0)),
                  pl.BlockSpec(masks.shape, lambda b, h, c: (0, 0, 0))],
        out_specs=[pl.BlockSpec((GLA_STEP, LANES), lambda b, h, c: (b * ns + c, h)),
                   pl.BlockSpec((1, 1, C_DK, C_DV), lambda b, h, c: (b, h, 0, 0))],
        out_shape=[jax.ShapeDtypeStruct((B * L, D_MODEL), BF16),
                   jax.ShapeDtypeStruct((B, H, C_DK, C_DV), F32)],
        scratch_shapes=[pltpu.VMEM((C_DK, C_DV), F32)],
        compiler_params=_cparams(("arbitrary", "arbitrary", "arbitrary")),
        name="hgrn_prompt",
    )(proj, proj, proj, proj, lb, og, masks)


def _hgrn_sample_kernel(p_ref, lb_ref, og_ref, s0_ref, o_ref, s_ref):
    Ld = p_ref.shape[0] // SAMPLE_GB
    og = og_ref[...]
    results = []
    for bb in range(SAMPLE_GB):
        rs = slice(bb * Ld, (bb + 1) * Ld)
        for h in range(C_HEADS):
            cs = lambda part: slice((part * C_HEADS + h) * LANES, (part * C_HEADS + h + 1) * LANES)
            o, S_new = _gla_chunk(p_ref[rs, cs(0)], p_ref[rs, cs(1)], p_ref[rs, cs(2)],
                                  lb_ref[:, h * LANES:(h + 1) * LANES], s0_ref[bb, h])
            results.append((bb, h, rs, S_new, _gated_out(o, og, p_ref[rs, cs(3)]).astype(BF16)))
    for bb, h, rs, S_new, out in results:
        s_ref[bb, h] = S_new
        o_ref[rs, h * LANES:(h + 1) * LANES] = out


def _hgrn_sample(proj, Bd, Ld, row0, lb, og, s0):
    GB = SAMPLE_GB
    rows = GB * Ld
    rb0 = row0 // rows
    H = C_HEADS
    return pl.pallas_call(
        _hgrn_sample_kernel,
        grid=(Bd // GB,),
        in_specs=[pl.BlockSpec((rows, 4 * C_HK), lambda i: (rb0 + i, 0)),
                  pl.BlockSpec((1, C_HK), lambda i: (0, 0)),
                  pl.BlockSpec((1, LANES), lambda i: (0, 0)),
                  pl.BlockSpec((GB, H, C_DK, C_DV), lambda i: (i, 0, 0, 0))],
        out_specs=[pl.BlockSpec((rows, D_MODEL), lambda i: (i, 0)),
                   pl.BlockSpec((GB, H, C_DK, C_DV), lambda i: (i, 0, 0, 0))],
        out_shape=[jax.ShapeDtypeStruct((Bd * Ld, D_MODEL), BF16),
                   jax.ShapeDtypeStruct((Bd, H, C_DK, C_DV), F32)],
        compiler_params=_cparams(("arbitrary",)),
        name="hgrn_sample",
    )(proj, lb, og, s0)


def _pack_bf16_pairs(xb):
    n = xb.shape[1] // 2
    lo = lax.bitcast_convert_type(xb[:, :n].astype(F32), jnp.uint32)
    hi = lax.bitcast_convert_type(xb[:, n:].astype(F32), jnp.uint32)
    return (lo >> 16) | (hi & jnp.uint32(0xFFFF0000))


def _unpack_bf16_pairs(w):
    lo = lax.bitcast_convert_type(w << 16, F32)
    hi = lax.bitcast_convert_type(w & jnp.uint32(0xFFFF0000), F32)
    return jnp.concatenate([lo, hi], axis=1).astype(BF16)


def _out_proj_kernel(n_mix, n_x, npt, *refs):
    m = _read_rows(refs[:n_mix], npt)
    x = _read_rows(refs[n_mix:n_mix + n_x], npt)
    w_ref, g_ref, wr_ref, br_ref, x1_ref, h_ref, lg_ref = refs[n_mix + n_x:]
    x1 = x + jnp.dot(m, w_ref[...], preferred_element_type=F32)
    x1_ref[...] = x1
    ms = jnp.mean(x1 * x1, axis=-1, keepdims=True)
    hb = (x1 * lax.rsqrt(ms + EPS) * g_ref[...]).astype(BF16)
    h_ref[...] = _pack_bf16_pairs(hb)
    lg_ref[...] = lax.dot_general(wr_ref[...], hb, (((1,), (1,)), ((), ())),
                                  preferred_element_type=F32) + br_ref[...]


def _out_proj(mix, w, x, g, wr, br, tm):
    K, D = w.shape
    m_specs, m_arrays, npt, T = _row_source(mix, tm)
    x_specs, x_arrays, npt_x, _ = _row_source(x, tm)
    assert npt_x in (0, npt)
    row = lambda n: pl.BlockSpec((tm, n), lambda i: (i, 0))
    full = lambda a, b: pl.BlockSpec((a, b), lambda i: (0, 0))
    return pl.pallas_call(
        functools.partial(_out_proj_kernel, len(m_arrays), len(x_arrays), npt),
        grid=(T // tm,),
        in_specs=m_specs + x_specs + [full(K, D), full(1, D), full(LOGIT_W, D), full(LOGIT_W, 1)],
        out_specs=[row(D), row(D // 2), pl.BlockSpec((LOGIT_W, tm), lambda i: (0, i))],
        out_shape=[jax.ShapeDtypeStruct((T, D), F32), jax.ShapeDtypeStruct((T, D // 2), jnp.uint32),
                   jax.ShapeDtypeStruct((LOGIT_W, T), F32)],
        compiler_params=_cparams(("arbitrary",)),
        name="out_proj",
    )(*m_arrays, *x_arrays, w, g.reshape(1, D), wr, br)


def _route_kernel(lg_ref, slot_ref, gate_ref, cnt_ref, carry, total, tri):
    phase = pl.program_id(0)
    step = pl.program_id(1)
    tm = lg_ref.shape[1]
    G, E = N_GROUPS, EXP_PER_GROUP
    row8 = lax.broadcasted_iota(jnp.int32, (G, tm), 0)

    glog = lg_ref[0:G, :]
    gmax = jnp.max(glog, axis=0, keepdims=True)
    g_idx = jnp.min(jnp.where(glog == gmax, row8, G), axis=0, keepdims=True)
    g_p = 1.0 / jnp.sum(jnp.exp(glog - gmax), axis=0, keepdims=True)

    elog = jnp.zeros((E, tm), F32)
    for g in range(G):
        elog = jnp.where(g_idx == g, lg_ref[G + g * E:G + (g + 1) * E, :], elog)
    emax = jnp.max(elog, axis=0, keepdims=True)
    ee = jnp.exp(elog - emax)
    prob = ee / jnp.sum(ee, axis=0, keepdims=True)
    p1 = jnp.max(prob, axis=0, keepdims=True)
    i1 = jnp.min(jnp.where(prob == p1, row8, E), axis=0, keepdims=True)
    prob2 = jnp.where(row8 == i1, -1.0, prob)
    p2 = jnp.max(prob2, axis=0, keepdims=True)
    i2 = jnp.min(jnp.where(prob2 == p2, row8, E), axis=0, keepdims=True)
    psum = p1 + p2
    w1 = p1 / psum * g_p
    w2 = p2 / psum * g_p

    erow = lax.broadcasted_iota(jnp.int32, (N_EXPERTS, tm), 0)
    oh1 = erow == g_idx * E + i1
    oh2 = erow == g_idx * E + i2
    both = jnp.where(oh1 | oh2, 1.0, 0.0)
    tile_cnt = jnp.sum(both, axis=1, keepdims=True)

    @pl.when((phase == 0) & (step == 0))
    def _():
        total[...] = jnp.zeros_like(total)
        r = lax.broadcasted_iota(jnp.int32, (tm, tm), 0)
        c = lax.broadcasted_iota(jnp.int32, (tm, tm), 1)
        tri[...] = jnp.where(r < c, 1.0, 0.0).astype(BF16)

    @pl.when(phase == 0)
    def _():
        total[...] += tile_cnt

    @pl.when((phase == 1) & (step == 0))
    def _():
        carry[...] = jnp.zeros_like(carry)

    @pl.when(phase == 1)
    def _():
        cnt = total[...]
        padded = jnp.floor((cnt + (MOE_BLOCK - 1)) * (1.0 / MOE_BLOCK)) * MOE_BLOCK
        pb = jnp.broadcast_to(padded, (N_EXPERTS, LANES))
        pstart = (_cumsum_rows(pb) - pb)[:, 0:1]
        before = (jnp.dot(both.astype(BF16), tri[...], preferred_element_type=F32)
                  + (carry[...] + pstart))
        s1 = jnp.sum(jnp.where(oh1, before, 0.0), axis=0, keepdims=True)
        s2 = jnp.sum(jnp.where(oh2, before, 0.0), axis=0, keepdims=True)
        carry[...] += tile_cnt
        slot_ref[...] = jnp.concatenate([s1, s2], axis=0).astype(jnp.int32)
        grow = lax.broadcasted_iota(jnp.int32, (LOGIT_W, tm), 0)
        gate_ref[...] = jnp.where(grow == 0, w1, jnp.where(grow == 1, w2, 0.0)).T
        cnt_ref[...] = jnp.broadcast_to(cnt, (N_EXPERTS, LANES))


def _route(logits_t):
    T = logits_t.shape[1]
    tm = ROUTE_TM
    return pl.pallas_call(
        _route_kernel,
        grid=(2, T // tm),
        in_specs=[pl.BlockSpec((LOGIT_W, tm), lambda p, i: (0, i))],
        out_specs=[pl.BlockSpec((2, tm), lambda p, i: (0, i * p)),
                   pl.BlockSpec((tm, LOGIT_W), lambda p, i: (i * p, 0)),
                   pl.BlockSpec((N_EXPERTS, LANES), lambda p, i: (0, 0))],
        out_shape=[jax.ShapeDtypeStruct((2, T), jnp.int32),
                   jax.ShapeDtypeStruct((T, LOGIT_W), F32),
                   jax.ShapeDtypeStruct((N_EXPERTS, LANES), F32)],
        scratch_shapes=[pltpu.VMEM((N_EXPERTS, 1), F32), pltpu.VMEM((N_EXPERTS, 1), F32),
                        pltpu.VMEM((tm, tm), BF16)],
        compiler_params=_cparams(("arbitrary", "arbitrary")),
        name="route",
    )(logits_t)


DISPATCH_TM = 512


def _dispatch_kernel(T, slot_ref, h_ref, xs_in_ref, xs_ref, sem):
    del xs_in_ref
    base = pl.program_id(0) * DISPATCH_TM

    def copy(r, k):
        return pltpu.make_async_copy(h_ref.at[pl.ds(r, 1)],
                                     xs_ref.at[pl.ds(slot_ref[k * T + base + r], 1)], sem)

    def start(r, c):
        copy(r, 0).start()
        copy(r, 1).start()
        return c

    def wait(r, c):
        copy(r, 0).wait()
        copy(r, 1).wait()
        return c

    lax.fori_loop(0, DISPATCH_TM, start, 0, unroll=8)
    lax.fori_loop(0, DISPATCH_TM, wait, 0, unroll=8)


def _dispatch(slots, h, n_slots):
    T, D = h.shape
    xs0 = jnp.zeros((n_slots, D), h.dtype)
    grid_spec = pltpu.PrefetchScalarGridSpec(
        num_scalar_prefetch=1,
        grid=(T // DISPATCH_TM,),
        in_specs=[pl.BlockSpec((DISPATCH_TM, D), lambda i, s: (i, 0)),
                  pl.BlockSpec(memory_space=pl.ANY)],
        out_specs=pl.BlockSpec(memory_space=pl.ANY),
        scratch_shapes=[pltpu.SemaphoreType.DMA(())],
    )
    return pl.pallas_call(
        functools.partial(_dispatch_kernel, T),
        grid_spec=grid_spec,
        out_shape=jax.ShapeDtypeStruct((n_slots, D), h.dtype),
        input_output_aliases={2: 0},
        compiler_params=_cparams(("arbitrary",)),
        name="moe_dispatch",
    )(slots, h, xs0)


def _ffn_kernel(nblk, layer, meta_ref, x_ref, w1_hbm, w3_hbm, w2_hbm, y_ref,
                wf1, wf3, wf2, sems, wb1, wb3, wb2):
    b = pl.program_id(0)
    used = b < meta_ref[nblk]
    e = meta_ref[b]
    new_expert = (b == 0) | (e != meta_ref[jnp.maximum(b - 1, 0)])
    slot = meta_ref[nblk + 1 + e] % 2
    nxt = meta_ref[nblk + 1 + N_EXPERTS + e]

    def fetch(expert, s):
        return (pltpu.make_async_copy(w1_hbm.at[layer, expert], wf1.at[s], sems.at[s, 0]),
                pltpu.make_async_copy(w3_hbm.at[layer, expert], wf3.at[s], sems.at[s, 1]),
                pltpu.make_async_copy(w2_hbm.at[layer, expert], wf2.at[s], sems.at[s, 2]))

    @pl.when(used & (b == 0))
    def _():
        for cp in fetch(e, slot):
            cp.start()

    @pl.when(used & new_expert)
    def _():
        for cp in fetch(e, slot):
            cp.wait()

        @pl.when(nxt >= 0)
        def _():
            for cp in fetch(nxt, 1 - slot):
                cp.start()

        wb1[...] = wf1[slot].astype(BF16)
        wb3[...] = wf3[slot].astype(BF16)
        wb2[...] = wf2[slot].astype(BF16)

    @pl.when(used)
    def _():
        x = _unpack_bf16_pairs(x_ref[...])
        a = jnp.dot(x, wb1[...], preferred_element_type=F32)
        c = jnp.dot(x, wb3[...], preferred_element_type=F32)
        hid = (a * _sigmoid(a)) * c
        y_ref[...] = jnp.dot(hid.astype(BF16), wb2[...], preferred_element_type=F32)

    @pl.when(jnp.logical_not(used))
    def _():
        y_ref[...] = jnp.zeros_like(y_ref)


def _ffn(meta, xs, w1, w3, w2, layer):
    n_slots = xs.shape[0]
    D = w1.shape[2]
    nblk = n_slots // MOE_BLOCK
    grid_spec = pltpu.PrefetchScalarGridSpec(
        num_scalar_prefetch=1,
        grid=(nblk,),
        in_specs=[pl.BlockSpec((MOE_BLOCK, xs.shape[1]), lambda b, e: (b, 0)),
                  pl.BlockSpec(memory_space=pl.ANY), pl.BlockSpec(memory_space=pl.ANY),
                  pl.BlockSpec(memory_space=pl.ANY)],
        out_specs=pl.BlockSpec((MOE_BLOCK, D), lambda b, e: (b, 0)),
        scratch_shapes=[pltpu.VMEM((2, D, D_EXPERT), F32), pltpu.VMEM((2, D, D_EXPERT), F32),
                        pltpu.VMEM((2, D_EXPERT, D), F32), pltpu.SemaphoreType.DMA((2, 3)),
                        pltpu.VMEM((D, D_EXPERT), BF16), pltpu.VMEM((D, D_EXPERT), BF16),
                        pltpu.VMEM((D_EXPERT, D), BF16)],
    )
    return pl.pallas_call(
        functools.partial(_ffn_kernel, nblk, layer),
        grid_spec=grid_spec,
        out_shape=jax.ShapeDtypeStruct((n_slots, D), F32),
        compiler_params=_cparams(("arbitrary",)),
        name="moe_ffn",
    )(meta, xs, w1, w3, w2)


COMBINE_TM = 512
COMBINE_CHUNK = 128


def _combine_kernel(T, npt, slot_ref, x_ref, gate_ref, ys_ref, *refs):
    out_refs, (buf, sems) = refs[:-2], refs[-2:]
    i = pl.program_id(0)
    base = i * COMBINE_TM
    n_chunks = COMBINE_TM // COMBINE_CHUNK

    def copy(r, k, q):
        return pltpu.make_async_copy(ys_ref.at[pl.ds(slot_ref[k * T + base + r], 1)],
                                     buf.at[k, pl.ds(r, 1)], sems.at[q])

    def for_chunk_rows(q, fn):
        def body(r, c):
            fn(copy(r, 0, q))
            fn(copy(r, 1, q))
            return c
        lax.fori_loop(q * COMBINE_CHUNK, (q + 1) * COMBINE_CHUNK, body, 0, unroll=8)

    for q in range(n_chunks):
        for_chunk_rows(q, lambda cp: cp.start())

    def emit(rs, y):
        if npt == 0:
            out_refs[0][rs, :] = y
        else:
            @pl.when(i < npt)
            def _():
                out_refs[0][rs, :] = y

            @pl.when(i >= npt)
            def _():
                out_refs[1][rs, :] = y

    for q in range(n_chunks):
        for_chunk_rows(q, lambda cp: cp.wait())
        rs = slice(q * COMBINE_CHUNK, (q + 1) * COMBINE_CHUNK)
        g = gate_ref[rs, :]
        emit(rs, x_ref[rs, :] + (buf[0, rs, :] * g[:, 0:1] + buf[1, rs, :] * g[:, 1:2]))


def _combine(slots, x1, gates, ys, split_rows=0):
    T, D = x1.shape
    tm = COMBINE_TM
    npt = split_rows // tm
    if npt == 0:
        out_specs = pl.BlockSpec((tm, D), lambda i, s: (i, 0))
        out_shape = jax.ShapeDtypeStruct((T, D), F32)
    else:
        out_specs = [pl.BlockSpec((tm, D), lambda i, s: (jnp.minimum(i, npt - 1), 0)),
                     pl.BlockSpec((tm, D), lambda i, s: (jnp.maximum(i - npt, 0), 0))]
        out_shape = [jax.ShapeDtypeStruct((split_rows, D), F32),
                     jax.ShapeDtypeStruct((T - split_rows, D), F32)]
    grid_spec = pltpu.PrefetchScalarGridSpec(
        num_scalar_prefetch=1,
        grid=(T // tm,),
        in_specs=[pl.BlockSpec((tm, D), lambda i, s: (i, 0)),
                  pl.BlockSpec((tm, LOGIT_W), lambda i, s: (i, 0)),
                  pl.BlockSpec(memory_space=pl.ANY)],
        out_specs=out_specs,
        scratch_shapes=[pltpu.VMEM((2, tm, D), F32),
                        pltpu.SemaphoreType.DMA((COMBINE_TM // COMBINE_CHUNK,))],
    )
    return pl.pallas_call(
        functools.partial(_combine_kernel, T, npt),
        grid_spec=grid_spec,
        out_shape=out_shape,
        compiler_params=_cparams(("arbitrary",)),
        name="moe_combine",
    )(slots, x1, gates, ys)


def _moe(x1, h, logits, w1, w3, w2, layer, split_rows=0):
    T = x1.shape[0]
    slot_l, gate_l, cnt = _route(logits)
    slots = slot_l.reshape(2 * T)
    n_blocks = -(-(2 * T) // MOE_BLOCK) + N_EXPERTS
    counts = cnt[:, 0].astype(jnp.int32)
    pend = jnp.cumsum((counts + MOE_BLOCK - 1) // MOE_BLOCK * MOE_BLOCK)
    starts = jnp.arange(n_blocks, dtype=jnp.int32) * MOE_BLOCK
    blk_e = jnp.minimum(jnp.sum((pend[None, :] <= starts[:, None]).astype(jnp.int32), axis=1),
                        N_EXPERTS - 1)
    present = (counts > 0).astype(jnp.int32)
    rank = jnp.cumsum(present) - present
    ids = jnp.where(present > 0, jnp.arange(N_EXPERTS, dtype=jnp.int32), N_EXPERTS)
    after = jnp.concatenate([lax.cummin(ids[::-1])[::-1][1:], jnp.full((1,), N_EXPERTS, jnp.int32)])
    nxt = jnp.where(after < N_EXPERTS, after, -1)
    meta = jnp.concatenate([blk_e, pend[-1:] // MOE_BLOCK, rank, nxt]).astype(jnp.int32)
    xs = _dispatch(slots, h, n_blocks * MOE_BLOCK)
    ys = _ffn(meta, xs, w1, w3, w2, layer)
    return _combine(slots, x1, gate_l, ys, split_rows)


def _router_weights(w_grp, b_grp, w_exp, b_exp):
    D = w_grp.shape[0]
    pad = LOGIT_W - N_GROUPS - N_EXPERTS
    wr = jnp.concatenate([w_grp, w_exp, jnp.zeros((D, pad), F32)], axis=1).astype(BF16)
    br = jnp.concatenate([b_grp, b_exp, jnp.zeros((pad,), F32)]).reshape(LOGIT_W, 1)
    return wr.T, br


def kernel(x_prompt, x_sample, cache_conv, cache_k, cache_v, state_hgrn, norm_mix, norm_ffn,
           ev_w_in, ev_conv, ev_q_norm, ev_k_norm, ev_sinks, ev_w_out,
           od_w_in, od_lb, od_o_norm, od_w_out,
           moe_w_grp, moe_b_grp, moe_w_exp, moe_b_exp, moe_w1, moe_w3, moe_w2):
    B, L, D = x_prompt.shape
    Bd, Ld, _ = x_sample.shape
    Tp = B * L
    T = Tp + Bd * Ld
    x = (x_prompt.reshape(Tp, D), x_sample.reshape(Bd * Ld, D))

    proj = _norm_matmul(x, norm_mix[0], ev_w_in[0].astype(BF16), 512)
    qg2 = jnp.tile(ev_q_norm[0], 2).reshape(1, LANES)
    kg2 = jnp.tile(ev_k_norm[0], 2).reshape(1, LANES)
    mix_p, k_p, v_p, conv_p = _even_prompt(proj, B, L, T, ev_conv[0], qg2, kg2, ev_sinks[0])
    mix_s, k_s, v_s, conv_s = _even_sample(
        proj, Bd, Ld, Tp, ev_conv[0], qg2, kg2, ev_sinks[0], cache_conv[0],
        cache_k[0].reshape(Bd, WINDOW, LANES), cache_v[0].reshape(Bd, WINDOW, LANES))
    wr, br = _router_weights(moe_w_grp[0], moe_b_grp[0], moe_w_exp[0], moe_b_exp[0])
    x1, h, logits = _out_proj((mix_p, mix_s), ev_w_out[0].astype(BF16), x, norm_ffn[0], wr, br, 512)
    x = _moe(x1, h, logits, moe_w1, moe_w3, moe_w2, 0)

    lb_all = jnp.cumsum(jax.nn.softmax(od_lb.astype(F32), axis=0), axis=0)
    lb = (lb_all - lb_all[0])[1].reshape(1, C_HK)
    og = od_o_norm[0].reshape(1, C_DV)
    proj = _norm_matmul(x, norm_mix[1], od_w_in[0].astype(BF16), 512)
    o_p, s_p = _hgrn_prompt(proj, B, L, T, lb, og)
    o_s, s_s = _hgrn_sample(proj, Bd, Ld, Tp, lb, og, state_hgrn[0])
    wr, br = _router_weights(moe_w_grp[1], moe_b_grp[1], moe_w_exp[1], moe_b_exp[1])
    x1, h, logits = _out_proj((o_p, o_s), od_w_out[0].astype(BF16), x, norm_ffn[1], wr, br, 512)
    y_p, y_s = _moe(x1, h, logits, moe_w1, moe_w3, moe_w2, 1, split_rows=Tp)

    y_prompt = y_p.reshape(B, L, D)
    y_sample = y_s.reshape(Bd, Ld, D)
    return (y_prompt, y_sample,
            conv_p[None], k_p.reshape(1, B, WINDOW, N_KV, HEAD_DIM), v_p.reshape(1, B, WINDOW, N_KV, HEAD_DIM),
            s_p[None],
            conv_s[None], k_s.reshape(1, Bd, WINDOW, N_KV, HEAD_DIM), v_s.reshape(1, Bd, WINDOW, N_KV, HEAD_DIM),
            s_s[None])
```

```python
import functools

import numpy as np
import jax
import jax.numpy as jnp
from jax import lax
from jax.experimental import pallas as pl
from jax.experimental.pallas import tpu as pltpu

F32 = jnp.float32
BF16 = jnp.bfloat16

D_MODEL = 1024
PAST_LEN = 16384
D_CONV = 512
CONV_W = 3
HEAD_DIM = 64
N_Q = 8
N_KV = 2
GQA_G = N_Q // N_KV
WINDOW = 128
ROPE_THETA = 10000.0
D_IN_EVEN = 3 * D_CONV + (N_Q + 2 * N_KV) * HEAD_DIM
C_HEADS = 8
C_DK = 128
C_DV = 128
C_HK = C_HEADS * C_DK
N_GROUPS = 8
EXP_PER_GROUP = 8
N_EXPERTS = N_GROUPS * EXP_PER_GROUP
D_EXPERT = 512
MOE_BLOCK = 256
EPS = 1e-6

LANES = 128
SUBLANES = 8
VMEM_LIMIT = 56 * 1024 * 1024

GLA_CHUNK = 128
GLA_STEP = 1024
SAMPLE_GB = 2
ROUTE_TM = 512
LOGIT_W = 128


def _cparams(sem):
    return pltpu.CompilerParams(dimension_semantics=sem, vmem_limit_bytes=VMEM_LIMIT)


def _sigmoid(x):
    return 1.0 / (1.0 + jnp.exp(-x))


def _silu(x):
    return x * (0.5 * jnp.tanh(0.5 * x) + 0.5)


def _row_source(src, tm):
    if isinstance(src, tuple):
        a, b = src
        n = a.shape[1]
        npt = a.shape[0] // tm
        specs = [pl.BlockSpec((tm, n), lambda i, *_: (jnp.minimum(i, npt - 1), 0)),
                 pl.BlockSpec((tm, n), lambda i, *_: (jnp.maximum(i - npt, 0), 0))]
        return specs, [a, b], npt, a.shape[0] + b.shape[0]
    return [pl.BlockSpec((tm, src.shape[1]), lambda i, *_: (i, 0))], [src], 0, src.shape[0]


def _read_rows(refs, npt):
    if len(refs) == 2:
        return jnp.where(pl.program_id(0) < npt, refs[0][...], refs[1][...])
    return refs[0][...]


def _norm_matmul_kernel(n_src, npt, *refs):
    x = _read_rows(refs[:n_src], npt)
    g_ref, w_ref, o_ref = refs[n_src:]
    ms = jnp.mean(x * x, axis=-1, keepdims=True)
    h = (x * lax.rsqrt(ms + EPS) * g_ref[...]).astype(BF16)
    o_ref[...] = jnp.dot(h, w_ref[...], preferred_element_type=F32)


def _norm_matmul(x, g, w, tm):
    D, N = w.shape
    specs, arrays, npt, T = _row_source(x, tm)
    return pl.pallas_call(
        functools.partial(_norm_matmul_kernel, len(arrays), npt),
        grid=(T // tm,),
        in_specs=specs + [pl.BlockSpec((1, D), lambda i: (0, 0)),
                          pl.BlockSpec((D, N), lambda i: (0, 0), pipeline_mode=pl.Buffered(1))],
        out_specs=pl.BlockSpec((tm, N), lambda i: (i, 0)),
        out_shape=jax.ShapeDtypeStruct((T, N), F32),
        compiler_params=_cparams(("arbitrary",)),
        name="norm_matmul",
    )(*arrays, g.reshape(1, D), w)


def _rope_tables(pos):
    inv = ROPE_THETA ** (-jnp.arange(0, HEAD_DIM, 2, dtype=F32) / HEAD_DIM)
    ang = pos.astype(F32)[:, None] * inv[None, :]
    cos = jnp.cos(ang)
    sin = jnp.sin(ang)
    return (jnp.concatenate([cos, cos, cos, cos], axis=1),
            jnp.concatenate([-sin, sin, -sin, sin], axis=1))


def _headnorm_rope(x, g2, cos, sin):
    lane = lax.broadcasted_iota(jnp.int32, x.shape, 1)
    lo = lane < HEAD_DIM
    x2 = x * x
    s_lo = jnp.sum(jnp.where(lo, x2, 0.0), axis=-1, keepdims=True)
    s_hi = jnp.sum(jnp.where(lo, 0.0, x2), axis=-1, keepdims=True)
    ms = jnp.where(lo, s_lo, s_hi) * (1.0 / HEAD_DIM)
    y = x * lax.rsqrt(ms + EPS) * g2
    first_half = (lane & (HEAD_DIM // 2)) == 0
    swapped = jnp.where(first_half, pltpu.roll(y, LANES - HEAD_DIM // 2, 1),
                        pltpu.roll(y, HEAD_DIM // 2, 1))
    return y * cos + swapped * sin


def _gated_conv(gb, u, c2, c1, cw):
    R = u.shape[0]
    row = lax.broadcasted_iota(jnp.int32, u.shape, 0)
    u1 = jnp.where(row == 0, c1, pltpu.roll(u, 1, 0))
    u2 = jnp.where(row == 0, c2, jnp.where(row == 1, c1, pltpu.roll(u, 2, 0)))
    del R
    return gb * (cw[0:1, :] * u2 + cw[1:2, :] * u1 + cw[2:3, :] * u)


def _band_attention(qs, kk, vv, valid, sinkv):
    s = lax.dot_general(qs.astype(BF16), kk.astype(BF16), (((1,), (1,)), ((), ())),
                        preferred_element_type=F32) * (HEAD_DIM ** -0.5)
    s = jnp.where(valid, s, -jnp.inf)
    m = jnp.maximum(jnp.max(s, axis=-1, keepdims=True), sinkv)
    e = jnp.exp(s - m)
    den = jnp.sum(e, axis=-1, keepdims=True) + jnp.exp(sinkv - m)
    p = e / den
    return jnp.dot(p.astype(BF16), vv.astype(BF16), preferred_element_type=F32)


def _sink_column(sinks_ref, hk, rows_per_head):
    R = GQA_G * rows_per_head
    row = lax.broadcasted_iota(jnp.int32, (R, 1), 0)
    col = jnp.full((R, 1), sinks_ref[hk * GQA_G + GQA_G - 1], F32)
    for j in range(GQA_G - 2, -1, -1):
        col = jnp.where(row < (j + 1) * rows_per_head, sinks_ref[hk * GQA_G + j], col)
    return col


def _even_prompt_kernel(sinks_ref, proj_ref, cos_ref, sin_ref, cw_ref, qg_ref, kg_ref,
                        mix_ref, kl_ref, vl_ref, cl_ref, kprev, vprev, ucar):
    blk = pl.program_id(1)
    W = WINDOW

    @pl.when(blk == 0)
    def _():
        kprev[...] = jnp.zeros_like(kprev)
        vprev[...] = jnp.zeros_like(vprev)
        ucar[...] = jnp.zeros_like(ucar)

    gb = proj_ref[:, 0:D_CONV]
    u = proj_ref[:, D_CONV:2 * D_CONV] * proj_ref[:, 2 * D_CONV:3 * D_CONV]
    car = ucar[...]
    a_out = _gated_conv(gb, u, car[0:1, :], car[1:2, :], cw_ref[...])
    ucar[0:2, :] = u[W - 2:W, :]
    mix_ref[:, 0:D_CONV] = a_out.astype(BF16)

    cos = cos_ref[...]
    sin = sin_ref[...]
    q0 = 3 * D_CONV
    k0 = q0 + N_Q * HEAD_DIM
    v0 = k0 + N_KV * HEAD_DIM
    k_r = _headnorm_rope(proj_ref[:, k0:k0 + LANES], kg_ref[...], cos, sin)
    v_r = proj_ref[:, v0:v0 + LANES]
    q_r = [_headnorm_rope(proj_ref[:, q0 + LANES * j:q0 + LANES * (j + 1)], qg_ref[...], cos, sin)
           for j in range(N_Q * HEAD_DIM // LANES)]
    k_p = kprev[...]
    v_p = vprev[...]

    R = GQA_G * W
    i = lax.broadcasted_iota(jnp.int32, (R, 2 * W), 0) & (W - 1)
    j = lax.broadcasted_iota(jnp.int32, (R, 2 * W), 1)
    diff = i + W - j
    valid = (diff >= 0) & (diff <= W) & ((blk > 0) | (j >= W))

    for hk in range(N_KV):
        ls = slice(hk * HEAD_DIM, (hk + 1) * HEAD_DIM)
        kk = jnp.concatenate([k_p[:, ls], k_r[:, ls]], axis=0)
        vv = jnp.concatenate([v_p[:, ls], v_r[:, ls]], axis=0)
        heads = []
        for g in range(GQA_G):
            h = hk * GQA_G + g
            tile = q_r[h // 2]
            heads.append(tile[:, (h % 2) * HEAD_DIM:(h % 2 + 1) * HEAD_DIM])
        qs = jnp.concatenate(heads, axis=0)
        o = _band_attention(qs, kk, vv, valid, _sink_column(sinks_ref, hk, W))
        for g in range(GQA_G):
            h = hk * GQA_G + g
            mix_ref[:, D_CONV + h * HEAD_DIM:D_CONV + (h + 1) * HEAD_DIM] = \
                o[g * W:(g + 1) * W, :].astype(BF16)

    kprev[...] = k_r
    vprev[...] = v_r

    @pl.when(blk == pl.num_programs(1) - 1)
    def _():
        kl_ref[0] = k_r
        vl_ref[0] = v_r
        cl_ref[0] = u[W - 2:W, :]


def _even_prompt(proj, B, L, T, cw, qg2, kg2, sinks):
    nb = L // WINDOW
    cos, sin = _rope_tables(jnp.arange(L, dtype=jnp.int32))
    full = lambda shape: pl.BlockSpec(shape, lambda b, i, *_: tuple(0 for _ in shape))
    grid_spec = pltpu.PrefetchScalarGridSpec(
        num_scalar_prefetch=1,
        grid=(B, nb),
        in_specs=[pl.BlockSpec((WINDOW, D_IN_EVEN), lambda b, i, s: (b * nb + i, 0)),
                  pl.BlockSpec((WINDOW, LANES), lambda b, i, s: (i, 0)),
                  pl.BlockSpec((WINDOW, LANES), lambda b, i, s: (i, 0)),
                  full((CONV_W, D_CONV)), full((1, LANES)), full((1, LANES))],
        out_specs=[pl.BlockSpec((WINDOW, D_MODEL), lambda b, i, s: (b * nb + i, 0)),
                   pl.BlockSpec((1, WINDOW, LANES), lambda b, i, s: (b, 0, 0)),
                   pl.BlockSpec((1, WINDOW, LANES), lambda b, i, s: (b, 0, 0)),
                   pl.BlockSpec((1, CONV_W - 1, D_CONV), lambda b, i, s: (b, 0, 0))],
        scratch_shapes=[pltpu.VMEM((WINDOW, LANES), F32), pltpu.VMEM((WINDOW, LANES), F32),
                        pltpu.VMEM((SUBLANES, D_CONV), F32)],
    )
    return pl.pallas_call(
        _even_prompt_kernel,
        grid_spec=grid_spec,
        out_shape=[jax.ShapeDtypeStruct((B * L, D_MODEL), BF16),
                   jax.ShapeDtypeStruct((B, WINDOW, LANES), F32),
                   jax.ShapeDtypeStruct((B, WINDOW, LANES), F32),
                   jax.ShapeDtypeStruct((B, CONV_W - 1, D_CONV), F32)],
        compiler_params=_cparams(("arbitrary", "arbitrary")),
        name="even_prompt",
    )(sinks, proj, cos, sin, cw, qg2, kg2)


def _even_sample_kernel(sinks_ref, proj_ref, cos_ref, sin_ref, cw_ref, qg_ref, kg_ref,
                        cc_ref, ck_ref, cv_ref, mix_ref, ko_ref, vo_ref, co_ref):
    W = WINDOW
    Ld = cos_ref.shape[0]
    cos = cos_ref[...]
    sin = sin_ref[...]
    q0 = 3 * D_CONV
    k0 = q0 + N_Q * HEAD_DIM
    v0 = k0 + N_KV * HEAD_DIM
    R = GQA_G * Ld
    i = lax.broadcasted_iota(jnp.int32, (R, 2 * W), 0) % Ld
    j = lax.broadcasted_iota(jnp.int32, (R, 2 * W), 1)
    diff = i + W - j
    valid = (diff >= 0) & (diff <= W)
    zpad = jnp.zeros((W - Ld, HEAD_DIM), F32)

    for bb in range(SAMPLE_GB):
        rs = slice(bb * Ld, (bb + 1) * Ld)
        gb = proj_ref[rs, 0:D_CONV]
        u = proj_ref[rs, D_CONV:2 * D_CONV] * proj_ref[rs, 2 * D_CONV:3 * D_CONV]
        car = cc_ref[bb]
        a_out = _gated_conv(gb, u, car[0:1, :], car[1:2, :], cw_ref[...])
        co_ref[bb] = u[Ld - 2:Ld, :]
        mix_ref[rs, 0:D_CONV] = a_out.astype(BF16)

        k_r = _headnorm_rope(proj_ref[rs, k0:k0 + LANES], kg_ref[...], cos, sin)
        v_r = proj_ref[rs, v0:v0 + LANES]
        q_r = [_headnorm_rope(proj_ref[rs, q0 + LANES * t:q0 + LANES * (t + 1)], qg_ref[...], cos, sin)
               for t in range(N_Q * HEAD_DIM // LANES)]
        k_c = ck_ref[bb]
        v_c = cv_ref[bb]
        ko_ref[bb, 0:W - Ld, :] = k_c[Ld:W, :]
        ko_ref[bb, W - Ld:W, :] = k_r
        vo_ref[bb, 0:W - Ld, :] = v_c[Ld:W, :]
        vo_ref[bb, W - Ld:W, :] = v_r

        for hk in range(N_KV):
            ls = slice(hk * HEAD_DIM, (hk + 1) * HEAD_DIM)
            kk = jnp.concatenate([k_c[:, ls], k_r[:, ls], zpad], axis=0)
            vv = jnp.concatenate([v_c[:, ls], v_r[:, ls], zpad], axis=0)
            heads = []
            for g in range(GQA_G):
                h = hk * GQA_G + g
                tile = q_r[h // 2]
                heads.append(tile[:, (h % 2) * HEAD_DIM:(h % 2 + 1) * HEAD_DIM])
            qs = jnp.concatenate(heads, axis=0)
            o = _band_attention(qs, kk, vv, valid, _sink_column(sinks_ref, hk, Ld))
            for g in range(GQA_G):
                h = hk * GQA_G + g
                mix_ref[rs, D_CONV + h * HEAD_DIM:D_CONV + (h + 1) * HEAD_DIM] = \
                    o[g * Ld:(g + 1) * Ld, :].astype(BF16)


def _even_sample(proj, Bd, Ld, row0, cw, qg2, kg2, sinks, cache_conv, cache_k, cache_v):
    GB = SAMPLE_GB
    rows = GB * Ld
    rb0 = row0 // rows
    cos, sin = _rope_tables(PAST_LEN + jnp.arange(Ld, dtype=jnp.int32))
    full = lambda shape: pl.BlockSpec(shape, lambda i, *_: tuple(0 for _ in shape))
    grid_spec = pltpu.PrefetchScalarGridSpec(
        num_scalar_prefetch=1,
        grid=(Bd // GB,),
        in_specs=[pl.BlockSpec((rows, D_IN_EVEN), lambda i, s: (rb0 + i, 0)),
                  full((Ld, LANES)), full((Ld, LANES)),
                  full((CONV_W, D_CONV)), full((1, LANES)), full((1, LANES)),
                  pl.BlockSpec((GB, CONV_W - 1, D_CONV), lambda i, s: (i, 0, 0)),
                  pl.BlockSpec((GB, WINDOW, LANES), lambda i, s: (i, 0, 0)),
                  pl.BlockSpec((GB, WINDOW, LANES), lambda i, s: (i, 0, 0))],
        out_specs=[pl.BlockSpec((rows, D_MODEL), lambda i, s: (i, 0)),
                   pl.BlockSpec((GB, WINDOW, LANES), lambda i, s: (i, 0, 0)),
                   pl.BlockSpec((GB, WINDOW, LANES), lambda i, s: (i, 0, 0)),
                   pl.BlockSpec((GB, CONV_W - 1, D_CONV), lambda i, s: (i, 0, 0))],
    )
    return pl.pallas_call(
        _even_sample_kernel,
        grid_spec=grid_spec,
        out_shape=[jax.ShapeDtypeStruct((Bd * Ld, D_MODEL), BF16),
                   jax.ShapeDtypeStruct((Bd, WINDOW, LANES), F32),
                   jax.ShapeDtypeStruct((Bd, WINDOW, LANES), F32),
                   jax.ShapeDtypeStruct((Bd, CONV_W - 1, D_CONV), F32)],
        compiler_params=_cparams(("arbitrary",)),
        name="even_sample",
    )(sinks, proj, cos, sin, cw, qg2, kg2, cache_conv, cache_k, cache_v)


def _cumsum_rows(x):
    C = x.shape[0]
    row = lax.broadcasted_iota(jnp.int32, x.shape, 0)
    s = 1
    while s < C:
        x = x + jnp.where(row >= s, pltpu.roll(x, s, 0), 0.0)
        s *= 2
    return x


def _group_ref(b, m, row):
    C, D = b.shape
    if 2 * m >= SUBLANES:
        n = C // (2 * m)
        b3 = b.reshape(n, 2 * m, D)
        return jnp.broadcast_to(b3[:, m - 1:m, :], (n, 2 * m, D)).reshape(C, D)
    r = row & (2 * m - 1)
    out = b
    for off in range(2 * m):
        if off == m - 1:
            continue
        shift = (off - (m - 1)) % C
        out = jnp.where(r == off, pltpu.roll(b, shift, 0), out)
    return out


def _gla_consts(C):
    t = np.arange(C)[:, None]
    s = np.arange(C)[None, :]
    masks = [s == t]
    m = C // 2
    while m >= 1:
        masks.append((s // (2 * m) == t // (2 * m)) & ((s & m) == 0) & ((t & m) != 0))
        m //= 2
    return jnp.asarray(np.stack(masks).astype(np.float32))


def _gla_chunk(qz, fz, v, lb, S, masks=None):
    C = qz.shape[0]
    q = _silu(qz)
    f = lb + (1.0 - lb) * _sigmoid(fz)
    k = 1.0 - f
    b = _cumsum_rows(jnp.log(f))
    vb = v.astype(BF16)

    inter = jnp.dot((q * jnp.exp(b)).astype(BF16), S.astype(BF16), preferred_element_type=F32)

    row = lax.broadcasted_iota(jnp.int32, (C, C_DK), 0)
    nt = (((1,), (1,)), ((), ()))
    if C == SUBLANES:
        vr = vb.astype(F32)
        intra = jnp.zeros((C, C_DV), F32)
        for t in range(C):
            p = jnp.where(row <= t, q[t:t + 1, :] * k * jnp.exp(jnp.minimum(b[t:t + 1, :] - b, 0.0)), 0.0)
            col = jnp.sum(p, axis=-1, keepdims=True).astype(BF16).astype(F32)
            intra = jnp.where(row == t, jnp.sum(col * vr, axis=0, keepdims=True), intra)
    else:
        sc = masks[0] * lax.dot_general(q.astype(BF16), k.astype(BF16), nt,
                                        preferred_element_type=F32)
        m = C // 2
        level = 1
        while m >= 1:
            rho = _group_ref(b, m, row)
            upper = (row & m) != 0
            d = b - rho
            x = (jnp.where(upper, q, k) * jnp.exp(jnp.where(upper, d, -d))).astype(BF16)
            sc = sc + masks[level] * lax.dot_general(x, x, nt, preferred_element_type=F32)
            level += 1
            m //= 2
        intra = jnp.dot(sc.astype(BF16), vb, preferred_element_type=F32)

    b_last = b[C - 1:C, :]
    eye = (lax.broadcasted_iota(jnp.int32, (C_DK, C_DK), 0)
           == lax.broadcasted_iota(jnp.int32, (C_DK, C_DK), 1))
    dcol = jnp.sum(jnp.where(eye, jnp.broadcast_to(jnp.exp(b_last), (C_DK, C_DK)), 0.0),
                   axis=-1, keepdims=True)
    kd = k * jnp.exp(b_last - b)
    S_new = dcol * S + lax.dot_general(kd.astype(BF16), vb, (((0,), (0,)), ((), ())),
                                       preferred_element_type=F32)
    return inter + intra, S_new


def _gated_out(o, og, gz):
    ms = jnp.mean(o * o, axis=-1, keepdims=True)
    return (o * lax.rsqrt(ms + EPS) * og) * _silu(gz)


def _hgrn_prompt_kernel(q_ref, f_ref, i_ref, g_ref, lb_ref, og_ref, mask_ref, o_ref, s_ref, S):
    c = pl.program_id(2)

    @pl.when(c == 0)
    def _():
        S[...] = jnp.zeros_like(S)

    lb = lb_ref[...]
    og = og_ref[...]

    Sv = S[...]
    for n in range(GLA_STEP // GLA_CHUNK):
        rs = slice(n * GLA_CHUNK, (n + 1) * GLA_CHUNK)
        o, Sv = _gla_chunk(q_ref[rs, :], f_ref[rs, :], i_ref[rs, :], lb, Sv, mask_ref)
        o_ref[rs, :] = _gated_out(o, og, g_ref[rs, :]).astype(BF16)
    S[...] = Sv

    @pl.when(c == pl.num_programs(2) - 1)
    def _():
        s_ref[0, 0] = S[...]


def _hgrn_prompt(proj, B, L, T, lb, og):
    ns = L // GLA_STEP
    H = C_HEADS
    col = lambda off: pl.BlockSpec((GLA_STEP, LANES), lambda b, h, c: (b * ns + c, off + h))
    masks = _gla_consts(GLA_CHUNK)
    return pl.pallas_call(
        _hgrn_prompt_kernel,
        grid=(B, H, ns),
        in_specs=[col(0), col(H), col(2 * H), col(3 * H),
                  pl.BlockSpec((1, LANES), lambda b, h, c: (0, h)),
                  pl.BlockSpec((1, LANES), lambda b, h, c: (0, 0)),
                  pl.BlockSpec(masks.shape, lambda b, h, c: (0, 0, 0))],
        out_specs=[pl.BlockSpec((GLA_STEP, LANES), lambda b, h, c: (b * ns + c, h)),
                   pl.BlockSpec((1, 1, C_DK, C_DV), lambda b, h, c: (b, h, 0, 0))],
        out_shape=[jax.ShapeDtypeStruct((B * L, D_MODEL), BF16),
                   jax.ShapeDtypeStruct((B, H, C_DK, C_DV), F32)],
        scratch_shapes=[pltpu.VMEM((C_DK, C_DV), F32)],
        compiler_params=_cparams(("arbitrary", "arbitrary", "arbitrary")),
        name="hgrn_prompt",
    )(proj, proj, proj, proj, lb, og, masks)


def _hgrn_sample_kernel(p_ref, lb_ref, og_ref, s0_ref, o_ref, s_ref):
    Ld = p_ref.shape[0] // SAMPLE_GB
    og = og_ref[...]
    results = []
    for bb in range(SAMPLE_GB):
        rs = slice(bb * Ld, (bb + 1) * Ld)
        for h in range(C_HEADS):
            cs = lambda part: slice((part * C_HEADS + h) * LANES, (part * C_HEADS + h + 1) * LANES)
            o, S_new = _gla_chunk(p_ref[rs, cs(0)], p_ref[rs, cs(1)], p_ref[rs, cs(2)],
                                  lb_ref[:, h * LANES:(h + 1) * LANES], s0_ref[bb, h])
            results.append((bb, h, rs, S_new, _gated_out(o, og, p_ref[rs, cs(3)]).astype(BF16)))
    for bb, h, rs, S_new, out in results:
        s_ref[bb, h] = S_new
        o_ref[rs, h * LANES:(h + 1) * LANES] = out


def _hgrn_sample(proj, Bd, Ld, row0, lb, og, s0):
    GB = SAMPLE_GB
    rows = GB * Ld
    rb0 = row0 // rows
    H = C_HEADS
    return pl.pallas_call(
        _hgrn_sample_kernel,
        grid=(Bd // GB,),
        in_specs=[pl.BlockSpec((rows, 4 * C_HK), lambda i: (rb0 + i, 0)),
                  pl.BlockSpec((1, C_HK), lambda i: (0, 0)),
                  pl.BlockSpec((1, LANES), lambda i: (0, 0)),
                  pl.BlockSpec((GB, H, C_DK, C_DV), lambda i: (i, 0, 0, 0))],
        out_specs=[pl.BlockSpec((rows, D_MODEL), lambda i: (i, 0)),
                   pl.BlockSpec((GB, H, C_DK, C_DV), lambda i: (i, 0, 0, 0))],
        out_shape=[jax.ShapeDtypeStruct((Bd * Ld, D_MODEL), BF16),
                   jax.ShapeDtypeStruct((Bd, H, C_DK, C_DV), F32)],
        compiler_params=_cparams(("arbitrary",)),
        name="hgrn_sample",
    )(proj, lb, og, s0)


def _pack_bf16_pairs(xb):
    n = xb.shape[1] // 2
    lo = lax.bitcast_convert_type(xb[:, :n].astype(F32), jnp.uint32)
    hi = lax.bitcast_convert_type(xb[:, n:].astype(F32), jnp.uint32)
    return (lo >> 16) | (hi & jnp.uint32(0xFFFF0000))


def _unpack_bf16_pairs(w):
    lo = lax.bitcast_convert_type(w << 16, F32)
    hi = lax.bitcast_convert_type(w & jnp.uint32(0xFFFF0000), F32)
    return jnp.concatenate([lo, hi], axis=1).astype(BF16)


def _out_proj_kernel(n_mix, n_x, npt, *refs):
    m = _read_rows(refs[:n_mix], npt)
    x = _read_rows(refs[n_mix:n_mix + n_x], npt)
    w_ref, g_ref, wr_ref, br_ref, x1_ref, h_ref, lg_ref = refs[n_mix + n_x:]
    x1 = x + jnp.dot(m, w_ref[...], preferred_element_type=F32)
    x1_ref[...] = x1
    ms = jnp.mean(x1 * x1, axis=-1, keepdims=True)
    hb = (x1 * lax.rsqrt(ms + EPS) * g_ref[...]).astype(BF16)
    h_ref[...] = _pack_bf16_pairs(hb)
    lg_ref[...] = lax.dot_general(wr_ref[...], hb, (((1,), (1,)), ((), ())),
                                  preferred_element_type=F32) + br_ref[...]


def _out_proj(mix, w, x, g, wr, br, tm):
    K, D = w.shape
    m_specs, m_arrays, npt, T = _row_source(mix, tm)
    x_specs, x_arrays, npt_x, _ = _row_source(x, tm)
    assert npt_x in (0, npt)
    row = lambda n: pl.BlockSpec((tm, n), lambda i: (i, 0))
    full = lambda a, b: pl.BlockSpec((a, b), lambda i: (0, 0))
    return pl.pallas_call(
        functools.partial(_out_proj_kernel, len(m_arrays), len(x_arrays), npt),
        grid=(T // tm,),
        in_specs=m_specs + x_specs + [full(K, D), full(1, D), full(LOGIT_W, D), full(LOGIT_W, 1)],
        out_specs=[row(D), row(D // 2), pl.BlockSpec((LOGIT_W, tm), lambda i: (0, i))],
        out_shape=[jax.ShapeDtypeStruct((T, D), F32), jax.ShapeDtypeStruct((T, D // 2), jnp.uint32),
                   jax.ShapeDtypeStruct((LOGIT_W, T), F32)],
        compiler_params=_cparams(("arbitrary",)),
        name="out_proj",
    )(*m_arrays, *x_arrays, w, g.reshape(1, D), wr, br)


def _route_kernel(lg_ref, slot_ref, gate_ref, cnt_ref, carry, total, tri):
    phase = pl.program_id(0)
    step = pl.program_id(1)
    tm = lg_ref.shape[1]
    G, E = N_GROUPS, EXP_PER_GROUP
    row8 = lax.broadcasted_iota(jnp.int32, (G, tm), 0)

    glog = lg_ref[0:G, :]
    gmax = jnp.max(glog, axis=0, keepdims=True)
    g_idx = jnp.min(jnp.where(glog == gmax, row8, G), axis=0, keepdims=True)
    g_p = 1.0 / jnp.sum(jnp.exp(glog - gmax), axis=0, keepdims=True)

    elog = jnp.zeros((E, tm), F32)
    for g in range(G):
        elog = jnp.where(g_idx == g, lg_ref[G + g * E:G + (g + 1) * E, :], elog)
    emax = jnp.max(elog, axis=0, keepdims=True)
    ee = jnp.exp(elog - emax)
    prob = ee / jnp.sum(ee, axis=0, keepdims=True)
    p1 = jnp.max(prob, axis=0, keepdims=True)
    i1 = jnp.min(jnp.where(prob == p1, row8, E), axis=0, keepdims=True)
    prob2 = jnp.where(row8 == i1, -1.0, prob)
    p2 = jnp.max(prob2, axis=0, keepdims=True)
    i2 = jnp.min(jnp.where(prob2 == p2, row8, E), axis=0, keepdims=True)
    psum = p1 + p2
    w1 = p1 / psum * g_p
    w2 = p2 / psum * g_p

    erow = lax.broadcasted_iota(jnp.int32, (N_EXPERTS, tm), 0)
    oh1 = erow == g_idx * E + i1
    oh2 = erow == g_idx * E + i2
    both = jnp.where(oh1 | oh2, 1.0, 0.0)
    tile_cnt = jnp.sum(both, axis=1, keepdims=True)

    @pl.when((phase == 0) & (step == 0))
    def _():
        total[...] = jnp.zeros_like(total)
        r = lax.broadcasted_iota(jnp.int32, (tm, tm), 0)
        c = lax.broadcasted_iota(jnp.int32, (tm, tm), 1)
        tri[...] = jnp.where(r < c, 1.0, 0.0).astype(BF16)

    @pl.when(phase == 0)
    def _():
        total[...] += tile_cnt

    @pl.when((phase == 1) & (step == 0))
    def _():
        carry[...] = jnp.zeros_like(carry)

    @pl.when(phase == 1)
    def _():
        cnt = total[...]
        padded = jnp.floor((cnt + (MOE_BLOCK - 1)) * (1.0 / MOE_BLOCK)) * MOE_BLOCK
        pb = jnp.broadcast_to(padded, (N_EXPERTS, LANES))
        pstart = (_cumsum_rows(pb) - pb)[:, 0:1]
        before = (jnp.dot(both.astype(BF16), tri[...], preferred_element_type=F32)
                  + (carry[...] + pstart))
        s1 = jnp.sum(jnp.where(oh1, before, 0.0), axis=0, keepdims=True)
        s2 = jnp.sum(jnp.where(oh2, before, 0.0), axis=0, keepdims=True)
        carry[...] += tile_cnt
        slot_ref[...] = jnp.concatenate([s1, s2], axis=0).astype(jnp.int32)
        grow = lax.broadcasted_iota(jnp.int32, (LOGIT_W, tm), 0)
        gate_ref[...] = jnp.where(grow == 0, w1, jnp.where(grow == 1, w2, 0.0)).T
        cnt_ref[...] = jnp.broadcast_to(cnt, (N_EXPERTS, LANES))


def _route(logits_t):
    T = logits_t.shape[1]
    tm = ROUTE_TM
    return pl.pallas_call(
        _route_kernel,
        grid=(2, T // tm),
        in_specs=[pl.BlockSpec((LOGIT_W, tm), lambda p, i: (0, i))],
        out_specs=[pl.BlockSpec((2, tm), lambda p, i: (0, i * p)),
                   pl.BlockSpec((tm, LOGIT_W), lambda p, i: (i * p, 0)),
                   pl.BlockSpec((N_EXPERTS, LANES), lambda p, i: (0, 0))],
        out_shape=[jax.ShapeDtypeStruct((2, T), jnp.int32),
                   jax.ShapeDtypeStruct((T, LOGIT_W), F32),
                   jax.ShapeDtypeStruct((N_EXPERTS, LANES), F32)],
        scratch_shapes=[pltpu.VMEM((N_EXPERTS, 1), F32), pltpu.VMEM((N_EXPERTS, 1), F32),
                        pltpu.VMEM((tm, tm), BF16)],
        compiler_params=_cparams(("arbitrary", "arbitrary")),
        name="route",
    )(logits_t)


DISPATCH_TM = 512


def _dispatch_kernel(T, slot_ref, h_ref, xs_in_ref, xs_ref, sem):
    del xs_in_ref
    base = pl.program_id(0) * DISPATCH_TM

    def copy(r, k):
        return pltpu.make_async_copy(h_ref.at[pl.ds(r, 1)],
                                     xs_ref.at[pl.ds(slot_ref[k * T + base + r], 1)], sem)

    def start(r, c):
        copy(r, 0).start()
        copy(r, 1).start()
        return c

    def wait(r, c):
        copy(r, 0).wait()
        copy(r, 1).wait()
        return c

    lax.fori_loop(0, DISPATCH_TM, start, 0, unroll=8)
    lax.fori_loop(0, DISPATCH_TM, wait, 0, unroll=8)


def _dispatch(slots, h, n_slots):
    T, D = h.shape
    xs0 = jnp.zeros((n_slots, D), h.dtype)
    grid_spec = pltpu.PrefetchScalarGridSpec(
        num_scalar_prefetch=1,
        grid=(T // DISPATCH_TM,),
        in_specs=[pl.BlockSpec((DISPATCH_TM, D), lambda i, s: (i, 0)),
                  pl.BlockSpec(memory_space=pl.ANY)],
        out_specs=pl.BlockSpec(memory_space=pl.ANY),
        scratch_shapes=[pltpu.SemaphoreType.DMA(())],
    )
    return pl.pallas_call(
        functools.partial(_dispatch_kernel, T),
        grid_spec=grid_spec,
        out_shape=jax.ShapeDtypeStruct((n_slots, D), h.dtype),
        input_output_aliases={2: 0},
        compiler_params=_cparams(("arbitrary",)),
        name="moe_dispatch",
    )(slots, h, xs0)


def _ffn_kernel(nblk, layer, meta_ref, x_ref, w1_hbm, w3_hbm, w2_hbm, y_ref,
                wf1, wf3, wf2, sems, wb1, wb3, wb2):
    b = pl.program_id(0)
    used = b < meta_ref[nblk]
    e = meta_ref[b]
    new_expert = (b == 0) | (e != meta_ref[jnp.maximum(b - 1, 0)])
    slot = meta_ref[nblk + 1 + e] % 2
    nxt = meta_ref[nblk + 1 + N_EXPERTS + e]

    def fetch(expert, s):
        return (pltpu.make_async_copy(w1_hbm.at[layer, expert], wf1.at[s], sems.at[s, 0]),
                pltpu.make_async_copy(w3_hbm.at[layer, expert], wf3.at[s], sems.at[s, 1]),
                pltpu.make_async_copy(w2_hbm.at[layer, expert], wf2.at[s], sems.at[s, 2]))

    @pl.when(used & (b == 0))
    def _():
        for cp in fetch(e, slot):
            cp.start()

    @pl.when(used & new_expert)
    def _():
        for cp in fetch(e, slot):
            cp.wait()

        @pl.when(nxt >= 0)
        def _():
            for cp in fetch(nxt, 1 - slot):
                cp.start()

        wb1[...] = wf1[slot].astype(BF16)
        wb3[...] = wf3[slot].astype(BF16)
        wb2[...] = wf2[slot].astype(BF16)

    @pl.when(used)
    def _():
        x = _unpack_bf16_pairs(x_ref[...])
        a = jnp.dot(x, wb1[...], preferred_element_type=F32)
        c = jnp.dot(x, wb3[...], preferred_element_type=F32)
        hid = (a * _sigmoid(a)) * c
        y_ref[...] = jnp.dot(hid.astype(BF16), wb2[...], preferred_element_type=F32)

    @pl.when(jnp.logical_not(used))
    def _():
        y_ref[...] = jnp.zeros_like(y_ref)


def _ffn(meta, xs, w1, w3, w2, layer):
    n_slots = xs.shape[0]
    D = w1.shape[2]
    nblk = n_slots // MOE_BLOCK
    grid_spec = pltpu.PrefetchScalarGridSpec(
        num_scalar_prefetch=1,
        grid=(nblk,),
        in_specs=[pl.BlockSpec((MOE_BLOCK, xs.shape[1]), lambda b, e: (b, 0)),
                  pl.BlockSpec(memory_space=pl.ANY), pl.BlockSpec(memory_space=pl.ANY),
                  pl.BlockSpec(memory_space=pl.ANY)],
        out_specs=pl.BlockSpec((MOE_BLOCK, D), lambda b, e: (b, 0)),
        scratch_shapes=[pltpu.VMEM((2, D, D_EXPERT), F32), pltpu.VMEM((2, D, D_EXPERT), F32),
                        pltpu.VMEM((2, D_EXPERT, D), F32), pltpu.SemaphoreType.DMA((2, 3)),
                        pltpu.VMEM((D, D_EXPERT), BF16), pltpu.VMEM((D, D_EXPERT), BF16),
                        pltpu.VMEM((D_EXPERT, D), BF16)],
    )
    return pl.pallas_call(
        functools.partial(_ffn_kernel, nblk, layer),
        grid_spec=grid_spec,
        out_shape=jax.ShapeDtypeStruct((n_slots, D), F32),
        compiler_params=_cparams(("arbitrary",)),
        name="moe_ffn",
    )(meta, xs, w1, w3, w2)


COMBINE_TM = 512
COMBINE_CHUNK = 128


def _combine_kernel(T, npt, slot_ref, x_ref, gate_ref, ys_ref, *refs):
    out_refs, (buf, sems) = refs[:-2], refs[-2:]
    i = pl.program_id(0)
    base = i * COMBINE_TM
    n_chunks = COMBINE_TM // COMBINE_CHUNK

    def copy(r, k, q):
        return pltpu.make_async_copy(ys_ref.at[pl.ds(slot_ref[k * T + base + r], 1)],
                                     buf.at[k, pl.ds(r, 1)], sems.at[q])

    def for_chunk_rows(q, fn):
        def body(r, c):
            fn(copy(r, 0, q))
            fn(copy(r, 1, q))
            return c
        lax.fori_loop(q * COMBINE_CHUNK, (q + 1) * COMBINE_CHUNK, body, 0, unroll=8)

    for q in range(n_chunks):
        for_chunk_rows(q, lambda cp: cp.start())

    def emit(rs, y):
        if npt == 0:
            out_refs[0][rs, :] = y
        else:
            @pl.when(i < npt)
            def _():
                out_refs[0][rs, :] = y

            @pl.when(i >= npt)
            def _():
                out_refs[1][rs, :] = y

    for q in range(n_chunks):
        for_chunk_rows(q, lambda cp: cp.wait())
        rs = slice(q * COMBINE_CHUNK, (q + 1) * COMBINE_CHUNK)
        g = gate_ref[rs, :]
        emit(rs, x_ref[rs, :] + (buf[0, rs, :] * g[:, 0:1] + buf[1, rs, :] * g[:, 1:2]))


def _combine(slots, x1, gates, ys, split_rows=0):
    T, D = x1.shape
    tm = COMBINE_TM
    npt = split_rows // tm
    if npt == 0:
        out_specs = pl.BlockSpec((tm, D), lambda i, s: (i, 0))
        out_shape = jax.ShapeDtypeStruct((T, D), F32)
    else:
        out_specs = [pl.BlockSpec((tm, D), lambda i, s: (jnp.minimum(i, npt - 1), 0)),
                     pl.BlockSpec((tm, D), lambda i, s: (jnp.maximum(i - npt, 0), 0))]
        out_shape = [jax.ShapeDtypeStruct((split_rows, D), F32),
                     jax.ShapeDtypeStruct((T - split_rows, D), F32)]
    grid_spec = pltpu.PrefetchScalarGridSpec(
        num_scalar_prefetch=1,
        grid=(T // tm,),
        in_specs=[pl.BlockSpec((tm, D), lambda i, s: (i, 0)),
                  pl.BlockSpec((tm, LOGIT_W), lambda i, s: (i, 0)),
                  pl.BlockSpec(memory_space=pl.ANY)],
        out_specs=out_specs,
        scratch_shapes=[pltpu.VMEM((2, tm, D), F32),
                        pltpu.SemaphoreType.DMA((COMBINE_TM // COMBINE_CHUNK,))],
    )
    return pl.pallas_call(
        functools.partial(_combine_kernel, T, npt),
        grid_spec=grid_spec,
        out_shape=out_shape,
        compiler_params=_cparams(("arbitrary",)),
        name="moe_combine",
    )(slots, x1, gates, ys)


def _moe(x1, h, logits, w1, w3, w2, layer, split_rows=0):
    T = x1.shape[0]
    slot_l, gate_l, cnt = _route(logits)
    slots = slot_l.reshape(2 * T)
    n_blocks = -(-(2 * T) // MOE_BLOCK) + N_EXPERTS
    counts = cnt[:, 0].astype(jnp.int32)
    pend = jnp.cumsum((counts + MOE_BLOCK - 1) // MOE_BLOCK * MOE_BLOCK)
    starts = jnp.arange(n_blocks, dtype=jnp.int32) * MOE_BLOCK
    blk_e = jnp.minimum(jnp.sum((pend[None, :] <= starts[:, None]).astype(jnp.int32), axis=1),
                        N_EXPERTS - 1)
    present = (counts > 0).astype(jnp.int32)
    rank = jnp.cumsum(present) - present
    ids = jnp.where(present > 0, jnp.arange(N_EXPERTS, dtype=jnp.int32), N_EXPERTS)
    after = jnp.concatenate([lax.cummin(ids[::-1])[::-1][1:], jnp.full((1,), N_EXPERTS, jnp.int32)])
    nxt = jnp.where(after < N_EXPERTS, after, -1)
    meta = jnp.concatenate([blk_e, pend[-1:] // MOE_BLOCK, rank, nxt]).astype(jnp.int32)
    xs = _dispatch(slots, h, n_blocks * MOE_BLOCK)
    ys = _ffn(meta, xs, w1, w3, w2, layer)
    return _combine(slots, x1, gate_l, ys, split_rows)


def _router_weights(w_grp, b_grp, w_exp, b_exp):
    D = w_grp.shape[0]
    pad = LOGIT_W - N_GROUPS - N_EXPERTS
    wr = jnp.concatenate([w_grp, w_exp, jnp.zeros((D, pad), F32)], axis=1).astype(BF16)
    br = jnp.concatenate([b_grp, b_exp, jnp.zeros((pad,), F32)]).reshape(LOGIT_W, 1)
    return wr.T, br


def kernel(x_prompt, x_sample, cache_conv, cache_k, cache_v, state_hgrn, norm_mix, norm_ffn,
           ev_w_in, ev_conv, ev_q_norm, ev_k_norm, ev_sinks, ev_w_out,
           od_w_in, od_lb, od_o_norm, od_w_out,
           moe_w_grp, moe_b_grp, moe_w_exp, moe_b_exp, moe_w1, moe_w3, moe_w2):
    B, L, D = x_prompt.shape
    Bd, Ld, _ = x_sample.shape
    Tp = B * L
    T = Tp + Bd * Ld
    x = (x_prompt.reshape(Tp, D), x_sample.reshape(Bd * Ld, D))

    proj = _norm_matmul(x, norm_mix[0], ev_w_in[0].astype(BF16), 512)
    qg2 = jnp.tile(ev_q_norm[0], 2).reshape(1, LANES)
    kg2 = jnp.tile(ev_k_norm[0], 2).reshape(1, LANES)
    mix_p, k_p, v_p, conv_p = _even_prompt(proj, B, L, T, ev_conv[0], qg2, kg2, ev_sinks[0])
    mix_s, k_s, v_s, conv_s = _even_sample(
        proj, Bd, Ld, Tp, ev_conv[0], qg2, kg2, ev_sinks[0], cache_conv[0],
        cache_k[0].reshape(Bd, WINDOW, LANES), cache_v[0].reshape(Bd, WINDOW, LANES))
    wr, br = _router_weights(moe_w_grp[0], moe_b_grp[0], moe_w_exp[0], moe_b_exp[0])
    x1, h, logits = _out_proj((mix_p, mix_s), ev_w_out[0].astype(BF16), x, norm_ffn[0], wr, br, 512)
    x = _moe(x1, h, logits, moe_w1, moe_w3, moe_w2, 0)

    lb_all = jnp.cumsum(jax.nn.softmax(od_lb.astype(F32), axis=0), axis=0)
    lb = (lb_all - lb_all[0])[1].reshape(1, C_HK)
    og = od_o_norm[0].reshape(1, C_DV)
    proj = _norm_matmul(x, norm_mix[1], od_w_in[0].astype(BF16), 512)
    o_p, s_p = _hgrn_prompt(proj, B, L, T, lb, og)
    o_s, s_s = _hgrn_sample(proj, Bd, Ld, Tp, lb, og, state_hgrn[0])
    wr, br = _router_weights(moe_w_grp[1], moe_b_grp[1], moe_w_exp[1], moe_b_exp[1])
    x1, h, logits = _out_proj((o_p, o_s), od_w_out[0].astype(BF16), x, norm_ffn[1], wr, br, 512)
    y_p, y_s = _moe(x1, h, logits, moe_w1, moe_w3, moe_w2, 1, split_rows=Tp)

    y_prompt = y_p.reshape(B, L, D)
    y_sample = y_s.reshape(Bd, Ld, D)
    return (y_prompt, y_sample,
            conv_p[None], k_p.reshape(1, B, WINDOW, N_KV, HEAD_DIM), v_p.reshape(1, B, WINDOW, N_KV, HEAD_DIM),
            s_p[None],
            conv_s[None], k_s.reshape(1, Bd, WINDOW, N_KV, HEAD_DIM), v_s.reshape(1, Bd, WINDOW, N_KV, HEAD_DIM),
            s_s[None])
```

```python
import functools

import numpy as np
import jax
import jax.numpy as jnp
from jax import lax
from jax.experimental import pallas as pl
from jax.experimental.pallas import tpu as pltpu

F32 = jnp.float32
BF16 = jnp.bfloat16

D_MODEL = 1024
PAST_LEN = 16384
D_CONV = 512
CONV_W = 3
HEAD_DIM = 64
N_Q = 8
N_KV = 2
GQA_G = N_Q // N_KV
WINDOW = 128
ROPE_THETA = 10000.0
D_IN_EVEN = 3 * D_CONV + (N_Q + 2 * N_KV) * HEAD_DIM
C_HEADS = 8
C_DK = 128
C_DV = 128
C_HK = C_HEADS * C_DK
N_GROUPS = 8
EXP_PER_GROUP = 8
N_EXPERTS = N_GROUPS * EXP_PER_GROUP
D_EXPERT = 512
MOE_BLOCK = 256
EPS = 1e-6

LANES = 128
SUBLANES = 8
VMEM_LIMIT = 56 * 1024 * 1024

GLA_CHUNK = 128
GLA_STEP = 1024
SAMPLE_GB = 4
HGRN_SAMPLE_GB = 2
ROUTE_TM = 512
LOGIT_W = 128


def _cparams(sem):
    return pltpu.CompilerParams(dimension_semantics=sem, vmem_limit_bytes=VMEM_LIMIT)


def _sigmoid(x):
    return 1.0 / (1.0 + jnp.exp(-x))


def _silu(x):
    return x * (0.5 * jnp.tanh(0.5 * x) + 0.5)


def _row_source(src, tm):
    if isinstance(src, tuple):
        a, b = src
        n = a.shape[1]
        npt = a.shape[0] // tm
        specs = [pl.BlockSpec((tm, n), lambda i, *_: (jnp.minimum(i, npt - 1), 0)),
                 pl.BlockSpec((tm, n), lambda i, *_: (jnp.maximum(i - npt, 0), 0))]
        return specs, [a, b], npt, a.shape[0] + b.shape[0]
    return [pl.BlockSpec((tm, src.shape[1]), lambda i, *_: (i, 0))], [src], 0, src.shape[0]


def _read_rows(refs, npt):
    if len(refs) == 2:
        return jnp.where(pl.program_id(0) < npt, refs[0][...], refs[1][...])
    return refs[0][...]


def _norm_matmul_kernel(n_src, npt, *refs):
    x = _read_rows(refs[:n_src], npt)
    g_ref, w_ref, o_ref = refs[n_src:]
    ms = jnp.mean(x * x, axis=-1, keepdims=True)
    h = (x * lax.rsqrt(ms + EPS) * g_ref[...]).astype(BF16)
    o_ref[...] = jnp.dot(h, w_ref[...], preferred_element_type=F32)


def _norm_matmul(x, g, w, tm):
    D, N = w.shape
    specs, arrays, npt, T = _row_source(x, tm)
    return pl.pallas_call(
        functools.partial(_norm_matmul_kernel, len(arrays), npt),
        grid=(T // tm,),
        in_specs=specs + [pl.BlockSpec((1, D), lambda i: (0, 0)),
                          pl.BlockSpec((D, N), lambda i: (0, 0), pipeline_mode=pl.Buffered(1))],
        out_specs=pl.BlockSpec((tm, N), lambda i: (i, 0)),
        out_shape=jax.ShapeDtypeStruct((T, N), F32),
        compiler_params=_cparams(("arbitrary",)),
        name="norm_matmul",
    )(*arrays, g.reshape(1, D), w)


def _rope_tables(pos):
    inv = ROPE_THETA ** (-jnp.arange(0, HEAD_DIM, 2, dtype=F32) / HEAD_DIM)
    ang = pos.astype(F32)[:, None] * inv[None, :]
    cos = jnp.cos(ang)
    sin = jnp.sin(ang)
    return (jnp.concatenate([cos, cos, cos, cos], axis=1),
            jnp.concatenate([-sin, sin, -sin, sin], axis=1))


def _headnorm_rope(x, g2, cos, sin):
    lane = lax.broadcasted_iota(jnp.int32, x.shape, 1)
    lo = lane < HEAD_DIM
    x2 = x * x
    s_lo = jnp.sum(jnp.where(lo, x2, 0.0), axis=-1, keepdims=True)
    s_hi = jnp.sum(jnp.where(lo, 0.0, x2), axis=-1, keepdims=True)
    ms = jnp.where(lo, s_lo, s_hi) * (1.0 / HEAD_DIM)
    y = x * lax.rsqrt(ms + EPS) * g2
    first_half = (lane & (HEAD_DIM // 2)) == 0
    swapped = jnp.where(first_half, pltpu.roll(y, LANES - HEAD_DIM // 2, 1),
                        pltpu.roll(y, HEAD_DIM // 2, 1))
    return y * cos + swapped * sin


def _gated_conv(gb, u, c2, c1, cw):
    R = u.shape[0]
    row = lax.broadcasted_iota(jnp.int32, u.shape, 0)
    u1 = jnp.where(row == 0, c1, pltpu.roll(u, 1, 0))
    u2 = jnp.where(row == 0, c2, jnp.where(row == 1, c1, pltpu.roll(u, 2, 0)))
    del R
    return gb * (cw[0:1, :] * u2 + cw[1:2, :] * u1 + cw[2:3, :] * u)


def _band_attention(qs, kk, vv, valid, sinkv):
    s = lax.dot_general(qs.astype(BF16), kk.astype(BF16), (((1,), (1,)), ((), ())),
                        preferred_element_type=F32) * (HEAD_DIM ** -0.5)
    s = jnp.where(valid, s, -jnp.inf)
    m = jnp.maximum(jnp.max(s, axis=-1, keepdims=True), sinkv)
    e = jnp.exp(s - m)
    den = jnp.sum(e, axis=-1, keepdims=True) + jnp.exp(sinkv - m)
    p = e / den
    return jnp.dot(p.astype(BF16), vv.astype(BF16), preferred_element_type=F32)


def _sink_column(sinks_ref, hk, rows_per_head):
    R = GQA_G * rows_per_head
    row = lax.broadcasted_iota(jnp.int32, (R, 1), 0)
    col = jnp.full((R, 1), sinks_ref[hk * GQA_G + GQA_G - 1], F32)
    for j in range(GQA_G - 2, -1, -1):
        col = jnp.where(row < (j + 1) * rows_per_head, sinks_ref[hk * GQA_G + j], col)
    return col


def _even_prompt_kernel(sinks_ref, proj_ref, cos_ref, sin_ref, cw_ref, qg_ref, kg_ref,
                        mix_ref, kl_ref, vl_ref, cl_ref, kprev, vprev, ucar):
    blk = pl.program_id(1)
    W = WINDOW

    @pl.when(blk == 0)
    def _():
        kprev[...] = jnp.zeros_like(kprev)
        vprev[...] = jnp.zeros_like(vprev)
        ucar[...] = jnp.zeros_like(ucar)

    gb = proj_ref[:, 0:D_CONV]
    u = proj_ref[:, D_CONV:2 * D_CONV] * proj_ref[:, 2 * D_CONV:3 * D_CONV]
    car = ucar[...]
    a_out = _gated_conv(gb, u, car[0:1, :], car[1:2, :], cw_ref[...])
    ucar[0:2, :] = u[W - 2:W, :]
    mix_ref[:, 0:D_CONV] = a_out.astype(BF16)

    cos = cos_ref[...]
    sin = sin_ref[...]
    q0 = 3 * D_CONV
    k0 = q0 + N_Q * HEAD_DIM
    v0 = k0 + N_KV * HEAD_DIM
    k_r = _headnorm_rope(proj_ref[:, k0:k0 + LANES], kg_ref[...], cos, sin)
    v_r = proj_ref[:, v0:v0 + LANES]
    q_r = [_headnorm_rope(proj_ref[:, q0 + LANES * j:q0 + LANES * (j + 1)], qg_ref[...], cos, sin)
           for j in range(N_Q * HEAD_DIM // LANES)]
    k_p = kprev[...]
    v_p = vprev[...]

    R = GQA_G * W
    i = lax.broadcasted_iota(jnp.int32, (R, 2 * W), 0) & (W - 1)
    j = lax.broadcasted_iota(jnp.int32, (R, 2 * W), 1)
    diff = i + W - j
    valid = (diff >= 0) & (diff <= W) & ((blk > 0) | (j >= W))

    for hk in range(N_KV):
        ls = slice(hk * HEAD_DIM, (hk + 1) * HEAD_DIM)
        kk = jnp.concatenate([k_p[:, ls], k_r[:, ls]], axis=0)
        vv = jnp.concatenate([v_p[:, ls], v_r[:, ls]], axis=0)
        heads = []
        for g in range(GQA_G):
            h = hk * GQA_G + g
            tile = q_r[h // 2]
            heads.append(tile[:, (h % 2) * HEAD_DIM:(h % 2 + 1) * HEAD_DIM])
        qs = jnp.concatenate(heads, axis=0)
        o = _band_attention(qs, kk, vv, valid, _sink_column(sinks_ref, hk, W))
        for g in range(GQA_G):
            h = hk * GQA_G + g
            mix_ref[:, D_CONV + h * HEAD_DIM:D_CONV + (h + 1) * HEAD_DIM] = \
                o[g * W:(g + 1) * W, :].astype(BF16)

    kprev[...] = k_r
    vprev[...] = v_r

    @pl.when(blk == pl.num_programs(1) - 1)
    def _():
        kl_ref[0] = k_r
        vl_ref[0] = v_r
        cl_ref[0] = u[W - 2:W, :]


def _even_prompt(proj, B, L, T, cw, qg2, kg2, sinks):
    nb = L // WINDOW
    cos, sin = _rope_tables(jnp.arange(L, dtype=jnp.int32))
    full = lambda shape: pl.BlockSpec(shape, lambda b, i, *_: tuple(0 for _ in shape))
    grid_spec = pltpu.PrefetchScalarGridSpec(
        num_scalar_prefetch=1,
        grid=(B, nb),
        in_specs=[pl.BlockSpec((WINDOW, D_IN_EVEN), lambda b, i, s: (b * nb + i, 0)),
                  pl.BlockSpec((WINDOW, LANES), lambda b, i, s: (i, 0)),
                  pl.BlockSpec((WINDOW, LANES), lambda b, i, s: (i, 0)),
                  full((CONV_W, D_CONV)), full((1, LANES)), full((1, LANES))],
        out_specs=[pl.BlockSpec((WINDOW, D_MODEL), lambda b, i, s: (b * nb + i, 0)),
                   pl.BlockSpec((1, WINDOW, LANES), lambda b, i, s: (b, 0, 0)),
                   pl.BlockSpec((1, WINDOW, LANES), lambda b, i, s: (b, 0, 0)),
                   pl.BlockSpec((1, CONV_W - 1, D_CONV), lambda b, i, s: (b, 0, 0))],
        scratch_shapes=[pltpu.VMEM((WINDOW, LANES), F32), pltpu.VMEM((WINDOW, LANES), F32),
                        pltpu.VMEM((SUBLANES, D_CONV), F32)],
    )
    return pl.pallas_call(
        _even_prompt_kernel,
        grid_spec=grid_spec,
        out_shape=[jax.ShapeDtypeStruct((B * L, D_MODEL), BF16),
                   jax.ShapeDtypeStruct((B, WINDOW, LANES), F32),
                   jax.ShapeDtypeStruct((B, WINDOW, LANES), F32),
                   jax.ShapeDtypeStruct((B, CONV_W - 1, D_CONV), F32)],
        compiler_params=_cparams(("arbitrary", "arbitrary")),
        name="even_prompt",
    )(sinks, proj, cos, sin, cw, qg2, kg2)


def _even_sample_kernel(sinks_ref, proj_ref, cos_ref, sin_ref, cw_ref, qg_ref, kg_ref,
                        cc_ref, ck_ref, cv_ref, mix_ref, ko_ref, vo_ref, co_ref):
    W = WINDOW
    Ld = cos_ref.shape[0]
    cos = cos_ref[...]
    sin = sin_ref[...]
    q0 = 3 * D_CONV
    k0 = q0 + N_Q * HEAD_DIM
    v0 = k0 + N_KV * HEAD_DIM
    R = N_Q * Ld
    i = lax.broadcasted_iota(jnp.int32, (R, 2 * W), 0) % Ld
    j = lax.broadcasted_iota(jnp.int32, (R, 2 * W), 1)
    diff = i + W - j
    valid = (diff >= 0) & (diff <= W)
    zpad = jnp.zeros((W - Ld, LANES), F32)
    lane = lax.broadcasted_iota(jnp.int32, (Ld, LANES), 1)
    hrow = lax.broadcasted_iota(jnp.int32, (R, 1), 0)
    sinkv = jnp.full((R, 1), sinks_ref[N_Q - 1], F32)
    for h in range(N_Q - 2, -1, -1):
        sinkv = jnp.where(hrow < (h + 1) * Ld, sinks_ref[h], sinkv)

    for bb in range(SAMPLE_GB):
        rs = slice(bb * Ld, (bb + 1) * Ld)
        gb = proj_ref[rs, 0:D_CONV]
        u = proj_ref[rs, D_CONV:2 * D_CONV] * proj_ref[rs, 2 * D_CONV:3 * D_CONV]
        car = cc_ref[bb]
        a_out = _gated_conv(gb, u, car[0:1, :], car[1:2, :], cw_ref[...])
        co_ref[bb] = u[Ld - 2:Ld, :]
        mix_ref[rs, 0:D_CONV] = a_out.astype(BF16)

        k_r = _headnorm_rope(proj_ref[rs, k0:k0 + LANES], kg_ref[...], cos, sin)
        v_r = proj_ref[rs, v0:v0 + LANES]
        q_r = [_headnorm_rope(proj_ref[rs, q0 + LANES * t:q0 + LANES * (t + 1)], qg_ref[...], cos, sin)
               for t in range(N_Q * HEAD_DIM // LANES)]
        k_c = ck_ref[bb]
        v_c = cv_ref[bb]
        ko_ref[bb, 0:W - Ld, :] = k_c[Ld:W, :]
        ko_ref[bb, W - Ld:W, :] = k_r
        vo_ref[bb, 0:W - Ld, :] = v_c[Ld:W, :]
        vo_ref[bb, W - Ld:W, :] = v_r

        kk = jnp.concatenate([k_c, k_r, zpad], axis=0)
        vv = jnp.concatenate([v_c, v_r, zpad], axis=0)
        heads = []
        for h in range(N_Q):
            tile = q_r[h // 2]
            want_hi = h // GQA_G == 1
            if (h % 2 == 1) != want_hi:
                tile = pltpu.roll(tile, HEAD_DIM, 1)
            heads.append(jnp.where((lane >= HEAD_DIM) == want_hi, tile, 0.0))
        o = _band_attention(jnp.concatenate(heads, axis=0), kk, vv, valid, sinkv)
        for h in range(N_Q):
            c0 = (h // GQA_G) * HEAD_DIM
            mix_ref[rs, D_CONV + h * HEAD_DIM:D_CONV + (h + 1) * HEAD_DIM] = \
                o[h * Ld:(h + 1) * Ld, c0:c0 + HEAD_DIM].astype(BF16)


def _even_sample(proj, Bd, Ld, row0, cw, qg2, kg2, sinks, cache_conv, cache_k, cache_v):
    GB = SAMPLE_GB
    rows = GB * Ld
    rb0 = row0 // rows
    cos, sin = _rope_tables(PAST_LEN + jnp.arange(Ld, dtype=jnp.int32))
    full = lambda shape: pl.BlockSpec(shape, lambda i, *_: tuple(0 for _ in shape))
    grid_spec = pltpu.PrefetchScalarGridSpec(
        num_scalar_prefetch=1,
        grid=(Bd // GB,),
        in_specs=[pl.BlockSpec((rows, D_IN_EVEN), lambda i, s: (rb0 + i, 0)),
                  full((Ld, LANES)), full((Ld, LANES)),
                  full((CONV_W, D_CONV)), full((1, LANES)), full((1, LANES)),
                  pl.BlockSpec((GB, CONV_W - 1, D_CONV), lambda i, s: (i, 0, 0)),
                  pl.BlockSpec((GB, WINDOW, LANES), lambda i, s: (i, 0, 0)),
                  pl.BlockSpec((GB, WINDOW, LANES), lambda i, s: (i, 0, 0))],
        out_specs=[pl.BlockSpec((rows, D_MODEL), lambda i, s: (i, 0)),
                   pl.BlockSpec((GB, WINDOW, LANES), lambda i, s: (i, 0, 0)),
                   pl.BlockSpec((GB, WINDOW, LANES), lambda i, s: (i, 0, 0)),
                   pl.BlockSpec((GB, CONV_W - 1, D_CONV), lambda i, s: (i, 0, 0))],
    )
    return pl.pallas_call(
        _even_sample_kernel,
        grid_spec=grid_spec,
        out_shape=[jax.ShapeDtypeStruct((Bd * Ld, D_MODEL), BF16),
                   jax.ShapeDtypeStruct((Bd, WINDOW, LANES), F32),
                   jax.ShapeDtypeStruct((Bd, WINDOW, LANES), F32),
                   jax.ShapeDtypeStruct((Bd, CONV_W - 1, D_CONV), F32)],
        compiler_params=_cparams(("arbitrary",)),
        name="even_sample",
    )(sinks, proj, cos, sin, cw, qg2, kg2, cache_conv, cache_k, cache_v)


def _cumsum_rows(x):
    C = x.shape[0]
    row = lax.broadcasted_iota(jnp.int32, x.shape, 0)
    s = 1
    while s < C:
        x = x + jnp.where(row >= s, pltpu.roll(x, s, 0), 0.0)
        s *= 2
    return x


def _group_ref(b, m, row):
    C, D = b.shape
    if 2 * m >= SUBLANES:
        n = C // (2 * m)
        b3 = b.reshape(n, 2 * m, D)
        return jnp.broadcast_to(b3[:, m - 1:m, :], (n, 2 * m, D)).reshape(C, D)
    r = row & (2 * m - 1)
    out = b
    for off in range(2 * m):
        if off == m - 1:
            continue
        shift = (off - (m - 1)) % C
        out = jnp.where(r == off, pltpu.roll(b, shift, 0), out)
    return out


def _gla_consts(C):
    t = np.arange(C)[:, None]
    s = np.arange(C)[None, :]
    masks = [s == t]
    m = C // 2
    while m >= 1:
        masks.append((s // (2 * m) == t // (2 * m)) & ((s & m) == 0) & ((t & m) != 0))
        m //= 2
    return jnp.asarray(np.stack(masks).astype(np.float32))


def _gla_chunk(qz, fz, v, lb, S, masks=None):
    C = qz.shape[0]
    q = _silu(qz)
    f = lb + (1.0 - lb) * _sigmoid(fz)
    k = 1.0 - f
    b = _cumsum_rows(jnp.log(f))
    vb = v.astype(BF16)

    inter = jnp.dot((q * jnp.exp(b)).astype(BF16), S.astype(BF16), preferred_element_type=F32)

    row = lax.broadcasted_iota(jnp.int32, (C, C_DK), 0)
    nt = (((1,), (1,)), ((), ()))
    if C == SUBLANES:
        vr = vb.astype(F32)
        intra = jnp.zeros((C, C_DV), F32)
        for t in range(C):
            p = jnp.where(row <= t, q[t:t + 1, :] * k * jnp.exp(jnp.minimum(b[t:t + 1, :] - b, 0.0)), 0.0)
            col = jnp.sum(p, axis=-1, keepdims=True).astype(BF16).astype(F32)
            intra = jnp.where(row == t, jnp.sum(col * vr, axis=0, keepdims=True), intra)
    else:
        sc = masks[0] * lax.dot_general(q.astype(BF16), k.astype(BF16), nt,
                                        preferred_element_type=F32)
        m = C // 2
        level = 1
        while m >= 1:
            rho = _group_ref(b, m, row)
            upper = (row & m) != 0
            d = b - rho
            x = (jnp.where(upper, q, k) * jnp.exp(jnp.where(upper, d, -d))).astype(BF16)
            sc = sc + masks[level] * lax.dot_general(x, x, nt, preferred_element_type=F32)
            level += 1
            m //= 2
        intra = jnp.dot(sc.astype(BF16), vb, preferred_element_type=F32)

    b_last = b[C - 1:C, :]
    eye = (lax.broadcasted_iota(jnp.int32, (C_DK, C_DK), 0)
           == lax.broadcasted_iota(jnp.int32, (C_DK, C_DK), 1))
    dcol = jnp.sum(jnp.where(eye, jnp.broadcast_to(jnp.exp(b_last), (C_DK, C_DK)), 0.0),
                   axis=-1, keepdims=True)
    kd = k * jnp.exp(b_last - b)
    S_new = dcol * S + lax.dot_general(kd.astype(BF16), vb, (((0,), (0,)), ((), ())),
                                       preferred_element_type=F32)
    return inter + intra, S_new


def _gated_out(o, og, gz):
    ms = jnp.mean(o * o, axis=-1, keepdims=True)
    return (o * lax.rsqrt(ms + EPS) * og) * _silu(gz)


def _hgrn_prompt_kernel(q_ref, f_ref, i_ref, g_ref, lb_ref, og_ref, mask_ref, o_ref, s_ref, S):
    c = pl.program_id(2)

    @pl.when(c == 0)
    def _():
        S[...] = jnp.zeros_like(S)

    lb = lb_ref[...]
    og = og_ref[...]

    Sv = S[...]
    for n in range(GLA_STEP // GLA_CHUNK):
        rs = slice(n * GLA_CHUNK, (n + 1) * GLA_CHUNK)
        o, Sv = _gla_chunk(q_ref[rs, :], f_ref[rs, :], i_ref[rs, :], lb, Sv, mask_ref)
        o_ref[rs, :] = _gated_out(o, og, g_ref[rs, :]).astype(BF16)
    S[...] = Sv

    @pl.when(c == pl.num_programs(2) - 1)
    def _():
        s_ref[0, 0] = S[...]


def _hgrn_prompt(proj, B, L, T, lb, og):
    ns = L // GLA_STEP
    H = C_HEADS
    col = lambda off: pl.BlockSpec((GLA_STEP, LANES), lambda b, h, c: (b * ns + c, off + h))
    masks = _gla_consts(GLA_CHUNK)
    return pl.pallas_call(
        _hgrn_prompt_kernel,
        grid=(B, H, ns),
        in_specs=[col(0), col(H), col(2 * H), col(3 * H),
                  pl.BlockSpec((1, LANES), lambda b, h, c: (0, h)),
                  pl.BlockSpec((1, LANES), lambda b, h, c: (0, 0)),
                  pl.BlockSpec(masks.shape, lambda b, h, c: (0, 0, 0))],
        out_specs=[pl.BlockSpec((GLA_STEP, LANES), lambda b, h, c: (b * ns + c, h)),
                   pl.BlockSpec((1, 1, C_DK, C_DV), lambda b, h, c: (b, h, 0, 0))],
        out_shape=[jax.ShapeDtypeStruct((B * L, D_MODEL), BF16),
                   jax.ShapeDtypeStruct((B, H, C_DK, C_DV), F32)],
        scratch_shapes=[pltpu.VMEM((C_DK, C_DV), F32)],
        compiler_params=_cparams(("arbitrary", "arbitrary", "arbitrary")),
        name="hgrn_prompt",
    )(proj, proj, proj, proj, lb, og, masks)


def _hgrn_sample_kernel(p_ref, lb_ref, og_ref, s0_ref, o_ref, s_ref):
    Ld = p_ref.shape[0] // HGRN_SAMPLE_GB
    og = og_ref[...]
    results = []
    for bb in range(HGRN_SAMPLE_GB):
        rs = slice(bb * Ld, (bb + 1) * Ld)
        for h in range(C_HEADS):
            cs = lambda part: slice((part * C_HEADS + h) * LANES, (part * C_HEADS + h + 1) * LANES)
            o, S_new = _gla_chunk(p_ref[rs, cs(0)], p_ref[rs, cs(1)], p_ref[rs, cs(2)],
                                  lb_ref[:, h * LANES:(h + 1) * LANES], s0_ref[bb, h])
            results.append((bb, h, rs, S_new, _gated_out(o, og, p_ref[rs, cs(3)]).astype(BF16)))
    for bb, h, rs, S_new, out in results:
        s_ref[bb, h] = S_new
        o_ref[rs, h * LANES:(h + 1) * LANES] = out


def _hgrn_sample(proj, Bd, Ld, row0, lb, og, s0):
    GB = HGRN_SAMPLE_GB
    rows = GB * Ld
    rb0 = row0 // rows
    H = C_HEADS
    return pl.pallas_call(
        _hgrn_sample_kernel,
        grid=(Bd // GB,),
        in_specs=[pl.BlockSpec((rows, 4 * C_HK), lambda i: (rb0 + i, 0)),
                  pl.BlockSpec((1, C_HK), lambda i: (0, 0)),
                  pl.BlockSpec((1, LANES), lambda i: (0, 0)),
                  pl.BlockSpec((GB, H, C_DK, C_DV), lambda i: (i, 0, 0, 0))],
        out_specs=[pl.BlockSpec((rows, D_MODEL), lambda i: (i, 0)),
                   pl.BlockSpec((GB, H, C_DK, C_DV), lambda i: (i, 0, 0, 0))],
        out_shape=[jax.ShapeDtypeStruct((Bd * Ld, D_MODEL), BF16),
                   jax.ShapeDtypeStruct((Bd, H, C_DK, C_DV), F32)],
        compiler_params=_cparams(("arbitrary",)),
        name="hgrn_sample",
    )(proj, lb, og, s0)


def _pack_bf16_pairs(xb):
    n = xb.shape[1] // 2
    lo = lax.bitcast_convert_type(xb[:, :n].astype(F32), jnp.uint32)
    hi = lax.bitcast_convert_type(xb[:, n:].astype(F32), jnp.uint32)
    return (lo >> 16) | (hi & jnp.uint32(0xFFFF0000))


def _unpack_bf16_pairs(w):
    lo = lax.bitcast_convert_type(w << 16, F32)
    hi = lax.bitcast_convert_type(w & jnp.uint32(0xFFFF0000), F32)
    return jnp.concatenate([lo, hi], axis=1).astype(BF16)


def _out_proj_kernel(n_mix, n_x, npt, *refs):
    m = _read_rows(refs[:n_mix], npt)
    x = _read_rows(refs[n_mix:n_mix + n_x], npt)
    w_ref, g_ref, wr_ref, br_ref, x1_ref, h_ref, lg_ref = refs[n_mix + n_x:]
    x1 = x + jnp.dot(m, w_ref[...], preferred_element_type=F32)
    x1_ref[...] = x1
    ms = jnp.mean(x1 * x1, axis=-1, keepdims=True)
    hb = (x1 * lax.rsqrt(ms + EPS) * g_ref[...]).astype(BF16)
    h_ref[...] = _pack_bf16_pairs(hb)
    lg_ref[...] = lax.dot_general(wr_ref[...], hb, (((1,), (1,)), ((), ())),
                                  preferred_element_type=F32) + br_ref[...]


def _out_proj(mix, w, x, g, wr, br, tm):
    K, D = w.shape
    m_specs, m_arrays, npt, T = _row_source(mix, tm)
    x_specs, x_arrays, npt_x, _ = _row_source(x, tm)
    assert npt_x in (0, npt)
    row = lambda n: pl.BlockSpec((tm, n), lambda i: (i, 0))
    full = lambda a, b: pl.BlockSpec((a, b), lambda i: (0, 0))
    return pl.pallas_call(
        functools.partial(_out_proj_kernel, len(m_arrays), len(x_arrays), npt),
        grid=(T // tm,),
        in_specs=m_specs + x_specs + [full(K, D), full(1, D), full(LOGIT_W, D), full(LOGIT_W, 1)],
        out_specs=[row(D), row(D // 2), pl.BlockSpec((LOGIT_W, tm), lambda i: (0, i))],
        out_shape=[jax.ShapeDtypeStruct((T, D), F32), jax.ShapeDtypeStruct((T, D // 2), jnp.uint32),
                   jax.ShapeDtypeStruct((LOGIT_W, T), F32)],
        compiler_params=_cparams(("arbitrary",)),
        name="out_proj",
    )(*m_arrays, *x_arrays, w, g.reshape(1, D), wr, br)


def _route_kernel(lg_ref, slot_ref, gate_ref, cnt_ref, carry, total, tri):
    phase = pl.program_id(0)
    step = pl.program_id(1)
    tm = lg_ref.shape[1]
    G, E = N_GROUPS, EXP_PER_GROUP
    row8 = lax.broadcasted_iota(jnp.int32, (G, tm), 0)

    glog = lg_ref[0:G, :]
    gmax = jnp.max(glog, axis=0, keepdims=True)
    g_idx = jnp.min(jnp.where(glog == gmax, row8, G), axis=0, keepdims=True)
    g_p = 1.0 / jnp.sum(jnp.exp(glog - gmax), axis=0, keepdims=True)

    elog = jnp.zeros((E, tm), F32)
    for g in range(G):
        elog = jnp.where(g_idx == g, lg_ref[G + g * E:G + (g + 1) * E, :], elog)
    emax = jnp.max(elog, axis=0, keepdims=True)
    ee = jnp.exp(elog - emax)
    prob = ee / jnp.sum(ee, axis=0, keepdims=True)
    p1 = jnp.max(prob, axis=0, keepdims=True)
    i1 = jnp.min(jnp.where(prob == p1, row8, E), axis=0, keepdims=True)
    prob2 = jnp.where(row8 == i1, -1.0, prob)
    p2 = jnp.max(prob2, axis=0, keepdims=True)
    i2 = jnp.min(jnp.where(prob2 == p2, row8, E), axis=0, keepdims=True)
    psum = p1 + p2
    w1 = p1 / psum * g_p
    w2 = p2 / psum * g_p

    erow = lax.broadcasted_iota(jnp.int32, (N_EXPERTS, tm), 0)
    oh1 = erow == g_idx * E + i1
    oh2 = erow == g_idx * E + i2
    both = jnp.where(oh1 | oh2, 1.0, 0.0)
    tile_cnt = jnp.sum(both, axis=1, keepdims=True)

    @pl.when((phase == 0) & (step == 0))
    def _():
        total[...] = jnp.zeros_like(total)
        r = lax.broadcasted_iota(jnp.int32, (tm, tm), 0)
        c = lax.broadcasted_iota(jnp.int32, (tm, tm), 1)
        tri[...] = jnp.where(r < c, 1.0, 0.0).astype(BF16)

    @pl.when(phase == 0)
    def _():
        total[...] += tile_cnt

    @pl.when((phase == 1) & (step == 0))
    def _():
        carry[...] = jnp.zeros_like(carry)

    @pl.when(phase == 1)
    def _():
        cnt = total[...]
        padded = jnp.floor((cnt + (MOE_BLOCK - 1)) * (1.0 / MOE_BLOCK)) * MOE_BLOCK
        pb = jnp.broadcast_to(padded, (N_EXPERTS, LANES))
        pstart = (_cumsum_rows(pb) - pb)[:, 0:1]
        before = (jnp.dot(both.astype(BF16), tri[...], preferred_element_type=F32)
                  + (carry[...] + pstart))
        s1 = jnp.sum(jnp.where(oh1, before, 0.0), axis=0, keepdims=True)
        s2 = jnp.sum(jnp.where(oh2, before, 0.0), axis=0, keepdims=True)
        carry[...] += tile_cnt
        slot_ref[...] = jnp.concatenate([s1, s2], axis=0).astype(jnp.int32)
        grow = lax.broadcasted_iota(jnp.int32, (LOGIT_W, tm), 0)
        gate_ref[...] = jnp.where(grow == 0, w1, jnp.where(grow == 1, w2, 0.0)).T
        cnt_ref[...] = jnp.broadcast_to(cnt, (N_EXPERTS, LANES))


def _route(logits_t):
    T = logits_t.shape[1]
    tm = ROUTE_TM
    return pl.pallas_call(
        _route_kernel,
        grid=(2, T // tm),
        in_specs=[pl.BlockSpec((LOGIT_W, tm), lambda p, i: (0, i))],
        out_specs=[pl.BlockSpec((2, tm), lambda p, i: (0, i * p)),
                   pl.BlockSpec((tm, LOGIT_W), lambda p, i: (i * p, 0)),
                   pl.BlockSpec((N_EXPERTS, LANES), lambda p, i: (0, 0))],
        out_shape=[jax.ShapeDtypeStruct((2, T), jnp.int32),
                   jax.ShapeDtypeStruct((T, LOGIT_W), F32),
                   jax.ShapeDtypeStruct((N_EXPERTS, LANES), F32)],
        scratch_shapes=[pltpu.VMEM((N_EXPERTS, 1), F32), pltpu.VMEM((N_EXPERTS, 1), F32),
                        pltpu.VMEM((tm, tm), BF16)],
        compiler_params=_cparams(("arbitrary", "arbitrary")),
        name="route",
    )(logits_t)


DISPATCH_TM = 512


def _dispatch_kernel(T, slot_ref, h_ref, xs_in_ref, xs_ref, sem):
    del xs_in_ref
    base = pl.program_id(0) * DISPATCH_TM

    def copy(r, k):
        return pltpu.make_async_copy(h_ref.at[pl.ds(r, 1)],
                                     xs_ref.at[pl.ds(slot_ref[k * T + base + r], 1)], sem)

    def start(r, c):
        copy(r, 0).start()
        copy(r, 1).start()
        return c

    def wait(r, c):
        copy(r, 0).wait()
        copy(r, 1).wait()
        return c

    lax.fori_loop(0, DISPATCH_TM, start, 0, unroll=8)
    lax.fori_loop(0, DISPATCH_TM, wait, 0, unroll=8)


def _dispatch(slots, h, n_slots):
    T, D = h.shape
    xs0 = jnp.zeros((n_slots, D), h.dtype)
    grid_spec = pltpu.PrefetchScalarGridSpec(
        num_scalar_prefetch=1,
        grid=(T // DISPATCH_TM,),
        in_specs=[pl.BlockSpec((DISPATCH_TM, D), lambda i, s: (i, 0)),
                  pl.BlockSpec(memory_space=pl.ANY)],
        out_specs=pl.BlockSpec(memory_space=pl.ANY),
        scratch_shapes=[pltpu.SemaphoreType.DMA(())],
    )
    return pl.pallas_call(
        functools.partial(_dispatch_kernel, T),
        grid_spec=grid_spec,
        out_shape=jax.ShapeDtypeStruct((n_slots, D), h.dtype),
        input_output_aliases={2: 0},
        compiler_params=_cparams(("arbitrary",)),
        name="moe_dispatch",
    )(slots, h, xs0)


def _ffn_kernel(nblk, layer, meta_ref, x_ref, w1_hbm, w3_hbm, w2_hbm, y_ref,
                wf1, wf3, wf2, sems, wb1, wb3, wb2):
    b = pl.program_id(0)
    used = b < meta_ref[nblk]
    e = meta_ref[b]
    new_expert = (b == 0) | (e != meta_ref[jnp.maximum(b - 1, 0)])
    slot = meta_ref[nblk + 1 + e] % 2
    nxt = meta_ref[nblk + 1 + N_EXPERTS + e]

    def fetch(expert, s):
        return (pltpu.make_async_copy(w1_hbm.at[layer, expert], wf1.at[s], sems.at[s, 0]),
                pltpu.make_async_copy(w3_hbm.at[layer, expert], wf3.at[s], sems.at[s, 1]),
                pltpu.make_async_copy(w2_hbm.at[layer, expert], wf2.at[s], sems.at[s, 2]))

    @pl.when(used & (b == 0))
    def _():
        for cp in fetch(e, slot):
            cp.start()

    @pl.when(used & new_expert)
    def _():
        for cp in fetch(e, slot):
            cp.wait()

        @pl.when(nxt >= 0)
        def _():
            for cp in fetch(nxt, 1 - slot):
                cp.start()

        wb1[...] = wf1[slot].astype(BF16)
        wb3[...] = wf3[slot].astype(BF16)
        wb2[...] = wf2[slot].astype(BF16)

    @pl.when(used)
    def _():
        x = _unpack_bf16_pairs(x_ref[...])
        a = jnp.dot(x, wb1[...], preferred_element_type=F32)
        c = jnp.dot(x, wb3[...], preferred_element_type=F32)
        hid = (a * _sigmoid(a)) * c
        y_ref[...] = jnp.dot(hid.astype(BF16), wb2[...], preferred_element_type=F32)

    @pl.when(jnp.logical_not(used))
    def _():
        y_ref[...] = jnp.zeros_like(y_ref)


def _ffn(meta, xs, w1, w3, w2, layer):
    n_slots = xs.shape[0]
    D = w1.shape[2]
    nblk = n_slots // MOE_BLOCK
    grid_spec = pltpu.PrefetchScalarGridSpec(
        num_scalar_prefetch=1,
        grid=(nblk,),
        in_specs=[pl.BlockSpec((MOE_BLOCK, xs.shape[1]), lambda b, e: (b, 0)),
                  pl.BlockSpec(memory_space=pl.ANY), pl.BlockSpec(memory_space=pl.ANY),
                  pl.BlockSpec(memory_space=pl.ANY)],
        out_specs=pl.BlockSpec((MOE_BLOCK, D), lambda b, e: (b, 0)),
        scratch_shapes=[pltpu.VMEM((2, D, D_EXPERT), F32), pltpu.VMEM((2, D, D_EXPERT), F32),
                        pltpu.VMEM((2, D_EXPERT, D), F32), pltpu.SemaphoreType.DMA((2, 3)),
                        pltpu.VMEM((D, D_EXPERT), BF16), pltpu.VMEM((D, D_EXPERT), BF16),
                        pltpu.VMEM((D_EXPERT, D), BF16)],
    )
    return pl.pallas_call(
        functools.partial(_ffn_kernel, nblk, layer),
        grid_spec=grid_spec,
        out_shape=jax.ShapeDtypeStruct((n_slots, D), F32),
        compiler_params=_cparams(("arbitrary",)),
        name="moe_ffn",
    )(meta, xs, w1, w3, w2)


COMBINE_TM = 512
COMBINE_CHUNK = 128


def _combine_kernel(T, npt, slot_ref, x_ref, gate_ref, ys_ref, *refs):
    out_refs, (buf, sems) = refs[:-2], refs[-2:]
    i = pl.program_id(0)
    base = i * COMBINE_TM
    n_chunks = COMBINE_TM // COMBINE_CHUNK

    def copy(r, k, q):
        return pltpu.make_async_copy(ys_ref.at[pl.ds(slot_ref[k * T + base + r], 1)],
                                     buf.at[k, pl.ds(r, 1)], sems.at[q])

    def for_chunk_rows(q, fn):
        def body(r, c):
            fn(copy(r, 0, q))
            fn(copy(r, 1, q))
            return c
        lax.fori_loop(q * COMBINE_CHUNK, (q + 1) * COMBINE_CHUNK, body, 0, unroll=8)

    for q in range(n_chunks):
        for_chunk_rows(q, lambda cp: cp.start())

    def emit(rs, y):
        if npt == 0:
            out_refs[0][rs, :] = y
        else:
            @pl.when(i < npt)
            def _():
                out_refs[0][rs, :] = y

            @pl.when(i >= npt)
            def _():
                out_refs[1][rs, :] = y

    for q in range(n_chunks):
        for_chunk_rows(q, lambda cp: cp.wait())
        rs = slice(q * COMBINE_CHUNK, (q + 1) * COMBINE_CHUNK)
        g = gate_ref[rs, :]
        emit(rs, x_ref[rs, :] + (buf[0, rs, :] * g[:, 0:1] + buf[1, rs, :] * g[:, 1:2]))


def _combine(slots, x1, gates, ys, split_rows=0):
    T, D = x1.shape
    tm = COMBINE_TM
    npt = split_rows // tm
    if npt == 0:
        out_specs = pl.BlockSpec((tm, D), lambda i, s: (i, 0))
        out_shape = jax.ShapeDtypeStruct((T, D), F32)
    else:
        out_specs = [pl.BlockSpec((tm, D), lambda i, s: (jnp.minimum(i, npt - 1), 0)),
                     pl.BlockSpec((tm, D), lambda i, s: (jnp.maximum(i - npt, 0), 0))]
        out_shape = [jax.ShapeDtypeStruct((split_rows, D), F32),
                     jax.ShapeDtypeStruct((T - split_rows, D), F32)]
    grid_spec = pltpu.PrefetchScalarGridSpec(
        num_scalar_prefetch=1,
        grid=(T // tm,),
        in_specs=[pl.BlockSpec((tm, D), lambda i, s: (i, 0)),
                  pl.BlockSpec((tm, LOGIT_W), lambda i, s: (i, 0)),
                  pl.BlockSpec(memory_space=pl.ANY)],
        out_specs=out_specs,
        scratch_shapes=[pltpu.VMEM((2, tm, D), F32),
                        pltpu.SemaphoreType.DMA((COMBINE_TM // COMBINE_CHUNK,))],
    )
    return pl.pallas_call(
        functools.partial(_combine_kernel, T, npt),
        grid_spec=grid_spec,
        out_shape=out_shape,
        compiler_params=_cparams(("arbitrary",)),
        name="moe_combine",
    )(slots, x1, gates, ys)


def _moe(x1, h, logits, w1, w3, w2, layer, split_rows=0):
    T = x1.shape[0]
    slot_l, gate_l, cnt = _route(logits)
    slots = slot_l.reshape(2 * T)
    n_blocks = -(-(2 * T) // MOE_BLOCK) + N_EXPERTS
    counts = cnt[:, 0].astype(jnp.int32)
    pend = jnp.cumsum((counts + MOE_BLOCK - 1) // MOE_BLOCK * MOE_BLOCK)
    starts = jnp.arange(n_blocks, dtype=jnp.int32) * MOE_BLOCK
    blk_e = jnp.minimum(jnp.sum((pend[None, :] <= starts[:, None]).astype(jnp.int32), axis=1),
                        N_EXPERTS - 1)
    present = (counts > 0).astype(jnp.int32)
    rank = jnp.cumsum(present) - present
    ids = jnp.where(present > 0, jnp.arange(N_EXPERTS, dtype=jnp.int32), N_EXPERTS)
    after = jnp.concatenate([lax.cummin(ids[::-1])[::-1][1:], jnp.full((1,), N_EXPERTS, jnp.int32)])
    nxt = jnp.where(after < N_EXPERTS, after, -1)
    meta = jnp.concatenate([blk_e, pend[-1:] // MOE_BLOCK, rank, nxt]).astype(jnp.int32)
    xs = _dispatch(slots, h, n_blocks * MOE_BLOCK)
    ys = _ffn(meta, xs, w1, w3, w2, layer)
    return _combine(slots, x1, gate_l, ys, split_rows)


def _router_weights(w_grp, b_grp, w_exp, b_exp):
    D = w_grp.shape[0]
    pad = LOGIT_W - N_GROUPS - N_EXPERTS
    wr = jnp.concatenate([w_grp, w_exp, jnp.zeros((D, pad), F32)], axis=1).astype(BF16)
    br = jnp.concatenate([b_grp, b_exp, jnp.zeros((pad,), F32)]).reshape(LOGIT_W, 1)
    return wr.T, br


def kernel(x_prompt, x_sample, cache_conv, cache_k, cache_v, state_hgrn, norm_mix, norm_ffn,
           ev_w_in, ev_conv, ev_q_norm, ev_k_norm, ev_sinks, ev_w_out,
           od_w_in, od_lb, od_o_norm, od_w_out,
           moe_w_grp, moe_b_grp, moe_w_exp, moe_b_exp, moe_w1, moe_w3, moe_w2):
    B, L, D = x_prompt.shape
    Bd, Ld, _ = x_sample.shape
    Tp = B * L
    T = Tp + Bd * Ld
    x = (x_prompt.reshape(Tp, D), x_sample.reshape(Bd * Ld, D))

    proj = _norm_matmul(x, norm_mix[0], ev_w_in[0].astype(BF16), 512)
    qg2 = jnp.tile(ev_q_norm[0], 2).reshape(1, LANES)
    kg2 = jnp.tile(ev_k_norm[0], 2).reshape(1, LANES)
    mix_p, k_p, v_p, conv_p = _even_prompt(proj, B, L, T, ev_conv[0], qg2, kg2, ev_sinks[0])
    mix_s, k_s, v_s, conv_s = _even_sample(
        proj, Bd, Ld, Tp, ev_conv[0], qg2, kg2, ev_sinks[0], cache_conv[0],
        cache_k[0].reshape(Bd, WINDOW, LANES), cache_v[0].reshape(Bd, WINDOW, LANES))
    wr, br = _router_weights(moe_w_grp[0], moe_b_grp[0], moe_w_exp[0], moe_b_exp[0])
    x1, h, logits = _out_proj((mix_p, mix_s), ev_w_out[0].astype(BF16), x, norm_ffn[0], wr, br, 512)
    x = _moe(x1, h, logits, moe_w1, moe_w3, moe_w2, 0)

    lb_all = jnp.cumsum(jax.nn.softmax(od_lb.astype(F32), axis=0), axis=0)
    lb = (lb_all - lb_all[0])[1].reshape(1, C_HK)
    og = od_o_norm[0].reshape(1, C_DV)
    proj = _norm_matmul(x, norm_mix[1], od_w_in[0].astype(BF16), 512)
    o_p, s_p = _hgrn_prompt(proj, B, L, T, lb, og)
    o_s, s_s = _hgrn_sample(proj, Bd, Ld, Tp, lb, og, state_hgrn[0])
    wr, br = _router_weights(moe_w_grp[1], moe_b_grp[1], moe_w_exp[1], moe_b_exp[1])
    x1, h, logits = _out_proj((o_p, o_s), od_w_out[0].astype(BF16), x, norm_ffn[1], wr, br, 512)
    y_p, y_s = _moe(x1, h, logits, moe_w1, moe_w3, moe_w2, 1, split_rows=Tp)

    y_prompt = y_p.reshape(B, L, D)
    y_sample = y_s.reshape(Bd, Ld, D)
    return (y_prompt, y_sample,
            conv_p[None], k_p.reshape(1, B, WINDOW, N_KV, HEAD_DIM), v_p.reshape(1, B, WINDOW, N_KV, HEAD_DIM),
            s_p[None],
            conv_s[None], k_s.reshape(1, Bd, WINDOW, N_KV, HEAD_DIM), v_s.reshape(1, Bd, WINDOW, N_KV, HEAD_DIM),
            s_s[None])
```

```python
import functools

import numpy as np
import jax
import jax.numpy as jnp
from jax import lax
from jax.experimental import pallas as pl
from jax.experimental.pallas import tpu as pltpu

F32 = jnp.float32
BF16 = jnp.bfloat16

D_MODEL = 1024
PAST_LEN = 16384
D_CONV = 512
CONV_W = 3
HEAD_DIM = 64
N_Q = 8
N_KV = 2
GQA_G = N_Q // N_KV
WINDOW = 128
ROPE_THETA = 10000.0
D_IN_EVEN = 3 * D_CONV + (N_Q + 2 * N_KV) * HEAD_DIM
C_HEADS = 8
C_DK = 128
C_DV = 128
C_HK = C_HEADS * C_DK
N_GROUPS = 8
EXP_PER_GROUP = 8
N_EXPERTS = N_GROUPS * EXP_PER_GROUP
D_EXPERT = 512
MOE_BLOCK = 256
EPS = 1e-6

LANES = 128
SUBLANES = 8
VMEM_LIMIT = 56 * 1024 * 1024

GLA_CHUNK = 128
GLA_STEP = 1024
SAMPLE_GB = 4
HGRN_SAMPLE_GB = 2
ROUTE_TM = 512
LOGIT_W = 128


def _cparams(sem):
    return pltpu.CompilerParams(dimension_semantics=sem, vmem_limit_bytes=VMEM_LIMIT)


def _sigmoid(x):
    return 1.0 / (1.0 + jnp.exp(-x))


def _silu(x):
    return x * (0.5 * jnp.tanh(0.5 * x) + 0.5)


def _row_source(src, tm):
    if isinstance(src, tuple):
        a, b = src
        n = a.shape[1]
        npt = a.shape[0] // tm
        specs = [pl.BlockSpec((tm, n), lambda i, *_: (jnp.minimum(i, npt - 1), 0)),
                 pl.BlockSpec((tm, n), lambda i, *_: (jnp.maximum(i - npt, 0), 0))]
        return specs, [a, b], npt, a.shape[0] + b.shape[0]
    return [pl.BlockSpec((tm, src.shape[1]), lambda i, *_: (i, 0))], [src], 0, src.shape[0]


def _read_rows(refs, npt):
    if len(refs) == 2:
        return jnp.where(pl.program_id(0) < npt, refs[0][...], refs[1][...])
    return refs[0][...]


def _norm_matmul_kernel(n_src, npt, *refs):
    x = _read_rows(refs[:n_src], npt)
    g_ref, w_ref, o_ref = refs[n_src:]
    ms = jnp.mean(x * x, axis=-1, keepdims=True)
    h = (x * lax.rsqrt(ms + EPS) * g_ref[...]).astype(BF16)
    o_ref[...] = jnp.dot(h, w_ref[...], preferred_element_type=F32)


def _norm_matmul(x, g, w, tm):
    D, N = w.shape
    specs, arrays, npt, T = _row_source(x, tm)
    return pl.pallas_call(
        functools.partial(_norm_matmul_kernel, len(arrays), npt),
        grid=(T // tm,),
        in_specs=specs + [pl.BlockSpec((1, D), lambda i: (0, 0)),
                          pl.BlockSpec((D, N), lambda i: (0, 0), pipeline_mode=pl.Buffered(1))],
        out_specs=pl.BlockSpec((tm, N), lambda i: (i, 0)),
        out_shape=jax.ShapeDtypeStruct((T, N), F32),
        compiler_params=_cparams(("arbitrary",)),
        name="norm_matmul",
    )(*arrays, g.reshape(1, D), w)


def _rope_tables(pos):
    inv = ROPE_THETA ** (-jnp.arange(0, HEAD_DIM, 2, dtype=F32) / HEAD_DIM)
    ang = pos.astype(F32)[:, None] * inv[None, :]
    cos = jnp.cos(ang)
    sin = jnp.sin(ang)
    return (jnp.concatenate([cos, cos, cos, cos], axis=1),
            jnp.concatenate([-sin, sin, -sin, sin], axis=1))


def _headnorm_rope(x, g2, cos, sin):
    lane = lax.broadcasted_iota(jnp.int32, x.shape, 1)
    lo = lane < HEAD_DIM
    x2 = x * x
    s_lo = jnp.sum(jnp.where(lo, x2, 0.0), axis=-1, keepdims=True)
    s_hi = jnp.sum(jnp.where(lo, 0.0, x2), axis=-1, keepdims=True)
    ms = jnp.where(lo, s_lo, s_hi) * (1.0 / HEAD_DIM)
    y = x * lax.rsqrt(ms + EPS) * g2
    first_half = (lane & (HEAD_DIM // 2)) == 0
    swapped = jnp.where(first_half, pltpu.roll(y, LANES - HEAD_DIM // 2, 1),
                        pltpu.roll(y, HEAD_DIM // 2, 1))
    return y * cos + swapped * sin


def _gated_conv(gb, u, c2, c1, cw):
    row = lax.broadcasted_iota(jnp.int32, u.shape, 0)
    u1 = jnp.where(row == 0, c1, pltpu.roll(u, 1, 0))
    u2 = jnp.where(row == 0, c2, jnp.where(row == 1, c1, pltpu.roll(u, 2, 0)))
    return gb * (cw[0:1, :] * u2 + cw[1:2, :] * u1 + cw[2:3, :] * u)


def _band_attention(qs, kk, vv, valid, sinkv):
    s = lax.dot_general(qs.astype(BF16), kk.astype(BF16), (((1,), (1,)), ((), ())),
                        preferred_element_type=F32) * (HEAD_DIM ** -0.5)
    s = jnp.where(valid, s, -jnp.inf)
    m = jnp.maximum(jnp.max(s, axis=-1, keepdims=True), sinkv)
    e = jnp.exp(s - m)
    den = jnp.sum(e, axis=-1, keepdims=True) + jnp.exp(sinkv - m)
    p = e / den
    return jnp.dot(p.astype(BF16), vv.astype(BF16), preferred_element_type=F32)


def _sink_column(sinks_ref, hk, rows_per_head):
    R = GQA_G * rows_per_head
    row = lax.broadcasted_iota(jnp.int32, (R, 1), 0)
    col = jnp.full((R, 1), sinks_ref[hk * GQA_G + GQA_G - 1], F32)
    for j in range(GQA_G - 2, -1, -1):
        col = jnp.where(row < (j + 1) * rows_per_head, sinks_ref[hk * GQA_G + j], col)
    return col


def _even_prompt_kernel(sinks_ref, proj_ref, cos_ref, sin_ref, cw_ref, qg_ref, kg_ref,
                        mix_ref, kl_ref, vl_ref, cl_ref, kprev, vprev, ucar):
    blk = pl.program_id(1)
    W = WINDOW

    @pl.when(blk == 0)
    def _():
        kprev[...] = jnp.zeros_like(kprev)
        vprev[...] = jnp.zeros_like(vprev)
        ucar[...] = jnp.zeros_like(ucar)

    gb = proj_ref[:, 0:D_CONV]
    u = proj_ref[:, D_CONV:2 * D_CONV] * proj_ref[:, 2 * D_CONV:3 * D_CONV]
    car = ucar[...]
    a_out = _gated_conv(gb, u, car[0:1, :], car[1:2, :], cw_ref[...])
    ucar[0:2, :] = u[W - 2:W, :]
    mix_ref[:, 0:D_CONV] = a_out.astype(BF16)

    cos = cos_ref[...]
    sin = sin_ref[...]
    q0 = 3 * D_CONV
    k0 = q0 + N_Q * HEAD_DIM
    v0 = k0 + N_KV * HEAD_DIM
    k_r = _headnorm_rope(proj_ref[:, k0:k0 + LANES], kg_ref[...], cos, sin)
    v_r = proj_ref[:, v0:v0 + LANES]
    q_r = [_headnorm_rope(proj_ref[:, q0 + LANES * j:q0 + LANES * (j + 1)], qg_ref[...], cos, sin)
           for j in range(N_Q * HEAD_DIM // LANES)]
    k_p = kprev[...]
    v_p = vprev[...]

    R = GQA_G * W
    i = lax.broadcasted_iota(jnp.int32, (R, 2 * W), 0) & (W - 1)
    j = lax.broadcasted_iota(jnp.int32, (R, 2 * W), 1)
    diff = i + W - j
    valid = (diff >= 0) & (diff <= W) & ((blk > 0) | (j >= W))

    for hk in range(N_KV):
        ls = slice(hk * HEAD_DIM, (hk + 1) * HEAD_DIM)
        kk = jnp.concatenate([k_p[:, ls], k_r[:, ls]], axis=0)
        vv = jnp.concatenate([v_p[:, ls], v_r[:, ls]], axis=0)
        heads = []
        for g in range(GQA_G):
            h = hk * GQA_G + g
            tile = q_r[h // 2]
            heads.append(tile[:, (h % 2) * HEAD_DIM:(h % 2 + 1) * HEAD_DIM])
        qs = jnp.concatenate(heads, axis=0)
        o = _band_attention(qs, kk, vv, valid, _sink_column(sinks_ref, hk, W))
        for g in range(GQA_G):
            h = hk * GQA_G + g
            mix_ref[:, D_CONV + h * HEAD_DIM:D_CONV + (h + 1) * HEAD_DIM] = \
                o[g * W:(g + 1) * W, :].astype(BF16)

    kprev[...] = k_r
    vprev[...] = v_r

    @pl.when(blk == pl.num_programs(1) - 1)
    def _():
        kl_ref[0] = k_r
        vl_ref[0] = v_r
        cl_ref[0] = u[W - 2:W, :]


def _even_prompt(proj, B, L, cw, qg2, kg2, sinks):
    nb = L // WINDOW
    cos, sin = _rope_tables(jnp.arange(L, dtype=jnp.int32))
    full = lambda shape: pl.BlockSpec(shape, lambda b, i, *_: tuple(0 for _ in shape))
    grid_spec = pltpu.PrefetchScalarGridSpec(
        num_scalar_prefetch=1,
        grid=(B, nb),
        in_specs=[pl.BlockSpec((WINDOW, D_IN_EVEN), lambda b, i, s: (b * nb + i, 0)),
                  pl.BlockSpec((WINDOW, LANES), lambda b, i, s: (i, 0)),
                  pl.BlockSpec((WINDOW, LANES), lambda b, i, s: (i, 0)),
                  full((CONV_W, D_CONV)), full((1, LANES)), full((1, LANES))],
        out_specs=[pl.BlockSpec((WINDOW, D_MODEL), lambda b, i, s: (b * nb + i, 0)),
                   pl.BlockSpec((1, WINDOW, LANES), lambda b, i, s: (b, 0, 0)),
                   pl.BlockSpec((1, WINDOW, LANES), lambda b, i, s: (b, 0, 0)),
                   pl.BlockSpec((1, CONV_W - 1, D_CONV), lambda b, i, s: (b, 0, 0))],
        scratch_shapes=[pltpu.VMEM((WINDOW, LANES), F32), pltpu.VMEM((WINDOW, LANES), F32),
                        pltpu.VMEM((SUBLANES, D_CONV), F32)],
    )
    return pl.pallas_call(
        _even_prompt_kernel,
        grid_spec=grid_spec,
        out_shape=[jax.ShapeDtypeStruct((B * L, D_MODEL), BF16),
                   jax.ShapeDtypeStruct((B, WINDOW, LANES), F32),
                   jax.ShapeDtypeStruct((B, WINDOW, LANES), F32),
                   jax.ShapeDtypeStruct((B, CONV_W - 1, D_CONV), F32)],
        compiler_params=_cparams(("arbitrary", "arbitrary")),
        name="even_prompt",
    )(sinks, proj, cos, sin, cw, qg2, kg2)


def _even_sample_kernel(sinks_ref, proj_ref, cos_ref, sin_ref, cw_ref, qg_ref, kg_ref,
                        cc_ref, ck_ref, cv_ref, mix_ref, ko_ref, vo_ref, co_ref):
    W = WINDOW
    Ld = cos_ref.shape[0]
    cos = cos_ref[...]
    sin = sin_ref[...]
    q0 = 3 * D_CONV
    k0 = q0 + N_Q * HEAD_DIM
    v0 = k0 + N_KV * HEAD_DIM
    R = N_Q * Ld
    i = lax.broadcasted_iota(jnp.int32, (R, 2 * W), 0) % Ld
    j = lax.broadcasted_iota(jnp.int32, (R, 2 * W), 1)
    diff = i + W - j
    valid = (diff >= 0) & (diff <= W)
    zpad = jnp.zeros((W - Ld, LANES), F32)
    lane = lax.broadcasted_iota(jnp.int32, (Ld, LANES), 1)
    hrow = lax.broadcasted_iota(jnp.int32, (R, 1), 0)
    sinkv = jnp.full((R, 1), sinks_ref[N_Q - 1], F32)
    for h in range(N_Q - 2, -1, -1):
        sinkv = jnp.where(hrow < (h + 1) * Ld, sinks_ref[h], sinkv)

    for bb in range(SAMPLE_GB):
        rs = slice(bb * Ld, (bb + 1) * Ld)
        gb = proj_ref[rs, 0:D_CONV]
        u = proj_ref[rs, D_CONV:2 * D_CONV] * proj_ref[rs, 2 * D_CONV:3 * D_CONV]
        car = cc_ref[bb]
        a_out = _gated_conv(gb, u, car[0:1, :], car[1:2, :], cw_ref[...])
        co_ref[bb] = u[Ld - 2:Ld, :]
        mix_ref[rs, 0:D_CONV] = a_out.astype(BF16)

        k_r = _headnorm_rope(proj_ref[rs, k0:k0 + LANES], kg_ref[...], cos, sin)
        v_r = proj_ref[rs, v0:v0 + LANES]
        q_r = [_headnorm_rope(proj_ref[rs, q0 + LANES * t:q0 + LANES * (t + 1)], qg_ref[...], cos, sin)
               for t in range(N_Q * HEAD_DIM // LANES)]
        k_c = ck_ref[bb]
        v_c = cv_ref[bb]
        ko_ref[bb, 0:W - Ld, :] = k_c[Ld:W, :]
        ko_ref[bb, W - Ld:W, :] = k_r
        vo_ref[bb, 0:W - Ld, :] = v_c[Ld:W, :]
        vo_ref[bb, W - Ld:W, :] = v_r

        kk = jnp.concatenate([k_c, k_r, zpad], axis=0)
        vv = jnp.concatenate([v_c, v_r, zpad], axis=0)
        heads = []
        for h in range(N_Q):
            tile = q_r[h // 2]
            want_hi = h // GQA_G == 1
            if (h % 2 == 1) != want_hi:
                tile = pltpu.roll(tile, HEAD_DIM, 1)
            heads.append(jnp.where((lane >= HEAD_DIM) == want_hi, tile, 0.0))
        o = _band_attention(jnp.concatenate(heads, axis=0), kk, vv, valid, sinkv)
        for h in range(N_Q):
            c0 = (h // GQA_G) * HEAD_DIM
            mix_ref[rs, D_CONV + h * HEAD_DIM:D_CONV + (h + 1) * HEAD_DIM] = \
                o[h * Ld:(h + 1) * Ld, c0:c0 + HEAD_DIM].astype(BF16)


def _even_sample(proj, Bd, Ld, row0, cw, qg2, kg2, sinks, cache_conv, cache_k, cache_v):
    GB = SAMPLE_GB
    rows = GB * Ld
    rb0 = row0 // rows
    cos, sin = _rope_tables(PAST_LEN + jnp.arange(Ld, dtype=jnp.int32))
    full = lambda shape: pl.BlockSpec(shape, lambda i, *_: tuple(0 for _ in shape))
    grid_spec = pltpu.PrefetchScalarGridSpec(
        num_scalar_prefetch=1,
        grid=(Bd // GB,),
        in_specs=[pl.BlockSpec((rows, D_IN_EVEN), lambda i, s: (rb0 + i, 0)),
                  full((Ld, LANES)), full((Ld, LANES)),
                  full((CONV_W, D_CONV)), full((1, LANES)), full((1, LANES)),
                  pl.BlockSpec((GB, CONV_W - 1, D_CONV), lambda i, s: (i, 0, 0)),
                  pl.BlockSpec((GB, WINDOW, LANES), lambda i, s: (i, 0, 0)),
                  pl.BlockSpec((GB, WINDOW, LANES), lambda i, s: (i, 0, 0))],
        out_specs=[pl.BlockSpec((rows, D_MODEL), lambda i, s: (i, 0)),
                   pl.BlockSpec((GB, WINDOW, LANES), lambda i, s: (i, 0, 0)),
                   pl.BlockSpec((GB, WINDOW, LANES), lambda i, s: (i, 0, 0)),
                   pl.BlockSpec((GB, CONV_W - 1, D_CONV), lambda i, s: (i, 0, 0))],
    )
    return pl.pallas_call(
        _even_sample_kernel,
        grid_spec=grid_spec,
        out_shape=[jax.ShapeDtypeStruct((Bd * Ld, D_MODEL), BF16),
                   jax.ShapeDtypeStruct((Bd, WINDOW, LANES), F32),
                   jax.ShapeDtypeStruct((Bd, WINDOW, LANES), F32),
                   jax.ShapeDtypeStruct((Bd, CONV_W - 1, D_CONV), F32)],
        compiler_params=_cparams(("arbitrary",)),
        name="even_sample",
    )(sinks, proj, cos, sin, cw, qg2, kg2, cache_conv, cache_k, cache_v)


def _cumsum_rows(x):
    C = x.shape[0]
    row = lax.broadcasted_iota(jnp.int32, x.shape, 0)
    s = 1
    while s < C:
        x = x + jnp.where(row >= s, pltpu.roll(x, s, 0), 0.0)
        s *= 2
    return x


def _group_ref(b, m, row):
    C, D = b.shape
    if 2 * m >= SUBLANES:
        n = C // (2 * m)
        b3 = b.reshape(n, 2 * m, D)
        return jnp.broadcast_to(b3[:, m - 1:m, :], (n, 2 * m, D)).reshape(C, D)
    r = row & (2 * m - 1)
    out = b
    for off in range(2 * m):
        if off == m - 1:
            continue
        shift = (off - (m - 1)) % C
        out = jnp.where(r == off, pltpu.roll(b, shift, 0), out)
    return out


def _gla_consts(C):
    t = np.arange(C)[:, None]
    s = np.arange(C)[None, :]
    masks = [s == t]
    m = C // 2
    while m >= 1:
        masks.append((s // (2 * m) == t // (2 * m)) & ((s & m) == 0) & ((t & m) != 0))
        m //= 2
    return jnp.asarray(np.stack(masks).astype(np.float32))


def _gla_chunk(qz, fz, v, lb, S, masks=None):
    C = qz.shape[0]
    q = _silu(qz)
    f = lb + (1.0 - lb) * _sigmoid(fz)
    k = 1.0 - f
    b = _cumsum_rows(jnp.log(f))
    vb = v.astype(BF16)

    inter = jnp.dot((q * jnp.exp(b)).astype(BF16), S.astype(BF16), preferred_element_type=F32)

    row = lax.broadcasted_iota(jnp.int32, (C, C_DK), 0)
    nt = (((1,), (1,)), ((), ()))
    if C == SUBLANES:
        vr = vb.astype(F32)
        intra = jnp.zeros((C, C_DV), F32)
        for t in range(C):
            p = jnp.where(row <= t, q[t:t + 1, :] * k * jnp.exp(jnp.minimum(b[t:t + 1, :] - b, 0.0)), 0.0)
            col = jnp.sum(p, axis=-1, keepdims=True).astype(BF16).astype(F32)
            intra = jnp.where(row == t, jnp.sum(col * vr, axis=0, keepdims=True), intra)
    else:
        sc = masks[0] * lax.dot_general(q.astype(BF16), k.astype(BF16), nt,
                                        preferred_element_type=F32)
        m = C // 2
        level = 1
        while m >= 1:
            rho = _group_ref(b, m, row)
            upper = (row & m) != 0
            d = b - rho
            x = (jnp.where(upper, q, k) * jnp.exp(jnp.where(upper, d, -d))).astype(BF16)
            sc = sc + masks[level] * lax.dot_general(x, x, nt, preferred_element_type=F32)
            level += 1
            m //= 2
        intra = jnp.dot(sc.astype(BF16), vb, preferred_element_type=F32)

    b_last = b[C - 1:C, :]
    eye = (lax.broadcasted_iota(jnp.int32, (C_DK, C_DK), 0)
           == lax.broadcasted_iota(jnp.int32, (C_DK, C_DK), 1))
    dcol = jnp.sum(jnp.where(eye, jnp.broadcast_to(jnp.exp(b_last), (C_DK, C_DK)), 0.0),
                   axis=-1, keepdims=True)
    kd = k * jnp.exp(b_last - b)
    S_new = dcol * S + lax.dot_general(kd.astype(BF16), vb, (((0,), (0,)), ((), ())),
                                       preferred_element_type=F32)
    return inter + intra, S_new


def _gated_out(o, og, gz):
    ms = jnp.mean(o * o, axis=-1, keepdims=True)
    return (o * lax.rsqrt(ms + EPS) * og) * _silu(gz)


def _hgrn_prompt_kernel(q_ref, f_ref, i_ref, g_ref, lb_ref, og_ref, mask_ref, o_ref, s_ref, S):
    c = pl.program_id(2)

    @pl.when(c == 0)
    def _():
        S[...] = jnp.zeros_like(S)

    lb = lb_ref[...]
    og = og_ref[...]

    Sv = S[...]
    for n in range(GLA_STEP // GLA_CHUNK):
        rs = slice(n * GLA_CHUNK, (n + 1) * GLA_CHUNK)
        o, Sv = _gla_chunk(q_ref[rs, :], f_ref[rs, :], i_ref[rs, :], lb, Sv, mask_ref)
        o_ref[rs, :] = _gated_out(o, og, g_ref[rs, :]).astype(BF16)
    S[...] = Sv

    @pl.when(c == pl.num_programs(2) - 1)
    def _():
        s_ref[0, 0] = S[...]


def _hgrn_prompt(proj, B, L, lb, og):
    ns = L // GLA_STEP
    H = C_HEADS
    col = lambda off: pl.BlockSpec((GLA_STEP, LANES), lambda b, h, c: (b * ns + c, off + h))
    masks = _gla_consts(GLA_CHUNK)
    return pl.pallas_call(
        _hgrn_prompt_kernel,
        grid=(B, H, ns),
        in_specs=[col(0), col(H), col(2 * H), col(3 * H),
                  pl.BlockSpec((1, LANES), lambda b, h, c: (0, h)),
                  pl.BlockSpec((1, LANES), lambda b, h, c: (0, 0)),
                  pl.BlockSpec(masks.shape, lambda b, h, c: (0, 0, 0))],
        out_specs=[pl.BlockSpec((GLA_STEP, LANES), lambda b, h, c: (b * ns + c, h)),
                   pl.BlockSpec((1, 1, C_DK, C_DV), lambda b, h, c: (b, h, 0, 0))],
        out_shape=[jax.ShapeDtypeStruct((B * L, D_MODEL), BF16),
                   jax.ShapeDtypeStruct((B, H, C_DK, C_DV), F32)],
        scratch_shapes=[pltpu.VMEM((C_DK, C_DV), F32)],
        compiler_params=_cparams(("arbitrary", "arbitrary", "arbitrary")),
        name="hgrn_prompt",
    )(proj, proj, proj, proj, lb, og, masks)


def _hgrn_sample_kernel(p_ref, lb_ref, og_ref, s0_ref, o_ref, s_ref):
    Ld = p_ref.shape[0] // HGRN_SAMPLE_GB
    og = og_ref[...]
    results = []
    for bb in range(HGRN_SAMPLE_GB):
        rs = slice(bb * Ld, (bb + 1) * Ld)
        for h in range(C_HEADS):
            cs = lambda part: slice((part * C_HEADS + h) * LANES, (part * C_HEADS + h + 1) * LANES)
            o, S_new = _gla_chunk(p_ref[rs, cs(0)], p_ref[rs, cs(1)], p_ref[rs, cs(2)],
                                  lb_ref[:, h * LANES:(h + 1) * LANES], s0_ref[bb, h])
            results.append((bb, h, rs, S_new, _gated_out(o, og, p_ref[rs, cs(3)]).astype(BF16)))
    for bb, h, rs, S_new, out in results:
        s_ref[bb, h] = S_new
        o_ref[rs, h * LANES:(h + 1) * LANES] = out


def _hgrn_sample(proj, Bd, Ld, row0, lb, og, s0):
    GB = HGRN_SAMPLE_GB
    rows = GB * Ld
    rb0 = row0 // rows
    H = C_HEADS
    return pl.pallas_call(
        _hgrn_sample_kernel,
        grid=(Bd // GB,),
        in_specs=[pl.BlockSpec((rows, 4 * C_HK), lambda i: (rb0 + i, 0)),
                  pl.BlockSpec((1, C_HK), lambda i: (0, 0)),
                  pl.BlockSpec((1, LANES), lambda i: (0, 0)),
                  pl.BlockSpec((GB, H, C_DK, C_DV), lambda i: (i, 0, 0, 0))],
        out_specs=[pl.BlockSpec((rows, D_MODEL), lambda i: (i, 0)),
                   pl.BlockSpec((GB, H, C_DK, C_DV), lambda i: (i, 0, 0, 0))],
        out_shape=[jax.ShapeDtypeStruct((Bd * Ld, D_MODEL), BF16),
                   jax.ShapeDtypeStruct((Bd, H, C_DK, C_DV), F32)],
        compiler_params=_cparams(("arbitrary",)),
        name="hgrn_sample",
    )(proj, lb, og, s0)


def _pack_bf16_pairs(xb):
    n = xb.shape[1] // 2
    lo = lax.bitcast_convert_type(xb[:, :n].astype(F32), jnp.uint32)
    hi = lax.bitcast_convert_type(xb[:, n:].astype(F32), jnp.uint32)
    return (lo >> 16) | (hi & jnp.uint32(0xFFFF0000))


def _unpack_bf16_pairs(w):
    lo = lax.bitcast_convert_type(w << 16, F32)
    hi = lax.bitcast_convert_type(w & jnp.uint32(0xFFFF0000), F32)
    return jnp.concatenate([lo, hi], axis=1).astype(BF16)


def _out_proj_kernel(n_mix, n_x, npt, *refs):
    m = _read_rows(refs[:n_mix], npt)
    x = _read_rows(refs[n_mix:n_mix + n_x], npt)
    w_ref, g_ref, wr_ref, br_ref, x1_ref, h_ref, lg_ref = refs[n_mix + n_x:]
    x1 = x + jnp.dot(m, w_ref[...], preferred_element_type=F32)
    x1_ref[...] = x1
    ms = jnp.mean(x1 * x1, axis=-1, keepdims=True)
    hb = (x1 * lax.rsqrt(ms + EPS) * g_ref[...]).astype(BF16)
    h_ref[...] = _pack_bf16_pairs(hb)
    lg_ref[...] = lax.dot_general(wr_ref[...], hb, (((1,), (1,)), ((), ())),
                                  preferred_element_type=F32) + br_ref[...]


def _out_proj(mix, w, x, g, wr, br, tm):
    K, D = w.shape
    m_specs, m_arrays, npt, T = _row_source(mix, tm)
    x_specs, x_arrays, npt_x, _ = _row_source(x, tm)
    assert npt_x in (0, npt)
    row = lambda n: pl.BlockSpec((tm, n), lambda i: (i, 0))
    full = lambda a, b: pl.BlockSpec((a, b), lambda i: (0, 0), pipeline_mode=pl.Buffered(1))
    return pl.pallas_call(
        functools.partial(_out_proj_kernel, len(m_arrays), len(x_arrays), npt),
        grid=(T // tm,),
        in_specs=m_specs + x_specs + [full(K, D), full(1, D), full(LOGIT_W, D), full(LOGIT_W, 1)],
        out_specs=[row(D), row(D // 2), pl.BlockSpec((LOGIT_W, tm), lambda i: (0, i))],
        out_shape=[jax.ShapeDtypeStruct((T, D), F32), jax.ShapeDtypeStruct((T, D // 2), jnp.uint32),
                   jax.ShapeDtypeStruct((LOGIT_W, T), F32)],
        compiler_params=_cparams(("arbitrary",)),
        name="out_proj",
    )(*m_arrays, *x_arrays, w, g.reshape(1, D), wr, br)


def _route_kernel(lg_ref, slot_ref, gate_ref, cnt_ref, carry, total, tri):
    phase = pl.program_id(0)
    step = pl.program_id(1)
    tm = lg_ref.shape[1]
    G, E = N_GROUPS, EXP_PER_GROUP
    row8 = lax.broadcasted_iota(jnp.int32, (G, tm), 0)

    glog = lg_ref[0:G, :]
    gmax = jnp.max(glog, axis=0, keepdims=True)
    g_idx = jnp.min(jnp.where(glog == gmax, row8, G), axis=0, keepdims=True)
    g_p = 1.0 / jnp.sum(jnp.exp(glog - gmax), axis=0, keepdims=True)

    elog = jnp.zeros((E, tm), F32)
    for g in range(G):
        elog = jnp.where(g_idx == g, lg_ref[G + g * E:G + (g + 1) * E, :], elog)
    emax = jnp.max(elog, axis=0, keepdims=True)
    ee = jnp.exp(elog - emax)
    prob = ee / jnp.sum(ee, axis=0, keepdims=True)
    p1 = jnp.max(prob, axis=0, keepdims=True)
    i1 = jnp.min(jnp.where(prob == p1, row8, E), axis=0, keepdims=True)
    prob2 = jnp.where(row8 == i1, -1.0, prob)
    p2 = jnp.max(prob2, axis=0, keepdims=True)
    i2 = jnp.min(jnp.where(prob2 == p2, row8, E), axis=0, keepdims=True)
    psum = p1 + p2
    w1 = p1 / psum * g_p
    w2 = p2 / psum * g_p

    erow = lax.broadcasted_iota(jnp.int32, (N_EXPERTS, tm), 0)
    oh1 = erow == g_idx * E + i1
    oh2 = erow == g_idx * E + i2
    both = jnp.where(oh1 | oh2, 1.0, 0.0)
    tile_cnt = jnp.sum(both, axis=1, keepdims=True)

    @pl.when((phase == 0) & (step == 0))
    def _():
        total[...] = jnp.zeros_like(total)
        r = lax.broadcasted_iota(jnp.int32, (tm, tm), 0)
        c = lax.broadcasted_iota(jnp.int32, (tm, tm), 1)
        tri[...] = jnp.where(r < c, 1.0, 0.0).astype(BF16)

    @pl.when(phase == 0)
    def _():
        total[...] += tile_cnt

    @pl.when((phase == 1) & (step == 0))
    def _():
        carry[...] = jnp.zeros_like(carry)

    @pl.when(phase == 1)
    def _():
        cnt = total[...]
        padded = jnp.floor((cnt + (MOE_BLOCK - 1)) * (1.0 / MOE_BLOCK)) * MOE_BLOCK
        pb = jnp.broadcast_to(padded, (N_EXPERTS, LANES))
        pstart = (_cumsum_rows(pb) - pb)[:, 0:1]
        before = (jnp.dot(both.astype(BF16), tri[...], preferred_element_type=F32)
                  + (carry[...] + pstart))
        s1 = jnp.sum(jnp.where(oh1, before, 0.0), axis=0, keepdims=True)
        s2 = jnp.sum(jnp.where(oh2, before, 0.0), axis=0, keepdims=True)
        carry[...] += tile_cnt
        slot_ref[...] = jnp.concatenate([s1, s2], axis=0).astype(jnp.int32)
        grow = lax.broadcasted_iota(jnp.int32, (LOGIT_W, tm), 0)
        gate_ref[...] = jnp.where(grow == 0, w1, jnp.where(grow == 1, w2, 0.0)).T
        cnt_ref[...] = jnp.broadcast_to(cnt, (N_EXPERTS, LANES))


def _route(logits_t):
    T = logits_t.shape[1]
    tm = ROUTE_TM
    return pl.pallas_call(
        _route_kernel,
        grid=(2, T // tm),
        in_specs=[pl.BlockSpec((LOGIT_W, tm), lambda p, i: (0, i))],
        out_specs=[pl.BlockSpec((2, tm), lambda p, i: (0, i * p)),
                   pl.BlockSpec((tm, LOGIT_W), lambda p, i: (i * p, 0)),
                   pl.BlockSpec((N_EXPERTS, LANES), lambda p, i: (0, 0))],
        out_shape=[jax.ShapeDtypeStruct((2, T), jnp.int32),
                   jax.ShapeDtypeStruct((T, LOGIT_W), F32),
                   jax.ShapeDtypeStruct((N_EXPERTS, LANES), F32)],
        scratch_shapes=[pltpu.VMEM((N_EXPERTS, 1), F32), pltpu.VMEM((N_EXPERTS, 1), F32),
                        pltpu.VMEM((tm, tm), BF16)],
        compiler_params=_cparams(("arbitrary", "arbitrary")),
        name="route",
    )(logits_t)


DISPATCH_TM = 512


def _dispatch_kernel(T, slot_ref, h_ref, xs_in_ref, xs_ref, sem):
    del xs_in_ref
    base = pl.program_id(0) * DISPATCH_TM

    def copy(r, k):
        return pltpu.make_async_copy(h_ref.at[pl.ds(r, 1)],
                                     xs_ref.at[pl.ds(slot_ref[k * T + base + r], 1)], sem)

    def start(r, c):
        copy(r, 0).start()
        copy(r, 1).start()
        return c

    def wait(r, c):
        copy(r, 0).wait()
        copy(r, 1).wait()
        return c

    lax.fori_loop(0, DISPATCH_TM, start, 0, unroll=8)
    lax.fori_loop(0, DISPATCH_TM, wait, 0, unroll=8)


def _dispatch(slots, h, n_slots):
    T, D = h.shape
    xs0 = jnp.zeros((n_slots, D), h.dtype)
    grid_spec = pltpu.PrefetchScalarGridSpec(
        num_scalar_prefetch=1,
        grid=(T // DISPATCH_TM,),
        in_specs=[pl.BlockSpec((DISPATCH_TM, D), lambda i, s: (i, 0)),
                  pl.BlockSpec(memory_space=pl.ANY)],
        out_specs=pl.BlockSpec(memory_space=pl.ANY),
        scratch_shapes=[pltpu.SemaphoreType.DMA(())],
    )
    return pl.pallas_call(
        functools.partial(_dispatch_kernel, T),
        grid_spec=grid_spec,
        out_shape=jax.ShapeDtypeStruct((n_slots, D), h.dtype),
        input_output_aliases={2: 0},
        compiler_params=_cparams(("arbitrary",)),
        name="moe_dispatch",
    )(slots, h, xs0)


def _ffn_kernel(nblk, layer, meta_ref, x_ref, w1_hbm, w3_hbm, w2_hbm, y_ref,
                wf1, wf3, wf2, sems, wb1, wb3, wb2):
    b = pl.program_id(0)
    used = b < meta_ref[nblk]
    e = meta_ref[b]
    new_expert = (b == 0) | (e != meta_ref[jnp.maximum(b - 1, 0)])
    slot = meta_ref[nblk + 1 + e] % 2
    nxt = meta_ref[nblk + 1 + N_EXPERTS + e]

    def fetch(expert, s):
        return (pltpu.make_async_copy(w1_hbm.at[layer, expert], wf1.at[s], sems.at[s, 0]),
                pltpu.make_async_copy(w3_hbm.at[layer, expert], wf3.at[s], sems.at[s, 1]),
                pltpu.make_async_copy(w2_hbm.at[layer, expert], wf2.at[s], sems.at[s, 2]))

    @pl.when(used & (b == 0))
    def _():
        for cp in fetch(e, slot):
            cp.start()

    @pl.when(used & new_expert)
    def _():
        for cp in fetch(e, slot):
            cp.wait()

        @pl.when(nxt >= 0)
        def _():
            for cp in fetch(nxt, 1 - slot):
                cp.start()

        wb1[...] = wf1[slot].astype(BF16)
        wb3[...] = wf3[slot].astype(BF16)
        wb2[...] = wf2[slot].astype(BF16)

    @pl.when(used)
    def _():
        x = _unpack_bf16_pairs(x_ref[...])
        a = jnp.dot(x, wb1[...], preferred_element_type=F32)
        c = jnp.dot(x, wb3[...], preferred_element_type=F32)
        hid = (a * _sigmoid(a)) * c
        y_ref[...] = jnp.dot(hid.astype(BF16), wb2[...], preferred_element_type=F32)

    @pl.when(jnp.logical_not(used))
    def _():
        y_ref[...] = jnp.zeros_like(y_ref)


def _ffn(meta, xs, w1, w3, w2, layer):
    n_slots = xs.shape[0]
    D = w1.shape[2]
    nblk = n_slots // MOE_BLOCK
    grid_spec = pltpu.PrefetchScalarGridSpec(
        num_scalar_prefetch=1,
        grid=(nblk,),
        in_specs=[pl.BlockSpec((MOE_BLOCK, xs.shape[1]), lambda b, e: (b, 0)),
                  pl.BlockSpec(memory_space=pl.ANY), pl.BlockSpec(memory_space=pl.ANY),
                  pl.BlockSpec(memory_space=pl.ANY)],
        out_specs=pl.BlockSpec((MOE_BLOCK, D), lambda b, e: (b, 0)),
        scratch_shapes=[pltpu.VMEM((2, D, D_EXPERT), F32), pltpu.VMEM((2, D, D_EXPERT), F32),
                        pltpu.VMEM((2, D_EXPERT, D), F32), pltpu.SemaphoreType.DMA((2, 3)),
                        pltpu.VMEM((D, D_EXPERT), BF16), pltpu.VMEM((D, D_EXPERT), BF16),
                        pltpu.VMEM((D_EXPERT, D), BF16)],
    )
    return pl.pallas_call(
        functools.partial(_ffn_kernel, nblk, layer),
        grid_spec=grid_spec,
        out_shape=jax.ShapeDtypeStruct((n_slots, D), F32),
        compiler_params=_cparams(("arbitrary",)),
        name="moe_ffn",
    )(meta, xs, w1, w3, w2)


COMBINE_TM = 512
COMBINE_CHUNK = 128


def _combine_kernel(T, npt, slot_ref, x_ref, gate_ref, ys_ref, *refs):
    out_refs, (buf, sems) = refs[:-2], refs[-2:]
    i = pl.program_id(0)
    base = i * COMBINE_TM
    n_chunks = COMBINE_TM // COMBINE_CHUNK

    def copy(r, k, q):
        return pltpu.make_async_copy(ys_ref.at[pl.ds(slot_ref[k * T + base + r], 1)],
                                     buf.at[k, pl.ds(r, 1)], sems.at[q])

    def for_chunk_rows(q, fn):
        def body(r, c):
            fn(copy(r, 0, q))
            fn(copy(r, 1, q))
            return c
        lax.fori_loop(q * COMBINE_CHUNK, (q + 1) * COMBINE_CHUNK, body, 0, unroll=8)

    for q in range(n_chunks):
        for_chunk_rows(q, lambda cp: cp.start())

    def emit(rs, y):
        if npt == 0:
            out_refs[0][rs, :] = y
        else:
            @pl.when(i < npt)
            def _():
                out_refs[0][rs, :] = y

            @pl.when(i >= npt)
            def _():
                out_refs[1][rs, :] = y

    for q in range(n_chunks):
        for_chunk_rows(q, lambda cp: cp.wait())
        rs = slice(q * COMBINE_CHUNK, (q + 1) * COMBINE_CHUNK)
        g = gate_ref[rs, :]
        emit(rs, x_ref[rs, :] + (buf[0, rs, :] * g[:, 0:1] + buf[1, rs, :] * g[:, 1:2]))


def _combine(slots, x1, gates, ys, split_rows=0):
    T, D = x1.shape
    tm = COMBINE_TM
    npt = split_rows // tm
    if npt == 0:
        out_specs = pl.BlockSpec((tm, D), lambda i, s: (i, 0))
        out_shape = jax.ShapeDtypeStruct((T, D), F32)
    else:
        out_specs = [pl.BlockSpec((tm, D), lambda i, s: (jnp.minimum(i, npt - 1), 0)),
                     pl.BlockSpec((tm, D), lambda i, s: (jnp.maximum(i - npt, 0), 0))]
        out_shape = [jax.ShapeDtypeStruct((split_rows, D), F32),
                     jax.ShapeDtypeStruct((T - split_rows, D), F32)]
    grid_spec = pltpu.PrefetchScalarGridSpec(
        num_scalar_prefetch=1,
        grid=(T // tm,),
        in_specs=[pl.BlockSpec((tm, D), lambda i, s: (i, 0)),
                  pl.BlockSpec((tm, LOGIT_W), lambda i, s: (i, 0)),
                  pl.BlockSpec(memory_space=pl.ANY)],
        out_specs=out_specs,
        scratch_shapes=[pltpu.VMEM((2, tm, D), F32),
                        pltpu.SemaphoreType.DMA((COMBINE_TM // COMBINE_CHUNK,))],
    )
    return pl.pallas_call(
        functools.partial(_combine_kernel, T, npt),
        grid_spec=grid_spec,
        out_shape=out_shape,
        compiler_params=_cparams(("arbitrary",)),
        name="moe_combine",
    )(slots, x1, gates, ys)


def _moe(x1, h, logits, w1, w3, w2, layer, split_rows=0):
    T = x1.shape[0]
    slot_l, gate_l, cnt = _route(logits)
    slots = slot_l.reshape(2 * T)
    n_blocks = -(-(2 * T) // MOE_BLOCK) + N_EXPERTS
    counts = cnt[:, 0].astype(jnp.int32)
    pend = jnp.cumsum((counts + MOE_BLOCK - 1) // MOE_BLOCK * MOE_BLOCK)
    starts = jnp.arange(n_blocks, dtype=jnp.int32) * MOE_BLOCK
    blk_e = jnp.minimum(jnp.sum((pend[None, :] <= starts[:, None]).astype(jnp.int32), axis=1),
                        N_EXPERTS - 1)
    present = (counts > 0).astype(jnp.int32)
    rank = jnp.cumsum(present) - present
    ids = jnp.where(present > 0, jnp.arange(N_EXPERTS, dtype=jnp.int32), N_EXPERTS)
    after = jnp.concatenate([lax.cummin(ids[::-1])[::-1][1:], jnp.full((1,), N_EXPERTS, jnp.int32)])
    nxt = jnp.where(after < N_EXPERTS, after, -1)
    meta = jnp.concatenate([blk_e, pend[-1:] // MOE_BLOCK, rank, nxt]).astype(jnp.int32)
    xs = _dispatch(slots, h, n_blocks * MOE_BLOCK)
    ys = _ffn(meta, xs, w1, w3, w2, layer)
    return _combine(slots, x1, gate_l, ys, split_rows)


def _router_weights(w_grp, b_grp, w_exp, b_exp):
    D = w_grp.shape[0]
    pad = LOGIT_W - N_GROUPS - N_EXPERTS
    wr = jnp.concatenate([w_grp, w_exp, jnp.zeros((D, pad), F32)], axis=1).astype(BF16)
    br = jnp.concatenate([b_grp, b_exp, jnp.zeros((pad,), F32)]).reshape(LOGIT_W, 1)
    return wr.T, br


def kernel(x_prompt, x_sample, cache_conv, cache_k, cache_v, state_hgrn, norm_mix, norm_ffn,
           ev_w_in, ev_conv, ev_q_norm, ev_k_norm, ev_sinks, ev_w_out,
           od_w_in, od_lb, od_o_norm, od_w_out,
           moe_w_grp, moe_b_grp, moe_w_exp, moe_b_exp, moe_w1, moe_w3, moe_w2):
    B, L, D = x_prompt.shape
    Bd, Ld, _ = x_sample.shape
    Tp = B * L
    x = (x_prompt.reshape(Tp, D), x_sample.reshape(Bd * Ld, D))

    proj = _norm_matmul(x, norm_mix[0], ev_w_in[0].astype(BF16), 1024)
    qg2 = jnp.tile(ev_q_norm[0], 2).reshape(1, LANES)
    kg2 = jnp.tile(ev_k_norm[0], 2).reshape(1, LANES)
    mix_p, k_p, v_p, conv_p = _even_prompt(proj, B, L, ev_conv[0], qg2, kg2, ev_sinks[0])
    mix_s, k_s, v_s, conv_s = _even_sample(
        proj, Bd, Ld, Tp, ev_conv[0], qg2, kg2, ev_sinks[0], cache_conv[0],
        cache_k[0].reshape(Bd, WINDOW, LANES), cache_v[0].reshape(Bd, WINDOW, LANES))
    wr, br = _router_weights(moe_w_grp[0], moe_b_grp[0], moe_w_exp[0], moe_b_exp[0])
    x1, h, logits = _out_proj((mix_p, mix_s), ev_w_out[0].astype(BF16), x, norm_ffn[0], wr, br, 1024)
    x = _moe(x1, h, logits, moe_w1, moe_w3, moe_w2, 0)

    lb_all = jnp.cumsum(jax.nn.softmax(od_lb.astype(F32), axis=0), axis=0)
    lb = (lb_all - lb_all[0])[1].reshape(1, C_HK)
    og = od_o_norm[0].reshape(1, C_DV)
    proj = _norm_matmul(x, norm_mix[1], od_w_in[0].astype(BF16), 512)
    o_p, s_p = _hgrn_prompt(proj, B, L, lb, og)
    o_s, s_s = _hgrn_sample(proj, Bd, Ld, Tp, lb, og, state_hgrn[0])
    wr, br = _router_weights(moe_w_grp[1], moe_b_grp[1], moe_w_exp[1], moe_b_exp[1])
    x1, h, logits = _out_proj((o_p, o_s), od_w_out[0].astype(BF16), x, norm_ffn[1], wr, br, 1024)
    y_p, y_s = _moe(x1, h, logits, moe_w1, moe_w3, moe_w2, 1, split_rows=Tp)

    y_prompt = y_p.reshape(B, L, D)
    y_sample = y_s.reshape(Bd, Ld, D)
    return (y_prompt, y_sample,
            conv_p[None], k_p.reshape(1, B, WINDOW, N_KV, HEAD_DIM), v_p.reshape(1, B, WINDOW, N_KV, HEAD_DIM),
            s_p[None],
            conv_s[None], k_s.reshape(1, Bd, WINDOW, N_KV, HEAD_DIM), v_s.reshape(1, Bd, WINDOW, N_KV, HEAD_DIM),
            s_s[None])
```

```python
import functools

import numpy as np
import jax
import jax.numpy as jnp
from jax import lax
from jax.experimental import pallas as pl
from jax.experimental.pallas import tpu as pltpu

F32 = jnp.float32
BF16 = jnp.bfloat16

D_MODEL = 1024
PAST_LEN = 16384
D_CONV = 512
CONV_W = 3
HEAD_DIM = 64
N_Q = 8
N_KV = 2
GQA_G = N_Q // N_KV
WINDOW = 128
ROPE_THETA = 10000.0
D_IN_EVEN = 3 * D_CONV + (N_Q + 2 * N_KV) * HEAD_DIM
C_HEADS = 8
C_DK = 128
C_DV = 128
C_HK = C_HEADS * C_DK
N_GROUPS = 8
EXP_PER_GROUP = 8
N_EXPERTS = N_GROUPS * EXP_PER_GROUP
D_EXPERT = 512
MOE_BLOCK = 256
EPS = 1e-6

LANES = 128
SUBLANES = 8
VMEM_LIMIT = 56 * 1024 * 1024

GLA_CHUNK = 128
GLA_STEP = 2048
SAMPLE_GB = 4
HGRN_SAMPLE_GB = 2
ROUTE_TM = 512
LOGIT_W = 128


def _cparams(sem):
    return pltpu.CompilerParams(dimension_semantics=sem, vmem_limit_bytes=VMEM_LIMIT)


def _sigmoid(x):
    return 1.0 / (1.0 + jnp.exp(-x))


def _silu(x):
    return x * (0.5 * jnp.tanh(0.5 * x) + 0.5)


def _row_source(src, tm):
    if isinstance(src, tuple):
        a, b = src
        n = a.shape[1]
        npt = a.shape[0] // tm
        specs = [pl.BlockSpec((tm, n), lambda i, *_: (jnp.minimum(i, npt - 1), 0)),
                 pl.BlockSpec((tm, n), lambda i, *_: (jnp.maximum(i - npt, 0), 0))]
        return specs, [a, b], npt, a.shape[0] + b.shape[0]
    return [pl.BlockSpec((tm, src.shape[1]), lambda i, *_: (i, 0))], [src], 0, src.shape[0]


def _read_rows(refs, npt):
    if len(refs) == 2:
        return jnp.where(pl.program_id(0) < npt, refs[0][...], refs[1][...])
    return refs[0][...]


def _norm_matmul_kernel(n_src, npt, *refs):
    x = _read_rows(refs[:n_src], npt)
    g_ref, w_ref, o_ref = refs[n_src:]
    ms = jnp.mean(x * x, axis=-1, keepdims=True)
    h = (x * lax.rsqrt(ms + EPS) * g_ref[...]).astype(BF16)
    o_ref[...] = jnp.dot(h, w_ref[...], preferred_element_type=F32)


def _norm_matmul(x, g, w, tm):
    D, N = w.shape
    specs, arrays, npt, T = _row_source(x, tm)
    return pl.pallas_call(
        functools.partial(_norm_matmul_kernel, len(arrays), npt),
        grid=(T // tm,),
        in_specs=specs + [pl.BlockSpec((1, D), lambda i: (0, 0)),
                          pl.BlockSpec((D, N), lambda i: (0, 0), pipeline_mode=pl.Buffered(1))],
        out_specs=pl.BlockSpec((tm, N), lambda i: (i, 0)),
        out_shape=jax.ShapeDtypeStruct((T, N), F32),
        compiler_params=_cparams(("arbitrary",)),
        name="norm_matmul",
    )(*arrays, g.reshape(1, D), w)


def _rope_tables(pos):
    inv = ROPE_THETA ** (-jnp.arange(0, HEAD_DIM, 2, dtype=F32) / HEAD_DIM)
    ang = pos.astype(F32)[:, None] * inv[None, :]
    cos = jnp.cos(ang)
    sin = jnp.sin(ang)
    return (jnp.concatenate([cos, cos, cos, cos], axis=1),
            jnp.concatenate([-sin, sin, -sin, sin], axis=1))


def _headnorm_rope(x, g2, cos, sin):
    lane = lax.broadcasted_iota(jnp.int32, x.shape, 1)
    lo = lane < HEAD_DIM
    x2 = x * x
    s_lo = jnp.sum(jnp.where(lo, x2, 0.0), axis=-1, keepdims=True)
    s_hi = jnp.sum(jnp.where(lo, 0.0, x2), axis=-1, keepdims=True)
    ms = jnp.where(lo, s_lo, s_hi) * (1.0 / HEAD_DIM)
    y = x * lax.rsqrt(ms + EPS) * g2
    first_half = (lane & (HEAD_DIM // 2)) == 0
    swapped = jnp.where(first_half, pltpu.roll(y, LANES - HEAD_DIM // 2, 1),
                        pltpu.roll(y, HEAD_DIM // 2, 1))
    return y * cos + swapped * sin


def _gated_conv(gb, u, c2, c1, cw):
    row = lax.broadcasted_iota(jnp.int32, u.shape, 0)
    u1 = jnp.where(row == 0, c1, pltpu.roll(u, 1, 0))
    u2 = jnp.where(row == 0, c2, jnp.where(row == 1, c1, pltpu.roll(u, 2, 0)))
    return gb * (cw[0:1, :] * u2 + cw[1:2, :] * u1 + cw[2:3, :] * u)


def _band_attention(qs, kk, vv, valid, sinkv):
    s = lax.dot_general(qs.astype(BF16), kk.astype(BF16), (((1,), (1,)), ((), ())),
                        preferred_element_type=F32) * (HEAD_DIM ** -0.5)
    s = jnp.where(valid, s, -jnp.inf)
    m = jnp.maximum(jnp.max(s, axis=-1, keepdims=True), sinkv)
    e = jnp.exp(s - m)
    den = jnp.sum(e, axis=-1, keepdims=True) + jnp.exp(sinkv - m)
    p = e / den
    return jnp.dot(p.astype(BF16), vv.astype(BF16), preferred_element_type=F32)


def _sink_column(sinks_ref, hk, rows_per_head):
    R = GQA_G * rows_per_head
    row = lax.broadcasted_iota(jnp.int32, (R, 1), 0)
    col = jnp.full((R, 1), sinks_ref[hk * GQA_G + GQA_G - 1], F32)
    for j in range(GQA_G - 2, -1, -1):
        col = jnp.where(row < (j + 1) * rows_per_head, sinks_ref[hk * GQA_G + j], col)
    return col


def _even_prompt_kernel(sinks_ref, proj_ref, cos_ref, sin_ref, cw_ref, qg_ref, kg_ref,
                        mix_ref, kl_ref, vl_ref, cl_ref, kprev, vprev, ucar):
    blk = pl.program_id(1)
    W = WINDOW

    @pl.when(blk == 0)
    def _():
        kprev[...] = jnp.zeros_like(kprev)
        vprev[...] = jnp.zeros_like(vprev)
        ucar[...] = jnp.zeros_like(ucar)

    gb = proj_ref[:, 0:D_CONV]
    u = proj_ref[:, D_CONV:2 * D_CONV] * proj_ref[:, 2 * D_CONV:3 * D_CONV]
    car = ucar[...]
    a_out = _gated_conv(gb, u, car[0:1, :], car[1:2, :], cw_ref[...])
    ucar[0:2, :] = u[W - 2:W, :]
    mix_ref[:, 0:D_CONV] = a_out.astype(BF16)

    cos = cos_ref[...]
    sin = sin_ref[...]
    q0 = 3 * D_CONV
    k0 = q0 + N_Q * HEAD_DIM
    v0 = k0 + N_KV * HEAD_DIM
    k_r = _headnorm_rope(proj_ref[:, k0:k0 + LANES], kg_ref[...], cos, sin)
    v_r = proj_ref[:, v0:v0 + LANES]
    q_r = [_headnorm_rope(proj_ref[:, q0 + LANES * j:q0 + LANES * (j + 1)], qg_ref[...], cos, sin)
           for j in range(N_Q * HEAD_DIM // LANES)]
    k_p = kprev[...]
    v_p = vprev[...]

    R = GQA_G * W
    i = lax.broadcasted_iota(jnp.int32, (R, 2 * W), 0) & (W - 1)
    j = lax.broadcasted_iota(jnp.int32, (R, 2 * W), 1)
    diff = i + W - j
    valid = (diff >= 0) & (diff <= W) & ((blk > 0) | (j >= W))

    for hk in range(N_KV):
        ls = slice(hk * HEAD_DIM, (hk + 1) * HEAD_DIM)
        kk = jnp.concatenate([k_p[:, ls], k_r[:, ls]], axis=0)
        vv = jnp.concatenate([v_p[:, ls], v_r[:, ls]], axis=0)
        heads = []
        for g in range(GQA_G):
            h = hk * GQA_G + g
            tile = q_r[h // 2]
            heads.append(tile[:, (h % 2) * HEAD_DIM:(h % 2 + 1) * HEAD_DIM])
        qs = jnp.concatenate(heads, axis=0)
        o = _band_attention(qs, kk, vv, valid, _sink_column(sinks_ref, hk, W))
        for g in range(GQA_G):
            h = hk * GQA_G + g
            mix_ref[:, D_CONV + h * HEAD_DIM:D_CONV + (h + 1) * HEAD_DIM] = \
                o[g * W:(g + 1) * W, :].astype(BF16)

    kprev[...] = k_r
    vprev[...] = v_r

    @pl.when(blk == pl.num_programs(1) - 1)
    def _():
        kl_ref[0] = k_r
        vl_ref[0] = v_r
        cl_ref[0] = u[W - 2:W, :]


def _even_prompt(proj, B, L, cw, qg2, kg2, sinks):
    nb = L // WINDOW
    cos, sin = _rope_tables(jnp.arange(L, dtype=jnp.int32))
    full = lambda shape: pl.BlockSpec(shape, lambda b, i, *_: tuple(0 for _ in shape))
    grid_spec = pltpu.PrefetchScalarGridSpec(
        num_scalar_prefetch=1,
        grid=(B, nb),
        in_specs=[pl.BlockSpec((WINDOW, D_IN_EVEN), lambda b, i, s: (b * nb + i, 0)),
                  pl.BlockSpec((WINDOW, LANES), lambda b, i, s: (i, 0)),
                  pl.BlockSpec((WINDOW, LANES), lambda b, i, s: (i, 0)),
                  full((CONV_W, D_CONV)), full((1, LANES)), full((1, LANES))],
        out_specs=[pl.BlockSpec((WINDOW, D_MODEL), lambda b, i, s: (b * nb + i, 0)),
                   pl.BlockSpec((1, WINDOW, LANES), lambda b, i, s: (b, 0, 0)),
                   pl.BlockSpec((1, WINDOW, LANES), lambda b, i, s: (b, 0, 0)),
                   pl.BlockSpec((1, CONV_W - 1, D_CONV), lambda b, i, s: (b, 0, 0))],
        scratch_shapes=[pltpu.VMEM((WINDOW, LANES), F32), pltpu.VMEM((WINDOW, LANES), F32),
                        pltpu.VMEM((SUBLANES, D_CONV), F32)],
    )
    return pl.pallas_call(
        _even_prompt_kernel,
        grid_spec=grid_spec,
        out_shape=[jax.ShapeDtypeStruct((B * L, D_MODEL), BF16),
                   jax.ShapeDtypeStruct((B, WINDOW, LANES), F32),
                   jax.ShapeDtypeStruct((B, WINDOW, LANES), F32),
                   jax.ShapeDtypeStruct((B, CONV_W - 1, D_CONV), F32)],
        compiler_params=_cparams(("arbitrary", "arbitrary")),
        name="even_prompt",
    )(sinks, proj, cos, sin, cw, qg2, kg2)


def _even_sample_kernel(sinks_ref, proj_ref, cos_ref, sin_ref, cw_ref, qg_ref, kg_ref,
                        cc_ref, ck_ref, cv_ref, mix_ref, ko_ref, vo_ref, co_ref):
    W = WINDOW
    Ld = cos_ref.shape[0]
    cos = cos_ref[...]
    sin = sin_ref[...]
    q0 = 3 * D_CONV
    k0 = q0 + N_Q * HEAD_DIM
    v0 = k0 + N_KV * HEAD_DIM
    R = N_Q * Ld
    i = lax.broadcasted_iota(jnp.int32, (R, 2 * W), 0) % Ld
    j = lax.broadcasted_iota(jnp.int32, (R, 2 * W), 1)
    diff = i + W - j
    valid = (diff >= 0) & (diff <= W)
    zpad = jnp.zeros((W - Ld, LANES), F32)
    lane = lax.broadcasted_iota(jnp.int32, (Ld, LANES), 1)
    hrow = lax.broadcasted_iota(jnp.int32, (R, 1), 0)
    sinkv = jnp.full((R, 1), sinks_ref[N_Q - 1], F32)
    for h in range(N_Q - 2, -1, -1):
        sinkv = jnp.where(hrow < (h + 1) * Ld, sinks_ref[h], sinkv)

    for bb in range(SAMPLE_GB):
        rs = slice(bb * Ld, (bb + 1) * Ld)
        gb = proj_ref[rs, 0:D_CONV]
        u = proj_ref[rs, D_CONV:2 * D_CONV] * proj_ref[rs, 2 * D_CONV:3 * D_CONV]
        car = cc_ref[bb]
        a_out = _gated_conv(gb, u, car[0:1, :], car[1:2, :], cw_ref[...])
        co_ref[bb] = u[Ld - 2:Ld, :]
        mix_ref[rs, 0:D_CONV] = a_out.astype(BF16)

        k_r = _headnorm_rope(proj_ref[rs, k0:k0 + LANES], kg_ref[...], cos, sin)
        v_r = proj_ref[rs, v0:v0 + LANES]
        q_r = [_headnorm_rope(proj_ref[rs, q0 + LANES * t:q0 + LANES * (t + 1)], qg_ref[...], cos, sin)
               for t in range(N_Q * HEAD_DIM // LANES)]
        k_c = ck_ref[bb]
        v_c = cv_ref[bb]
        ko_ref[bb, 0:W - Ld, :] = k_c[Ld:W, :]
        ko_ref[bb, W - Ld:W, :] = k_r
        vo_ref[bb, 0:W - Ld, :] = v_c[Ld:W, :]
        vo_ref[bb, W - Ld:W, :] = v_r

        kk = jnp.concatenate([k_c, k_r, zpad], axis=0)
        vv = jnp.concatenate([v_c, v_r, zpad], axis=0)
        heads = []
        for h in range(N_Q):
            tile = q_r[h // 2]
            want_hi = h // GQA_G == 1
            if (h % 2 == 1) != want_hi:
                tile = pltpu.roll(tile, HEAD_DIM, 1)
            heads.append(jnp.where((lane >= HEAD_DIM) == want_hi, tile, 0.0))
        o = _band_attention(jnp.concatenate(heads, axis=0), kk, vv, valid, sinkv)
        for h in range(N_Q):
            c0 = (h // GQA_G) * HEAD_DIM
            mix_ref[rs, D_CONV + h * HEAD_DIM:D_CONV + (h + 1) * HEAD_DIM] = \
                o[h * Ld:(h + 1) * Ld, c0:c0 + HEAD_DIM].astype(BF16)


def _even_sample(proj, Bd, Ld, row0, cw, qg2, kg2, sinks, cache_conv, cache_k, cache_v):
    GB = SAMPLE_GB
    rows = GB * Ld
    rb0 = row0 // rows
    cos, sin = _rope_tables(PAST_LEN + jnp.arange(Ld, dtype=jnp.int32))
    full = lambda shape: pl.BlockSpec(shape, lambda i, *_: tuple(0 for _ in shape))
    grid_spec = pltpu.PrefetchScalarGridSpec(
        num_scalar_prefetch=1,
        grid=(Bd // GB,),
        in_specs=[pl.BlockSpec((rows, D_IN_EVEN), lambda i, s: (rb0 + i, 0)),
                  full((Ld, LANES)), full((Ld, LANES)),
                  full((CONV_W, D_CONV)), full((1, LANES)), full((1, LANES)),
                  pl.BlockSpec((GB, CONV_W - 1, D_CONV), lambda i, s: (i, 0, 0)),
                  pl.BlockSpec((GB, WINDOW, LANES), lambda i, s: (i, 0, 0)),
                  pl.BlockSpec((GB, WINDOW, LANES), lambda i, s: (i, 0, 0))],
        out_specs=[pl.BlockSpec((rows, D_MODEL), lambda i, s: (i, 0)),
                   pl.BlockSpec((GB, WINDOW, LANES), lambda i, s: (i, 0, 0)),
                   pl.BlockSpec((GB, WINDOW, LANES), lambda i, s: (i, 0, 0)),
                   pl.BlockSpec((GB, CONV_W - 1, D_CONV), lambda i, s: (i, 0, 0))],
    )
    return pl.pallas_call(
        _even_sample_kernel,
        grid_spec=grid_spec,
        out_shape=[jax.ShapeDtypeStruct((Bd * Ld, D_MODEL), BF16),
                   jax.ShapeDtypeStruct((Bd, WINDOW, LANES), F32),
                   jax.ShapeDtypeStruct((Bd, WINDOW, LANES), F32),
                   jax.ShapeDtypeStruct((Bd, CONV_W - 1, D_CONV), F32)],
        compiler_params=_cparams(("arbitrary",)),
        name="even_sample",
    )(sinks, proj, cos, sin, cw, qg2, kg2, cache_conv, cache_k, cache_v)


def _cumsum_rows(x):
    C = x.shape[0]
    row = lax.broadcasted_iota(jnp.int32, x.shape, 0)
    s = 1
    while s < C:
        x = x + jnp.where(row >= s, pltpu.roll(x, s, 0), 0.0)
        s *= 2
    return x


def _group_ref(b, m, row):
    C, D = b.shape
    if 2 * m >= SUBLANES:
        n = C // (2 * m)
        b3 = b.reshape(n, 2 * m, D)
        return jnp.broadcast_to(b3[:, m - 1:m, :], (n, 2 * m, D)).reshape(C, D)
    r = row & (2 * m - 1)
    out = b
    for off in range(2 * m):
        if off == m - 1:
            continue
        shift = (off - (m - 1)) % C
        out = jnp.where(r == off, pltpu.roll(b, shift, 0), out)
    return out


def _gla_consts(C):
    t = np.arange(C)[:, None]
    s = np.arange(C)[None, :]
    masks = [s == t]
    m = C // 2
    while m >= 1:
        masks.append((s // (2 * m) == t // (2 * m)) & ((s & m) == 0) & ((t & m) != 0))
        m //= 2
    return jnp.asarray(np.stack(masks).astype(np.float32))


def _gla_chunk(qz, fz, v, lb, S, masks=None):
    C = qz.shape[0]
    q = _silu(qz)
    f = lb + (1.0 - lb) * _sigmoid(fz)
    k = 1.0 - f
    b = _cumsum_rows(jnp.log(f))
    vb = v.astype(BF16)

    inter = jnp.dot((q * jnp.exp(b)).astype(BF16), S.astype(BF16), preferred_element_type=F32)

    row = lax.broadcasted_iota(jnp.int32, (C, C_DK), 0)
    nt = (((1,), (1,)), ((), ()))
    if C == SUBLANES:
        vr = vb.astype(F32)
        intra = jnp.zeros((C, C_DV), F32)
        for t in range(C):
            p = jnp.where(row <= t, q[t:t + 1, :] * k * jnp.exp(jnp.minimum(b[t:t + 1, :] - b, 0.0)), 0.0)
            col = jnp.sum(p, axis=-1, keepdims=True).astype(BF16).astype(F32)
            intra = jnp.where(row == t, jnp.sum(col * vr, axis=0, keepdims=True), intra)
    else:
        sc = masks[0] * lax.dot_general(q.astype(BF16), k.astype(BF16), nt,
                                        preferred_element_type=F32)
        m = C // 2
        level = 1
        while m >= 1:
            rho = _group_ref(b, m, row)
            upper = (row & m) != 0
            d = b - rho
            x = (jnp.where(upper, q, k) * jnp.exp(jnp.where(upper, d, -d))).astype(BF16)
            sc = sc + masks[level] * lax.dot_general(x, x, nt, preferred_element_type=F32)
            level += 1
            m //= 2
        intra = jnp.dot(sc.astype(BF16), vb, preferred_element_type=F32)

    b_last = b[C - 1:C, :]
    eye = (lax.broadcasted_iota(jnp.int32, (C_DK, C_DK), 0)
           == lax.broadcasted_iota(jnp.int32, (C_DK, C_DK), 1))
    dcol = jnp.sum(jnp.where(eye, jnp.broadcast_to(jnp.exp(b_last), (C_DK, C_DK)), 0.0),
                   axis=-1, keepdims=True)
    kd = k * jnp.exp(b_last - b)
    S_new = dcol * S + lax.dot_general(kd.astype(BF16), vb, (((0,), (0,)), ((), ())),
                                       preferred_element_type=F32)
    return inter + intra, S_new


def _gated_out(o, og, gz):
    ms = jnp.mean(o * o, axis=-1, keepdims=True)
    return (o * lax.rsqrt(ms + EPS) * og) * _silu(gz)


def _hgrn_prompt_kernel(q_ref, f_ref, i_ref, g_ref, lb_ref, og_ref, mask_ref, o_ref, s_ref, S):
    c = pl.program_id(2)

    @pl.when(c == 0)
    def _():
        S[...] = jnp.zeros_like(S)

    lb = lb_ref[...]
    og = og_ref[...]

    Sv = S[...]
    for n in range(GLA_STEP // GLA_CHUNK):
        rs = slice(n * GLA_CHUNK, (n + 1) * GLA_CHUNK)
        o, Sv = _gla_chunk(q_ref[rs, :], f_ref[rs, :], i_ref[rs, :], lb, Sv, mask_ref)
        o_ref[rs, :] = _gated_out(o, og, g_ref[rs, :]).astype(BF16)
    S[...] = Sv

    @pl.when(c == pl.num_programs(2) - 1)
    def _():
        s_ref[0, 0] = S[...]


def _hgrn_prompt(proj, B, L, lb, og):
    ns = L // GLA_STEP
    H = C_HEADS
    col = lambda off: pl.BlockSpec((GLA_STEP, LANES), lambda b, h, c: (b * ns + c, off + h))
    masks = _gla_consts(GLA_CHUNK)
    return pl.pallas_call(
        _hgrn_prompt_kernel,
        grid=(B, H, ns),
        in_specs=[col(0), col(H), col(2 * H), col(3 * H),
                  pl.BlockSpec((1, LANES), lambda b, h, c: (0, h)),
                  pl.BlockSpec((1, LANES), lambda b, h, c: (0, 0)),
                  pl.BlockSpec(masks.shape, lambda b, h, c: (0, 0, 0))],
        out_specs=[pl.BlockSpec((GLA_STEP, LANES), lambda b, h, c: (b * ns + c, h)),
                   pl.BlockSpec((1, 1, C_DK, C_DV), lambda b, h, c: (b, h, 0, 0))],
        out_shape=[jax.ShapeDtypeStruct((B * L, D_MODEL), BF16),
                   jax.ShapeDtypeStruct((B, H, C_DK, C_DV), F32)],
        scratch_shapes=[pltpu.VMEM((C_DK, C_DV), F32)],
        compiler_params=_cparams(("arbitrary", "arbitrary", "arbitrary")),
        name="hgrn_prompt",
    )(proj, proj, proj, proj, lb, og, masks)


def _hgrn_sample_kernel(p_ref, lb_ref, og_ref, s0_ref, o_ref, s_ref):
    Ld = p_ref.shape[0] // HGRN_SAMPLE_GB
    og = og_ref[...]
    results = []
    for bb in range(HGRN_SAMPLE_GB):
        rs = slice(bb * Ld, (bb + 1) * Ld)
        for h in range(C_HEADS):
            cs = lambda part: slice((part * C_HEADS + h) * LANES, (part * C_HEADS + h + 1) * LANES)
            o, S_new = _gla_chunk(p_ref[rs, cs(0)], p_ref[rs, cs(1)], p_ref[rs, cs(2)],
                                  lb_ref[:, h * LANES:(h + 1) * LANES], s0_ref[bb, h])
            results.append((bb, h, rs, S_new, _gated_out(o, og, p_ref[rs, cs(3)]).astype(BF16)))
    for bb, h, rs, S_new, out in results:
        s_ref[bb, h] = S_new
        o_ref[rs, h * LANES:(h + 1) * LANES] = out


def _hgrn_sample(proj, Bd, Ld, row0, lb, og, s0):
    GB = HGRN_SAMPLE_GB
    rows = GB * Ld
    rb0 = row0 // rows
    H = C_HEADS
    return pl.pallas_call(
        _hgrn_sample_kernel,
        grid=(Bd // GB,),
        in_specs=[pl.BlockSpec((rows, 4 * C_HK), lambda i: (rb0 + i, 0)),
                  pl.BlockSpec((1, C_HK), lambda i: (0, 0)),
                  pl.BlockSpec((1, LANES), lambda i: (0, 0)),
                  pl.BlockSpec((GB, H, C_DK, C_DV), lambda i: (i, 0, 0, 0))],
        out_specs=[pl.BlockSpec((rows, D_MODEL), lambda i: (i, 0)),
                   pl.BlockSpec((GB, H, C_DK, C_DV), lambda i: (i, 0, 0, 0))],
        out_shape=[jax.ShapeDtypeStruct((Bd * Ld, D_MODEL), BF16),
                   jax.ShapeDtypeStruct((Bd, H, C_DK, C_DV), F32)],
        compiler_params=_cparams(("arbitrary",)),
        name="hgrn_sample",
    )(proj, lb, og, s0)


def _pack_bf16_pairs(xb):
    n = xb.shape[1] // 2
    lo = lax.bitcast_convert_type(xb[:, :n].astype(F32), jnp.uint32)
    hi = lax.bitcast_convert_type(xb[:, n:].astype(F32), jnp.uint32)
    return (lo >> 16) | (hi & jnp.uint32(0xFFFF0000))


def _unpack_bf16_pairs(w):
    lo = lax.bitcast_convert_type(w << 16, F32)
    hi = lax.bitcast_convert_type(w & jnp.uint32(0xFFFF0000), F32)
    return jnp.concatenate([lo, hi], axis=1).astype(BF16)


def _out_proj_kernel(n_mix, n_x, npt, *refs):
    m = _read_rows(refs[:n_mix], npt)
    x = _read_rows(refs[n_mix:n_mix + n_x], npt)
    w_ref, g_ref, wr_ref, br_ref, x1_ref, h_ref, lg_ref = refs[n_mix + n_x:]
    x1 = x + jnp.dot(m, w_ref[...], preferred_element_type=F32)
    x1_ref[...] = x1
    ms = jnp.mean(x1 * x1, axis=-1, keepdims=True)
    hb = (x1 * lax.rsqrt(ms + EPS) * g_ref[...]).astype(BF16)
    h_ref[...] = _pack_bf16_pairs(hb)
    lg_ref[...] = lax.dot_general(wr_ref[...], hb, (((1,), (1,)), ((), ())),
                                  preferred_element_type=F32) + br_ref[...]


def _out_proj(mix, w, x, g, wr, br, tm):
    K, D = w.shape
    m_specs, m_arrays, npt, T = _row_source(mix, tm)
    x_specs, x_arrays, npt_x, _ = _row_source(x, tm)
    assert npt_x in (0, npt)
    row = lambda n: pl.BlockSpec((tm, n), lambda i: (i, 0))
    full = lambda a, b: pl.BlockSpec((a, b), lambda i: (0, 0), pipeline_mode=pl.Buffered(1))
    return pl.pallas_call(
        functools.partial(_out_proj_kernel, len(m_arrays), len(x_arrays), npt),
        grid=(T // tm,),
        in_specs=m_specs + x_specs + [full(K, D), full(1, D), full(LOGIT_W, D), full(LOGIT_W, 1)],
        out_specs=[row(D), row(D // 2), pl.BlockSpec((LOGIT_W, tm), lambda i: (0, i))],
        out_shape=[jax.ShapeDtypeStruct((T, D), F32), jax.ShapeDtypeStruct((T, D // 2), jnp.uint32),
                   jax.ShapeDtypeStruct((LOGIT_W, T), F32)],
        compiler_params=_cparams(("arbitrary",)),
        name="out_proj",
    )(*m_arrays, *x_arrays, w, g.reshape(1, D), wr, br)


def _route_kernel(lg_ref, slot_ref, gate_ref, cnt_ref, carry, total, tri):
    phase = pl.program_id(0)
    step = pl.program_id(1)
    tm = lg_ref.shape[1]
    G, E = N_GROUPS, EXP_PER_GROUP
    row8 = lax.broadcasted_iota(jnp.int32, (G, tm), 0)

    glog = lg_ref[0:G, :]
    gmax = jnp.max(glog, axis=0, keepdims=True)
    g_idx = jnp.min(jnp.where(glog == gmax, row8, G), axis=0, keepdims=True)
    g_p = 1.0 / jnp.sum(jnp.exp(glog - gmax), axis=0, keepdims=True)

    elog = jnp.zeros((E, tm), F32)
    for g in range(G):
        elog = jnp.where(g_idx == g, lg_ref[G + g * E:G + (g + 1) * E, :], elog)
    emax = jnp.max(elog, axis=0, keepdims=True)
    ee = jnp.exp(elog - emax)
    prob = ee / jnp.sum(ee, axis=0, keepdims=True)
    p1 = jnp.max(prob, axis=0, keepdims=True)
    i1 = jnp.min(jnp.where(prob == p1, row8, E), axis=0, keepdims=True)
    prob2 = jnp.where(row8 == i1, -1.0, prob)
    p2 = jnp.max(prob2, axis=0, keepdims=True)
    i2 = jnp.min(jnp.where(prob2 == p2, row8, E), axis=0, keepdims=True)
    psum = p1 + p2
    w1 = p1 / psum * g_p
    w2 = p2 / psum * g_p

    erow = lax.broadcasted_iota(jnp.int32, (N_EXPERTS, tm), 0)
    oh1 = erow == g_idx * E + i1
    oh2 = erow == g_idx * E + i2
    both = jnp.where(oh1 | oh2, 1.0, 0.0)
    tile_cnt = jnp.sum(both, axis=1, keepdims=True)

    @pl.when((phase == 0) & (step == 0))
    def _():
        total[...] = jnp.zeros_like(total)
        r = lax.broadcasted_iota(jnp.int32, (tm, tm), 0)
        c = lax.broadcasted_iota(jnp.int32, (tm, tm), 1)
        tri[...] = jnp.where(r < c, 1.0, 0.0).astype(BF16)

    @pl.when(phase == 0)
    def _():
        total[...] += tile_cnt

    @pl.when((phase == 1) & (step == 0))
    def _():
        carry[...] = jnp.zeros_like(carry)

    @pl.when(phase == 1)
    def _():
        cnt = total[...]
        padded = jnp.floor((cnt + (MOE_BLOCK - 1)) * (1.0 / MOE_BLOCK)) * MOE_BLOCK
        pb = jnp.broadcast_to(padded, (N_EXPERTS, LANES))
        pstart = (_cumsum_rows(pb) - pb)[:, 0:1]
        before = (jnp.dot(both.astype(BF16), tri[...], preferred_element_type=F32)
                  + (carry[...] + pstart))
        s1 = jnp.sum(jnp.where(oh1, before, 0.0), axis=0, keepdims=True)
        s2 = jnp.sum(jnp.where(oh2, before, 0.0), axis=0, keepdims=True)
        carry[...] += tile_cnt
        slot_ref[...] = jnp.concatenate([s1, s2], axis=0).astype(jnp.int32)
        grow = lax.broadcasted_iota(jnp.int32, (LOGIT_W, tm), 0)
        gate_ref[...] = jnp.where(grow == 0, w1, jnp.where(grow == 1, w2, 0.0)).T
        cnt_ref[...] = jnp.broadcast_to(cnt, (N_EXPERTS, LANES))


def _route(logits_t):
    T = logits_t.shape[1]
    tm = ROUTE_TM
    return pl.pallas_call(
        _route_kernel,
        grid=(2, T // tm),
        in_specs=[pl.BlockSpec((LOGIT_W, tm), lambda p, i: (0, i))],
        out_specs=[pl.BlockSpec((2, tm), lambda p, i: (0, i * p)),
                   pl.BlockSpec((tm, LOGIT_W), lambda p, i: (i * p, 0)),
                   pl.BlockSpec((N_EXPERTS, LANES), lambda p, i: (0, 0))],
        out_shape=[jax.ShapeDtypeStruct((2, T), jnp.int32),
                   jax.ShapeDtypeStruct((T, LOGIT_W), F32),
                   jax.ShapeDtypeStruct((N_EXPERTS, LANES), F32)],
        scratch_shapes=[pltpu.VMEM((N_EXPERTS, 1), F32), pltpu.VMEM((N_EXPERTS, 1), F32),
                        pltpu.VMEM((tm, tm), BF16)],
        compiler_params=_cparams(("arbitrary", "arbitrary")),
        name="route",
    )(logits_t)


DISPATCH_TM = 512


def _dispatch_kernel(T, slot_ref, h_ref, xs_in_ref, xs_ref, sem):
    del xs_in_ref
    base = pl.program_id(0) * DISPATCH_TM

    def copy(r, k):
        return pltpu.make_async_copy(h_ref.at[pl.ds(r, 1)],
                                     xs_ref.at[pl.ds(slot_ref[k * T + base + r], 1)], sem)

    def start(r, c):
        copy(r, 0).start()
        copy(r, 1).start()
        return c

    def wait(r, c):
        copy(r, 0).wait()
        copy(r, 1).wait()
        return c

    lax.fori_loop(0, DISPATCH_TM, start, 0, unroll=8)
    lax.fori_loop(0, DISPATCH_TM, wait, 0, unroll=8)


def _dispatch(slots, h, n_slots):
    T, D = h.shape
    xs0 = jnp.zeros((n_slots, D), h.dtype)
    grid_spec = pltpu.PrefetchScalarGridSpec(
        num_scalar_prefetch=1,
        grid=(T // DISPATCH_TM,),
        in_specs=[pl.BlockSpec((DISPATCH_TM, D), lambda i, s: (i, 0)),
                  pl.BlockSpec(memory_space=pl.ANY)],
        out_specs=pl.BlockSpec(memory_space=pl.ANY),
        scratch_shapes=[pltpu.SemaphoreType.DMA(())],
    )
    return pl.pallas_call(
        functools.partial(_dispatch_kernel, T),
        grid_spec=grid_spec,
        out_shape=jax.ShapeDtypeStruct((n_slots, D), h.dtype),
        input_output_aliases={2: 0},
        compiler_params=_cparams(("arbitrary",)),
        name="moe_dispatch",
    )(slots, h, xs0)


def _ffn_kernel(nblk, layer, meta_ref, x_ref, w1_hbm, w3_hbm, w2_hbm, y_ref,
                wf1, wf3, wf2, sems, wb1, wb3, wb2):
    b = pl.program_id(0)
    used = b < meta_ref[nblk]
    e = meta_ref[b]
    new_expert = (b == 0) | (e != meta_ref[jnp.maximum(b - 1, 0)])
    slot = meta_ref[nblk + 1 + e] % 2
    nxt = meta_ref[nblk + 1 + N_EXPERTS + e]

    def fetch(expert, s):
        return (pltpu.make_async_copy(w1_hbm.at[layer, expert], wf1.at[s], sems.at[s, 0]),
                pltpu.make_async_copy(w3_hbm.at[layer, expert], wf3.at[s], sems.at[s, 1]),
                pltpu.make_async_copy(w2_hbm.at[layer, expert], wf2.at[s], sems.at[s, 2]))

    @pl.when(used & (b == 0))
    def _():
        for cp in fetch(e, slot):
            cp.start()

    @pl.when(used & new_expert)
    def _():
        for cp in fetch(e, slot):
            cp.wait()

        @pl.when(nxt >= 0)
        def _():
            for cp in fetch(nxt, 1 - slot):
                cp.start()

        wb1[...] = wf1[slot].astype(BF16)
        wb3[...] = wf3[slot].astype(BF16)
        wb2[...] = wf2[slot].astype(BF16)

    @pl.when(used)
    def _():
        x = _unpack_bf16_pairs(x_ref[...])
        a = jnp.dot(x, wb1[...], preferred_element_type=F32)
        c = jnp.dot(x, wb3[...], preferred_element_type=F32)
        hid = (a * _sigmoid(a)) * c
        y_ref[...] = jnp.dot(hid.astype(BF16), wb2[...], preferred_element_type=F32)

    @pl.when(jnp.logical_not(used))
    def _():
        y_ref[...] = jnp.zeros_like(y_ref)


def _ffn(meta, xs, w1, w3, w2, layer):
    n_slots = xs.shape[0]
    D = w1.shape[2]
    nblk = n_slots // MOE_BLOCK
    grid_spec = pltpu.PrefetchScalarGridSpec(
        num_scalar_prefetch=1,
        grid=(nblk,),
        in_specs=[pl.BlockSpec((MOE_BLOCK, xs.shape[1]), lambda b, e: (b, 0)),
                  pl.BlockSpec(memory_space=pl.ANY), pl.BlockSpec(memory_space=pl.ANY),
                  pl.BlockSpec(memory_space=pl.ANY)],
        out_specs=pl.BlockSpec((MOE_BLOCK, D), lambda b, e: (b, 0)),
        scratch_shapes=[pltpu.VMEM((2, D, D_EXPERT), F32), pltpu.VMEM((2, D, D_EXPERT), F32),
                        pltpu.VMEM((2, D_EXPERT, D), F32), pltpu.SemaphoreType.DMA((2, 3)),
                        pltpu.VMEM((D, D_EXPERT), BF16), pltpu.VMEM((D, D_EXPERT), BF16),
                        pltpu.VMEM((D_EXPERT, D), BF16)],
    )
    return pl.pallas_call(
        functools.partial(_ffn_kernel, nblk, layer),
        grid_spec=grid_spec,
        out_shape=jax.ShapeDtypeStruct((n_slots, D), F32),
        compiler_params=_cparams(("arbitrary",)),
        name="moe_ffn",
    )(meta, xs, w1, w3, w2)


COMBINE_TM = 512
COMBINE_CHUNK = 128


def _combine_kernel(T, npt, slot_ref, x_ref, gate_ref, ys_ref, *refs):
    out_refs, (buf, sems) = refs[:-2], refs[-2:]
    i = pl.program_id(0)
    base = i * COMBINE_TM
    n_chunks = COMBINE_TM // COMBINE_CHUNK

    def copy(r, k, q):
        return pltpu.make_async_copy(ys_ref.at[pl.ds(slot_ref[k * T + base + r], 1)],
                                     buf.at[k, pl.ds(r, 1)], sems.at[q])

    def for_chunk_rows(q, fn):
        def body(r, c):
            fn(copy(r, 0, q))
            fn(copy(r, 1, q))
            return c
        lax.fori_loop(q * COMBINE_CHUNK, (q + 1) * COMBINE_CHUNK, body, 0, unroll=8)

    for q in range(n_chunks):
        for_chunk_rows(q, lambda cp: cp.start())

    def emit(rs, y):
        if npt == 0:
            out_refs[0][rs, :] = y
        else:
            @pl.when(i < npt)
            def _():
                out_refs[0][rs, :] = y

            @pl.when(i >= npt)
            def _():
                out_refs[1][rs, :] = y

    for q in range(n_chunks):
        for_chunk_rows(q, lambda cp: cp.wait())
        rs = slice(q * COMBINE_CHUNK, (q + 1) * COMBINE_CHUNK)
        g = gate_ref[rs, :]
        emit(rs, x_ref[rs, :] + (buf[0, rs, :] * g[:, 0:1] + buf[1, rs, :] * g[:, 1:2]))


def _combine(slots, x1, gates, ys, split_rows=0):
    T, D = x1.shape
    tm = COMBINE_TM
    npt = split_rows // tm
    if npt == 0:
        out_specs = pl.BlockSpec((tm, D), lambda i, s: (i, 0))
        out_shape = jax.ShapeDtypeStruct((T, D), F32)
    else:
        out_specs = [pl.BlockSpec((tm, D), lambda i, s: (jnp.minimum(i, npt - 1), 0)),
                     pl.BlockSpec((tm, D), lambda i, s: (jnp.maximum(i - npt, 0), 0))]
        out_shape = [jax.ShapeDtypeStruct((split_rows, D), F32),
                     jax.ShapeDtypeStruct((T - split_rows, D), F32)]
    grid_spec = pltpu.PrefetchScalarGridSpec(
        num_scalar_prefetch=1,
        grid=(T // tm,),
        in_specs=[pl.BlockSpec((tm, D), lambda i, s: (i, 0)),
                  pl.BlockSpec((tm, LOGIT_W), lambda i, s: (i, 0)),
                  pl.BlockSpec(memory_space=pl.ANY)],
        out_specs=out_specs,
        scratch_shapes=[pltpu.VMEM((2, tm, D), F32),
                        pltpu.SemaphoreType.DMA((COMBINE_TM // COMBINE_CHUNK,))],
    )
    return pl.pallas_call(
        functools.partial(_combine_kernel, T, npt),
        grid_spec=grid_spec,
        out_shape=out_shape,
        compiler_params=_cparams(("arbitrary",)),
        name="moe_combine",
    )(slots, x1, gates, ys)


def _moe(x1, h, logits, w1, w3, w2, layer, split_rows=0):
    T = x1.shape[0]
    slot_l, gate_l, cnt = _route(logits)
    slots = slot_l.reshape(2 * T)
    n_blocks = -(-(2 * T) // MOE_BLOCK) + N_EXPERTS
    counts = cnt[:, 0].astype(jnp.int32)
    pend = jnp.cumsum((counts + MOE_BLOCK - 1) // MOE_BLOCK * MOE_BLOCK)
    starts = jnp.arange(n_blocks, dtype=jnp.int32) * MOE_BLOCK
    blk_e = jnp.minimum(jnp.sum((pend[None, :] <= starts[:, None]).astype(jnp.int32), axis=1),
                        N_EXPERTS - 1)
    present = (counts > 0).astype(jnp.int32)
    rank = jnp.cumsum(present) - present
    ids = jnp.where(present > 0, jnp.arange(N_EXPERTS, dtype=jnp.int32), N_EXPERTS)
    after = jnp.concatenate([lax.cummin(ids[::-1])[::-1][1:], jnp.full((1,), N_EXPERTS, jnp.int32)])
    nxt = jnp.where(after < N_EXPERTS, after, -1)
    meta = jnp.concatenate([blk_e, pend[-1:] // MOE_BLOCK, rank, nxt]).astype(jnp.int32)
    xs = _dispatch(slots, h, n_blocks * MOE_BLOCK)
    ys = _ffn(meta, xs, w1, w3, w2, layer)
    return _combine(slots, x1, gate_l, ys, split_rows)


def _router_weights(w_grp, b_grp, w_exp, b_exp):
    D = w_grp.shape[0]
    pad = LOGIT_W - N_GROUPS - N_EXPERTS
    wr = jnp.concatenate([w_grp, w_exp, jnp.zeros((D, pad), F32)], axis=1).astype(BF16)
    br = jnp.concatenate([b_grp, b_exp, jnp.zeros((pad,), F32)]).reshape(LOGIT_W, 1)
    return wr.T, br


def kernel(x_prompt, x_sample, cache_conv, cache_k, cache_v, state_hgrn, norm_mix, norm_ffn,
           ev_w_in, ev_conv, ev_q_norm, ev_k_norm, ev_sinks, ev_w_out,
           od_w_in, od_lb, od_o_norm, od_w_out,
           moe_w_grp, moe_b_grp, moe_w_exp, moe_b_exp, moe_w1, moe_w3, moe_w2):
    B, L, D = x_prompt.shape
    Bd, Ld, _ = x_sample.shape
    Tp = B * L
    x = (x_prompt.reshape(Tp, D), x_sample.reshape(Bd * Ld, D))

    proj = _norm_matmul(x, norm_mix[0], ev_w_in[0].astype(BF16), 1024)
    qg2 = jnp.tile(ev_q_norm[0], 2).reshape(1, LANES)
    kg2 = jnp.tile(ev_k_norm[0], 2).reshape(1, LANES)
    mix_p, k_p, v_p, conv_p = _even_prompt(proj, B, L, ev_conv[0], qg2, kg2, ev_sinks[0])
    mix_s, k_s, v_s, conv_s = _even_sample(
        proj, Bd, Ld, Tp, ev_conv[0], qg2, kg2, ev_sinks[0], cache_conv[0],
        cache_k[0].reshape(Bd, WINDOW, LANES), cache_v[0].reshape(Bd, WINDOW, LANES))
    wr, br = _router_weights(moe_w_grp[0], moe_b_grp[0], moe_w_exp[0], moe_b_exp[0])
    x1, h, logits = _out_proj((mix_p, mix_s), ev_w_out[0].astype(BF16), x, norm_ffn[0], wr, br, 1024)
    x = _moe(x1, h, logits, moe_w1, moe_w3, moe_w2, 0)

    lb_all = jnp.cumsum(jax.nn.softmax(od_lb.astype(F32), axis=0), axis=0)
    lb = (lb_all - lb_all[0])[1].reshape(1, C_HK)
    og = od_o_norm[0].reshape(1, C_DV)
    proj = _norm_matmul(x, norm_mix[1], od_w_in[0].astype(BF16), 512)
    o_p, s_p = _hgrn_prompt(proj, B, L, lb, og)
    o_s, s_s = _hgrn_sample(proj, Bd, Ld, Tp, lb, og, state_hgrn[0])
    wr, br = _router_weights(moe_w_grp[1], moe_b_grp[1], moe_w_exp[1], moe_b_exp[1])
    x1, h, logits = _out_proj((o_p, o_s), od_w_out[0].astype(BF16), x, norm_ffn[1], wr, br, 1024)
    y_p, y_s = _moe(x1, h, logits, moe_w1, moe_w3, moe_w2, 1, split_rows=Tp)

    y_prompt = y_p.reshape(B, L, D)
    y_sample = y_s.reshape(Bd, Ld, D)
    return (y_prompt, y_sample,
            conv_p[None], k_p.reshape(1, B, WINDOW, N_KV, HEAD_DIM), v_p.reshape(1, B, WINDOW, N_KV, HEAD_DIM),
            s_p[None],
            conv_s[None], k_s.reshape(1, Bd, WINDOW, N_KV, HEAD_DIM), v_s.reshape(1, Bd, WINDOW, N_KV, HEAD_DIM),
            s_s[None])
```

```python
import functools

import numpy as np
import jax
import jax.numpy as jnp
from jax import lax
from jax.experimental import pallas as pl
from jax.experimental.pallas import tpu as pltpu

F32 = jnp.float32
BF16 = jnp.bfloat16

D_MODEL = 1024
PAST_LEN = 16384
D_CONV = 512
CONV_W = 3
HEAD_DIM = 64
N_Q = 8
N_KV = 2
GQA_G = N_Q // N_KV
WINDOW = 128
ROPE_THETA = 10000.0
D_IN_EVEN = 3 * D_CONV + (N_Q + 2 * N_KV) * HEAD_DIM
C_HEADS = 8
C_DK = 128
C_DV = 128
C_HK = C_HEADS * C_DK
N_GROUPS = 8
EXP_PER_GROUP = 8
N_EXPERTS = N_GROUPS * EXP_PER_GROUP
D_EXPERT = 512
MOE_BLOCK = 256
EPS = 1e-6

LANES = 128
SUBLANES = 8
VMEM_LIMIT = 56 * 1024 * 1024

GLA_CHUNK = 128
GLA_STEP = 2048
SAMPLE_GB = 4
HGRN_SAMPLE_GB = 2
ROUTE_TM = 512
LOGIT_W = 128


def _cparams(sem):
    return pltpu.CompilerParams(dimension_semantics=sem, vmem_limit_bytes=VMEM_LIMIT)


def _sigmoid(x):
    return 1.0 / (1.0 + jnp.exp(-x))


def _silu(x):
    return x * (0.5 * jnp.tanh(0.5 * x) + 0.5)


def _row_source(src, tm):
    if isinstance(src, tuple):
        a, b = src
        n = a.shape[1]
        npt = a.shape[0] // tm
        specs = [pl.BlockSpec((tm, n), lambda i, *_: (jnp.minimum(i, npt - 1), 0)),
                 pl.BlockSpec((tm, n), lambda i, *_: (jnp.maximum(i - npt, 0), 0))]
        return specs, [a, b], npt, a.shape[0] + b.shape[0]
    return [pl.BlockSpec((tm, src.shape[1]), lambda i, *_: (i, 0))], [src], 0, src.shape[0]


def _read_rows(refs, npt):
    if len(refs) == 2:
        return jnp.where(pl.program_id(0) < npt, refs[0][...], refs[1][...])
    return refs[0][...]


def _norm_matmul_kernel(n_src, npt, *refs):
    x = _read_rows(refs[:n_src], npt)
    g_ref, w_ref, o_ref = refs[n_src:]
    ms = jnp.mean(x * x, axis=-1, keepdims=True)
    h = (x * lax.rsqrt(ms + EPS) * g_ref[...]).astype(BF16)
    o_ref[...] = jnp.dot(h, w_ref[...], preferred_element_type=F32)


def _norm_matmul(x, g, w, tm):
    D, N = w.shape
    specs, arrays, npt, T = _row_source(x, tm)
    return pl.pallas_call(
        functools.partial(_norm_matmul_kernel, len(arrays), npt),
        grid=(T // tm,),
        in_specs=specs + [pl.BlockSpec((1, D), lambda i: (0, 0)),
                          pl.BlockSpec((D, N), lambda i: (0, 0), pipeline_mode=pl.Buffered(1))],
        out_specs=pl.BlockSpec((tm, N), lambda i: (i, 0)),
        out_shape=jax.ShapeDtypeStruct((T, N), F32),
        compiler_params=_cparams(("arbitrary",)),
        name="norm_matmul",
    )(*arrays, g.reshape(1, D), w)


def _rope_tables(pos):
    inv = ROPE_THETA ** (-jnp.arange(0, HEAD_DIM, 2, dtype=F32) / HEAD_DIM)
    ang = pos.astype(F32)[:, None] * inv[None, :]
    cos = jnp.cos(ang)
    sin = jnp.sin(ang)
    return (jnp.concatenate([cos, cos, cos, cos], axis=1),
            jnp.concatenate([-sin, sin, -sin, sin], axis=1))


def _headnorm_rope(x, g2, cos, sin):
    lane = lax.broadcasted_iota(jnp.int32, x.shape, 1)
    lo = lane < HEAD_DIM
    x2 = x * x
    s_lo = jnp.sum(jnp.where(lo, x2, 0.0), axis=-1, keepdims=True)
    s_hi = jnp.sum(jnp.where(lo, 0.0, x2), axis=-1, keepdims=True)
    ms = jnp.where(lo, s_lo, s_hi) * (1.0 / HEAD_DIM)
    y = x * lax.rsqrt(ms + EPS) * g2
    first_half = (lane & (HEAD_DIM // 2)) == 0
    swapped = jnp.where(first_half, pltpu.roll(y, LANES - HEAD_DIM // 2, 1),
                        pltpu.roll(y, HEAD_DIM // 2, 1))
    return y * cos + swapped * sin


def _gated_conv(gb, u, c2, c1, cw):
    row = lax.broadcasted_iota(jnp.int32, u.shape, 0)
    u1 = jnp.where(row == 0, c1, pltpu.roll(u, 1, 0))
    u2 = jnp.where(row == 0, c2, jnp.where(row == 1, c1, pltpu.roll(u, 2, 0)))
    return gb * (cw[0:1, :] * u2 + cw[1:2, :] * u1 + cw[2:3, :] * u)


def _band_attention(qs, kk, vv, valid, sinkv):
    s = lax.dot_general(qs.astype(BF16), kk.astype(BF16), (((1,), (1,)), ((), ())),
                        preferred_element_type=F32) * (HEAD_DIM ** -0.5)
    s = jnp.where(valid, s, -jnp.inf)
    m = jnp.maximum(jnp.max(s, axis=-1, keepdims=True), sinkv)
    e = jnp.exp(s - m)
    den = jnp.sum(e, axis=-1, keepdims=True) + jnp.exp(sinkv - m)
    p = e / den
    return jnp.dot(p.astype(BF16), vv.astype(BF16), preferred_element_type=F32)


def _sink_column(sinks_ref, hk, rows_per_head):
    R = GQA_G * rows_per_head
    row = lax.broadcasted_iota(jnp.int32, (R, 1), 0)
    col = jnp.full((R, 1), sinks_ref[hk * GQA_G + GQA_G - 1], F32)
    for j in range(GQA_G - 2, -1, -1):
        col = jnp.where(row < (j + 1) * rows_per_head, sinks_ref[hk * GQA_G + j], col)
    return col


def _even_prompt_kernel(sinks_ref, proj_ref, cos_ref, sin_ref, cw_ref, qg_ref, kg_ref,
                        mix_ref, kl_ref, vl_ref, cl_ref, kprev, vprev, ucar):
    blk = pl.program_id(1)
    W = WINDOW

    @pl.when(blk == 0)
    def _():
        kprev[...] = jnp.zeros_like(kprev)
        vprev[...] = jnp.zeros_like(vprev)
        ucar[...] = jnp.zeros_like(ucar)

    gb = proj_ref[:, 0:D_CONV]
    u = proj_ref[:, D_CONV:2 * D_CONV] * proj_ref[:, 2 * D_CONV:3 * D_CONV]
    car = ucar[...]
    a_out = _gated_conv(gb, u, car[0:1, :], car[1:2, :], cw_ref[...])
    ucar[0:2, :] = u[W - 2:W, :]
    mix_ref[:, 0:D_CONV] = a_out.astype(BF16)

    cos = cos_ref[...]
    sin = sin_ref[...]
    q0 = 3 * D_CONV
    k0 = q0 + N_Q * HEAD_DIM
    v0 = k0 + N_KV * HEAD_DIM
    k_r = _headnorm_rope(proj_ref[:, k0:k0 + LANES], kg_ref[...], cos, sin)
    v_r = proj_ref[:, v0:v0 + LANES]
    q_r = [_headnorm_rope(proj_ref[:, q0 + LANES * j:q0 + LANES * (j + 1)], qg_ref[...], cos, sin)
           for j in range(N_Q * HEAD_DIM // LANES)]
    k_p = kprev[...]
    v_p = vprev[...]

    R = GQA_G * W
    i = lax.broadcasted_iota(jnp.int32, (R, 2 * W), 0) & (W - 1)
    j = lax.broadcasted_iota(jnp.int32, (R, 2 * W), 1)
    diff = i + W - j
    valid = (diff >= 0) & (diff <= W) & ((blk > 0) | (j >= W))

    for hk in range(N_KV):
        ls = slice(hk * HEAD_DIM, (hk + 1) * HEAD_DIM)
        kk = jnp.concatenate([k_p[:, ls], k_r[:, ls]], axis=0)
        vv = jnp.concatenate([v_p[:, ls], v_r[:, ls]], axis=0)
        heads = []
        for g in range(GQA_G):
            h = hk * GQA_G + g
            tile = q_r[h // 2]
            heads.append(tile[:, (h % 2) * HEAD_DIM:(h % 2 + 1) * HEAD_DIM])
        qs = jnp.concatenate(heads, axis=0)
        o = _band_attention(qs, kk, vv, valid, _sink_column(sinks_ref, hk, W))
        for g in range(GQA_G):
            h = hk * GQA_G + g
            mix_ref[:, D_CONV + h * HEAD_DIM:D_CONV + (h + 1) * HEAD_DIM] = \
                o[g * W:(g + 1) * W, :].astype(BF16)

    kprev[...] = k_r
    vprev[...] = v_r

    @pl.when(blk == pl.num_programs(1) - 1)
    def _():
        kl_ref[0] = k_r
        vl_ref[0] = v_r
        cl_ref[0] = u[W - 2:W, :]


def _even_prompt(proj, B, L, cw, qg2, kg2, sinks):
    nb = L // WINDOW
    cos, sin = _rope_tables(jnp.arange(L, dtype=jnp.int32))
    full = lambda shape: pl.BlockSpec(shape, lambda b, i, *_: tuple(0 for _ in shape))
    grid_spec = pltpu.PrefetchScalarGridSpec(
        num_scalar_prefetch=1,
        grid=(B, nb),
        in_specs=[pl.BlockSpec((WINDOW, D_IN_EVEN), lambda b, i, s: (b * nb + i, 0)),
                  pl.BlockSpec((WINDOW, LANES), lambda b, i, s: (i, 0)),
                  pl.BlockSpec((WINDOW, LANES), lambda b, i, s: (i, 0)),
                  full((CONV_W, D_CONV)), full((1, LANES)), full((1, LANES))],
        out_specs=[pl.BlockSpec((WINDOW, D_MODEL), lambda b, i, s: (b * nb + i, 0)),
                   pl.BlockSpec((1, WINDOW, LANES), lambda b, i, s: (b, 0, 0)),
                   pl.BlockSpec((1, WINDOW, LANES), lambda b, i, s: (b, 0, 0)),
                   pl.BlockSpec((1, CONV_W - 1, D_CONV), lambda b, i, s: (b, 0, 0))],
        scratch_shapes=[pltpu.VMEM((WINDOW, LANES), F32), pltpu.VMEM((WINDOW, LANES), F32),
                        pltpu.VMEM((SUBLANES, D_CONV), F32)],
    )
    return pl.pallas_call(
        _even_prompt_kernel,
        grid_spec=grid_spec,
        out_shape=[jax.ShapeDtypeStruct((B * L, D_MODEL), BF16),
                   jax.ShapeDtypeStruct((B, WINDOW, LANES), F32),
                   jax.ShapeDtypeStruct((B, WINDOW, LANES), F32),
                   jax.ShapeDtypeStruct((B, CONV_W - 1, D_CONV), F32)],
        compiler_params=_cparams(("arbitrary", "arbitrary")),
        name="even_prompt",
    )(sinks, proj, cos, sin, cw, qg2, kg2)


def _even_sample_kernel(sinks_ref, proj_ref, cos_ref, sin_ref, cw_ref, qg_ref, kg_ref,
                        cc_ref, ck_ref, cv_ref, mix_ref, ko_ref, vo_ref, co_ref):
    W = WINDOW
    Ld = cos_ref.shape[0]
    cos = cos_ref[...]
    sin = sin_ref[...]
    q0 = 3 * D_CONV
    k0 = q0 + N_Q * HEAD_DIM
    v0 = k0 + N_KV * HEAD_DIM
    R = N_Q * Ld
    i = lax.broadcasted_iota(jnp.int32, (R, 2 * W), 0) % Ld
    j = lax.broadcasted_iota(jnp.int32, (R, 2 * W), 1)
    diff = i + W - j
    valid = (diff >= 0) & (diff <= W)
    zpad = jnp.zeros((W - Ld, LANES), F32)
    lane = lax.broadcasted_iota(jnp.int32, (Ld, LANES), 1)
    hrow = lax.broadcasted_iota(jnp.int32, (R, 1), 0)
    sinkv = jnp.full((R, 1), sinks_ref[N_Q - 1], F32)
    for h in range(N_Q - 2, -1, -1):
        sinkv = jnp.where(hrow < (h + 1) * Ld, sinks_ref[h], sinkv)

    for bb in range(SAMPLE_GB):
        rs = slice(bb * Ld, (bb + 1) * Ld)
        gb = proj_ref[rs, 0:D_CONV]
        u = proj_ref[rs, D_CONV:2 * D_CONV] * proj_ref[rs, 2 * D_CONV:3 * D_CONV]
        car = cc_ref[bb]
        a_out = _gated_conv(gb, u, car[0:1, :], car[1:2, :], cw_ref[...])
        co_ref[bb] = u[Ld - 2:Ld, :]
        mix_ref[rs, 0:D_CONV] = a_out.astype(BF16)

        k_r = _headnorm_rope(proj_ref[rs, k0:k0 + LANES], kg_ref[...], cos, sin)
        v_r = proj_ref[rs, v0:v0 + LANES]
        q_r = [_headnorm_rope(proj_ref[rs, q0 + LANES * t:q0 + LANES * (t + 1)], qg_ref[...], cos, sin)
               for t in range(N_Q * HEAD_DIM // LANES)]
        k_c = ck_ref[bb]
        v_c = cv_ref[bb]
        ko_ref[bb, 0:W - Ld, :] = k_c[Ld:W, :]
        ko_ref[bb, W - Ld:W, :] = k_r
        vo_ref[bb, 0:W - Ld, :] = v_c[Ld:W, :]
        vo_ref[bb, W - Ld:W, :] = v_r

        kk = jnp.concatenate([k_c, k_r, zpad], axis=0)
        vv = jnp.concatenate([v_c, v_r, zpad], axis=0)
        heads = []
        for h in range(N_Q):
            tile = q_r[h // 2]
            want_hi = h // GQA_G == 1
            if (h % 2 == 1) != want_hi:
                tile = pltpu.roll(tile, HEAD_DIM, 1)
            heads.append(jnp.where((lane >= HEAD_DIM) == want_hi, tile, 0.0))
        o = _band_attention(jnp.concatenate(heads, axis=0), kk, vv, valid, sinkv)
        for h in range(N_Q):
            c0 = (h // GQA_G) * HEAD_DIM
            mix_ref[rs, D_CONV + h * HEAD_DIM:D_CONV + (h + 1) * HEAD_DIM] = \
                o[h * Ld:(h + 1) * Ld, c0:c0 + HEAD_DIM].astype(BF16)


def _even_sample(proj, Bd, Ld, row0, cw, qg2, kg2, sinks, cache_conv, cache_k, cache_v):
    GB = SAMPLE_GB
    rows = GB * Ld
    rb0 = row0 // rows
    cos, sin = _rope_tables(PAST_LEN + jnp.arange(Ld, dtype=jnp.int32))
    full = lambda shape: pl.BlockSpec(shape, lambda i, *_: tuple(0 for _ in shape))
    grid_spec = pltpu.PrefetchScalarGridSpec(
        num_scalar_prefetch=1,
        grid=(Bd // GB,),
        in_specs=[pl.BlockSpec((rows, D_IN_EVEN), lambda i, s: (rb0 + i, 0)),
                  full((Ld, LANES)), full((Ld, LANES)),
                  full((CONV_W, D_CONV)), full((1, LANES)), full((1, LANES)),
                  pl.BlockSpec((GB, CONV_W - 1, D_CONV), lambda i, s: (i, 0, 0)),
                  pl.BlockSpec((GB, WINDOW, LANES), lambda i, s: (i, 0, 0)),
                  pl.BlockSpec((GB, WINDOW, LANES), lambda i, s: (i, 0, 0))],
        out_specs=[pl.BlockSpec((rows, D_MODEL), lambda i, s: (i, 0)),
                   pl.BlockSpec((GB, WINDOW, LANES), lambda i, s: (i, 0, 0)),
                   pl.BlockSpec((GB, WINDOW, LANES), lambda i, s: (i, 0, 0)),
                   pl.BlockSpec((GB, CONV_W - 1, D_CONV), lambda i, s: (i, 0, 0))],
    )
    return pl.pallas_call(
        _even_sample_kernel,
        grid_spec=grid_spec,
        out_shape=[jax.ShapeDtypeStruct((Bd * Ld, D_MODEL), BF16),
                   jax.ShapeDtypeStruct((Bd, WINDOW, LANES), F32),
                   jax.ShapeDtypeStruct((Bd, WINDOW, LANES), F32),
                   jax.ShapeDtypeStruct((Bd, CONV_W - 1, D_CONV), F32)],
        compiler_params=_cparams(("arbitrary",)),
        name="even_sample",
    )(sinks, proj, cos, sin, cw, qg2, kg2, cache_conv, cache_k, cache_v)


def _cumsum_rows(x):
    C = x.shape[0]
    row = lax.broadcasted_iota(jnp.int32, x.shape, 0)
    s = 1
    while s < C:
        x = x + jnp.where(row >= s, pltpu.roll(x, s, 0), 0.0)
        s *= 2
    return x


def _group_ref(b, m, row):
    C, D = b.shape
    if 2 * m >= SUBLANES:
        n = C // (2 * m)
        b3 = b.reshape(n, 2 * m, D)
        return jnp.broadcast_to(b3[:, m - 1:m, :], (n, 2 * m, D)).reshape(C, D)
    r = row & (2 * m - 1)
    out = b
    for off in range(2 * m):
        if off == m - 1:
            continue
        shift = (off - (m - 1)) % C
        out = jnp.where(r == off, pltpu.roll(b, shift, 0), out)
    return out


def _gla_consts(C):
    t = np.arange(C)[:, None]
    s = np.arange(C)[None, :]
    masks = [s == t]
    m = C // 2
    while m >= 1:
        masks.append((s // (2 * m) == t // (2 * m)) & ((s & m) == 0) & ((t & m) != 0))
        m //= 2
    return jnp.asarray(np.stack(masks).astype(np.float32))


def _gla_chunk(qz, fz, v, lb, S, masks=None):
    C = qz.shape[0]
    q = _silu(qz)
    f = lb + (1.0 - lb) * _sigmoid(fz)
    k = 1.0 - f
    b = _cumsum_rows(jnp.log(f))
    vb = v.astype(BF16)

    inter = jnp.dot((q * jnp.exp(b)).astype(BF16), S.astype(BF16), preferred_element_type=F32)

    row = lax.broadcasted_iota(jnp.int32, (C, C_DK), 0)
    nt = (((1,), (1,)), ((), ()))
    if C == SUBLANES:
        vr = vb.astype(F32)
        intra = jnp.zeros((C, C_DV), F32)
        for t in range(C):
            p = jnp.where(row <= t, q[t:t + 1, :] * k * jnp.exp(jnp.minimum(b[t:t + 1, :] - b, 0.0)), 0.0)
            col = jnp.sum(p, axis=-1, keepdims=True).astype(BF16).astype(F32)
            intra = jnp.where(row == t, jnp.sum(col * vr, axis=0, keepdims=True), intra)
    else:
        sc = masks[0] * lax.dot_general(q.astype(BF16), k.astype(BF16), nt,
                                        preferred_element_type=F32)
        m = C // 2
        level = 1
        while m >= 1:
            rho = _group_ref(b, m, row)
            upper = (row & m) != 0
            d = b - rho
            x = (jnp.where(upper, q, k) * jnp.exp(jnp.where(upper, d, -d))).astype(BF16)
            sc = sc + masks[level] * lax.dot_general(x, x, nt, preferred_element_type=F32)
            level += 1
            m //= 2
        intra = jnp.dot(sc.astype(BF16), vb, preferred_element_type=F32)

    b_last = b[C - 1:C, :]
    eye = (lax.broadcasted_iota(jnp.int32, (C_DK, C_DK), 0)
           == lax.broadcasted_iota(jnp.int32, (C_DK, C_DK), 1))
    dcol = jnp.sum(jnp.where(eye, jnp.broadcast_to(jnp.exp(b_last), (C_DK, C_DK)), 0.0),
                   axis=-1, keepdims=True)
    kd = k * jnp.exp(b_last - b)
    S_new = dcol * S + lax.dot_general(kd.astype(BF16), vb, (((0,), (0,)), ((), ())),
                                       preferred_element_type=F32)
    return inter + intra, S_new


def _gated_out(o, og, gz):
    ms = jnp.mean(o * o, axis=-1, keepdims=True)
    return (o * lax.rsqrt(ms + EPS) * og) * _silu(gz)


def _hgrn_prompt_kernel(q_ref, f_ref, i_ref, g_ref, lb_ref, og_ref, mask_ref, o_ref, s_ref, S):
    c = pl.program_id(2)

    @pl.when(c == 0)
    def _():
        S[...] = jnp.zeros_like(S)

    lb = lb_ref[...]
    og = og_ref[...]

    Sv = S[...]
    for n in range(GLA_STEP // GLA_CHUNK):
        rs = slice(n * GLA_CHUNK, (n + 1) * GLA_CHUNK)
        o, Sv = _gla_chunk(q_ref[rs, :], f_ref[rs, :], i_ref[rs, :], lb, Sv, mask_ref)
        o_ref[rs, :] = _gated_out(o, og, g_ref[rs, :]).astype(BF16)
    S[...] = Sv

    @pl.when(c == pl.num_programs(2) - 1)
    def _():
        s_ref[0, 0] = S[...]


def _hgrn_prompt(proj, B, L, lb, og):
    ns = L // GLA_STEP
    H = C_HEADS
    col = lambda off: pl.BlockSpec((GLA_STEP, LANES), lambda b, h, c: (b * ns + c, off + h))
    masks = _gla_consts(GLA_CHUNK)
    return pl.pallas_call(
        _hgrn_prompt_kernel,
        grid=(B, H, ns),
        in_specs=[col(0), col(H), col(2 * H), col(3 * H),
                  pl.BlockSpec((1, LANES), lambda b, h, c: (0, h)),
                  pl.BlockSpec((1, LANES), lambda b, h, c: (0, 0)),
                  pl.BlockSpec(masks.shape, lambda b, h, c: (0, 0, 0))],
        out_specs=[pl.BlockSpec((GLA_STEP, LANES), lambda b, h, c: (b * ns + c, h)),
                   pl.BlockSpec((1, 1, C_DK, C_DV), lambda b, h, c: (b, h, 0, 0))],
        out_shape=[jax.ShapeDtypeStruct((B * L, D_MODEL), BF16),
                   jax.ShapeDtypeStruct((B, H, C_DK, C_DV), F32)],
        scratch_shapes=[pltpu.VMEM((C_DK, C_DV), F32)],
        compiler_params=_cparams(("arbitrary", "arbitrary", "arbitrary")),
        name="hgrn_prompt",
    )(proj, proj, proj, proj, lb, og, masks)


def _hgrn_sample_kernel(p_ref, lb_ref, og_ref, s0_ref, o_ref, s_ref):
    Ld = p_ref.shape[0] // HGRN_SAMPLE_GB
    og = og_ref[...]
    results = []
    for bb in range(HGRN_SAMPLE_GB):
        rs = slice(bb * Ld, (bb + 1) * Ld)
        for h in range(C_HEADS):
            cs = lambda part: slice((part * C_HEADS + h) * LANES, (part * C_HEADS + h + 1) * LANES)
            o, S_new = _gla_chunk(p_ref[rs, cs(0)], p_ref[rs, cs(1)], p_ref[rs, cs(2)],
                                  lb_ref[:, h * LANES:(h + 1) * LANES], s0_ref[bb, h])
            results.append((bb, h, rs, S_new, _gated_out(o, og, p_ref[rs, cs(3)]).astype(BF16)))
    for bb, h, rs, S_new, out in results:
        s_ref[bb, h] = S_new
        o_ref[rs, h * LANES:(h + 1) * LANES] = out


def _hgrn_sample(proj, Bd, Ld, row0, lb, og, s0):
    GB = HGRN_SAMPLE_GB
    rows = GB * Ld
    rb0 = row0 // rows
    H = C_HEADS
    return pl.pallas_call(
        _hgrn_sample_kernel,
        grid=(Bd // GB,),
        in_specs=[pl.BlockSpec((rows, 4 * C_HK), lambda i: (rb0 + i, 0)),
                  pl.BlockSpec((1, C_HK), lambda i: (0, 0)),
                  pl.BlockSpec((1, LANES), lambda i: (0, 0)),
                  pl.BlockSpec((GB, H, C_DK, C_DV), lambda i: (i, 0, 0, 0))],
        out_specs=[pl.BlockSpec((rows, D_MODEL), lambda i: (i, 0)),
                   pl.BlockSpec((GB, H, C_DK, C_DV), lambda i: (i, 0, 0, 0))],
        out_shape=[jax.ShapeDtypeStruct((Bd * Ld, D_MODEL), BF16),
                   jax.ShapeDtypeStruct((Bd, H, C_DK, C_DV), F32)],
        compiler_params=_cparams(("arbitrary",)),
        name="hgrn_sample",
    )(proj, lb, og, s0)


def _pack_bf16_pairs(xb):
    n = xb.shape[1] // 2
    lo = lax.bitcast_convert_type(xb[:, :n].astype(F32), jnp.uint32)
    hi = lax.bitcast_convert_type(xb[:, n:].astype(F32), jnp.uint32)
    return (lo >> 16) | (hi & jnp.uint32(0xFFFF0000))


def _unpack_bf16_pairs(w):
    lo = lax.bitcast_convert_type(w << 16, F32)
    hi = lax.bitcast_convert_type(w & jnp.uint32(0xFFFF0000), F32)
    return jnp.concatenate([lo, hi], axis=1).astype(BF16)


def _out_proj_kernel(n_mix, n_x, npt, *refs):
    m = _read_rows(refs[:n_mix], npt)
    x = _read_rows(refs[n_mix:n_mix + n_x], npt)
    w_ref, g_ref, wr_ref, br_ref, x1_ref, h_ref, lg_ref = refs[n_mix + n_x:]
    x1 = x + jnp.dot(m, w_ref[...], preferred_element_type=F32)
    x1_ref[...] = x1
    ms = jnp.mean(x1 * x1, axis=-1, keepdims=True)
    hb = (x1 * lax.rsqrt(ms + EPS) * g_ref[...]).astype(BF16)
    h_ref[...] = _pack_bf16_pairs(hb)
    lg_ref[...] = lax.dot_general(wr_ref[...], hb, (((1,), (1,)), ((), ())),
                                  preferred_element_type=F32) + br_ref[...]


def _out_proj(mix, w, x, g, wr, br, tm):
    K, D = w.shape
    m_specs, m_arrays, npt, T = _row_source(mix, tm)
    x_specs, x_arrays, npt_x, _ = _row_source(x, tm)
    assert npt_x in (0, npt)
    row = lambda n: pl.BlockSpec((tm, n), lambda i: (i, 0))
    full = lambda a, b: pl.BlockSpec((a, b), lambda i: (0, 0), pipeline_mode=pl.Buffered(1))
    return pl.pallas_call(
        functools.partial(_out_proj_kernel, len(m_arrays), len(x_arrays), npt),
        grid=(T // tm,),
        in_specs=m_specs + x_specs + [full(K, D), full(1, D), full(LOGIT_W, D), full(LOGIT_W, 1)],
        out_specs=[row(D), row(D // 2), pl.BlockSpec((LOGIT_W, tm), lambda i: (0, i))],
        out_shape=[jax.ShapeDtypeStruct((T, D), F32), jax.ShapeDtypeStruct((T, D // 2), jnp.uint32),
                   jax.ShapeDtypeStruct((LOGIT_W, T), F32)],
        compiler_params=_cparams(("arbitrary",)),
        name="out_proj",
    )(*m_arrays, *x_arrays, w, g.reshape(1, D), wr, br)


def _route_kernel(lg_ref, slot_ref, gate_ref, cnt_ref, carry, total, tri):
    phase = pl.program_id(0)
    step = pl.program_id(1)
    tm = lg_ref.shape[1]
    G, E = N_GROUPS, EXP_PER_GROUP
    row8 = lax.broadcasted_iota(jnp.int32, (G, tm), 0)

    glog = lg_ref[0:G, :]
    gmax = jnp.max(glog, axis=0, keepdims=True)
    g_idx = jnp.min(jnp.where(glog == gmax, row8, G), axis=0, keepdims=True)
    g_p = 1.0 / jnp.sum(jnp.exp(glog - gmax), axis=0, keepdims=True)

    elog = jnp.zeros((E, tm), F32)
    for g in range(G):
        elog = jnp.where(g_idx == g, lg_ref[G + g * E:G + (g + 1) * E, :], elog)
    emax = jnp.max(elog, axis=0, keepdims=True)
    ee = jnp.exp(elog - emax)
    prob = ee / jnp.sum(ee, axis=0, keepdims=True)
    p1 = jnp.max(prob, axis=0, keepdims=True)
    i1 = jnp.min(jnp.where(prob == p1, row8, E), axis=0, keepdims=True)
    prob2 = jnp.where(row8 == i1, -1.0, prob)
    p2 = jnp.max(prob2, axis=0, keepdims=True)
    i2 = jnp.min(jnp.where(prob2 == p2, row8, E), axis=0, keepdims=True)
    psum = p1 + p2
    w1 = p1 / psum * g_p
    w2 = p2 / psum * g_p

    erow = lax.broadcasted_iota(jnp.int32, (N_EXPERTS, tm), 0)
    oh1 = erow == g_idx * E + i1
    oh2 = erow == g_idx * E + i2
    both = jnp.where(oh1 | oh2, 1.0, 0.0)
    tile_cnt = jnp.sum(both, axis=1, keepdims=True)

    @pl.when((phase == 0) & (step == 0))
    def _():
        total[...] = jnp.zeros_like(total)
        r = lax.broadcasted_iota(jnp.int32, (tm, tm), 0)
        c = lax.broadcasted_iota(jnp.int32, (tm, tm), 1)
        tri[...] = jnp.where(r < c, 1.0, 0.0).astype(BF16)

    @pl.when(phase == 0)
    def _():
        total[...] += tile_cnt

    @pl.when((phase == 1) & (step == 0))
    def _():
        carry[...] = jnp.zeros_like(carry)

    @pl.when(phase == 1)
    def _():
        cnt = total[...]
        padded = jnp.floor((cnt + (MOE_BLOCK - 1)) * (1.0 / MOE_BLOCK)) * MOE_BLOCK
        pb = jnp.broadcast_to(padded, (N_EXPERTS, LANES))
        pstart = (_cumsum_rows(pb) - pb)[:, 0:1]
        before = (jnp.dot(both.astype(BF16), tri[...], preferred_element_type=F32)
                  + (carry[...] + pstart))
        s1 = jnp.sum(jnp.where(oh1, before, 0.0), axis=0, keepdims=True)
        s2 = jnp.sum(jnp.where(oh2, before, 0.0), axis=0, keepdims=True)
        carry[...] += tile_cnt
        slot_ref[...] = jnp.concatenate([s1, s2], axis=0).astype(jnp.int32)
        grow = lax.broadcasted_iota(jnp.int32, (LOGIT_W, tm), 0)
        gate_ref[...] = jnp.where(grow == 0, w1, jnp.where(grow == 1, w2, 0.0)).T
        cnt_ref[...] = jnp.broadcast_to(cnt, (N_EXPERTS, LANES))


def _route(logits_t):
    T = logits_t.shape[1]
    tm = ROUTE_TM
    return pl.pallas_call(
        _route_kernel,
        grid=(2, T // tm),
        in_specs=[pl.BlockSpec((LOGIT_W, tm), lambda p, i: (0, i))],
        out_specs=[pl.BlockSpec((2, tm), lambda p, i: (0, i * p)),
                   pl.BlockSpec((tm, LOGIT_W), lambda p, i: (i * p, 0)),
                   pl.BlockSpec((N_EXPERTS, LANES), lambda p, i: (0, 0))],
        out_shape=[jax.ShapeDtypeStruct((2, T), jnp.int32),
                   jax.ShapeDtypeStruct((T, LOGIT_W), F32),
                   jax.ShapeDtypeStruct((N_EXPERTS, LANES), F32)],
        scratch_shapes=[pltpu.VMEM((N_EXPERTS, 1), F32), pltpu.VMEM((N_EXPERTS, 1), F32),
                        pltpu.VMEM((tm, tm), BF16)],
        compiler_params=_cparams(("arbitrary", "arbitrary")),
        name="route",
    )(logits_t)


DISPATCH_TM = 512


def _dispatch_kernel(T, slot_ref, h_ref, xs_in_ref, xs_ref, sem):
    del xs_in_ref
    base = pl.program_id(0) * DISPATCH_TM

    def copy(r, k):
        return pltpu.make_async_copy(h_ref.at[pl.ds(r, 1)],
                                     xs_ref.at[pl.ds(slot_ref[k * T + base + r], 1)], sem)

    def start(r, c):
        copy(r, 0).start(priority=0)
        copy(r, 1).start(priority=1)
        return c

    def wait(r, c):
        copy(r, 0).wait()
        copy(r, 1).wait()
        return c

    lax.fori_loop(0, DISPATCH_TM, start, 0, unroll=8)
    lax.fori_loop(0, DISPATCH_TM, wait, 0, unroll=8)


def _dispatch(slots, h, n_slots):
    T, D = h.shape
    xs0 = jnp.zeros((n_slots, D), h.dtype)
    grid_spec = pltpu.PrefetchScalarGridSpec(
        num_scalar_prefetch=1,
        grid=(T // DISPATCH_TM,),
        in_specs=[pl.BlockSpec((DISPATCH_TM, D), lambda i, s: (i, 0)),
                  pl.BlockSpec(memory_space=pl.ANY)],
        out_specs=pl.BlockSpec(memory_space=pl.ANY),
        scratch_shapes=[pltpu.SemaphoreType.DMA(())],
    )
    return pl.pallas_call(
        functools.partial(_dispatch_kernel, T),
        grid_spec=grid_spec,
        out_shape=jax.ShapeDtypeStruct((n_slots, D), h.dtype),
        input_output_aliases={2: 0},
        compiler_params=_cparams(("arbitrary",)),
        name="moe_dispatch",
    )(slots, h, xs0)


def _ffn_kernel(nblk, layer, meta_ref, x_ref, w1_hbm, w3_hbm, w2_hbm, y_ref,
                wf1, wf3, wf2, sems, wb1, wb3, wb2):
    b = pl.program_id(0)
    used = b < meta_ref[nblk]
    e = meta_ref[b]
    new_expert = (b == 0) | (e != meta_ref[jnp.maximum(b - 1, 0)])
    slot = meta_ref[nblk + 1 + e] % 2
    nxt = meta_ref[nblk + 1 + N_EXPERTS + e]

    def fetch(expert, s):
        return (pltpu.make_async_copy(w1_hbm.at[layer, expert], wf1.at[s], sems.at[s, 0]),
                pltpu.make_async_copy(w3_hbm.at[layer, expert], wf3.at[s], sems.at[s, 1]),
                pltpu.make_async_copy(w2_hbm.at[layer, expert], wf2.at[s], sems.at[s, 2]))

    @pl.when(used & (b == 0))
    def _():
        for cp in fetch(e, slot):
            cp.start()

    @pl.when(used & new_expert)
    def _():
        for cp in fetch(e, slot):
            cp.wait()

        @pl.when(nxt >= 0)
        def _():
            for cp in fetch(nxt, 1 - slot):
                cp.start()

        wb1[...] = wf1[slot].astype(BF16)
        wb3[...] = wf3[slot].astype(BF16)
        wb2[...] = wf2[slot].astype(BF16)

    @pl.when(used)
    def _():
        x = _unpack_bf16_pairs(x_ref[...])
        a = jnp.dot(x, wb1[...], preferred_element_type=F32)
        c = jnp.dot(x, wb3[...], preferred_element_type=F32)
        hid = (a * _sigmoid(a)) * c
        y_ref[...] = jnp.dot(hid.astype(BF16), wb2[...], preferred_element_type=F32)

    @pl.when(jnp.logical_not(used))
    def _():
        y_ref[...] = jnp.zeros_like(y_ref)


def _ffn(meta, xs, w1, w3, w2, layer):
    n_slots = xs.shape[0]
    D = w1.shape[2]
    nblk = n_slots // MOE_BLOCK
    grid_spec = pltpu.PrefetchScalarGridSpec(
        num_scalar_prefetch=1,
        grid=(nblk,),
        in_specs=[pl.BlockSpec((MOE_BLOCK, xs.shape[1]), lambda b, e: (b, 0)),
                  pl.BlockSpec(memory_space=pl.ANY), pl.BlockSpec(memory_space=pl.ANY),
                  pl.BlockSpec(memory_space=pl.ANY)],
        out_specs=pl.BlockSpec((MOE_BLOCK, D), lambda b, e: (b, 0)),
        scratch_shapes=[pltpu.VMEM((2, D, D_EXPERT), F32), pltpu.VMEM((2, D, D_EXPERT), F32),
                        pltpu.VMEM((2, D_EXPERT, D), F32), pltpu.SemaphoreType.DMA((2, 3)),
                        pltpu.VMEM((D, D_EXPERT), BF16), pltpu.VMEM((D, D_EXPERT), BF16),
                        pltpu.VMEM((D_EXPERT, D), BF16)],
    )
    return pl.pallas_call(
        functools.partial(_ffn_kernel, nblk, layer),
        grid_spec=grid_spec,
        out_shape=jax.ShapeDtypeStruct((n_slots, D), F32),
        compiler_params=_cparams(("arbitrary",)),
        name="moe_ffn",
    )(meta, xs, w1, w3, w2)


COMBINE_TM = 512
COMBINE_CHUNK = 128


def _combine_kernel(T, npt, slot_ref, x_ref, gate_ref, ys_ref, *refs):
    out_refs, (buf, sems) = refs[:-2], refs[-2:]
    i = pl.program_id(0)
    base = i * COMBINE_TM
    n_chunks = COMBINE_TM // COMBINE_CHUNK

    def copy(r, k, q):
        return pltpu.make_async_copy(ys_ref.at[pl.ds(slot_ref[k * T + base + r], 1)],
                                     buf.at[k, pl.ds(r, 1)], sems.at[q])

    def for_chunk_rows(q, fn):
        def body(r, c):
            fn(copy(r, 0, q), 0)
            fn(copy(r, 1, q), 1)
            return c
        lax.fori_loop(q * COMBINE_CHUNK, (q + 1) * COMBINE_CHUNK, body, 0, unroll=8)

    for q in range(n_chunks):
        for_chunk_rows(q, lambda cp, k: cp.start(priority=k))

    def emit(rs, y):
        if npt == 0:
            out_refs[0][rs, :] = y
        else:
            @pl.when(i < npt)
            def _():
                out_refs[0][rs, :] = y

            @pl.when(i >= npt)
            def _():
                out_refs[1][rs, :] = y

    for q in range(n_chunks):
        for_chunk_rows(q, lambda cp, k: cp.wait())
        rs = slice(q * COMBINE_CHUNK, (q + 1) * COMBINE_CHUNK)
        g = gate_ref[rs, :]
        emit(rs, x_ref[rs, :] + (buf[0, rs, :] * g[:, 0:1] + buf[1, rs, :] * g[:, 1:2]))


def _combine(slots, x1, gates, ys, split_rows=0):
    T, D = x1.shape
    tm = COMBINE_TM
    npt = split_rows // tm
    if npt == 0:
        out_specs = pl.BlockSpec((tm, D), lambda i, s: (i, 0))
        out_shape = jax.ShapeDtypeStruct((T, D), F32)
    else:
        out_specs = [pl.BlockSpec((tm, D), lambda i, s: (jnp.minimum(i, npt - 1), 0)),
                     pl.BlockSpec((tm, D), lambda i, s: (jnp.maximum(i - npt, 0), 0))]
        out_shape = [jax.ShapeDtypeStruct((split_rows, D), F32),
                     jax.ShapeDtypeStruct((T - split_rows, D), F32)]
    grid_spec = pltpu.PrefetchScalarGridSpec(
        num_scalar_prefetch=1,
        grid=(T // tm,),
        in_specs=[pl.BlockSpec((tm, D), lambda i, s: (i, 0)),
                  pl.BlockSpec((tm, LOGIT_W), lambda i, s: (i, 0)),
                  pl.BlockSpec(memory_space=pl.ANY)],
        out_specs=out_specs,
        scratch_shapes=[pltpu.VMEM((2, tm, D), F32),
                        pltpu.SemaphoreType.DMA((COMBINE_TM // COMBINE_CHUNK,))],
    )
    return pl.pallas_call(
        functools.partial(_combine_kernel, T, npt),
        grid_spec=grid_spec,
        out_shape=out_shape,
        compiler_params=_cparams(("arbitrary",)),
        name="moe_combine",
    )(slots, x1, gates, ys)


def _moe(x1, h, logits, w1, w3, w2, layer, split_rows=0):
    T = x1.shape[0]
    slot_l, gate_l, cnt = _route(logits)
    slots = slot_l.reshape(2 * T)
    n_blocks = -(-(2 * T) // MOE_BLOCK) + N_EXPERTS
    counts = cnt[:, 0].astype(jnp.int32)
    pend = jnp.cumsum((counts + MOE_BLOCK - 1) // MOE_BLOCK * MOE_BLOCK)
    starts = jnp.arange(n_blocks, dtype=jnp.int32) * MOE_BLOCK
    blk_e = jnp.minimum(jnp.sum((pend[None, :] <= starts[:, None]).astype(jnp.int32), axis=1),
                        N_EXPERTS - 1)
    present = (counts > 0).astype(jnp.int32)
    rank = jnp.cumsum(present) - present
    ids = jnp.where(present > 0, jnp.arange(N_EXPERTS, dtype=jnp.int32), N_EXPERTS)
    after = jnp.concatenate([lax.cummin(ids[::-1])[::-1][1:], jnp.full((1,), N_EXPERTS, jnp.int32)])
    nxt = jnp.where(after < N_EXPERTS, after, -1)
    meta = jnp.concatenate([blk_e, pend[-1:] // MOE_BLOCK, rank, nxt]).astype(jnp.int32)
    xs = _dispatch(slots, h, n_blocks * MOE_BLOCK)
    ys = _ffn(meta, xs, w1, w3, w2, layer)
    return _combine(slots, x1, gate_l, ys, split_rows)


def _router_weights(w_grp, b_grp, w_exp, b_exp):
    D = w_grp.shape[0]
    pad = LOGIT_W - N_GROUPS - N_EXPERTS
    wr = jnp.concatenate([w_grp, w_exp, jnp.zeros((D, pad), F32)], axis=1).astype(BF16)
    br = jnp.concatenate([b_grp, b_exp, jnp.zeros((pad,), F32)]).reshape(LOGIT_W, 1)
    return wr.T, br


def kernel(x_prompt, x_sample, cache_conv, cache_k, cache_v, state_hgrn, norm_mix, norm_ffn,
           ev_w_in, ev_conv, ev_q_norm, ev_k_norm, ev_sinks, ev_w_out,
           od_w_in, od_lb, od_o_norm, od_w_out,
           moe_w_grp, moe_b_grp, moe_w_exp, moe_b_exp, moe_w1, moe_w3, moe_w2):
    B, L, D = x_prompt.shape
    Bd, Ld, _ = x_sample.shape
    Tp = B * L
    x = (x_prompt.reshape(Tp, D), x_sample.reshape(Bd * Ld, D))

    proj = _norm_matmul(x, norm_mix[0], ev_w_in[0].astype(BF16), 1024)
    qg2 = jnp.tile(ev_q_norm[0], 2).reshape(1, LANES)
    kg2 = jnp.tile(ev_k_norm[0], 2).reshape(1, LANES)
    mix_p, k_p, v_p, conv_p = _even_prompt(proj, B, L, ev_conv[0], qg2, kg2, ev_sinks[0])
    mix_s, k_s, v_s, conv_s = _even_sample(
        proj, Bd, Ld, Tp, ev_conv[0], qg2, kg2, ev_sinks[0], cache_conv[0],
        cache_k[0].reshape(Bd, WINDOW, LANES), cache_v[0].reshape(Bd, WINDOW, LANES))
    wr, br = _router_weights(moe_w_grp[0], moe_b_grp[0], moe_w_exp[0], moe_b_exp[0])
    x1, h, logits = _out_proj((mix_p, mix_s), ev_w_out[0].astype(BF16), x, norm_ffn[0], wr, br, 1024)
    x = _moe(x1, h, logits, moe_w1, moe_w3, moe_w2, 0)

    lb_all = jnp.cumsum(jax.nn.softmax(od_lb.astype(F32), axis=0), axis=0)
    lb = (lb_all - lb_all[0])[1].reshape(1, C_HK)
    og = od_o_norm[0].reshape(1, C_DV)
    proj = _norm_matmul(x, norm_mix[1], od_w_in[0].astype(BF16), 512)
    o_p, s_p = _hgrn_prompt(proj, B, L, lb, og)
    o_s, s_s = _hgrn_sample(proj, Bd, Ld, Tp, lb, og, state_hgrn[0])
    wr, br = _router_weights(moe_w_grp[1], moe_b_grp[1], moe_w_exp[1], moe_b_exp[1])
    x1, h, logits = _out_proj((o_p, o_s), od_w_out[0].astype(BF16), x, norm_ffn[1], wr, br, 1024)
    y_p, y_s = _moe(x1, h, logits, moe_w1, moe_w3, moe_w2, 1, split_rows=Tp)

    y_prompt = y_p.reshape(B, L, D)
    y_sample = y_s.reshape(Bd, Ld, D)
    return (y_prompt, y_sample,
            conv_p[None], k_p.reshape(1, B, WINDOW, N_KV, HEAD_DIM), v_p.reshape(1, B, WINDOW, N_KV, HEAD_DIM),
            s_p[None],
            conv_s[None], k_s.reshape(1, Bd, WINDOW, N_KV, HEAD_DIM), v_s.reshape(1, Bd, WINDOW, N_KV, HEAD_DIM),
            s_s[None])
```
